```python
import math
import jax, jax.numpy as jnp
from jax import lax
import numpy as np

D_MODEL = 1024
BATCH = 4
SEQ = 4096
DEPTH = 1
DEC_BATCH = 32
DEC_SEQ = 1
PAST_LEN = 8192
PAGE_SIZE = 128

N_HEADS = 8
HEAD_DIM = 64
ATTN_WIDTH = N_HEADS * HEAD_DIM
MOBA_BLOCK = 256
MOBA_TOPK = 3
Q_CHUNK = 16
ROPE_THETA = 10000.0
SSM_WIDTH = D_MODEL // 2
SSM_GROUP = 16
SSM_GROUPS = SSM_WIDTH // SSM_GROUP
SSM_STATE = 64
DT_MIN = 1e-3
DT_MAX = 1e-1
FFN_HIDDEN = ((8 * D_MODEL + 3 * 256 - 1) // (3 * 256)) * 256
IN_WIDTH = 3 * ATTN_WIDTH + SSM_WIDTH + 2 * D_MODEL
RMS_EPS = 1e-6

kernel_name = 'moba_s5_gated_hybrid_step'

F32 = jnp.float32


def rmsnorm(x, g):
    xf = x.astype(F32)
    y = xf * lax.rsqrt(jnp.mean(xf * xf, axis=-1, keepdims=True) + RMS_EPS)
    return (y * g.astype(F32)).astype(x.dtype)


def rotary(x, pos):
    half = HEAD_DIM // 2
    inv = jnp.power(jnp.float32(ROPE_THETA), -2.0 * jnp.arange(half, dtype=F32) / HEAD_DIM)
    ang = pos.astype(F32)[:, None] * inv[None, :]
    cos = jnp.cos(ang)[None, :, None, :]
    sin = jnp.sin(ang)[None, :, None, :]
    xf = x.astype(F32)
    x1, x2 = xf[..., :half], xf[..., half:]
    return jnp.concatenate([x1 * cos - x2 * sin, x2 * cos + x1 * sin], axis=-1).astype(x.dtype)


def to_blocks(k, v):
    b, L = k.shape[:2]
    nb = -(-L // MOBA_BLOCK)
    pad = nb * MOBA_BLOCK - L
    kb = jnp.pad(k, ((0, 0), (0, pad), (0, 0), (0, 0))).reshape(b, nb, MOBA_BLOCK, N_HEADS, HEAD_DIM)
    vb = jnp.pad(v, ((0, 0), (0, pad), (0, 0), (0, 0))).reshape(b, nb, MOBA_BLOCK, N_HEADS, HEAD_DIM)
    k_mean = jnp.mean(kb, axis=2, dtype=F32)
    return kb, vb, k_mean


def moba_attend(q, kb, vb, k_mean, q_pos):
    bsz, nq = q.shape[:2]
    nb = kb.shape[1]
    q_blk = q_pos // MOBA_BLOCK
    qf = q.astype(F32)
    s_blk = jnp.einsum('bqhd,bnhd->bhqn', qf, k_mean)
    fully_past = jnp.arange(nb)[None, :] < q_blk[:, None]
    s_blk = jnp.where(fully_past[None, None], s_blk, -jnp.inf)
    n_sel = min(MOBA_TOPK, nb)
    _, top_idx = lax.top_k(s_blk, n_sel)
    own = jnp.broadcast_to(q_blk[None, None, :, None], (bsz, N_HEADS, nq, 1)).astype(top_idx.dtype)
    idx = jnp.concatenate([top_idx, own], axis=-1)
    slot_ok = jnp.concatenate([jnp.arange(n_sel)[None, :] < q_blk[:, None],
                               jnp.ones((nq, 1), dtype=bool)], axis=-1)
    bi = jnp.arange(bsz)[:, None, None, None]
    hi = jnp.arange(N_HEADS)[None, :, None, None]
    kg = kb[bi, idx, :, hi, :]
    vg = vb[bi, idx, :, hi, :]
    key_pos = idx[..., None] * MOBA_BLOCK + jnp.arange(MOBA_BLOCK)
    mask = slot_ok[None, None, :, :, None] & (key_pos <= q_pos[None, None, :, None, None])
    s = jnp.einsum('bqhd,bhqnkd->bhqnk', q, kg, preferred_element_type=F32) * (HEAD_DIM ** -0.5)
    s = jnp.where(mask, s, -jnp.inf)
    p = jax.nn.softmax(s.reshape(bsz, N_HEADS, nq, -1), axis=-1).reshape(s.shape)
    o = jnp.einsum('bhqnk,bhqnkd->bqhd', p.astype(vg.dtype), vg, preferred_element_type=F32)
    return o.astype(q.dtype)


def prompt_attend(q, k, v, pos):
    bsz, L = q.shape[:2]
    kb, vb, km = to_blocks(k, v)
    nc = L // Q_CHUNK
    qc = q.reshape(bsz, nc, Q_CHUNK, N_HEADS, HEAD_DIM).swapaxes(0, 1)
    pc = pos.reshape(nc, Q_CHUNK)
    o = lax.map(lambda a: moba_attend(a[0], kb, vb, km, a[1]), (qc, pc))
    return o.swapaxes(0, 1).reshape(bsz, L, N_HEADS, HEAD_DIM)


def s5_scan(u, h0, a_re, a_im, log_dt, b_re, b_im, c_re, c_im, d):
    bsz, L = u.shape[:2]
    uf = u.astype(F32).reshape(bsz, L, SSM_GROUPS, SSM_GROUP)
    lam = lax.complex(a_re.astype(F32), a_im.astype(F32))
    dt = jnp.exp(log_dt.astype(F32))[:, None]
    a_bar = jnp.exp(lam * dt)
    b_c = lax.complex(b_re.astype(F32), b_im.astype(F32))
    b_bar = ((a_bar - 1.0) / lam)[..., None] * b_c
    c_c = lax.complex(c_re.astype(F32), c_im.astype(F32))
    bu = jnp.einsum('gpc,blgc->blgp', b_bar, uf.astype(jnp.complex64))
    bu = bu.at[:, 0].add(a_bar[None] * h0)
    a_seq = jnp.broadcast_to(a_bar, bu.shape)

    def combine(e1, e2):
        return (e2[0] * e1[0], e2[0] * e1[1] + e2[1])

    _, h = lax.associative_scan(combine, (a_seq, bu), axis=1)
    y = jnp.einsum('gcp,blgp->blgc', c_c, h).real + d.astype(F32).reshape(SSM_GROUPS, SSM_GROUP) * uf
    return y.reshape(bsz, L, SSM_WIDTH).astype(u.dtype), h[:, -1]


def trunk_layer(x, pos, attend, h0, lp):
    bsz, L = x.shape[:2]
    h = rmsnorm(x, lp['norm_mix'])
    proj = h @ lp['w_in']
    cuts = [ATTN_WIDTH, 2 * ATTN_WIDTH, 3 * ATTN_WIDTH, 3 * ATTN_WIDTH + SSM_WIDTH,
            3 * ATTN_WIDTH + SSM_WIDTH + D_MODEL]
    q, k, v, u, g_attn, g_ssm = jnp.split(proj, cuts, axis=-1)
    q = rotary(q.reshape(bsz, L, N_HEADS, HEAD_DIM), pos)
    k = rotary(k.reshape(bsz, L, N_HEADS, HEAD_DIM), pos)
    v = v.reshape(bsz, L, N_HEADS, HEAD_DIM)
    o = attend(q, k, v, pos)
    attn_out = o.reshape(bsz, L, ATTN_WIDTH) @ lp['w_attn_proj']
    y, h_last = s5_scan(u, h0, lp['a_re'], lp['a_im'], lp['log_dt'], lp['b_re'], lp['b_im'],
                        lp['c_re'], lp['c_im'], lp['d'])
    z = jax.nn.gelu(y)
    z = z * jax.nn.sigmoid(z @ lp['w_glu'] + lp['b_glu'])
    ssm_out = z @ lp['w_ssm_proj']
    merged = jax.nn.sigmoid(g_attn) * attn_out + jax.nn.sigmoid(g_ssm) * ssm_out
    x = x + merged @ lp['w_out']
    hf = rmsnorm(x, lp['norm_ffn'])
    a, g = jnp.split(hf @ lp['w_ffn_in'], 2, axis=-1)
    x = x + (jax.nn.silu(a) * g) @ lp['w_ffn_out']
    return x, k, v, h_last


def setup_inputs(seed: int = 0) -> dict:
    key = jax.random.key(seed)
    ks = jax.random.split(key, 32)
    n_pages = PAST_LEN // PAGE_SIZE
    n_used = DEC_BATCH * n_pages
    n_phys = n_used + (n_used + 3) // 4

    def nrm(k, shape, scale):
        return jax.random.normal(k, shape, F32) * scale

    n_idx = jnp.arange(SSM_STATE, dtype=F32)
    return {
        'x_prompt': nrm(ks[0], (BATCH, SEQ, D_MODEL), 1.0),
        'x_sample': nrm(ks[1], (DEC_BATCH, DEC_SEQ, D_MODEL), 1.0),
        'cache_k': nrm(ks[2], (DEPTH, n_phys, PAGE_SIZE, N_HEADS, HEAD_DIM), 1.0),
        'cache_v': nrm(ks[3], (DEPTH, n_phys, PAGE_SIZE, N_HEADS, HEAD_DIM), 1.0),
        'state_ssm_re': nrm(ks[4], (DEPTH, DEC_BATCH, SSM_GROUPS, SSM_STATE), 0.5),
        'state_ssm_im': nrm(ks[5], (DEPTH, DEC_BATCH, SSM_GROUPS, SSM_STATE), 0.5),
        'page_table': jax.random.permutation(ks[6], n_phys)[:n_used].reshape(DEC_BATCH, n_pages).astype(jnp.int32),
        'norm_mix': 1.0 + nrm(ks[7], (DEPTH, D_MODEL), 0.02),
        'w_in': nrm(ks[8], (DEPTH, D_MODEL, IN_WIDTH), D_MODEL ** -0.5),
        'w_attn_proj': nrm(ks[9], (DEPTH, ATTN_WIDTH, D_MODEL), ATTN_WIDTH ** -0.5),
        'ssm_a_re': -0.5 + nrm(ks[10], (DEPTH, SSM_GROUPS, SSM_STATE), 0.01),
        'ssm_a_im': math.pi * n_idx + nrm(ks[11], (DEPTH, SSM_GROUPS, SSM_STATE), 0.01),
        'ssm_log_dt': jax.random.uniform(ks[12], (DEPTH, SSM_GROUPS), F32, math.log(DT_MIN), math.log(DT_MAX)),
        'ssm_b_re': nrm(ks[13], (DEPTH, SSM_GROUPS, SSM_STATE, SSM_GROUP), (2 * SSM_GROUP) ** -0.5),
        'ssm_b_im': nrm(ks[14], (DEPTH, SSM_GROUPS, SSM_STATE, SSM_GROUP), (2 * SSM_GROUP) ** -0.5),
        'ssm_c_re': nrm(ks[15], (DEPTH, SSM_GROUPS, SSM_GROUP, SSM_STATE), (2 * SSM_STATE) ** -0.5),
        'ssm_c_im': nrm(ks[16], (DEPTH, SSM_GROUPS, SSM_GROUP, SSM_STATE), (2 * SSM_STATE) ** -0.5),
        'ssm_d': nrm(ks[17], (DEPTH, SSM_WIDTH), 1.0),
        'w_glu': nrm(ks[18], (DEPTH, SSM_WIDTH, SSM_WIDTH), SSM_WIDTH ** -0.5),
        'b_glu': nrm(ks[19], (DEPTH, SSM_WIDTH), 0.01),
        'w_ssm_proj': nrm(ks[20], (DEPTH, SSM_WIDTH, D_MODEL), SSM_WIDTH ** -0.5),
        'w_out': nrm(ks[21], (DEPTH, D_MODEL, D_MODEL), D_MODEL ** -0.5),
        'norm_ffn': 1.0 + nrm(ks[22], (DEPTH, D_MODEL), 0.02),
        'w_ffn_in': nrm(ks[23], (DEPTH, D_MODEL, 2 * FFN_HIDDEN), D_MODEL ** -0.5),
        'w_ffn_out': nrm(ks[24], (DEPTH, FFN_HIDDEN, D_MODEL), FFN_HIDDEN ** -0.5),
        'norm_final': 1.0 + nrm(ks[25], (D_MODEL,), 0.02),
    }


def reference(x_prompt, x_sample, cache_k, cache_v, state_ssm_re, state_ssm_im, page_table,
              norm_mix, w_in, w_attn_proj, ssm_a_re, ssm_a_im, ssm_log_dt, ssm_b_re, ssm_b_im,
              ssm_c_re, ssm_c_im, ssm_d, w_glu, b_glu, w_ssm_proj, w_out, norm_ffn,
              w_ffn_in, w_ffn_out, norm_final):
    n_prompt, seq = x_prompt.shape[:2]
    n_seq, dec_seq = x_sample.shape[:2]
    pos_p = jnp.arange(seq, dtype=jnp.int32)
    pos_s = PAST_LEN + jnp.arange(dec_seq, dtype=jnp.int32)
    h0_prompt = jnp.zeros((n_prompt, SSM_GROUPS, SSM_STATE), jnp.complex64)

    xp, xs = x_prompt, x_sample
    kp_l, vp_l, rp_l, ip_l, ks_l, vs_l, rs_l, is_l = [], [], [], [], [], [], [], []
    for l in range(DEPTH):
        lp = dict(norm_mix=norm_mix[l], w_in=w_in[l], w_attn_proj=w_attn_proj[l],
                  a_re=ssm_a_re[l], a_im=ssm_a_im[l], log_dt=ssm_log_dt[l],
                  b_re=ssm_b_re[l], b_im=ssm_b_im[l], c_re=ssm_c_re[l], c_im=ssm_c_im[l],
                  d=ssm_d[l], w_glu=w_glu[l], b_glu=b_glu[l], w_ssm_proj=w_ssm_proj[l],
                  w_out=w_out[l], norm_ffn=norm_ffn[l], w_ffn_in=w_ffn_in[l], w_ffn_out=w_ffn_out[l])

        xp, kp, vp, hp = trunk_layer(xp, pos_p, prompt_attend, h0_prompt, lp)

        past_k = cache_k[l][page_table].reshape(n_seq, -1, N_HEADS, HEAD_DIM)
        past_v = cache_v[l][page_table].reshape(n_seq, -1, N_HEADS, HEAD_DIM)

        def sample_attend(q, k, v, pos, past_k=past_k, past_v=past_v):
            k_all = jnp.concatenate([past_k.astype(k.dtype), k], axis=1)
            v_all = jnp.concatenate([past_v.astype(v.dtype), v], axis=1)
            kb, vb, km = to_blocks(k_all, v_all)
            return moba_attend(q, kb, vb, km, pos)

        h0_s = lax.complex(state_ssm_re[l].astype(F32), state_ssm_im[l].astype(F32))
        xs, ks_new, vs_new, hs = trunk_layer(xs, pos_s, sample_attend, h0_s, lp)

        kp_l.append(kp); vp_l.append(vp); rp_l.append(hp.real); ip_l.append(hp.imag)
        ks_l.append(ks_new); vs_l.append(vs_new); rs_l.append(hs.real); is_l.append(hs.imag)

    y_prompt = rmsnorm(xp, norm_final)
    y_sample = rmsnorm(xs, norm_final)
    new_k_prompt = jnp.stack(kp_l)
    new_v_prompt = jnp.stack(vp_l)
    new_ssm_re_prompt = jnp.stack(rp_l)
    new_ssm_im_prompt = jnp.stack(ip_l)
    new_k_sample = jnp.stack(ks_l)
    new_v_sample = jnp.stack(vs_l)
    new_ssm_re_sample = jnp.stack(rs_l)
    new_ssm_im_sample = jnp.stack(is_l)
    return (y_prompt, y_sample, new_k_prompt, new_v_prompt, new_ssm_re_prompt, new_ssm_im_prompt,
            new_k_sample, new_v_sample, new_ssm_re_sample, new_ssm_im_sample)
```

```python
import functools
import math

import jax
import jax.numpy as jnp
from jax import lax
from jax.experimental import pallas as pl
from jax.experimental.pallas import tpu as pltpu

F32 = jnp.float32
BF16 = jnp.bfloat16

D_MODEL = 1024
N_HEADS = 8
HEAD_DIM = 64
ATTN_WIDTH = N_HEADS * HEAD_DIM
MOBA_BLOCK = 256
MOBA_TOPK = 3
ROPE_THETA = 10000.0
SSM_WIDTH = 512
SSM_GROUP = 16
SSM_GROUPS = 32
SSM_STATE = 64
FFN_HIDDEN = 2816
RMS_EPS = 1e-6
PAGE_SIZE = 128

LANES = 128
SUBLANES = 8
GROUPS_PER_SG = LANES // SSM_GROUP
N_SG = SSM_GROUPS // GROUPS_PER_SG
SG_STATE = GROUPS_PER_SG * SSM_STATE
SSM_CHUNK = 8
NEG_BIG = -1e30
VMEM_LIMIT = 56 * 1024 * 1024


def _cparams(sem):
    return pltpu.CompilerParams(dimension_semantics=sem, vmem_limit_bytes=VMEM_LIMIT)


def _dot(a, b):
    return jnp.dot(a, b, preferred_element_type=F32)


def _rms(x, g):
    return x * lax.rsqrt(jnp.mean(x * x, axis=-1, keepdims=True) + RMS_EPS) * g


def _inproj_core(x, g, w_ref, cos, sin):
    h = _rms(x, g).astype(BF16)
    lane = lax.broadcasted_iota(jnp.int32, (1, ATTN_WIDTH), 1)
    first_half = (lane % HEAD_DIM) < (HEAD_DIM // 2)

    def rot(t):
        partner = jnp.where(first_half,
                            pltpu.roll(t, ATTN_WIDTH - HEAD_DIM // 2, 1),
                            pltpu.roll(t, HEAD_DIM // 2, 1))
        return t * cos + partner * sin

    a = ATTN_WIDTH
    q = rot(_dot(h, w_ref[:, 0:a])) * (HEAD_DIM ** -0.5)
    k = rot(_dot(h, w_ref[:, a:2 * a]))
    v = _dot(h, w_ref[:, 2 * a:3 * a])
    u = _dot(h, w_ref[:, 3 * a:3 * a + SSM_WIDTH])
    o = 3 * a + SSM_WIDTH
    sga = jax.nn.sigmoid(_dot(h, w_ref[:, o:o + D_MODEL]))
    sgs = jax.nn.sigmoid(_dot(h, w_ref[:, o + D_MODEL:o + 2 * D_MODEL]))
    return q, k, v, u, sga, sgs


def _inproj_prompt_kernel(x_ref, g_ref, w_ref, cos_ref, sin_ref,
                          qT_ref, kT_ref, kbf_ref, kmean_ref, vT_ref, vTb_ref, u_ref, sga_ref, sgs_ref):
    reps = ATTN_WIDTH // LANES
    cos = jnp.tile(cos_ref[...], (1, reps))
    sin = jnp.tile(sin_ref[...], (1, reps))
    q, k, v, u, sga, sgs = _inproj_core(x_ref[...], g_ref[...], w_ref, cos, sin)
    tm = q.shape[0]
    kT_ref[0] = k.T
    vT_ref[0] = v.T
    kbf_ref[...] = k.astype(BF16)
    for s in range(tm // MOBA_BLOCK):
        rows = slice(s * MOBA_BLOCK, (s + 1) * MOBA_BLOCK)
        qT_ref[0, s] = q[rows].T.astype(BF16)
        vTb_ref[0, s] = v[rows].T.astype(BF16)
        kmean_ref[0, s] = jnp.mean(k[rows], axis=0, keepdims=True)
    u_ref[...] = u
    sga_ref[...] = sga.astype(BF16)
    sgs_ref[...] = sgs.astype(BF16)


def _inproj_prompt(x2d, g, w_bf, cos, sin, bsz, seq, tm):
    m = bsz * seq
    nb = seq // MOBA_BLOCK
    tpb = seq // tm
    sub = tm // MOBA_BLOCK
    a = ATTN_WIDTH
    full = lambda shape: pl.BlockSpec(shape, lambda b, t: (0,) * len(shape))
    tok = lambda w: pl.BlockSpec((tm, w), lambda b, t: (b * tpb + t, 0))
    out_shape = (
        jax.ShapeDtypeStruct((bsz, nb, a, MOBA_BLOCK), BF16),
        jax.ShapeDtypeStruct((bsz, a, seq), F32),
        jax.ShapeDtypeStruct((m, a), BF16),
        jax.ShapeDtypeStruct((bsz, nb, 1, a), F32),
        jax.ShapeDtypeStruct((bsz, a, seq), F32),
        jax.ShapeDtypeStruct((bsz, nb, a, MOBA_BLOCK), BF16),
        jax.ShapeDtypeStruct((m, SSM_WIDTH), F32),
        jax.ShapeDtypeStruct((m, D_MODEL), BF16),
        jax.ShapeDtypeStruct((m, D_MODEL), BF16),
    )
    blk_t = pl.BlockSpec((1, sub, a, MOBA_BLOCK), lambda b, t: (b, t, 0, 0))
    lane_t = pl.BlockSpec((1, a, tm), lambda b, t: (b, 0, t))
    out_specs = (blk_t, lane_t, tok(a),
                 pl.BlockSpec((1, sub, 1, a), lambda b, t: (b, t, 0, 0)),
                 lane_t, blk_t, tok(SSM_WIDTH), tok(D_MODEL), tok(D_MODEL))
    return pl.pallas_call(
        _inproj_prompt_kernel,
        grid=(bsz, tpb),
        in_specs=[tok(D_MODEL), full((1, D_MODEL)), full(w_bf.shape),
                  pl.BlockSpec((tm, LANES), lambda b, t: (t, 0)),
                  pl.BlockSpec((tm, LANES), lambda b, t: (t, 0))],
        out_specs=out_specs,
        out_shape=out_shape,
        compiler_params=_cparams(("parallel", "parallel")),
        name="inproj_prompt",
    )(x2d, g, w_bf, cos, sin)


def _inproj_sample_kernel(x_ref, g_ref, w_ref, cos_ref, sin_ref,
                          q_ref, k_ref, v_ref, u_ref, sga_ref, sgs_ref):
    reps = ATTN_WIDTH // LANES
    cos = jnp.tile(cos_ref[...], (1, reps))
    sin = jnp.tile(sin_ref[...], (1, reps))
    q, k, v, u, sga, sgs = _inproj_core(x_ref[...], g_ref[...], w_ref, cos, sin)
    q_ref[...] = q
    k_ref[...] = k
    v_ref[...] = v
    u_ref[...] = u
    sga_ref[...] = sga.astype(BF16)
    sgs_ref[...] = sgs.astype(BF16)


def _inproj_sample(x2d, g, w_bf, cos, sin):
    m = x2d.shape[0]
    a = ATTN_WIDTH
    full = lambda shape: pl.BlockSpec(shape, lambda i: (0,) * len(shape))
    shapes = [(m, a), (m, a), (m, a), (m, SSM_WIDTH), (m, D_MODEL), (m, D_MODEL)]
    dts = [F32, F32, F32, F32, BF16, BF16]
    return pl.pallas_call(
        _inproj_sample_kernel,
        grid=(1,),
        in_specs=[full(x2d.shape), full(g.shape), full(w_bf.shape), full(cos.shape), full(sin.shape)],
        out_specs=tuple(full(s) for s in shapes),
        out_shape=tuple(jax.ShapeDtypeStruct(s, d) for s, d in zip(shapes, dts)),
        compiler_params=_cparams(("arbitrary",)),
        name="inproj_sample",
    )(x2d, g, w_bf, cos, sin)


def _moba_prompt_kernel(qT_ref, k_ref, kmean_ref, vT_ref, o_ref, bias_ref):
    nb = qT_ref.shape[1]
    blk = MOBA_BLOCK
    row2 = lax.broadcasted_iota(jnp.int32, (2 * HEAD_DIM, 1), 0)
    lane_km = lax.broadcasted_iota(jnp.int32, (1, 2 * HEAD_DIM), 1)
    blk_row = lax.broadcasted_iota(jnp.int32, (nb, blk), 0)
    key_i = lax.broadcasted_iota(jnp.int32, (blk, blk), 0)
    qry_i = lax.broadcasted_iota(jnp.int32, (blk, blk), 1)
    causal = key_i <= qry_i

    def q_block(i, _):
        q_pair = qT_ref[0, i]
        k_own = k_ref[pl.ds(pl.multiple_of(i * blk, blk), blk), :]
        outs = []
        for hh in range(2):
            in_head = (row2 >= hh * HEAD_DIM) & (row2 < (hh + 1) * HEAD_DIM)
            qz = jnp.where(in_head, q_pair, jnp.zeros_like(q_pair))
            km = jnp.where((lane_km >= hh * HEAD_DIM) & (lane_km < (hh + 1) * HEAD_DIM),
                           kmean_ref[0], 0.0)
            sb = jnp.dot(km, q_pair.astype(F32), preferred_element_type=F32,
                         precision=lax.Precision.HIGHEST)
            sb = jnp.where(blk_row < i, sb, -jnp.inf)
            sel = jnp.zeros((nb, blk), dtype=jnp.bool_)
            for _r in range(MOBA_TOPK):
                mx = jnp.max(sb, axis=0, keepdims=True)
                first = jnp.min(jnp.where(sb == mx, blk_row, nb), axis=0, keepdims=True)
                pick = (blk_row == first) & (mx > -jnp.inf)
                sel = sel | pick
                sb = jnp.where(pick, -jnp.inf, sb)
            bias_ref[hh] = jnp.where(sel, 0.0, NEG_BIG)

            s = _dot(k_own, qz)
            s = jnp.where(causal, s, NEG_BIG)
            m0 = jnp.max(s, axis=0, keepdims=True)
            p = jnp.exp(s - m0)
            l0 = jnp.sum(p, axis=0, keepdims=True)
            v_own = vT_ref[0, i, hh * HEAD_DIM:(hh + 1) * HEAD_DIM, :]
            acc0 = _dot(v_own, p.astype(BF16))

            def kv_block(j, carry):
                m, l, acc = carry
                kj = k_ref[pl.ds(pl.multiple_of(j * blk, blk), blk), :]
                s = _dot(kj, qz)
                bj = bias_ref[hh, pl.ds(j, 1), :]
                m_new = jnp.maximum(m, jnp.max(s, axis=0, keepdims=True) + bj)
                alpha = jnp.exp(m - m_new)
                p = jnp.exp(s - (m_new - bj))
                l = alpha * l + jnp.sum(p, axis=0, keepdims=True)
                vj = vT_ref[0, j, hh * HEAD_DIM:(hh + 1) * HEAD_DIM, :]
                acc = alpha * acc + _dot(vj, p.astype(BF16))
                return m_new, l, acc

            m, l, acc = lax.fori_loop(0, i, kv_block, (m0, l0, acc0))
            outs.append(acc / l)
        oT = jnp.concatenate(outs, axis=0)
        o_ref[pl.ds(pl.multiple_of(i * blk, blk), blk), :] = oT.T.astype(o_ref.dtype)
        return 0

    lax.fori_loop(0, nb, q_block, 0)


def _moba_prompt(qT, kbf, kmean, vTb, bsz, seq):
    nb = seq // MOBA_BLOCK
    hp = N_HEADS // 2
    pair = 2 * HEAD_DIM
    return pl.pallas_call(
        _moba_prompt_kernel,
        grid=(bsz, hp),
        in_specs=[pl.BlockSpec((1, nb, pair, MOBA_BLOCK), lambda b, h: (b, 0, h, 0)),
                  pl.BlockSpec((seq, pair), lambda b, h: (b, h)),
                  pl.BlockSpec((1, nb, pair), lambda b, h: (b, 0, h)),
                  pl.BlockSpec((1, nb, pair, MOBA_BLOCK), lambda b, h: (b, 0, h, 0))],
        out_specs=pl.BlockSpec((seq, pair), lambda b, h: (b, h)),
        out_shape=jax.ShapeDtypeStruct((bsz * seq, ATTN_WIDTH), BF16),
        scratch_shapes=[pltpu.VMEM((2, nb, MOBA_BLOCK), F32)],
        compiler_params=_cparams(("parallel", "parallel")),
        name="moba_prompt",
    )(qT, kbf, kmean, vTb)


def _moba_paged_kernel(pt_ref, qcol_ref, kn_ref, vn_ref, ck_ref, cv_ref, o_ref,
                       kbuf, vbuf, s_ref, p_ref, ksem, vsem):
    b = pl.program_id(0)
    nseq = pl.num_programs(0)
    n_pages = kbuf.shape[1]
    nblk = n_pages // 2
    slot = b % 2

    def k_copy(seq_i, sl, p):
        return pltpu.make_async_copy(ck_ref.at[pt_ref[seq_i, p]], kbuf.at[sl, p], ksem.at[sl])

    def start_k(seq_i, sl):
        def body(p, _):
            k_copy(seq_i, sl, p).start()
            return 0
        lax.fori_loop(0, n_pages, body, 0)

    @pl.when(b == 0)
    def _():
        start_k(0, 0)

    def wait_body(p, _):
        k_copy(b, slot, p).wait()
        return 0
    lax.fori_loop(0, n_pages, wait_body, 0)

    @pl.when(b + 1 < nseq)
    def _():
        start_k(b + 1, 1 - slot)

    qcol = qcol_ref[0]
    own = qcol * kn_ref[0]
    blk_i = lax.broadcasted_iota(jnp.int32, (nblk, 1), 0)

    for h in range(N_HEADS):
        qb = jnp.broadcast_to(qcol[h * HEAD_DIM:(h + 1) * HEAD_DIM, :], (HEAD_DIM, PAGE_SIZE))

        def score_body(n, _):
            for par in range(2):
                kt = kbuf[slot, 2 * n + par, h]
                s_ref[par, h, pl.ds(n, 1), :] = jnp.sum(kt * qb, axis=0, keepdims=True)
            return 0
        lax.fori_loop(0, nblk, score_body, 0)

    v_copies = []
    stats = []
    for h in range(N_HEADS):
        s0 = s_ref[0, h]
        s1 = s_ref[1, h]
        bs = jnp.sum(s0 + s1, axis=1, keepdims=True)
        sel = jnp.zeros((nblk, 1), dtype=jnp.bool_)
        for r in range(MOBA_TOPK):
            mx = jnp.max(bs, axis=0, keepdims=True)
            first = jnp.min(jnp.where(bs == mx, blk_i, nblk), axis=0, keepdims=True)
            pick = (blk_i == first) & (mx > -jnp.inf)
            sel = sel | pick
            bs = jnp.where(pick, -jnp.inf, bs)
            blk_id = jnp.max(jnp.where(pick, blk_i, 0))
            for par in range(2):
                cp = pltpu.make_async_copy(cv_ref.at[pt_ref[b, 2 * blk_id + par], h],
                                           vbuf.at[h, r, par], vsem.at[0])
                cp.start()
                v_copies.append((cp, h, r, par, blk_id))
        s_own = jnp.sum(own[h * HEAD_DIM:(h + 1) * HEAD_DIM, :], axis=0, keepdims=True)
        sm0 = jnp.where(sel, s0, NEG_BIG)
        sm1 = jnp.where(sel, s1, NEG_BIG)
        mx = jnp.maximum(jnp.max(jnp.max(jnp.maximum(sm0, sm1), axis=1, keepdims=True),
                                 axis=0, keepdims=True), s_own)
        p0 = jnp.exp(sm0 - mx)
        p1 = jnp.exp(sm1 - mx)
        p_own = jnp.exp(s_own - mx)
        l = jnp.sum(jnp.sum(p0 + p1, axis=1, keepdims=True), axis=0, keepdims=True) + p_own
        p_ref[0, h] = p0
        p_ref[1, h] = p1
        stats.append((p_own, l))

    for cp, _h, _r, _par, _blk in v_copies:
        cp.wait()

    vn = vn_ref[0]
    for h in range(N_HEADS):
        p_own, l = stats[h]
        acc = jnp.zeros((HEAD_DIM, PAGE_SIZE), F32)
        for cp, hh, r, par, blk_id in v_copies:
            if hh != h:
                continue
            prow = p_ref[par, h, pl.ds(blk_id, 1), :]
            acc = acc + vbuf[h, r, par] * prow
        o_h = jnp.sum(acc, axis=1, keepdims=True) + p_own * vn[h * HEAD_DIM:(h + 1) * HEAD_DIM, :]
        o_ref[0, h * HEAD_DIM:(h + 1) * HEAD_DIM, :] = o_h / l


def _moba_paged(page_table, qcol, kncol, vncol, cache_kT, cache_vT):
    nseq, n_pages = page_table.shape
    col = pl.BlockSpec((1, ATTN_WIDTH, 1), lambda b, pt: (b, 0, 0))
    any_spec = pl.BlockSpec(memory_space=pl.ANY)
    grid_spec = pltpu.PrefetchScalarGridSpec(
        num_scalar_prefetch=1,
        grid=(nseq,),
        in_specs=[col, col, col, any_spec, any_spec],
        out_specs=col,
        scratch_shapes=[
            pltpu.VMEM((2, n_pages, N_HEADS, HEAD_DIM, PAGE_SIZE), F32),
            pltpu.VMEM((N_HEADS, MOBA_TOPK, 2, HEAD_DIM, PAGE_SIZE), F32),
            pltpu.VMEM((2, N_HEADS, n_pages // 2, PAGE_SIZE), F32),
            pltpu.VMEM((2, N_HEADS, n_pages // 2, PAGE_SIZE), F32),
            pltpu.SemaphoreType.DMA((2,)),
            pltpu.SemaphoreType.DMA((1,)),
        ])
    return pl.pallas_call(
        _moba_paged_kernel,
        grid_spec=grid_spec,
        out_shape=jax.ShapeDtypeStruct((nseq, ATTN_WIDTH, 1), F32),
        compiler_params=_cparams(("arbitrary",)),
        name="moba_paged",
    )(page_table, qcol, kncol, vncol, cache_kT, cache_vT)


def _ssm_weights(a_re, a_im, log_dt, b_re, b_im, c_re, c_im, chunk, n_pow):
    t_ = chunk
    lam = lax.complex(a_re.astype(F32), a_im.astype(F32))
    dt = jnp.exp(log_dt.astype(F32))[:, None]
    ldt = lam * dt
    a_bar = jnp.exp(ldt)
    b_bar = ((a_bar - 1.0) / lam)[..., None] * lax.complex(b_re.astype(F32), b_im.astype(F32))
    c_c = lax.complex(c_re.astype(F32), c_im.astype(F32))
    taus = jnp.arange(t_ + 1, dtype=F32).astype(jnp.complex64)
    apow = jnp.exp(ldt[None] * taus[:, None, None])
    eye = jnp.eye(GROUPS_PER_SG, dtype=F32)
    sg = lambda x, axis: x.reshape(x.shape[:axis] + (N_SG, GROUPS_PER_SG) + x.shape[axis + 1:])

    ktoep = jnp.einsum('gcp,tgp,gpd->tgcd', c_c, apow[:t_], b_bar).real
    s_i = jnp.arange(t_)[:, None]
    t_i = jnp.arange(t_)[None, :]
    kst = jnp.where((s_i <= t_i)[:, :, None, None, None],
                    ktoep[jnp.clip(t_i - s_i, 0, t_ - 1)], 0.0)
    w_toep = jnp.einsum('stgjcd,jk->gsjdtkc', sg(kst, 2), eye)
    w_toep = w_toep.reshape(N_SG, t_ * LANES, t_ * LANES)

    mst = apow[:t_][::-1][..., None] * b_bar[None]
    mst = jnp.stack([mst.real, mst.imag], axis=-1)
    w_state = jnp.einsum('sgjpdr,jk->gsjdrkp', sg(mst, 1), eye)
    w_state = w_state.reshape(N_SG, t_ * LANES, 2 * SG_STATE)

    nout = c_c[None] * apow[1:][:, :, None, :]
    nout = jnp.stack([nout.real, -nout.imag], axis=-1)
    w_out = jnp.einsum('tgjcpr,jk->grjptkc', sg(nout, 1), eye)
    w_out = w_out.reshape(N_SG, 2 * SG_STATE, t_ * LANES)

    rs = jnp.arange(1, n_pow + 1, dtype=F32).astype(jnp.complex64)
    ap = jnp.exp((ldt * t_)[None] * rs[:, None, None])
    ap = jnp.stack([ap.real, ap.imag], axis=0)
    a_pow = sg(ap, 2).transpose(2, 1, 0, 3, 4).reshape(N_SG, n_pow, 2 * SG_STATE)
    return w_toep, w_state, w_out, a_pow


def _ssm_prompt_kernel(u_ref, wt_ref, ws_ref, wo_ref, ap_ref, d_ref, h0_ref,
                       y_ref, hf_ref, uc_ref, x_ref, hin_ref):
    t_ = SSM_CHUNK
    nc = uc_ref.shape[0]
    ns = SG_STATE
    for t in range(t_):
        uc_ref[:, t * LANES:(t + 1) * LANES] = u_ref[pl.ds(t, nc, stride=t_), :].astype(BF16)
    uc = uc_ref[...]
    x_ref[...] = _dot(uc, ws_ref[0])

    ap = ap_ref[0]
    apr, api = ap[:, :ns], ap[:, ns:]
    row = lax.broadcasted_iota(jnp.int32, (SUBLANES, 1), 0)

    def tile_body(k, carry):
        hr, hi = carry
        rows = pl.ds(pl.multiple_of(k * SUBLANES, SUBLANES), SUBLANES)
        xr = x_ref[rows, 0:ns]
        xi = x_ref[rows, ns:2 * ns]
        for d in (1, 2, 4):
            ar, ai = apr[d - 1:d], api[d - 1:d]
            sr = jnp.where(row >= d, pltpu.roll(xr, d, 0), 0.0)
            si = jnp.where(row >= d, pltpu.roll(xi, d, 0), 0.0)
            xr, xi = xr + ar * sr - ai * si, xi + ar * si + ai * sr
        outr = xr + apr * hr - api * hi
        outi = xi + apr * hi + api * hr
        hin_ref[rows, 0:ns] = jnp.where(row >= 1, pltpu.roll(outr, 1, 0), hr)
        hin_ref[rows, ns:2 * ns] = jnp.where(row >= 1, pltpu.roll(outi, 1, 0), hi)
        return outr[SUBLANES - 1:SUBLANES], outi[SUBLANES - 1:SUBLANES]

    h0 = h0_ref[0, 0]
    hr, hi = lax.fori_loop(0, nc // SUBLANES, tile_body, (h0[:, :ns], h0[:, ns:]))
    hf_ref[0, 0] = jnp.concatenate([hr, hi], axis=1)

    y = _dot(uc, wt_ref[0]) + _dot(hin_ref[...].astype(BF16), wo_ref[0])
    dv = d_ref[...]
    for t in range(t_):
        rows = pl.ds(t, nc, stride=t_)
        y_ref[rows, :] = y[:, t * LANES:(t + 1) * LANES] + dv * u_ref[rows, :]


def _ssm_prompt(u, w_toep, w_state, w_out, a_pow, d_row, h0, bsz, seq):
    t_ = SSM_CHUNK
    nc = seq // t_
    kw = t_ * LANES
    wspec = lambda shape: pl.BlockSpec((1,) + shape, lambda s, b: (s, 0, 0))
    return pl.pallas_call(
        _ssm_prompt_kernel,
        grid=(N_SG, bsz),
        in_specs=[pl.BlockSpec((seq, LANES), lambda s, b: (b, s)),
                  wspec((kw, kw)), wspec((kw, 2 * SG_STATE)), wspec((2 * SG_STATE, kw)),
                  wspec((SUBLANES, 2 * SG_STATE)),
                  pl.BlockSpec((1, LANES), lambda s, b: (0, s)),
                  pl.BlockSpec((1, 1, 1, 2 * SG_STATE), lambda s, b: (b, s, 0, 0))],
        out_specs=(pl.BlockSpec((seq, LANES), lambda s, b: (b, s)),
                   pl.BlockSpec((1, 1, 1, 2 * SG_STATE), lambda s, b: (b, s, 0, 0))),
        out_shape=(jax.ShapeDtypeStruct((bsz * seq, SSM_WIDTH), F32),
                   jax.ShapeDtypeStruct((bsz, N_SG, 1, 2 * SG_STATE), F32)),
        scratch_shapes=[pltpu.VMEM((nc, kw), BF16),
                        pltpu.VMEM((nc, 2 * SG_STATE), F32),
                        pltpu.VMEM((nc, 2 * SG_STATE), F32)],
        compiler_params=_cparams(("parallel", "parallel")),
        name="ssm_prompt",
    )(u, w_toep, w_state, w_out, a_pow, d_row, h0)


def _ssm_step_kernel(u_ref, wt_ref, ws_ref, wo_ref, ap_ref, d_ref, h0_ref, y_ref, hf_ref):
    ns = SG_STATE
    hp = lax.Precision.HIGHEST
    for s in range(N_SG):
        us = u_ref[:, s * LANES:(s + 1) * LANES]
        h0 = h0_ref[s]
        x = jnp.dot(us, ws_ref[s], preferred_element_type=F32, precision=hp)
        ar, ai = ap_ref[s, 0:1, :ns], ap_ref[s, 0:1, ns:]
        hr, hi = h0[:, :ns], h0[:, ns:]
        hf_ref[s] = jnp.concatenate([x[:, :ns] + ar * hr - ai * hi,
                                     x[:, ns:] + ar * hi + ai * hr], axis=1)
        y = (jnp.dot(us, wt_ref[s], preferred_element_type=F32, precision=hp)
             + jnp.dot(h0, wo_ref[s], preferred_element_type=F32, precision=hp))
        y_ref[:, s * LANES:(s + 1) * LANES] = y + d_ref[:, s * LANES:(s + 1) * LANES] * us


def _ssm_step(u, w_toep, w_state, w_out, a_pow, d_row, h0):
    m = u.shape[0]
    full = lambda a: pl.BlockSpec(a.shape, lambda i: (0,) * a.ndim)
    args = (u, w_toep, w_state, w_out, a_pow, d_row, h0)
    shapes = [(m, SSM_WIDTH), (N_SG, m, 2 * SG_STATE)]
    return pl.pallas_call(
        _ssm_step_kernel,
        grid=(1,),
        in_specs=[full(a) for a in args],
        out_specs=tuple(pl.BlockSpec(s, lambda i, n=len(s): (0,) * n) for s in shapes),
        out_shape=tuple(jax.ShapeDtypeStruct(s, F32) for s in shapes),
        compiler_params=_cparams(("arbitrary",)),
        name="ssm_step",
    )(*args)


FFN_SPLIT = 2
FFN_CHUNK = FFN_HIDDEN // FFN_SPLIT


def _post_kernel(x_ref, o_ref, y_ref, sga_ref, sgs_ref, wap_ref, wglu_ref, bglu_ref, wsp_ref,
                 wout_ref, nffn_ref, wfi_ref, wfo_ref, nfin_ref, out_ref):
    z = jax.nn.gelu(y_ref[...])
    z = z * jax.nn.sigmoid(_dot(z.astype(BF16), wglu_ref[...]) + bglu_ref[...])
    ssm_out = _dot(z.astype(BF16), wsp_ref[...])
    attn_out = _dot(o_ref[...].astype(BF16), wap_ref[...])
    merged = sga_ref[...].astype(F32) * attn_out + sgs_ref[...].astype(F32) * ssm_out
    x1 = x_ref[...] + _dot(merged.astype(BF16), wout_ref[...])
    hf = _rms(x1, nffn_ref[...]).astype(BF16)
    acc = x1
    for c in range(FFN_SPLIT):
        lo = c * FFN_CHUNK
        a = _dot(hf, wfi_ref[:, lo:lo + FFN_CHUNK])
        g = _dot(hf, wfi_ref[:, FFN_HIDDEN + lo:FFN_HIDDEN + lo + FFN_CHUNK])
        act = (jax.nn.silu(a) * g).astype(BF16)
        acc = acc + _dot(act, wfo_ref[lo:lo + FFN_CHUNK, :])
    out_ref[...] = _rms(acc, nfin_ref[...])


def _post(x2d, o, y, sga, sgs, wap, wglu, bglu, wsp, wout, nffn, wfi, wfo, nfin, tm):
    m = x2d.shape[0]
    tok = lambda w: pl.BlockSpec((tm, w), lambda i: (i, 0))
    const = lambda a: pl.BlockSpec(a.shape, lambda i: (0,) * a.ndim, pipeline_mode=pl.Buffered(1))
    weights = (wap, wglu, bglu, wsp, wout, nffn, wfi, wfo, nfin)
    return pl.pallas_call(
        _post_kernel,
        grid=(m // tm,),
        in_specs=[tok(D_MODEL), tok(ATTN_WIDTH), tok(SSM_WIDTH), tok(D_MODEL), tok(D_MODEL)]
                 + [const(w) for w in weights],
        out_specs=tok(D_MODEL),
        out_shape=jax.ShapeDtypeStruct((m, D_MODEL), F32),
        compiler_params=_cparams(("parallel",)),
        name="post",
    )(x2d, o, y, sga, sgs, *weights)


def _rope_tables(pos):
    half = HEAD_DIM // 2
    inv = jnp.power(jnp.float32(ROPE_THETA), -2.0 * jnp.arange(half, dtype=F32) / HEAD_DIM)
    ang = pos.astype(F32)[:, None] * inv[None, :]
    cos, sin = jnp.cos(ang), jnp.sin(ang)
    reps = LANES // HEAD_DIM
    return (jnp.tile(jnp.concatenate([cos, cos], axis=1), (1, reps)),
            jnp.tile(jnp.concatenate([-sin, sin], axis=1), (1, reps)))


def _leaf_from_T(xT, bsz, seq):
    return xT.reshape(bsz, N_HEADS, HEAD_DIM, seq).transpose(0, 3, 1, 2)[None]


def _state_in(re, im):
    n = re.shape[0]
    h = jnp.concatenate([re.reshape(n, N_SG, SG_STATE), im.reshape(n, N_SG, SG_STATE)], axis=-1)
    return h.transpose(1, 0, 2)


def _state_out(h):
    n = h.shape[0]
    return (h[..., :SG_STATE].reshape(1, n, SSM_GROUPS, SSM_STATE),
            h[..., SG_STATE:].reshape(1, n, SSM_GROUPS, SSM_STATE))


def kernel(x_prompt, x_sample, cache_k, cache_v, state_ssm_re, state_ssm_im, page_table, norm_mix, w_in,
           w_attn_proj, ssm_a_re, ssm_a_im, ssm_log_dt, ssm_b_re, ssm_b_im, ssm_c_re, ssm_c_im, ssm_d, w_glu,
           b_glu, w_ssm_proj, w_out, norm_ffn, w_ffn_in, w_ffn_out, norm_final):
    assert w_in.shape[0] == 1, "single layer"
    bsz, seq = x_prompt.shape[:2]
    nseq = x_sample.shape[0]
    past_len = page_table.shape[1] * PAGE_SIZE
    assert seq % MOBA_BLOCK == 0 and past_len % MOBA_BLOCK == 0 and x_sample.shape[1] == 1

    w_in_bf = w_in[0].astype(BF16)
    post_w = (w_attn_proj[0].astype(BF16), w_glu[0].astype(BF16), b_glu, w_ssm_proj[0].astype(BF16),
              w_out[0].astype(BF16), norm_ffn, w_ffn_in[0].astype(BF16), w_ffn_out[0].astype(BF16),
              norm_final[None])
    ssm_p = (ssm_a_re[0], ssm_a_im[0], ssm_log_dt[0], ssm_b_re[0], ssm_b_im[0], ssm_c_re[0], ssm_c_im[0])

    cos_p, sin_p = _rope_tables(jnp.arange(seq, dtype=jnp.int32))
    xp2 = x_prompt.reshape(bsz * seq, D_MODEL)
    qT, kT, kbf, kmean, vT, vTb, u_p, sga_p, sgs_p = _inproj_prompt(
        xp2, norm_mix, w_in_bf, cos_p, sin_p, bsz, seq, tm=512)
    o_p = _moba_prompt(qT, kbf, kmean.reshape(bsz, seq // MOBA_BLOCK, ATTN_WIDTH), vTb, bsz, seq)
    wt, ws, wo, ap = _ssm_weights(*ssm_p, chunk=SSM_CHUNK, n_pow=SUBLANES)
    h0_p = jnp.zeros((bsz, N_SG, 1, 2 * SG_STATE), F32)
    y_p, hf_p = _ssm_prompt(u_p, wt.astype(BF16), ws.astype(BF16), wo.astype(BF16), ap, ssm_d, h0_p, bsz, seq)
    y_prompt = _post(xp2, o_p, y_p, sga_p, sgs_p, *post_w, tm=512).reshape(bsz, seq, D_MODEL)
    new_ssm_re_p, new_ssm_im_p = _state_out(hf_p.reshape(bsz, N_SG, 2 * SG_STATE))

    cos_s, sin_s = _rope_tables(jnp.full((1,), past_len, dtype=jnp.int32))
    xs2 = x_sample.reshape(nseq, D_MODEL)
    q_s, k_s, v_s, u_s, sga_s, sgs_s = _inproj_sample(xs2, norm_mix, w_in_bf, cos_s, sin_s)
    cache_kT = cache_k[0].transpose(0, 2, 3, 1)
    cache_vT = cache_v[0].transpose(0, 2, 3, 1)
    o_s = _moba_paged(page_table, q_s[:, :, None], k_s[:, :, None], v_s[:, :, None], cache_kT, cache_vT)
    wt1, ws1, wo1, ap1 = _ssm_weights(*ssm_p, chunk=1, n_pow=1)
    h0_s = _state_in(state_ssm_re[0], state_ssm_im[0])
    y_s, hf_s = _ssm_step(u_s, wt1, ws1, wo1, ap1, ssm_d, h0_s)
    y_sample = _post(xs2, o_s.reshape(nseq, ATTN_WIDTH), y_s, sga_s, sgs_s, *post_w, tm=nseq)
    new_ssm_re_s, new_ssm_im_s = _state_out(hf_s.transpose(1, 0, 2))

    return (y_prompt, y_sample.reshape(nseq, 1, D_MODEL),
            _leaf_from_T(kT, bsz, seq), _leaf_from_T(vT, bsz, seq), new_ssm_re_p, new_ssm_im_p,
            k_s.reshape(1, nseq, 1, N_HEADS, HEAD_DIM), v_s.reshape(1, nseq, 1, N_HEADS, HEAD_DIM),
            new_ssm_re_s, new_ssm_im_s)
```

```python
import functools
import math

import jax
import jax.numpy as jnp
from jax import lax
from jax.experimental import pallas as pl
from jax.experimental.pallas import tpu as pltpu

F32 = jnp.float32
BF16 = jnp.bfloat16

D_MODEL = 1024
N_HEADS = 8
HEAD_DIM = 64
ATTN_WIDTH = N_HEADS * HEAD_DIM
MOBA_BLOCK = 256
MOBA_TOPK = 3
ROPE_THETA = 10000.0
SSM_WIDTH = 512
SSM_GROUP = 16
SSM_GROUPS = 32
SSM_STATE = 64
FFN_HIDDEN = 2816
RMS_EPS = 1e-6
PAGE_SIZE = 128

LANES = 128
SUBLANES = 8
GROUPS_PER_SG = LANES // SSM_GROUP
N_SG = SSM_GROUPS // GROUPS_PER_SG
SG_STATE = GROUPS_PER_SG * SSM_STATE
SSM_CHUNK = 8
NEG_BIG = -1e30
MOBA_HEADS_PER_STEP = 4
VMEM_LIMIT = 56 * 1024 * 1024


def _cparams(sem):
    return pltpu.CompilerParams(dimension_semantics=sem, vmem_limit_bytes=VMEM_LIMIT)


def _dot(a, b):
    return jnp.dot(a, b, preferred_element_type=F32)


def _rms(x, g):
    return x * lax.rsqrt(jnp.mean(x * x, axis=-1, keepdims=True) + RMS_EPS) * g


def _inproj_core(x, g, w_ref, cos, sin):
    h = _rms(x, g).astype(BF16)
    lane = lax.broadcasted_iota(jnp.int32, (1, ATTN_WIDTH), 1)
    first_half = (lane % HEAD_DIM) < (HEAD_DIM // 2)

    def rot(t):
        partner = jnp.where(first_half,
                            pltpu.roll(t, ATTN_WIDTH - HEAD_DIM // 2, 1),
                            pltpu.roll(t, HEAD_DIM // 2, 1))
        return t * cos + partner * sin

    a = ATTN_WIDTH
    q = rot(_dot(h, w_ref[:, 0:a])) * (HEAD_DIM ** -0.5)
    k = rot(_dot(h, w_ref[:, a:2 * a]))
    v = _dot(h, w_ref[:, 2 * a:3 * a])
    u = _dot(h, w_ref[:, 3 * a:3 * a + SSM_WIDTH])
    o = 3 * a + SSM_WIDTH
    sga = jax.nn.sigmoid(_dot(h, w_ref[:, o:o + D_MODEL]))
    sgs = jax.nn.sigmoid(_dot(h, w_ref[:, o + D_MODEL:o + 2 * D_MODEL]))
    return q, k, v, u, sga, sgs


def _inproj_prompt_kernel(x_ref, g_ref, w_ref, cos_ref, sin_ref,
                          qT_ref, kT_ref, kbf_ref, kmean_ref, vT_ref, vTb_ref, u_ref, sga_ref, sgs_ref):
    reps = ATTN_WIDTH // LANES
    cos = jnp.tile(cos_ref[...], (1, reps))
    sin = jnp.tile(sin_ref[...], (1, reps))
    q, k, v, u, sga, sgs = _inproj_core(x_ref[...], g_ref[...], w_ref, cos, sin)
    tm = q.shape[0]
    kT_ref[0] = k.T
    vT_ref[0] = v.T
    kbf_ref[...] = k.astype(BF16)
    for s in range(tm // MOBA_BLOCK):
        rows = slice(s * MOBA_BLOCK, (s + 1) * MOBA_BLOCK)
        qT_ref[0, s] = q[rows].T.astype(BF16)
        vTb_ref[0, s] = v[rows].T.astype(BF16)
        kmean_ref[0, s] = jnp.mean(k[rows], axis=0, keepdims=True)
    u_ref[...] = u
    sga_ref[...] = sga.astype(BF16)
    sgs_ref[...] = sgs.astype(BF16)


def _inproj_prompt(x2d, g, w_bf, cos, sin, bsz, seq, tm):
    m = bsz * seq
    nb = seq // MOBA_BLOCK
    tpb = seq // tm
    sub = tm // MOBA_BLOCK
    a = ATTN_WIDTH
    full = lambda shape: pl.BlockSpec(shape, lambda b, t: (0,) * len(shape))
    tok = lambda w: pl.BlockSpec((tm, w), lambda b, t: (b * tpb + t, 0))
    out_shape = (
        jax.ShapeDtypeStruct((bsz, nb, a, MOBA_BLOCK), BF16),
        jax.ShapeDtypeStruct((bsz, a, seq), F32),
        jax.ShapeDtypeStruct((m, a), BF16),
        jax.ShapeDtypeStruct((bsz, nb, 1, a), F32),
        jax.ShapeDtypeStruct((bsz, a, seq), F32),
        jax.ShapeDtypeStruct((bsz, nb, a, MOBA_BLOCK), BF16),
        jax.ShapeDtypeStruct((m, SSM_WIDTH), F32),
        jax.ShapeDtypeStruct((m, D_MODEL), BF16),
        jax.ShapeDtypeStruct((m, D_MODEL), BF16),
    )
    blk_t = pl.BlockSpec((1, sub, a, MOBA_BLOCK), lambda b, t: (b, t, 0, 0))
    lane_t = pl.BlockSpec((1, a, tm), lambda b, t: (b, 0, t))
    out_specs = (blk_t, lane_t, tok(a),
                 pl.BlockSpec((1, sub, 1, a), lambda b, t: (b, t, 0, 0)),
                 lane_t, blk_t, tok(SSM_WIDTH), tok(D_MODEL), tok(D_MODEL))
    return pl.pallas_call(
        _inproj_prompt_kernel,
        grid=(bsz, tpb),
        in_specs=[tok(D_MODEL), full((1, D_MODEL)), full(w_bf.shape),
                  pl.BlockSpec((tm, LANES), lambda b, t: (t, 0)),
                  pl.BlockSpec((tm, LANES), lambda b, t: (t, 0))],
        out_specs=out_specs,
        out_shape=out_shape,
        compiler_params=_cparams(("parallel", "parallel")),
        name="inproj_prompt",
    )(x2d, g, w_bf, cos, sin)


def _inproj_sample_kernel(x_ref, g_ref, w_ref, cos_ref, sin_ref,
                          q_ref, k_ref, v_ref, u_ref, sga_ref, sgs_ref):
    reps = ATTN_WIDTH // LANES
    cos = jnp.tile(cos_ref[...], (1, reps))
    sin = jnp.tile(sin_ref[...], (1, reps))
    q, k, v, u, sga, sgs = _inproj_core(x_ref[...], g_ref[...], w_ref, cos, sin)
    q_ref[...] = q
    k_ref[...] = k
    v_ref[...] = v
    u_ref[...] = u
    sga_ref[...] = sga.astype(BF16)
    sgs_ref[...] = sgs.astype(BF16)


def _inproj_sample(x2d, g, w_bf, cos, sin):
    m = x2d.shape[0]
    a = ATTN_WIDTH
    full = lambda shape: pl.BlockSpec(shape, lambda i: (0,) * len(shape))
    shapes = [(m, a), (m, a), (m, a), (m, SSM_WIDTH), (m, D_MODEL), (m, D_MODEL)]
    dts = [F32, F32, F32, F32, BF16, BF16]
    return pl.pallas_call(
        _inproj_sample_kernel,
        grid=(1,),
        in_specs=[full(x2d.shape), full(g.shape), full(w_bf.shape), full(cos.shape), full(sin.shape)],
        out_specs=tuple(full(s) for s in shapes),
        out_shape=tuple(jax.ShapeDtypeStruct(s, d) for s, d in zip(shapes, dts)),
        compiler_params=_cparams(("arbitrary",)),
        name="inproj_sample",
    )(x2d, g, w_bf, cos, sin)


def _moba_prompt_kernel(qT_ref, k_ref, kmean_ref, vT_ref, o_ref, bias_ref):
    nb = qT_ref.shape[1]
    blk = MOBA_BLOCK
    nh = MOBA_HEADS_PER_STEP
    row2 = lax.broadcasted_iota(jnp.int32, (nh * HEAD_DIM, 1), 0)
    lane_km = lax.broadcasted_iota(jnp.int32, (1, nh * HEAD_DIM), 1)
    blk_row = lax.broadcasted_iota(jnp.int32, (nb, blk), 0)
    key_i = lax.broadcasted_iota(jnp.int32, (blk, blk), 0)
    qry_i = lax.broadcasted_iota(jnp.int32, (blk, blk), 1)
    causal = key_i <= qry_i
    in_head = [(row2 >= hh * HEAD_DIM) & (row2 < (hh + 1) * HEAD_DIM) for hh in range(nh)]
    hrows = [slice(hh * HEAD_DIM, (hh + 1) * HEAD_DIM) for hh in range(nh)]

    km_all = jnp.concatenate(
        [jnp.where((lane_km >= hh * HEAD_DIM) & (lane_km < (hh + 1) * HEAD_DIM), kmean_ref[0], 0.0)
         for hh in range(nh)], axis=0)
    for i in range(1, nb):
        sb_all = jnp.dot(km_all, qT_ref[0, i].astype(F32), preferred_element_type=F32,
                         precision=lax.Precision.HIGHEST)
        for hh in range(nh):
            sb = jnp.where(blk_row < i, sb_all[hh * nb:(hh + 1) * nb], -jnp.inf)
            sel = jnp.zeros((nb, blk), dtype=jnp.bool_)
            for _r in range(min(MOBA_TOPK, i)):
                mx = jnp.max(sb, axis=0, keepdims=True)
                first = jnp.min(jnp.where(sb == mx, blk_row, nb), axis=0, keepdims=True)
                pick = (blk_row == first) & (mx > -jnp.inf)
                sel = sel | pick
                sb = jnp.where(pick, -jnp.inf, sb)
            bias_ref[hh, i] = jnp.where(sel, 0.0, NEG_BIG)

    def q_block(i, _):
        q_pair = qT_ref[0, i]
        k_own = k_ref[pl.ds(pl.multiple_of(i * blk, blk), blk), :]
        qz = [jnp.where(in_head[hh], q_pair, jnp.zeros_like(q_pair)) for hh in range(nh)]
        s_own = [_dot(k_own, qz[hh]) for hh in range(nh)]
        mid0 = []
        for hh in range(nh):
            s = jnp.where(causal, s_own[hh], NEG_BIG)
            m0 = jnp.max(s, axis=0, keepdims=True)
            p = jnp.exp(s - m0)
            mid0.append((m0, jnp.sum(p, axis=0, keepdims=True), p.astype(BF16)))
        init = []
        for hh in range(nh):
            m0, l0, p = mid0[hh]
            init += [m0, l0, _dot(vT_ref[0, i, hrows[hh], :], p)]

        def kv_pair(jp, carry):
            j0 = 2 * jp
            kk = k_ref[pl.ds(pl.multiple_of(j0 * blk, blk), 2 * blk), :]
            s_all = [_dot(kk, qz[hh]) for hh in range(nh)]
            mid = []
            for hh in range(nh):
                m, l, _ = carry[3 * hh:3 * hh + 3]
                ba = bias_ref[hh, i, pl.ds(j0, 1), :]
                bb = bias_ref[hh, i, pl.ds(j0 + 1, 1), :]
                sa, sb_ = s_all[hh][:blk], s_all[hh][blk:]
                m_new = jnp.maximum(m, jnp.maximum(jnp.max(sa, axis=0, keepdims=True) + ba,
                                                   jnp.max(sb_, axis=0, keepdims=True) + bb))
                alpha = jnp.exp(m - m_new)
                pa = jnp.exp(sa - (m_new - ba))
                pb = jnp.exp(sb_ - (m_new - bb))
                l = alpha * l + jnp.sum(pa, axis=0, keepdims=True) + jnp.sum(pb, axis=0, keepdims=True)
                mid.append((m_new, l, alpha, jnp.concatenate([pa, pb], axis=0).astype(BF16)))
            out = []
            for hh in range(nh):
                m_new, l, alpha, pp = mid[hh]
                vv = jnp.concatenate([vT_ref[0, j0, hrows[hh], :], vT_ref[0, j0 + 1, hrows[hh], :]], axis=1)
                out += [m_new, l, alpha * carry[3 * hh + 2] + _dot(vv, pp)]
            return tuple(out)

        fin = lax.fori_loop(0, (i + 1) // 2, kv_pair, tuple(init))
        oT = jnp.concatenate([fin[3 * hh + 2] / fin[3 * hh + 1] for hh in range(nh)], axis=0)
        o_ref[pl.ds(pl.multiple_of(i * blk, blk), blk), :] = oT.T.astype(o_ref.dtype)
        return 0

    lax.fori_loop(0, nb, q_block, 0)


def _moba_prompt(qT, kbf, kmean, vTb, bsz, seq):
    nb = seq // MOBA_BLOCK
    hp = N_HEADS // MOBA_HEADS_PER_STEP
    pair = MOBA_HEADS_PER_STEP * HEAD_DIM
    return pl.pallas_call(
        _moba_prompt_kernel,
        grid=(bsz, hp),
        in_specs=[pl.BlockSpec((1, nb, pair, MOBA_BLOCK), lambda b, h: (b, 0, h, 0)),
                  pl.BlockSpec((seq, pair), lambda b, h: (b, h)),
                  pl.BlockSpec((1, nb, pair), lambda b, h: (b, 0, h)),
                  pl.BlockSpec((1, nb, pair, MOBA_BLOCK), lambda b, h: (b, 0, h, 0))],
        out_specs=pl.BlockSpec((seq, pair), lambda b, h: (b, h)),
        out_shape=jax.ShapeDtypeStruct((bsz * seq, ATTN_WIDTH), BF16),
        scratch_shapes=[pltpu.VMEM((MOBA_HEADS_PER_STEP, nb, nb, MOBA_BLOCK), F32)],
        compiler_params=_cparams(("parallel", "parallel")),
        name="moba_prompt",
    )(qT, kbf, kmean, vTb)


def _moba_paged_kernel(pt_ref, qcol_ref, kn_ref, vn_ref, ck_ref, cv_ref, o_ref,
                       kbuf, vbuf, s_ref, p_ref, ksem, vsem):
    b = pl.program_id(0)
    nseq = pl.num_programs(0)
    n_pages = kbuf.shape[1]
    nblk = n_pages // 2
    slot = b % 2

    def k_copy(seq_i, sl, p):
        return pltpu.make_async_copy(ck_ref.at[pt_ref[seq_i, p]], kbuf.at[sl, p], ksem.at[sl])

    def start_k(seq_i, sl):
        def body(p, _):
            k_copy(seq_i, sl, p).start()
            return 0
        lax.fori_loop(0, n_pages, body, 0)

    @pl.when(b == 0)
    def _():
        start_k(0, 0)

    def wait_body(p, _):
        k_copy(b, slot, p).wait()
        return 0
    lax.fori_loop(0, n_pages, wait_body, 0)

    @pl.when(b + 1 < nseq)
    def _():
        start_k(b + 1, 1 - slot)

    qcol = qcol_ref[0]
    own = qcol * kn_ref[0]
    blk_i = lax.broadcasted_iota(jnp.int32, (nblk, 1), 0)

    for h in range(N_HEADS):
        qb = jnp.broadcast_to(qcol[h * HEAD_DIM:(h + 1) * HEAD_DIM, :], (HEAD_DIM, PAGE_SIZE))

        def score_body(n, _):
            for par in range(2):
                kt = kbuf[slot, 2 * n + par, h]
                s_ref[par, h, pl.ds(n, 1), :] = jnp.sum(kt * qb, axis=0, keepdims=True)
            return 0
        lax.fori_loop(0, nblk, score_body, 0)

    v_copies = []
    stats = []
    for h in range(N_HEADS):
        s0 = s_ref[0, h]
        s1 = s_ref[1, h]
        bs = jnp.sum(s0 + s1, axis=1, keepdims=True)
        sel = jnp.zeros((nblk, 1), dtype=jnp.bool_)
        for r in range(MOBA_TOPK):
            mx = jnp.max(bs, axis=0, keepdims=True)
            first = jnp.min(jnp.where(bs == mx, blk_i, nblk), axis=0, keepdims=True)
            pick = (blk_i == first) & (mx > -jnp.inf)
            sel = sel | pick
            bs = jnp.where(pick, -jnp.inf, bs)
            blk_id = jnp.max(jnp.where(pick, blk_i, 0))
            for par in range(2):
                cp = pltpu.make_async_copy(cv_ref.at[pt_ref[b, 2 * blk_id + par], h],
                                           vbuf.at[h, r, par], vsem.at[0])
                cp.start()
                v_copies.append((cp, h, r, par, blk_id))
        s_own = jnp.sum(own[h * HEAD_DIM:(h + 1) * HEAD_DIM, :], axis=0, keepdims=True)
        sm0 = jnp.where(sel, s0, NEG_BIG)
        sm1 = jnp.where(sel, s1, NEG_BIG)
        mx = jnp.maximum(jnp.max(jnp.max(jnp.maximum(sm0, sm1), axis=1, keepdims=True),
                                 axis=0, keepdims=True), s_own)
        p0 = jnp.exp(sm0 - mx)
        p1 = jnp.exp(sm1 - mx)
        p_own = jnp.exp(s_own - mx)
        l = jnp.sum(jnp.sum(p0 + p1, axis=1, keepdims=True), axis=0, keepdims=True) + p_own
        p_ref[0, h] = p0
        p_ref[1, h] = p1
        stats.append((p_own, l))

    for cp, _h, _r, _par, _blk in v_copies:
        cp.wait()

    vn = vn_ref[0]
    for h in range(N_HEADS):
        p_own, l = stats[h]
        acc = jnp.zeros((HEAD_DIM, PAGE_SIZE), F32)
        for cp, hh, r, par, blk_id in v_copies:
            if hh != h:
                continue
            prow = p_ref[par, h, pl.ds(blk_id, 1), :]
            acc = acc + vbuf[h, r, par] * prow
        o_h = jnp.sum(acc, axis=1, keepdims=True) + p_own * vn[h * HEAD_DIM:(h + 1) * HEAD_DIM, :]
        o_ref[0, h * HEAD_DIM:(h + 1) * HEAD_DIM, :] = o_h / l


def _moba_paged(page_table, qcol, kncol, vncol, cache_kT, cache_vT):
    nseq, n_pages = page_table.shape
    col = pl.BlockSpec((1, ATTN_WIDTH, 1), lambda b, pt: (b, 0, 0))
    any_spec = pl.BlockSpec(memory_space=pl.ANY)
    grid_spec = pltpu.PrefetchScalarGridSpec(
        num_scalar_prefetch=1,
        grid=(nseq,),
        in_specs=[col, col, col, any_spec, any_spec],
        out_specs=col,
        scratch_shapes=[
            pltpu.VMEM((2, n_pages, N_HEADS, HEAD_DIM, PAGE_SIZE), F32),
            pltpu.VMEM((N_HEADS, MOBA_TOPK, 2, HEAD_DIM, PAGE_SIZE), F32),
            pltpu.VMEM((2, N_HEADS, n_pages // 2, PAGE_SIZE), F32),
            pltpu.VMEM((2, N_HEADS, n_pages // 2, PAGE_SIZE), F32),
            pltpu.SemaphoreType.DMA((2,)),
            pltpu.SemaphoreType.DMA((1,)),
        ])
    return pl.pallas_call(
        _moba_paged_kernel,
        grid_spec=grid_spec,
        out_shape=jax.ShapeDtypeStruct((nseq, ATTN_WIDTH, 1), F32),
        compiler_params=_cparams(("arbitrary",)),
        name="moba_paged",
    )(page_table, qcol, kncol, vncol, cache_kT, cache_vT)


def _ssm_weights(a_re, a_im, log_dt, b_re, b_im, c_re, c_im, chunk, n_pow):
    t_ = chunk
    lam = lax.complex(a_re.astype(F32), a_im.astype(F32))
    dt = jnp.exp(log_dt.astype(F32))[:, None]
    ldt = lam * dt
    a_bar = jnp.exp(ldt)
    b_bar = ((a_bar - 1.0) / lam)[..., None] * lax.complex(b_re.astype(F32), b_im.astype(F32))
    c_c = lax.complex(c_re.astype(F32), c_im.astype(F32))
    taus = jnp.arange(t_ + 1, dtype=F32).astype(jnp.complex64)
    apow = jnp.exp(ldt[None] * taus[:, None, None])
    eye = jnp.eye(GROUPS_PER_SG, dtype=F32)
    sg = lambda x, axis: x.reshape(x.shape[:axis] + (N_SG, GROUPS_PER_SG) + x.shape[axis + 1:])

    ktoep = jnp.einsum('gcp,tgp,gpd->tgcd', c_c, apow[:t_], b_bar).real
    s_i = jnp.arange(t_)[:, None]
    t_i = jnp.arange(t_)[None, :]
    kst = jnp.where((s_i <= t_i)[:, :, None, None, None],
                    ktoep[jnp.clip(t_i - s_i, 0, t_ - 1)], 0.0)
    w_toep = jnp.einsum('stgjcd,jk->gsjdtkc', sg(kst, 2), eye)
    w_toep = w_toep.reshape(N_SG, t_ * LANES, t_ * LANES)

    mst = apow[:t_][::-1][..., None] * b_bar[None]
    mst = jnp.stack([mst.real, mst.imag], axis=-1)
    w_state = jnp.einsum('sgjpdr,jk->gsjdrkp', sg(mst, 1), eye)
    w_state = w_state.reshape(N_SG, t_ * LANES, 2 * SG_STATE)

    nout = c_c[None] * apow[1:][:, :, None, :]
    nout = jnp.stack([nout.real, -nout.imag], axis=-1)
    w_out = jnp.einsum('tgjcpr,jk->grjptkc', sg(nout, 1), eye)
    w_out = w_out.reshape(N_SG, 2 * SG_STATE, t_ * LANES)

    rs = jnp.arange(1, n_pow + 1, dtype=F32).astype(jnp.complex64)
    ap = jnp.exp((ldt * t_)[None] * rs[:, None, None])
    ap = jnp.stack([ap.real, ap.imag], axis=0)
    a_pow = sg(ap, 2).transpose(2, 1, 0, 3, 4).reshape(N_SG, n_pow, 2 * SG_STATE)
    return w_toep, w_state, w_out, a_pow


def _ssm_prompt_kernel(u_ref, wt_ref, ws_ref, wo_ref, ap_ref, d_ref, h0_ref,
                       y_ref, hf_ref, uc_ref, x_ref, hin_ref):
    t_ = SSM_CHUNK
    nc = uc_ref.shape[0]
    ns = SG_STATE
    for t in range(t_):
        uc_ref[:, t * LANES:(t + 1) * LANES] = u_ref[pl.ds(t, nc, stride=t_), :].astype(BF16)
    uc = uc_ref[...]
    x_ref[...] = _dot(uc, ws_ref[0])

    ap = ap_ref[0]
    apr, api = ap[:, :ns], ap[:, ns:]
    row = lax.broadcasted_iota(jnp.int32, (SUBLANES, 1), 0)

    def tile_body(k, carry):
        hr, hi = carry
        rows = pl.ds(pl.multiple_of(k * SUBLANES, SUBLANES), SUBLANES)
        xr = x_ref[rows, 0:ns]
        xi = x_ref[rows, ns:2 * ns]
        for d in (1, 2, 4):
            ar, ai = apr[d - 1:d], api[d - 1:d]
            sr = jnp.where(row >= d, pltpu.roll(xr, d, 0), 0.0)
            si = jnp.where(row >= d, pltpu.roll(xi, d, 0), 0.0)
            xr, xi = xr + ar * sr - ai * si, xi + ar * si + ai * sr
        outr = xr + apr * hr - api * hi
        outi = xi + apr * hi + api * hr
        hin_ref[rows, 0:ns] = jnp.where(row >= 1, pltpu.roll(outr, 1, 0), hr)
        hin_ref[rows, ns:2 * ns] = jnp.where(row >= 1, pltpu.roll(outi, 1, 0), hi)
        return outr[SUBLANES - 1:SUBLANES], outi[SUBLANES - 1:SUBLANES]

    h0 = h0_ref[0, 0]
    hr, hi = lax.fori_loop(0, nc // SUBLANES, tile_body, (h0[:, :ns], h0[:, ns:]))
    hf_ref[0, 0] = jnp.concatenate([hr, hi], axis=1)

    y = _dot(uc, wt_ref[0]) + _dot(hin_ref[...].astype(BF16), wo_ref[0])
    dv = d_ref[...]
    for t in range(t_):
        rows = pl.ds(t, nc, stride=t_)
        y_ref[rows, :] = y[:, t * LANES:(t + 1) * LANES] + dv * u_ref[rows, :]


def _ssm_prompt(u, w_toep, w_state, w_out, a_pow, d_row, h0, bsz, seq):
    t_ = SSM_CHUNK
    nc = seq // t_
    kw = t_ * LANES
    wspec = lambda shape: pl.BlockSpec((1,) + shape, lambda s, b: (s, 0, 0))
    return pl.pallas_call(
        _ssm_prompt_kernel,
        grid=(N_SG, bsz),
        in_specs=[pl.BlockSpec((seq, LANES), lambda s, b: (b, s)),
                  wspec((kw, kw)), wspec((kw, 2 * SG_STATE)), wspec((2 * SG_STATE, kw)),
                  wspec((SUBLANES, 2 * SG_STATE)),
                  pl.BlockSpec((1, LANES), lambda s, b: (0, s)),
                  pl.BlockSpec((1, 1, 1, 2 * SG_STATE), lambda s, b: (b, s, 0, 0))],
        out_specs=(pl.BlockSpec((seq, LANES), lambda s, b: (b, s)),
                   pl.BlockSpec((1, 1, 1, 2 * SG_STATE), lambda s, b: (b, s, 0, 0))),
        out_shape=(jax.ShapeDtypeStruct((bsz * seq, SSM_WIDTH), F32),
                   jax.ShapeDtypeStruct((bsz, N_SG, 1, 2 * SG_STATE), F32)),
        scratch_shapes=[pltpu.VMEM((nc, kw), BF16),
                        pltpu.VMEM((nc, 2 * SG_STATE), F32),
                        pltpu.VMEM((nc, 2 * SG_STATE), F32)],
        compiler_params=_cparams(("parallel", "parallel")),
        name="ssm_prompt",
    )(u, w_toep, w_state, w_out, a_pow, d_row, h0)


def _ssm_step_kernel(u_ref, wt_ref, ws_ref, wo_ref, ap_ref, d_ref, h0_ref, y_ref, hf_ref):
    ns = SG_STATE
    hp = lax.Precision.HIGHEST
    for s in range(N_SG):
        us = u_ref[:, s * LANES:(s + 1) * LANES]
        h0 = h0_ref[s]
        x = jnp.dot(us, ws_ref[s], preferred_element_type=F32, precision=hp)
        ar, ai = ap_ref[s, 0:1, :ns], ap_ref[s, 0:1, ns:]
        hr, hi = h0[:, :ns], h0[:, ns:]
        hf_ref[s] = jnp.concatenate([x[:, :ns] + ar * hr - ai * hi,
                                     x[:, ns:] + ar * hi + ai * hr], axis=1)
        y = (jnp.dot(us, wt_ref[s], preferred_element_type=F32, precision=hp)
             + jnp.dot(h0, wo_ref[s], preferred_element_type=F32, precision=hp))
        y_ref[:, s * LANES:(s + 1) * LANES] = y + d_ref[:, s * LANES:(s + 1) * LANES] * us


def _ssm_step(u, w_toep, w_state, w_out, a_pow, d_row, h0):
    m = u.shape[0]
    full = lambda a: pl.BlockSpec(a.shape, lambda i: (0,) * a.ndim)
    args = (u, w_toep, w_state, w_out, a_pow, d_row, h0)
    shapes = [(m, SSM_WIDTH), (N_SG, m, 2 * SG_STATE)]
    return pl.pallas_call(
        _ssm_step_kernel,
        grid=(1,),
        in_specs=[full(a) for a in args],
        out_specs=tuple(pl.BlockSpec(s, lambda i, n=len(s): (0,) * n) for s in shapes),
        out_shape=tuple(jax.ShapeDtypeStruct(s, F32) for s in shapes),
        compiler_params=_cparams(("arbitrary",)),
        name="ssm_step",
    )(*args)


FFN_SPLIT = 2
FFN_CHUNK = FFN_HIDDEN // FFN_SPLIT


def _post_kernel(x_ref, o_ref, y_ref, sga_ref, sgs_ref, wap_ref, wglu_ref, bglu_ref, wsp_ref,
                 wout_ref, nffn_ref, wfi_ref, wfo_ref, nfin_ref, out_ref):
    z = jax.nn.gelu(y_ref[...])
    z = z * jax.nn.sigmoid(_dot(z.astype(BF16), wglu_ref[...]) + bglu_ref[...])
    ssm_out = _dot(z.astype(BF16), wsp_ref[...])
    attn_out = _dot(o_ref[...].astype(BF16), wap_ref[...])
    merged = sga_ref[...].astype(F32) * attn_out + sgs_ref[...].astype(F32) * ssm_out
    x1 = x_ref[...] + _dot(merged.astype(BF16), wout_ref[...])
    hf = _rms(x1, nffn_ref[...]).astype(BF16)
    acc = x1
    for c in range(FFN_SPLIT):
        lo = c * FFN_CHUNK
        a = _dot(hf, wfi_ref[:, lo:lo + FFN_CHUNK])
        g = _dot(hf, wfi_ref[:, FFN_HIDDEN + lo:FFN_HIDDEN + lo + FFN_CHUNK])
        act = (jax.nn.silu(a) * g).astype(BF16)
        acc = acc + _dot(act, wfo_ref[lo:lo + FFN_CHUNK, :])
    out_ref[...] = _rms(acc, nfin_ref[...])


def _post(x2d, o, y, sga, sgs, wap, wglu, bglu, wsp, wout, nffn, wfi, wfo, nfin, tm):
    m = x2d.shape[0]
    tok = lambda w: pl.BlockSpec((tm, w), lambda i: (i, 0))
    const = lambda a: pl.BlockSpec(a.shape, lambda i: (0,) * a.ndim, pipeline_mode=pl.Buffered(1))
    weights = (wap, wglu, bglu, wsp, wout, nffn, wfi, wfo, nfin)
    return pl.pallas_call(
        _post_kernel,
        grid=(m // tm,),
        in_specs=[tok(D_MODEL), tok(ATTN_WIDTH), tok(SSM_WIDTH), tok(D_MODEL), tok(D_MODEL)]
                 + [const(w) for w in weights],
        out_specs=tok(D_MODEL),
        out_shape=jax.ShapeDtypeStruct((m, D_MODEL), F32),
        compiler_params=_cparams(("parallel",)),
        name="post",
    )(x2d, o, y, sga, sgs, *weights)


def _rope_tables(pos):
    half = HEAD_DIM // 2
    inv = jnp.power(jnp.float32(ROPE_THETA), -2.0 * jnp.arange(half, dtype=F32) / HEAD_DIM)
    ang = pos.astype(F32)[:, None] * inv[None, :]
    cos, sin = jnp.cos(ang), jnp.sin(ang)
    reps = LANES // HEAD_DIM
    return (jnp.tile(jnp.concatenate([cos, cos], axis=1), (1, reps)),
            jnp.tile(jnp.concatenate([-sin, sin], axis=1), (1, reps)))


def _leaf_from_T(xT, bsz, seq):
    return xT.reshape(bsz, N_HEADS, HEAD_DIM, seq).transpose(0, 3, 1, 2)[None]


def _state_in(re, im):
    n = re.shape[0]
    h = jnp.concatenate([re.reshape(n, N_SG, SG_STATE), im.reshape(n, N_SG, SG_STATE)], axis=-1)
    return h.transpose(1, 0, 2)


def _state_out(h):
    n = h.shape[0]
    return (h[..., :SG_STATE].reshape(1, n, SSM_GROUPS, SSM_STATE),
            h[..., SG_STATE:].reshape(1, n, SSM_GROUPS, SSM_STATE))


def kernel(x_prompt, x_sample, cache_k, cache_v, state_ssm_re, state_ssm_im, page_table, norm_mix, w_in,
           w_attn_proj, ssm_a_re, ssm_a_im, ssm_log_dt, ssm_b_re, ssm_b_im, ssm_c_re, ssm_c_im, ssm_d, w_glu,
           b_glu, w_ssm_proj, w_out, norm_ffn, w_ffn_in, w_ffn_out, norm_final):
    assert w_in.shape[0] == 1, "single layer"
    bsz, seq = x_prompt.shape[:2]
    nseq = x_sample.shape[0]
    past_len = page_table.shape[1] * PAGE_SIZE
    assert seq % MOBA_BLOCK == 0 and past_len % MOBA_BLOCK == 0 and x_sample.shape[1] == 1

    w_in_bf = w_in[0].astype(BF16)
    post_w = (w_attn_proj[0].astype(BF16), w_glu[0].astype(BF16), b_glu, w_ssm_proj[0].astype(BF16),
              w_out[0].astype(BF16), norm_ffn, w_ffn_in[0].astype(BF16), w_ffn_out[0].astype(BF16),
              norm_final[None])
    ssm_p = (ssm_a_re[0], ssm_a_im[0], ssm_log_dt[0], ssm_b_re[0], ssm_b_im[0], ssm_c_re[0], ssm_c_im[0])

    cos_p, sin_p = _rope_tables(jnp.arange(seq, dtype=jnp.int32))
    xp2 = x_prompt.reshape(bsz * seq, D_MODEL)
    qT, kT, kbf, kmean, vT, vTb, u_p, sga_p, sgs_p = _inproj_prompt(
        xp2, norm_mix, w_in_bf, cos_p, sin_p, bsz, seq, tm=512)
    o_p = _moba_prompt(qT, kbf, kmean.reshape(bsz, seq // MOBA_BLOCK, ATTN_WIDTH), vTb, bsz, seq)
    wt, ws, wo, ap = _ssm_weights(*ssm_p, chunk=SSM_CHUNK, n_pow=SUBLANES)
    h0_p = jnp.zeros((bsz, N_SG, 1, 2 * SG_STATE), F32)
    y_p, hf_p = _ssm_prompt(u_p, wt.astype(BF16), ws.astype(BF16), wo.astype(BF16), ap, ssm_d, h0_p, bsz, seq)
    y_prompt = _post(xp2, o_p, y_p, sga_p, sgs_p, *post_w, tm=512).reshape(bsz, seq, D_MODEL)
    new_ssm_re_p, new_ssm_im_p = _state_out(hf_p.reshape(bsz, N_SG, 2 * SG_STATE))

    cos_s, sin_s = _rope_tables(jnp.full((1,), past_len, dtype=jnp.int32))
    xs2 = x_sample.reshape(nseq, D_MODEL)
    q_s, k_s, v_s, u_s, sga_s, sgs_s = _inproj_sample(xs2, norm_mix, w_in_bf, cos_s, sin_s)
    cache_kT = cache_k[0].transpose(0, 2, 3, 1)
    cache_vT = cache_v[0].transpose(0, 2, 3, 1)
    o_s = _moba_paged(page_table, q_s[:, :, None], k_s[:, :, None], v_s[:, :, None], cache_kT, cache_vT)
    wt1, ws1, wo1, ap1 = _ssm_weights(*ssm_p, chunk=1, n_pow=1)
    h0_s = _state_in(state_ssm_re[0], state_ssm_im[0])
    y_s, hf_s = _ssm_step(u_s, wt1, ws1, wo1, ap1, ssm_d, h0_s)
    y_sample = _post(xs2, o_s.reshape(nseq, ATTN_WIDTH), y_s, sga_s, sgs_s, *post_w, tm=nseq)
    new_ssm_re_s, new_ssm_im_s = _state_out(hf_s.transpose(1, 0, 2))

    return (y_prompt, y_sample.reshape(nseq, 1, D_MODEL),
            _leaf_from_T(kT, bsz, seq), _leaf_from_T(vT, bsz, seq), new_ssm_re_p, new_ssm_im_p,
            k_s.reshape(1, nseq, 1, N_HEADS, HEAD_DIM), v_s.reshape(1, nseq, 1, N_HEADS, HEAD_DIM),
            new_ssm_re_s, new_ssm_im_s)
```

```python
import functools
import math

import jax
import jax.numpy as jnp
from jax import lax
from jax.experimental import pallas as pl
from jax.experimental.pallas import tpu as pltpu

F32 = jnp.float32
BF16 = jnp.bfloat16

D_MODEL = 1024
N_HEADS = 8
HEAD_DIM = 64
ATTN_WIDTH = N_HEADS * HEAD_DIM
MOBA_BLOCK = 256
MOBA_TOPK = 3
ROPE_THETA = 10000.0
SSM_WIDTH = 512
SSM_GROUP = 16
SSM_GROUPS = 32
SSM_STATE = 64
FFN_HIDDEN = 2816
RMS_EPS = 1e-6
PAGE_SIZE = 128

LANES = 128
SUBLANES = 8
GROUPS_PER_SG = LANES // SSM_GROUP
N_SG = SSM_GROUPS // GROUPS_PER_SG
SG_STATE = GROUPS_PER_SG * SSM_STATE
SSM_CHUNK = 8
NEG_BIG = -1e30
MOBA_HEADS_PER_STEP = 4
MOBA_LOOKAHEAD = 2
LOG2E = math.log2(math.e)
VMEM_LIMIT = 56 * 1024 * 1024


def _cparams(sem):
    return pltpu.CompilerParams(dimension_semantics=sem, vmem_limit_bytes=VMEM_LIMIT)


def _dot(a, b):
    return jnp.dot(a, b, preferred_element_type=F32)


def _rms(x, g):
    return x * lax.rsqrt(jnp.mean(x * x, axis=-1, keepdims=True) + RMS_EPS) * g


def _inproj_core(x, g, w_ref, cos, sin, q_scale):
    h = _rms(x, g).astype(BF16)
    lane = lax.broadcasted_iota(jnp.int32, (1, ATTN_WIDTH), 1)
    first_half = (lane % HEAD_DIM) < (HEAD_DIM // 2)

    def rot(t):
        partner = jnp.where(first_half,
                            pltpu.roll(t, ATTN_WIDTH - HEAD_DIM // 2, 1),
                            pltpu.roll(t, HEAD_DIM // 2, 1))
        return t * cos + partner * sin

    a = ATTN_WIDTH
    q = rot(_dot(h, w_ref[:, 0:a])) * q_scale
    k = rot(_dot(h, w_ref[:, a:2 * a]))
    v = _dot(h, w_ref[:, 2 * a:3 * a])
    u = _dot(h, w_ref[:, 3 * a:3 * a + SSM_WIDTH])
    o = 3 * a + SSM_WIDTH
    sga = jax.nn.sigmoid(_dot(h, w_ref[:, o:o + D_MODEL]))
    sgs = jax.nn.sigmoid(_dot(h, w_ref[:, o + D_MODEL:o + 2 * D_MODEL]))
    return q, k, v, u, sga, sgs


def _inproj_prompt_kernel(x_ref, g_ref, w_ref, cos_ref, sin_ref,
                          qT_ref, kT_ref, kbf_ref, kmean_ref, vT_ref, vTb_ref, u_ref, sga_ref, sgs_ref):
    reps = ATTN_WIDTH // LANES
    cos = jnp.tile(cos_ref[...], (1, reps))
    sin = jnp.tile(sin_ref[...], (1, reps))
    q, k, v, u, sga, sgs = _inproj_core(x_ref[...], g_ref[...], w_ref, cos, sin, LOG2E * HEAD_DIM ** -0.5)
    tm = q.shape[0]
    kT_ref[0] = k.T
    vT_ref[0] = v.T
    kbf_ref[...] = k.astype(BF16)
    for s in range(tm // MOBA_BLOCK):
        rows = slice(s * MOBA_BLOCK, (s + 1) * MOBA_BLOCK)
        qT_ref[0, s] = q[rows].T.astype(BF16)
        vTb_ref[0, s] = v[rows].T.astype(BF16)
        kmean_ref[0, s] = jnp.mean(k[rows], axis=0, keepdims=True)
    u_ref[...] = u
    sga_ref[...] = sga.astype(BF16)
    sgs_ref[...] = sgs.astype(BF16)


def _inproj_prompt(x2d, g, w_bf, cos, sin, bsz, seq, tm):
    m = bsz * seq
    nb = seq // MOBA_BLOCK
    tpb = seq // tm
    sub = tm // MOBA_BLOCK
    a = ATTN_WIDTH
    full = lambda shape: pl.BlockSpec(shape, lambda b, t: (0,) * len(shape))
    tok = lambda w: pl.BlockSpec((tm, w), lambda b, t: (b * tpb + t, 0))
    out_shape = (
        jax.ShapeDtypeStruct((bsz, nb, a, MOBA_BLOCK), BF16),
        jax.ShapeDtypeStruct((bsz, a, seq), F32),
        jax.ShapeDtypeStruct((m, a), BF16),
        jax.ShapeDtypeStruct((bsz, nb, 1, a), F32),
        jax.ShapeDtypeStruct((bsz, a, seq), F32),
        jax.ShapeDtypeStruct((bsz, nb, a, MOBA_BLOCK), BF16),
        jax.ShapeDtypeStruct((m, SSM_WIDTH), F32),
        jax.ShapeDtypeStruct((m, D_MODEL), BF16),
        jax.ShapeDtypeStruct((m, D_MODEL), BF16),
    )
    blk_t = pl.BlockSpec((1, sub, a, MOBA_BLOCK), lambda b, t: (b, t, 0, 0))
    lane_t = pl.BlockSpec((1, a, tm), lambda b, t: (b, 0, t))
    out_specs = (blk_t, lane_t, tok(a),
                 pl.BlockSpec((1, sub, 1, a), lambda b, t: (b, t, 0, 0)),
                 lane_t, blk_t, tok(SSM_WIDTH), tok(D_MODEL), tok(D_MODEL))
    return pl.pallas_call(
        _inproj_prompt_kernel,
        grid=(bsz, tpb),
        in_specs=[tok(D_MODEL), full((1, D_MODEL)), full(w_bf.shape),
                  pl.BlockSpec((tm, LANES), lambda b, t: (t, 0)),
                  pl.BlockSpec((tm, LANES), lambda b, t: (t, 0))],
        out_specs=out_specs,
        out_shape=out_shape,
        compiler_params=_cparams(("parallel", "parallel")),
        name="inproj_prompt",
    )(x2d, g, w_bf, cos, sin)


def _inproj_sample_kernel(x_ref, g_ref, w_ref, cos_ref, sin_ref,
                          q_ref, k_ref, v_ref, u_ref, sga_ref, sgs_ref):
    reps = ATTN_WIDTH // LANES
    cos = jnp.tile(cos_ref[...], (1, reps))
    sin = jnp.tile(sin_ref[...], (1, reps))
    q, k, v, u, sga, sgs = _inproj_core(x_ref[...], g_ref[...], w_ref, cos, sin, HEAD_DIM ** -0.5)
    q_ref[...] = q
    k_ref[...] = k
    v_ref[...] = v
    u_ref[...] = u
    sga_ref[...] = sga.astype(BF16)
    sgs_ref[...] = sgs.astype(BF16)


def _inproj_sample(x2d, g, w_bf, cos, sin):
    m = x2d.shape[0]
    a = ATTN_WIDTH
    full = lambda shape: pl.BlockSpec(shape, lambda i: (0,) * len(shape))
    shapes = [(m, a), (m, a), (m, a), (m, SSM_WIDTH), (m, D_MODEL), (m, D_MODEL)]
    dts = [F32, F32, F32, F32, BF16, BF16]
    return pl.pallas_call(
        _inproj_sample_kernel,
        grid=(1,),
        in_specs=[full(x2d.shape), full(g.shape), full(w_bf.shape), full(cos.shape), full(sin.shape)],
        out_specs=tuple(full(s) for s in shapes),
        out_shape=tuple(jax.ShapeDtypeStruct(s, d) for s, d in zip(shapes, dts)),
        compiler_params=_cparams(("arbitrary",)),
        name="inproj_sample",
    )(x2d, g, w_bf, cos, sin)


def _moba_prompt_kernel(qT_ref, k_ref, kmean_ref, vT_ref, o_ref, bias_ref, qz_s, m_s, l_s, acc_s):
    nb = qT_ref.shape[1]
    blk = MOBA_BLOCK
    nh = MOBA_HEADS_PER_STEP
    row2 = lax.broadcasted_iota(jnp.int32, (nh * HEAD_DIM, 1), 0)
    lane_km = lax.broadcasted_iota(jnp.int32, (1, nh * HEAD_DIM), 1)
    blk_row = lax.broadcasted_iota(jnp.int32, (nb, blk), 0)
    key_i = lax.broadcasted_iota(jnp.int32, (blk, blk), 0)
    qry_i = lax.broadcasted_iota(jnp.int32, (blk, blk), 1)
    causal = key_i <= qry_i
    in_head = [(row2 >= hh * HEAD_DIM) & (row2 < (hh + 1) * HEAD_DIM) for hh in range(nh)]
    hrows = [slice(hh * HEAD_DIM, (hh + 1) * HEAD_DIM) for hh in range(nh)]

    km_all = jnp.concatenate(
        [jnp.where((lane_km >= hh * HEAD_DIM) & (lane_km < (hh + 1) * HEAD_DIM), kmean_ref[0], 0.0)
         for hh in range(nh)], axis=0)
    for hh in range(nh):
        bias_ref[hh, 0] = jnp.full((nb, blk), NEG_BIG, F32)
    for i in range(1, nb):
        sb_all = jnp.dot(km_all, qT_ref[0, i].astype(F32), preferred_element_type=F32,
                         precision=lax.Precision.HIGHEST)
        for hh in range(nh):
            sb = jnp.where(blk_row < i, sb_all[hh * nb:(hh + 1) * nb], -jnp.inf)
            sel = jnp.zeros((nb, blk), dtype=jnp.bool_)
            for _r in range(min(MOBA_TOPK, i)):
                mx = jnp.max(sb, axis=0, keepdims=True)
                first = jnp.min(jnp.where(sb == mx, blk_row, nb), axis=0, keepdims=True)
                pick = (blk_row == first) & (mx > -jnp.inf)
                sel = sel | pick
                sb = jnp.where(pick, -jnp.inf, sb)
            bias_ref[hh, i] = jnp.where(sel, 0.0, NEG_BIG)

    n_items = nb // 2

    def couple(i, _):
        iq = (i, nb - 1 - i)
        n_first = (i + 1) // 2

        def diag_scores(x):
            q_pair = qT_ref[0, iq[x]]
            k_own = k_ref[pl.ds(pl.multiple_of(iq[x] * blk, blk), blk), :]
            out = []
            for hh in range(nh):
                qz = jnp.where(in_head[hh], q_pair, jnp.zeros_like(q_pair))
                qz_s[x, hh] = qz
                out.append(_dot(k_own, qz))
            return out

        def diag_absorb(x, s_own):
            for hh in range(nh):
                s = jnp.where(causal, s_own[hh], NEG_BIG)
                m0 = jnp.max(s, axis=0, keepdims=True)
                p = jnp.exp2(s - m0)
                m_s[x, hh] = m0
                l_s[x, hh] = jnp.sum(p, axis=0, keepdims=True)
                acc_s[x, hh] = _dot(vT_ref[0, iq[x], hrows[hh], :], p.astype(BF16))

        def item_params(k):
            first = k < n_first
            x = jnp.where(first, 0, 1)
            return x, jnp.where(first, iq[0], iq[1]), jnp.where(first, k, k - n_first)

        def item_scores(k):
            x, _, jp = item_params(k)
            kk = k_ref[pl.ds(pl.multiple_of(2 * jp * blk, blk), 2 * blk), :]
            out = []
            for hh in range(nh):
                s = _dot(kk, qz_s[x, hh])
                out.append((s, jnp.max(s[:blk], axis=0, keepdims=True), jnp.max(s[blk:], axis=0, keepdims=True)))
            return out

        def item_absorb(k, sc):
            x, qb, jp = item_params(k)
            j0 = 2 * jp
            for hh in range(nh):
                s, cma, cmb = sc[hh]
                ba = bias_ref[hh, qb, pl.ds(j0, 1), :]
                bb = bias_ref[hh, qb, pl.ds(j0 + 1, 1), :]
                m = m_s[x, hh]
                m_new = jnp.maximum(m, jnp.maximum(cma + ba, cmb + bb))
                alpha = jnp.exp2(m - m_new)
                pa = jnp.exp2(s[:blk] - (m_new - ba))
                pb = jnp.exp2(s[blk:] - (m_new - bb))
                m_s[x, hh] = m_new
                l_s[x, hh] = alpha * l_s[x, hh] + jnp.sum(pa, axis=0, keepdims=True) + jnp.sum(pb, axis=0, keepdims=True)
                pp = jnp.concatenate([pa, pb], axis=0).astype(BF16)
                vv = jnp.concatenate([vT_ref[0, j0, hrows[hh], :], vT_ref[0, j0 + 1, hrows[hh], :]], axis=1)
                acc_s[x, hh] = alpha * acc_s[x, hh] + _dot(vv, pp)

        s_diag = [diag_scores(0), diag_scores(1)]
        pending = [item_scores(k) for k in range(min(MOBA_LOOKAHEAD, n_items))]
        diag_absorb(0, s_diag[0])
        diag_absorb(1, s_diag[1])
        for k in range(n_items):
            if k + MOBA_LOOKAHEAD < n_items:
                pending.append(item_scores(k + MOBA_LOOKAHEAD))
            item_absorb(k, pending[k])
        for x in range(2):
            oT = jnp.concatenate([acc_s[x, hh] / l_s[x, hh] for hh in range(nh)], axis=0)
            o_ref[pl.ds(pl.multiple_of(iq[x] * blk, blk), blk), :] = oT.T.astype(o_ref.dtype)
        return 0

    lax.fori_loop(0, nb // 2, couple, 0)


def _moba_prompt(qT, kbf, kmean, vTb, bsz, seq):
    nb = seq // MOBA_BLOCK
    hp = N_HEADS // MOBA_HEADS_PER_STEP
    pair = MOBA_HEADS_PER_STEP * HEAD_DIM
    return pl.pallas_call(
        _moba_prompt_kernel,
        grid=(bsz, hp),
        in_specs=[pl.BlockSpec((1, nb, pair, MOBA_BLOCK), lambda b, h: (b, 0, h, 0)),
                  pl.BlockSpec((seq, pair), lambda b, h: (b, h)),
                  pl.BlockSpec((1, nb, pair), lambda b, h: (b, 0, h)),
                  pl.BlockSpec((1, nb, pair, MOBA_BLOCK), lambda b, h: (b, 0, h, 0))],
        out_specs=pl.BlockSpec((seq, pair), lambda b, h: (b, h)),
        out_shape=jax.ShapeDtypeStruct((bsz * seq, ATTN_WIDTH), BF16),
        scratch_shapes=[pltpu.VMEM((MOBA_HEADS_PER_STEP, nb, nb, MOBA_BLOCK), F32),
                        pltpu.VMEM((2, MOBA_HEADS_PER_STEP, pair, MOBA_BLOCK), BF16),
                        pltpu.VMEM((2, MOBA_HEADS_PER_STEP, 1, MOBA_BLOCK), F32),
                        pltpu.VMEM((2, MOBA_HEADS_PER_STEP, 1, MOBA_BLOCK), F32),
                        pltpu.VMEM((2, MOBA_HEADS_PER_STEP, HEAD_DIM, MOBA_BLOCK), F32)],
        compiler_params=_cparams(("parallel", "parallel")),
        name="moba_prompt",
    )(qT, kbf, kmean, vTb)


def _moba_paged_kernel(pt_ref, qcol_ref, kn_ref, vn_ref, ck_ref, cv_ref, o_ref,
                       kbuf, vbuf, s_ref, p_ref, ksem, vsem):
    b = pl.program_id(0)
    nseq = pl.num_programs(0)
    n_pages = kbuf.shape[1]
    nblk = n_pages // 2
    slot = b % 2

    def k_copy(seq_i, sl, p):
        return pltpu.make_async_copy(ck_ref.at[pt_ref[seq_i, p]], kbuf.at[sl, p], ksem.at[sl])

    def start_k(seq_i, sl):
        def body(p, _):
            k_copy(seq_i, sl, p).start()
            return 0
        lax.fori_loop(0, n_pages, body, 0)

    @pl.when(b == 0)
    def _():
        start_k(0, 0)

    def wait_body(p, _):
        k_copy(b, slot, p).wait()
        return 0
    lax.fori_loop(0, n_pages, wait_body, 0)

    @pl.when(b + 1 < nseq)
    def _():
        start_k(b + 1, 1 - slot)

    qcol = qcol_ref[0]
    own = qcol * kn_ref[0]
    blk_i = lax.broadcasted_iota(jnp.int32, (nblk, 1), 0)

    for h in range(N_HEADS):
        qb = jnp.broadcast_to(qcol[h * HEAD_DIM:(h + 1) * HEAD_DIM, :], (HEAD_DIM, PAGE_SIZE))

        def score_body(n, _):
            for par in range(2):
                kt = kbuf[slot, 2 * n + par, h]
                s_ref[par, h, pl.ds(n, 1), :] = jnp.sum(kt * qb, axis=0, keepdims=True)
            return 0
        lax.fori_loop(0, nblk, score_body, 0)

    v_copies = []
    stats = []
    for h in range(N_HEADS):
        s0 = s_ref[0, h]
        s1 = s_ref[1, h]
        bs = jnp.sum(s0 + s1, axis=1, keepdims=True)
        sel = jnp.zeros((nblk, 1), dtype=jnp.bool_)
        for r in range(MOBA_TOPK):
            mx = jnp.max(bs, axis=0, keepdims=True)
            first = jnp.min(jnp.where(bs == mx, blk_i, nblk), axis=0, keepdims=True)
            pick = (blk_i == first) & (mx > -jnp.inf)
            sel = sel | pick
            bs = jnp.where(pick, -jnp.inf, bs)
            blk_id = jnp.max(jnp.where(pick, blk_i, 0))
            for par in range(2):
                cp = pltpu.make_async_copy(cv_ref.at[pt_ref[b, 2 * blk_id + par], h],
                                           vbuf.at[h, r, par], vsem.at[0])
                cp.start()
                v_copies.append((cp, h, r, par, blk_id))
        s_own = jnp.sum(own[h * HEAD_DIM:(h + 1) * HEAD_DIM, :], axis=0, keepdims=True)
        sm0 = jnp.where(sel, s0, NEG_BIG)
        sm1 = jnp.where(sel, s1, NEG_BIG)
        mx = jnp.maximum(jnp.max(jnp.max(jnp.maximum(sm0, sm1), axis=1, keepdims=True),
                                 axis=0, keepdims=True), s_own)
        p0 = jnp.exp(sm0 - mx)
        p1 = jnp.exp(sm1 - mx)
        p_own = jnp.exp(s_own - mx)
        l = jnp.sum(jnp.sum(p0 + p1, axis=1, keepdims=True), axis=0, keepdims=True) + p_own
        p_ref[0, h] = p0
        p_ref[1, h] = p1
        stats.append((p_own, l))

    for cp, _h, _r, _par, _blk in v_copies:
        cp.wait()

    vn = vn_ref[0]
    for h in range(N_HEADS):
        p_own, l = stats[h]
        acc = jnp.zeros((HEAD_DIM, PAGE_SIZE), F32)
        for cp, hh, r, par, blk_id in v_copies:
            if hh != h:
                continue
            prow = p_ref[par, h, pl.ds(blk_id, 1), :]
            acc = acc + vbuf[h, r, par] * prow
        o_h = jnp.sum(acc, axis=1, keepdims=True) + p_own * vn[h * HEAD_DIM:(h + 1) * HEAD_DIM, :]
        o_ref[0, h * HEAD_DIM:(h + 1) * HEAD_DIM, :] = o_h / l


def _moba_paged(page_table, qcol, kncol, vncol, cache_kT, cache_vT):
    nseq, n_pages = page_table.shape
    col = pl.BlockSpec((1, ATTN_WIDTH, 1), lambda b, pt: (b, 0, 0))
    any_spec = pl.BlockSpec(memory_space=pl.ANY)
    grid_spec = pltpu.PrefetchScalarGridSpec(
        num_scalar_prefetch=1,
        grid=(nseq,),
        in_specs=[col, col, col, any_spec, any_spec],
        out_specs=col,
        scratch_shapes=[
            pltpu.VMEM((2, n_pages, N_HEADS, HEAD_DIM, PAGE_SIZE), F32),
            pltpu.VMEM((N_HEADS, MOBA_TOPK, 2, HEAD_DIM, PAGE_SIZE), F32),
            pltpu.VMEM((2, N_HEADS, n_pages // 2, PAGE_SIZE), F32),
            pltpu.VMEM((2, N_HEADS, n_pages // 2, PAGE_SIZE), F32),
            pltpu.SemaphoreType.DMA((2,)),
            pltpu.SemaphoreType.DMA((1,)),
        ])
    return pl.pallas_call(
        _moba_paged_kernel,
        grid_spec=grid_spec,
        out_shape=jax.ShapeDtypeStruct((nseq, ATTN_WIDTH, 1), F32),
        compiler_params=_cparams(("arbitrary",)),
        name="moba_paged",
    )(page_table, qcol, kncol, vncol, cache_kT, cache_vT)


def _ssm_weights(a_re, a_im, log_dt, b_re, b_im, c_re, c_im, chunk, n_pow):
    t_ = chunk
    lam = lax.complex(a_re.astype(F32), a_im.astype(F32))
    dt = jnp.exp(log_dt.astype(F32))[:, None]
    ldt = lam * dt
    a_bar = jnp.exp(ldt)
    b_bar = ((a_bar - 1.0) / lam)[..., None] * lax.complex(b_re.astype(F32), b_im.astype(F32))
    c_c = lax.complex(c_re.astype(F32), c_im.astype(F32))
    taus = jnp.arange(t_ + 1, dtype=F32).astype(jnp.complex64)
    apow = jnp.exp(ldt[None] * taus[:, None, None])
    eye = jnp.eye(GROUPS_PER_SG, dtype=F32)
    sg = lambda x, axis: x.reshape(x.shape[:axis] + (N_SG, GROUPS_PER_SG) + x.shape[axis + 1:])

    ktoep = jnp.einsum('gcp,tgp,gpd->tgcd', c_c, apow[:t_], b_bar).real
    s_i = jnp.arange(t_)[:, None]
    t_i = jnp.arange(t_)[None, :]
    kst = jnp.where((s_i <= t_i)[:, :, None, None, None],
                    ktoep[jnp.clip(t_i - s_i, 0, t_ - 1)], 0.0)
    w_toep = jnp.einsum('stgjcd,jk->gsjdtkc', sg(kst, 2), eye)
    w_toep = w_toep.reshape(N_SG, t_ * LANES, t_ * LANES)

    mst = apow[:t_][::-1][..., None] * b_bar[None]
    mst = jnp.stack([mst.real, mst.imag], axis=-1)
    w_state = jnp.einsum('sgjpdr,jk->gsjdrkp', sg(mst, 1), eye)
    w_state = w_state.reshape(N_SG, t_ * LANES, 2 * SG_STATE)

    nout = c_c[None] * apow[1:][:, :, None, :]
    nout = jnp.stack([nout.real, -nout.imag], axis=-1)
    w_out = jnp.einsum('tgjcpr,jk->grjptkc', sg(nout, 1), eye)
    w_out = w_out.reshape(N_SG, 2 * SG_STATE, t_ * LANES)

    rs = jnp.arange(1, n_pow + 1, dtype=F32).astype(jnp.complex64)
    ap = jnp.exp((ldt * t_)[None] * rs[:, None, None])
    ap = jnp.stack([ap.real, ap.imag], axis=0)
    a_pow = sg(ap, 2).transpose(2, 1, 0, 3, 4).reshape(N_SG, n_pow, 2 * SG_STATE)
    return w_toep, w_state, w_out, a_pow


def _ssm_tables(a_re, a_im, log_dt, b_re, b_im, c_re, c_im, chunk, n_pow):
    t_ = chunk
    lam = lax.complex(a_re.astype(F32), a_im.astype(F32))
    ldt = lam * jnp.exp(log_dt.astype(F32))[:, None]
    a_bar = jnp.exp(ldt)
    b_bar = ((a_bar - 1.0) / lam)[..., None] * lax.complex(b_re.astype(F32), b_im.astype(F32))
    c_c = lax.complex(c_re.astype(F32), c_im.astype(F32))
    taus = jnp.arange(t_ + 1, dtype=F32).astype(jnp.complex64)
    apow = jnp.exp(ldt[None] * taus[:, None, None])
    gq, ssg, c_, p_ = GROUPS_PER_SG, N_SG, SSM_GROUP, SSM_STATE
    c4 = c_c.reshape(ssg, gq, c_, p_)
    b4 = b_bar.reshape(ssg, gq, p_, c_)
    ap4 = apow.reshape(t_ + 1, ssg, gq, p_)

    kc = jnp.einsum('sjcp,tsjp,sjpd->sjdtc', c4, ap4[:t_], b4).real
    kc = kc.reshape(ssg, LANES, t_ * c_)
    mst = jnp.einsum('tsjp,sjpd->stjdp', ap4[:t_][::-1], b4)
    mc = jnp.concatenate([mst.real, mst.imag], axis=-1).reshape(ssg, t_ * LANES, 2 * p_)
    nout = jnp.einsum('sjcp,tsjp->sjptc', c4, ap4[1:])
    nc = jnp.concatenate([nout.real, -nout.imag], axis=1).reshape(ssg, 2 * SG_STATE, t_ * c_)
    rs = jnp.arange(1, n_pow + 1, dtype=F32).astype(jnp.complex64)
    ap = jnp.exp((ldt * t_)[None] * rs[:, None, None])
    ap = ap.reshape(n_pow, ssg, SG_STATE).transpose(1, 0, 2)
    a_pow = jnp.concatenate([ap.real, ap.imag], axis=-1)
    return kc, mc, nc, a_pow


def _expand_ssm_weights(kc_ref, mc_ref, nc_ref, wt_s, ws_s, wo_s):
    t_ = SSM_CHUNK
    kw = t_ * LANES
    lg = lambda n: n.bit_length() - 1
    r = lax.broadcasted_iota(jnp.int32, (LANES, kw), 0)
    q = lax.broadcasted_iota(jnp.int32, (LANES, kw), 1)
    sh_c, sh_p = lg(SSM_GROUP), lg(SSM_STATE)
    ecol = jnp.where(((r >> sh_c) == (q >> lg(LANES))) & ((r & (SSM_GROUP - 1)) == (q & (SSM_GROUP - 1))),
                     1.0, 0.0).astype(BF16)
    e2 = jnp.where(((r >> sh_p) == (q >> lg(SG_STATE))) & ((r & (SSM_STATE - 1)) == (q & (SSM_STATE - 1))),
                   1.0, 0.0).astype(BF16)
    col_c = (q & (LANES - 1)) >> sh_c
    taps = jnp.where((r >> sh_c) == col_c, _dot(kc_ref[0].astype(BF16), ecol), 0.0).astype(BF16)
    blank = jnp.zeros((LANES, LANES), BF16)
    for s in range(t_):
        for t in range(t_):
            wt_s[s * LANES:(s + 1) * LANES, t * LANES:(t + 1) * LANES] = (
                taps[:, (t - s) * LANES:(t - s + 1) * LANES] if t >= s else blank)
    rr = lax.broadcasted_iota(jnp.int32, (kw, kw), 0)
    qq = lax.broadcasted_iota(jnp.int32, (kw, kw), 1)
    row_c = (rr & (LANES - 1)) >> sh_c
    row_p = (rr & (SG_STATE - 1)) >> sh_p
    colq_c = (qq & (LANES - 1)) >> sh_c
    colq_p = (qq & (SG_STATE - 1)) >> sh_p
    ws_s[...] = jnp.where(row_c == colq_p, _dot(mc_ref[0].astype(BF16), e2), 0.0).astype(BF16)
    wo_s[...] = jnp.where(row_p == colq_c, _dot(nc_ref[0].astype(BF16), ecol), 0.0).astype(BF16)


def _ssm_prompt_kernel(u_ref, kc_ref, mc_ref, nc_ref, ap_ref, d_ref, h0_ref,
                       y_ref, hf_ref, uc_ref, x_ref, hin_ref, wt_s, ws_s, wo_s):
    t_ = SSM_CHUNK
    nc = uc_ref.shape[0]
    ns = SG_STATE

    @pl.when(pl.program_id(1) == 0)
    def _():
        _expand_ssm_weights(kc_ref, mc_ref, nc_ref, wt_s, ws_s, wo_s)

    for t in range(t_):
        uc_ref[:, t * LANES:(t + 1) * LANES] = u_ref[pl.ds(t, nc, stride=t_), :].astype(BF16)
    uc = uc_ref[...]
    x_ref[...] = _dot(uc, ws_s[...])

    ap = ap_ref[0]
    apr, api = ap[:, :ns], ap[:, ns:]
    row = lax.broadcasted_iota(jnp.int32, (SUBLANES, 1), 0)

    def tile_body(k, carry):
        hr, hi = carry
        rows = pl.ds(pl.multiple_of(k * SUBLANES, SUBLANES), SUBLANES)
        xr = x_ref[rows, 0:ns]
        xi = x_ref[rows, ns:2 * ns]
        for d in (1, 2, 4):
            ar, ai = apr[d - 1:d], api[d - 1:d]
            sr = jnp.where(row >= d, pltpu.roll(xr, d, 0), 0.0)
            si = jnp.where(row >= d, pltpu.roll(xi, d, 0), 0.0)
            xr, xi = xr + ar * sr - ai * si, xi + ar * si + ai * sr
        outr = xr + apr * hr - api * hi
        outi = xi + apr * hi + api * hr
        hin_ref[rows, 0:ns] = jnp.where(row >= 1, pltpu.roll(outr, 1, 0), hr)
        hin_ref[rows, ns:2 * ns] = jnp.where(row >= 1, pltpu.roll(outi, 1, 0), hi)
        return outr[SUBLANES - 1:SUBLANES], outi[SUBLANES - 1:SUBLANES]

    h0 = h0_ref[0, 0]
    hr, hi = lax.fori_loop(0, nc // SUBLANES, tile_body, (h0[:, :ns], h0[:, ns:]))
    hf_ref[0, 0] = jnp.concatenate([hr, hi], axis=1)

    y = _dot(uc, wt_s[...]) + _dot(hin_ref[...].astype(BF16), wo_s[...])
    dv = d_ref[...]
    for t in range(t_):
        rows = pl.ds(t, nc, stride=t_)
        y_ref[rows, :] = y[:, t * LANES:(t + 1) * LANES] + dv * u_ref[rows, :]


def _ssm_prompt(u, kc, mc, nc_tab, a_pow, d_row, h0, bsz, seq):
    t_ = SSM_CHUNK
    nc = seq // t_
    kw = t_ * LANES
    assert kc.shape[2] == LANES and 2 * SG_STATE == kw
    wspec = lambda shape: pl.BlockSpec((1,) + shape, lambda s, b: (s, 0, 0))
    return pl.pallas_call(
        _ssm_prompt_kernel,
        grid=(N_SG, bsz),
        in_specs=[pl.BlockSpec((seq, LANES), lambda s, b: (b, s)),
                  wspec(kc.shape[1:]), wspec(mc.shape[1:]), wspec(nc_tab.shape[1:]),
                  wspec((SUBLANES, 2 * SG_STATE)),
                  pl.BlockSpec((1, LANES), lambda s, b: (0, s)),
                  pl.BlockSpec((1, 1, 1, 2 * SG_STATE), lambda s, b: (b, s, 0, 0))],
        out_specs=(pl.BlockSpec((seq, LANES), lambda s, b: (b, s)),
                   pl.BlockSpec((1, 1, 1, 2 * SG_STATE), lambda s, b: (b, s, 0, 0))),
        out_shape=(jax.ShapeDtypeStruct((bsz * seq, SSM_WIDTH), F32),
                   jax.ShapeDtypeStruct((bsz, N_SG, 1, 2 * SG_STATE), F32)),
        scratch_shapes=[pltpu.VMEM((nc, kw), BF16),
                        pltpu.VMEM((nc, 2 * SG_STATE), F32),
                        pltpu.VMEM((nc, 2 * SG_STATE), F32),
                        pltpu.VMEM((kw, kw), BF16),
                        pltpu.VMEM((kw, 2 * SG_STATE), BF16),
                        pltpu.VMEM((2 * SG_STATE, kw), BF16)],
        compiler_params=_cparams(("arbitrary", "arbitrary")),
        name="ssm_prompt",
    )(u, kc, mc, nc_tab, a_pow, d_row, h0)


def _ssm_step_kernel(u_ref, wt_ref, ws_ref, wo_ref, ap_ref, d_ref, h0_ref, y_ref, hf_ref):
    ns = SG_STATE
    hp = lax.Precision.HIGHEST
    for s in range(N_SG):
        us = u_ref[:, s * LANES:(s + 1) * LANES]
        h0 = h0_ref[s]
        x = jnp.dot(us, ws_ref[s], preferred_element_type=F32, precision=hp)
        ar, ai = ap_ref[s, 0:1, :ns], ap_ref[s, 0:1, ns:]
        hr, hi = h0[:, :ns], h0[:, ns:]
        hf_ref[s] = jnp.concatenate([x[:, :ns] + ar * hr - ai * hi,
                                     x[:, ns:] + ar * hi + ai * hr], axis=1)
        y = (jnp.dot(us, wt_ref[s], preferred_element_type=F32, precision=hp)
             + jnp.dot(h0, wo_ref[s], preferred_element_type=F32, precision=hp))
        y_ref[:, s * LANES:(s + 1) * LANES] = y + d_ref[:, s * LANES:(s + 1) * LANES] * us


def _ssm_step(u, w_toep, w_state, w_out, a_pow, d_row, h0):
    m = u.shape[0]
    full = lambda a: pl.BlockSpec(a.shape, lambda i: (0,) * a.ndim)
    args = (u, w_toep, w_state, w_out, a_pow, d_row, h0)
    shapes = [(m, SSM_WIDTH), (N_SG, m, 2 * SG_STATE)]
    return pl.pallas_call(
        _ssm_step_kernel,
        grid=(1,),
        in_specs=[full(a) for a in args],
        out_specs=tuple(pl.BlockSpec(s, lambda i, n=len(s): (0,) * n) for s in shapes),
        out_shape=tuple(jax.ShapeDtypeStruct(s, F32) for s in shapes),
        compiler_params=_cparams(("arbitrary",)),
        name="ssm_step",
    )(*args)


FFN_SPLIT = 2
FFN_CHUNK = FFN_HIDDEN // FFN_SPLIT


def _post_kernel(x_ref, o_ref, y_ref, sga_ref, sgs_ref, wap_ref, wglu_ref, bglu_ref, wsp_ref,
                 wout_ref, nffn_ref, wfi_ref, wfo_ref, nfin_ref, out_ref):
    z = jax.nn.gelu(y_ref[...])
    z = z * jax.nn.sigmoid(_dot(z.astype(BF16), wglu_ref[...]) + bglu_ref[...])
    ssm_out = _dot(z.astype(BF16), wsp_ref[...])
    attn_out = _dot(o_ref[...].astype(BF16), wap_ref[...])
    merged = sga_ref[...].astype(F32) * attn_out + sgs_ref[...].astype(F32) * ssm_out
    x1 = x_ref[...] + _dot(merged.astype(BF16), wout_ref[...])
    hf = _rms(x1, nffn_ref[...]).astype(BF16)
    acc = x1
    for c in range(FFN_SPLIT):
        lo = c * FFN_CHUNK
        a = _dot(hf, wfi_ref[:, lo:lo + FFN_CHUNK])
        g = _dot(hf, wfi_ref[:, FFN_HIDDEN + lo:FFN_HIDDEN + lo + FFN_CHUNK])
        act = (jax.nn.silu(a) * g).astype(BF16)
        acc = acc + _dot(act, wfo_ref[lo:lo + FFN_CHUNK, :])
    out_ref[...] = _rms(acc, nfin_ref[...])


def _post(x2d, o, y, sga, sgs, wap, wglu, bglu, wsp, wout, nffn, wfi, wfo, nfin, tm):
    m = x2d.shape[0]
    tok = lambda w: pl.BlockSpec((tm, w), lambda i: (i, 0))
    const = lambda a: pl.BlockSpec(a.shape, lambda i: (0,) * a.ndim, pipeline_mode=pl.Buffered(1))
    weights = (wap, wglu, bglu, wsp, wout, nffn, wfi, wfo, nfin)
    return pl.pallas_call(
        _post_kernel,
        grid=(m // tm,),
        in_specs=[tok(D_MODEL), tok(ATTN_WIDTH), tok(SSM_WIDTH), tok(D_MODEL), tok(D_MODEL)]
                 + [const(w) for w in weights],
        out_specs=tok(D_MODEL),
        out_shape=jax.ShapeDtypeStruct((m, D_MODEL), F32),
        compiler_params=_cparams(("parallel",)),
        name="post",
    )(x2d, o, y, sga, sgs, *weights)


def _rope_tables(pos):
    half = HEAD_DIM // 2
    inv = jnp.power(jnp.float32(ROPE_THETA), -2.0 * jnp.arange(half, dtype=F32) / HEAD_DIM)
    ang = pos.astype(F32)[:, None] * inv[None, :]
    cos, sin = jnp.cos(ang), jnp.sin(ang)
    reps = LANES // HEAD_DIM
    return (jnp.tile(jnp.concatenate([cos, cos], axis=1), (1, reps)),
            jnp.tile(jnp.concatenate([-sin, sin], axis=1), (1, reps)))


def _leaf_from_T(xT, bsz, seq):
    return xT.reshape(bsz, N_HEADS, HEAD_DIM, seq).transpose(0, 3, 1, 2)[None]


def _state_in(re, im):
    n = re.shape[0]
    h = jnp.concatenate([re.reshape(n, N_SG, SG_STATE), im.reshape(n, N_SG, SG_STATE)], axis=-1)
    return h.transpose(1, 0, 2)


def _state_out(h):
    n = h.shape[0]
    return (h[..., :SG_STATE].reshape(1, n, SSM_GROUPS, SSM_STATE),
            h[..., SG_STATE:].reshape(1, n, SSM_GROUPS, SSM_STATE))


def kernel(x_prompt, x_sample, cache_k, cache_v, state_ssm_re, state_ssm_im, page_table, norm_mix, w_in,
           w_attn_proj, ssm_a_re, ssm_a_im, ssm_log_dt, ssm_b_re, ssm_b_im, ssm_c_re, ssm_c_im, ssm_d, w_glu,
           b_glu, w_ssm_proj, w_out, norm_ffn, w_ffn_in, w_ffn_out, norm_final):
    assert w_in.shape[0] == 1, "single layer"
    bsz, seq = x_prompt.shape[:2]
    nseq = x_sample.shape[0]
    past_len = page_table.shape[1] * PAGE_SIZE
    assert seq % MOBA_BLOCK == 0 and past_len % MOBA_BLOCK == 0 and x_sample.shape[1] == 1

    w_in_bf = w_in[0].astype(BF16)
    post_w = (w_attn_proj[0].astype(BF16), w_glu[0].astype(BF16), b_glu, w_ssm_proj[0].astype(BF16),
              w_out[0].astype(BF16), norm_ffn, w_ffn_in[0].astype(BF16), w_ffn_out[0].astype(BF16),
              norm_final[None])
    ssm_p = (ssm_a_re[0], ssm_a_im[0], ssm_log_dt[0], ssm_b_re[0], ssm_b_im[0], ssm_c_re[0], ssm_c_im[0])

    cos_p, sin_p = _rope_tables(jnp.arange(seq, dtype=jnp.int32))
    xp2 = x_prompt.reshape(bsz * seq, D_MODEL)
    qT, kT, kbf, kmean, vT, vTb, u_p, sga_p, sgs_p = _inproj_prompt(
        xp2, norm_mix, w_in_bf, cos_p, sin_p, bsz, seq, tm=512)
    o_p = _moba_prompt(qT, kbf, kmean.reshape(bsz, seq // MOBA_BLOCK, ATTN_WIDTH), vTb, bsz, seq)
    kc, mc, nc_tab, ap = _ssm_tables(*ssm_p, chunk=SSM_CHUNK, n_pow=SUBLANES)
    h0_p = jnp.zeros((bsz, N_SG, 1, 2 * SG_STATE), F32)
    y_p, hf_p = _ssm_prompt(u_p, kc, mc, nc_tab, ap, ssm_d, h0_p, bsz, seq)
    y_prompt = _post(xp2, o_p, y_p, sga_p, sgs_p, *post_w, tm=512).reshape(bsz, seq, D_MODEL)
    new_ssm_re_p, new_ssm_im_p = _state_out(hf_p.reshape(bsz, N_SG, 2 * SG_STATE))

    cos_s, sin_s = _rope_tables(jnp.full((1,), past_len, dtype=jnp.int32))
    xs2 = x_sample.reshape(nseq, D_MODEL)
    q_s, k_s, v_s, u_s, sga_s, sgs_s = _inproj_sample(xs2, norm_mix, w_in_bf, cos_s, sin_s)
    cache_kT = cache_k[0].transpose(0, 2, 3, 1)
    cache_vT = cache_v[0].transpose(0, 2, 3, 1)
    o_s = _moba_paged(page_table, q_s[:, :, None], k_s[:, :, None], v_s[:, :, None], cache_kT, cache_vT)
    wt1, ws1, wo1, ap1 = _ssm_weights(*ssm_p, chunk=1, n_pow=1)
    h0_s = _state_in(state_ssm_re[0], state_ssm_im[0])
    y_s, hf_s = _ssm_step(u_s, wt1, ws1, wo1, ap1, ssm_d, h0_s)
    y_sample = _post(xs2, o_s.reshape(nseq, ATTN_WIDTH), y_s, sga_s, sgs_s, *post_w, tm=nseq)
    new_ssm_re_s, new_ssm_im_s = _state_out(hf_s.transpose(1, 0, 2))

    return (y_prompt, y_sample.reshape(nseq, 1, D_MODEL),
            _leaf_from_T(kT, bsz, seq), _leaf_from_T(vT, bsz, seq), new_ssm_re_p, new_ssm_im_p,
            k_s.reshape(1, nseq, 1, N_HEADS, HEAD_DIM), v_s.reshape(1, nseq, 1, N_HEADS, HEAD_DIM),
            new_ssm_re_s, new_ssm_im_s)
```

```python
import functools
import math

import jax
import jax.numpy as jnp
from jax import lax
from jax.experimental import pallas as pl
from jax.experimental.pallas import tpu as pltpu

F32 = jnp.float32
BF16 = jnp.bfloat16

D_MODEL = 1024
N_HEADS = 8
HEAD_DIM = 64
ATTN_WIDTH = N_HEADS * HEAD_DIM
MOBA_BLOCK = 256
MOBA_TOPK = 3
ROPE_THETA = 10000.0
SSM_WIDTH = 512
SSM_GROUP = 16
SSM_GROUPS = 32
SSM_STATE = 64
FFN_HIDDEN = 2816
RMS_EPS = 1e-6
PAGE_SIZE = 128

LANES = 128
SUBLANES = 8
GROUPS_PER_SG = LANES // SSM_GROUP
N_SG = SSM_GROUPS // GROUPS_PER_SG
SG_STATE = GROUPS_PER_SG * SSM_STATE
SSM_CHUNK = 8
NEG_BIG = -1e30
MOBA_HEADS_PER_STEP = 4
MOBA_LOOKAHEAD = 2
LOG2E = math.log2(math.e)
MOBA_SUM_ROWS = 16
PAGED_SCORE_UNROLL = 4
VMEM_LIMIT = 56 * 1024 * 1024


def _cparams(sem):
    return pltpu.CompilerParams(dimension_semantics=sem, vmem_limit_bytes=VMEM_LIMIT)


def _dot(a, b):
    return jnp.dot(a, b, preferred_element_type=F32)


def _rms(x, g):
    return x * lax.rsqrt(jnp.mean(x * x, axis=-1, keepdims=True) + RMS_EPS) * g


def _inproj_core(x, g, w_ref, cos, sin, q_scale):
    h = _rms(x, g).astype(BF16)
    lane = lax.broadcasted_iota(jnp.int32, (1, ATTN_WIDTH), 1)
    first_half = (lane % HEAD_DIM) < (HEAD_DIM // 2)

    def rot(t):
        partner = jnp.where(first_half,
                            pltpu.roll(t, ATTN_WIDTH - HEAD_DIM // 2, 1),
                            pltpu.roll(t, HEAD_DIM // 2, 1))
        return t * cos + partner * sin

    a = ATTN_WIDTH
    q = rot(_dot(h, w_ref[:, 0:a])) * q_scale
    k = rot(_dot(h, w_ref[:, a:2 * a]))
    v = _dot(h, w_ref[:, 2 * a:3 * a])
    u = _dot(h, w_ref[:, 3 * a:3 * a + SSM_WIDTH])
    o = 3 * a + SSM_WIDTH
    sga = jax.nn.sigmoid(_dot(h, w_ref[:, o:o + D_MODEL]))
    sgs = jax.nn.sigmoid(_dot(h, w_ref[:, o + D_MODEL:o + 2 * D_MODEL]))
    return q, k, v, u, sga, sgs


def _inproj_prompt_kernel(x_ref, g_ref, w_ref, cos_ref, sin_ref,
                          qT_ref, kT_ref, kbf_ref, kmean_ref, vT_ref, vTb_ref, u_ref, sga_ref, sgs_ref):
    reps = ATTN_WIDTH // LANES
    cos = jnp.tile(cos_ref[...], (1, reps))
    sin = jnp.tile(sin_ref[...], (1, reps))
    q, k, v, u, sga, sgs = _inproj_core(x_ref[...], g_ref[...], w_ref, cos, sin, LOG2E * HEAD_DIM ** -0.5)
    tm = q.shape[0]
    kT_ref[0] = k.T
    vT_ref[0] = v.T
    kbf_ref[...] = k.astype(BF16)
    for s in range(tm // MOBA_BLOCK):
        rows = slice(s * MOBA_BLOCK, (s + 1) * MOBA_BLOCK)
        qT_ref[0, s] = q[rows].T.astype(BF16)
        vTb_ref[0, s] = v[rows].T.astype(BF16)
        kmean_ref[0, s] = jnp.mean(k[rows], axis=0, keepdims=True)
    u_ref[...] = u
    sga_ref[...] = sga.astype(BF16)
    sgs_ref[...] = sgs.astype(BF16)


def _inproj_prompt(x2d, g, w_bf, cos, sin, bsz, seq, tm):
    m = bsz * seq
    nb = seq // MOBA_BLOCK
    tpb = seq // tm
    sub = tm // MOBA_BLOCK
    a = ATTN_WIDTH
    full = lambda shape: pl.BlockSpec(shape, lambda b, t: (0,) * len(shape))
    tok = lambda w: pl.BlockSpec((tm, w), lambda b, t: (b * tpb + t, 0))
    out_shape = (
        jax.ShapeDtypeStruct((bsz, nb, a, MOBA_BLOCK), BF16),
        jax.ShapeDtypeStruct((bsz, a, seq), F32),
        jax.ShapeDtypeStruct((m, a), BF16),
        jax.ShapeDtypeStruct((bsz, nb, 1, a), F32),
        jax.ShapeDtypeStruct((bsz, a, seq), F32),
        jax.ShapeDtypeStruct((bsz, nb, a, MOBA_BLOCK), BF16),
        jax.ShapeDtypeStruct((m, SSM_WIDTH), F32),
        jax.ShapeDtypeStruct((m, D_MODEL), BF16),
        jax.ShapeDtypeStruct((m, D_MODEL), BF16),
    )
    blk_t = pl.BlockSpec((1, sub, a, MOBA_BLOCK), lambda b, t: (b, t, 0, 0))
    lane_t = pl.BlockSpec((1, a, tm), lambda b, t: (b, 0, t))
    out_specs = (blk_t, lane_t, tok(a),
                 pl.BlockSpec((1, sub, 1, a), lambda b, t: (b, t, 0, 0)),
                 lane_t, blk_t, tok(SSM_WIDTH), tok(D_MODEL), tok(D_MODEL))
    return pl.pallas_call(
        _inproj_prompt_kernel,
        grid=(bsz, tpb),
        in_specs=[tok(D_MODEL), full((1, D_MODEL)), full(w_bf.shape),
                  pl.BlockSpec((tm, LANES), lambda b, t: (t, 0)),
                  pl.BlockSpec((tm, LANES), lambda b, t: (t, 0))],
        out_specs=out_specs,
        out_shape=out_shape,
        compiler_params=_cparams(("parallel", "parallel")),
        name="inproj_prompt",
    )(x2d, g, w_bf, cos, sin)


def _inproj_sample_kernel(x_ref, g_ref, w_ref, cos_ref, sin_ref,
                          q_ref, k_ref, v_ref, u_ref, sga_ref, sgs_ref):
    reps = ATTN_WIDTH // LANES
    cos = jnp.tile(cos_ref[...], (1, reps))
    sin = jnp.tile(sin_ref[...], (1, reps))
    q, k, v, u, sga, sgs = _inproj_core(x_ref[...], g_ref[...], w_ref, cos, sin, HEAD_DIM ** -0.5)
    q_ref[...] = q
    k_ref[...] = k
    v_ref[...] = v
    u_ref[...] = u
    sga_ref[...] = sga.astype(BF16)
    sgs_ref[...] = sgs.astype(BF16)


def _inproj_sample(x2d, g, w_bf, cos, sin):
    m = x2d.shape[0]
    a = ATTN_WIDTH
    full = lambda shape: pl.BlockSpec(shape, lambda i: (0,) * len(shape))
    shapes = [(m, a), (m, a), (m, a), (m, SSM_WIDTH), (m, D_MODEL), (m, D_MODEL)]
    dts = [F32, F32, F32, F32, BF16, BF16]
    return pl.pallas_call(
        _inproj_sample_kernel,
        grid=(1,),
        in_specs=[full(x2d.shape), full(g.shape), full(w_bf.shape), full(cos.shape), full(sin.shape)],
        out_specs=tuple(full(s) for s in shapes),
        out_shape=tuple(jax.ShapeDtypeStruct(s, d) for s, d in zip(shapes, dts)),
        compiler_params=_cparams(("arbitrary",)),
        name="inproj_sample",
    )(x2d, g, w_bf, cos, sin)


def _moba_prompt_kernel(qT_ref, k_ref, kmean_ref, vT_ref, o_ref, bias_ref, qz_s, m_s, acc_s):
    nb = qT_ref.shape[1]
    blk = MOBA_BLOCK
    nh = MOBA_HEADS_PER_STEP
    row2 = lax.broadcasted_iota(jnp.int32, (nh * HEAD_DIM, 1), 0)
    lane_km = lax.broadcasted_iota(jnp.int32, (1, nh * HEAD_DIM), 1)
    blk_row = lax.broadcasted_iota(jnp.int32, (nb, blk), 0)
    key_i = lax.broadcasted_iota(jnp.int32, (blk, blk), 0)
    qry_i = lax.broadcasted_iota(jnp.int32, (blk, blk), 1)
    causal = key_i <= qry_i
    in_head = [(row2 >= hh * HEAD_DIM) & (row2 < (hh + 1) * HEAD_DIM) for hh in range(nh)]
    hrows = [slice(hh * HEAD_DIM, (hh + 1) * HEAD_DIM) for hh in range(nh)]

    km_all = jnp.concatenate(
        [jnp.where((lane_km >= hh * HEAD_DIM) & (lane_km < (hh + 1) * HEAD_DIM), kmean_ref[0], 0.0)
         for hh in range(nh)], axis=0)
    for hh in range(nh):
        bias_ref[hh, 0] = jnp.full((nb, blk), NEG_BIG, F32)
    for i in range(1, nb):
        sb_all = jnp.dot(km_all, qT_ref[0, i].astype(F32), preferred_element_type=F32,
                         precision=lax.Precision.HIGHEST)
        for hh in range(nh):
            sb = jnp.where(blk_row < i, sb_all[hh * nb:(hh + 1) * nb], -jnp.inf)
            sel = jnp.zeros((nb, blk), dtype=jnp.bool_)
            for _r in range(min(MOBA_TOPK, i)):
                mx = jnp.max(sb, axis=0, keepdims=True)
                first = jnp.min(jnp.where(sb == mx, blk_row, nb), axis=0, keepdims=True)
                pick = (blk_row == first) & (mx > -jnp.inf)
                sel = sel | pick
                sb = jnp.where(pick, -jnp.inf, sb)
            bias_ref[hh, i] = jnp.where(sel, 0.0, NEG_BIG)

    n_items = nb // 2
    ones_rows = jnp.ones((MOBA_SUM_ROWS, 2 * blk), BF16)

    def couple(i, _):
        iq = (i, nb - 1 - i)
        n_first = (i + 1) // 2

        def diag_scores(x):
            q_pair = qT_ref[0, iq[x]]
            k_own = k_ref[pl.ds(pl.multiple_of(iq[x] * blk, blk), blk), :]
            out = []
            for hh in range(nh):
                qz = jnp.where(in_head[hh], q_pair, jnp.zeros_like(q_pair))
                qz_s[x, hh] = qz
                out.append(_dot(k_own, qz))
            return out

        def diag_absorb(x, s_own):
            for hh in range(nh):
                s = jnp.where(causal, s_own[hh], NEG_BIG)
                m0 = jnp.max(s, axis=0, keepdims=True)
                p = jnp.exp2(s - m0)
                m_s[x, hh] = m0
                vv = jnp.concatenate([vT_ref[0, iq[x], hrows[hh], :], ones_rows[:, :blk]], axis=0)
                acc_s[x, hh] = _dot(vv, p.astype(BF16))

        def item_params(k):
            first = k < n_first
            x = jnp.where(first, 0, 1)
            return x, jnp.where(first, iq[0], iq[1]), jnp.where(first, k, k - n_first)

        def item_scores(k):
            x, _, jp = item_params(k)
            kk = k_ref[pl.ds(pl.multiple_of(2 * jp * blk, blk), 2 * blk), :]
            out = []
            for hh in range(nh):
                s = _dot(kk, qz_s[x, hh])
                out.append((s, jnp.max(s[:blk], axis=0, keepdims=True), jnp.max(s[blk:], axis=0, keepdims=True)))
            return out

        def item_absorb(k, sc):
            x, qb, jp = item_params(k)
            j0 = 2 * jp
            for hh in range(nh):
                s, cma, cmb = sc[hh]
                ba = bias_ref[hh, qb, pl.ds(j0, 1), :]
                bb = bias_ref[hh, qb, pl.ds(j0 + 1, 1), :]
                m = m_s[x, hh]
                m_new = jnp.maximum(m, jnp.maximum(cma + ba, cmb + bb))
                alpha = jnp.exp2(m - m_new)
                pa = jnp.exp2(s[:blk] - (m_new - ba))
                pb = jnp.exp2(s[blk:] - (m_new - bb))
                m_s[x, hh] = m_new
                pp = jnp.concatenate([pa, pb], axis=0).astype(BF16)
                vv = jnp.concatenate([vT_ref[0, j0, hrows[hh], :], vT_ref[0, j0 + 1, hrows[hh], :]], axis=1)
                vv = jnp.concatenate([vv, ones_rows], axis=0)
                acc_s[x, hh] = alpha * acc_s[x, hh] + _dot(vv, pp)

        s_diag = [diag_scores(0), diag_scores(1)]
        pending = [item_scores(k) for k in range(min(MOBA_LOOKAHEAD, n_items))]
        diag_absorb(0, s_diag[0])
        diag_absorb(1, s_diag[1])
        for k in range(n_items):
            if k + MOBA_LOOKAHEAD < n_items:
                pending.append(item_scores(k + MOBA_LOOKAHEAD))
            item_absorb(k, pending[k])
        for x in range(2):
            oT = jnp.concatenate([acc_s[x, hh, 0:HEAD_DIM, :] / acc_s[x, hh, HEAD_DIM:HEAD_DIM + 1, :]
                                  for hh in range(nh)], axis=0)
            o_ref[pl.ds(pl.multiple_of(iq[x] * blk, blk), blk), :] = oT.T.astype(o_ref.dtype)
        return 0

    lax.fori_loop(0, nb // 2, couple, 0)


def _moba_prompt(qT, kbf, kmean, vTb, bsz, seq):
    nb = seq // MOBA_BLOCK
    hp = N_HEADS // MOBA_HEADS_PER_STEP
    pair = MOBA_HEADS_PER_STEP * HEAD_DIM
    return pl.pallas_call(
        _moba_prompt_kernel,
        grid=(bsz, hp),
        in_specs=[pl.BlockSpec((1, nb, pair, MOBA_BLOCK), lambda b, h: (b, 0, h, 0)),
                  pl.BlockSpec((seq, pair), lambda b, h: (b, h)),
                  pl.BlockSpec((1, nb, pair), lambda b, h: (b, 0, h)),
                  pl.BlockSpec((1, nb, pair, MOBA_BLOCK), lambda b, h: (b, 0, h, 0))],
        out_specs=pl.BlockSpec((seq, pair), lambda b, h: (b, h)),
        out_shape=jax.ShapeDtypeStruct((bsz * seq, ATTN_WIDTH), BF16),
        scratch_shapes=[pltpu.VMEM((MOBA_HEADS_PER_STEP, nb, nb, MOBA_BLOCK), F32),
                        pltpu.VMEM((2, MOBA_HEADS_PER_STEP, pair, MOBA_BLOCK), BF16),
                        pltpu.VMEM((2, MOBA_HEADS_PER_STEP, 1, MOBA_BLOCK), F32),
                        pltpu.VMEM((2, MOBA_HEADS_PER_STEP, HEAD_DIM + MOBA_SUM_ROWS, MOBA_BLOCK), F32)],
        compiler_params=_cparams(("parallel", "parallel")),
        name="moba_prompt",
    )(qT, kbf, kmean, vTb)


def _moba_paged_kernel(pt_ref, qcol_ref, kn_ref, vn_ref, ck_ref, cv_ref, o_ref,
                       kbuf, vbuf, s_ref, psel_ref, stat_ref, qb_ref, ksem, vsem):
    b = pl.program_id(0)
    nseq = pl.num_programs(0) - 1
    n_pages = kbuf.shape[1]
    nblk = n_pages // 2
    slot = b % 2
    n_v = N_HEADS * MOBA_TOPK * 2

    def k_copy(seq_i, sl, p):
        return pltpu.make_async_copy(ck_ref.at[pt_ref[seq_i, p]], kbuf.at[sl, p], ksem.at[sl])

    def start_k(seq_i, sl):
        def body(p, _):
            k_copy(seq_i, sl, p).start()
            return 0
        lax.fori_loop(0, n_pages, body, 0)

    def v_copy(page, h, r, par):
        return pltpu.make_async_copy(cv_ref.at[page, h], vbuf.at[h, r, par], vsem.at[0])

    @pl.when(b == 0)
    def _():
        start_k(0, 0)

    @pl.when(b < nseq)
    def _():
        def wait_body(p, _):
            k_copy(b, slot, p).wait()
            return 0
        lax.fori_loop(0, n_pages, wait_body, 0)

    @pl.when(b + 1 < nseq)
    def _():
        start_k(b + 1, 1 - slot)

    @pl.when(b >= 1)
    def _():
        for h in range(N_HEADS):
            for r in range(MOBA_TOPK):
                for par in range(2):
                    v_copy(0, h, r, par).wait()
        vn = vn_ref[0]
        for h in range(N_HEADS):
            acc = jnp.zeros((HEAD_DIM, PAGE_SIZE), F32)
            for r in range(MOBA_TOPK):
                for par in range(2):
                    acc = acc + vbuf[h, r, par] * psel_ref[h, 2 * r + par]
            p_own = stat_ref[0, h][:, 0:1]
            l = stat_ref[1, h][:, 0:1]
            o_h = jnp.sum(acc, axis=1, keepdims=True) + p_own * vn[h * HEAD_DIM:(h + 1) * HEAD_DIM, :]
            o_ref[0, h * HEAD_DIM:(h + 1) * HEAD_DIM, :] = o_h / l

    @pl.when(b < nseq)
    def _():
        qcol = qcol_ref[0]
        own = qcol * kn_ref[0]
        blk_i = lax.broadcasted_iota(jnp.int32, (nblk, 1), 0)
        for h in range(N_HEADS):
            qb_ref[h] = jnp.broadcast_to(qcol[h * HEAD_DIM:(h + 1) * HEAD_DIM, :], (HEAD_DIM, PAGE_SIZE))
        for h in range(N_HEADS):
            def score_body(g, _):
                qb = qb_ref[h]
                for dn in range(PAGED_SCORE_UNROLL):
                    n = g * PAGED_SCORE_UNROLL + dn
                    for par in range(2):
                        kt = kbuf[slot, 2 * n + par, h]
                        s_ref[par, h, pl.ds(n, 1), :] = jnp.sum(kt * qb, axis=0, keepdims=True)
                return 0
            lax.fori_loop(0, nblk // PAGED_SCORE_UNROLL, score_body, 0)

        for h in range(N_HEADS):
            s0 = s_ref[0, h]
            s1 = s_ref[1, h]
            bs = jnp.sum(s0 + s1, axis=1, keepdims=True)
            sel = jnp.zeros((nblk, 1), dtype=jnp.bool_)
            picks = []
            for r in range(MOBA_TOPK):
                mx = jnp.max(bs, axis=0, keepdims=True)
                first = jnp.min(jnp.where(bs == mx, blk_i, nblk), axis=0, keepdims=True)
                pick = (blk_i == first) & (mx > -jnp.inf)
                sel = sel | pick
                bs = jnp.where(pick, -jnp.inf, bs)
                blk_id = jnp.max(jnp.where(pick, blk_i, 0))
                for par in range(2):
                    v_copy(pt_ref[b, 2 * blk_id + par], h, r, par).start()
                picks.append(blk_id)
            s_own = jnp.sum(own[h * HEAD_DIM:(h + 1) * HEAD_DIM, :], axis=0, keepdims=True)
            sm0 = jnp.where(sel, s0, NEG_BIG)
            sm1 = jnp.where(sel, s1, NEG_BIG)
            mx = jnp.maximum(jnp.max(jnp.max(jnp.maximum(sm0, sm1), axis=1, keepdims=True),
                                     axis=0, keepdims=True), s_own)
            p0 = jnp.exp(sm0 - mx)
            p1 = jnp.exp(sm1 - mx)
            p_own = jnp.exp(s_own - mx)
            l = jnp.sum(jnp.sum(p0 + p1, axis=1, keepdims=True), axis=0, keepdims=True) + p_own
            s_ref[0, h] = p0
            s_ref[1, h] = p1
            for r in range(MOBA_TOPK):
                for par in range(2):
                    psel_ref[h, 2 * r + par] = s_ref[par, h, pl.ds(picks[r], 1), :]
            stat_ref[0, h] = jnp.broadcast_to(p_own, (1, PAGE_SIZE))
            stat_ref[1, h] = jnp.broadcast_to(l, (1, PAGE_SIZE))


def _moba_paged(page_table, qcol, kncol, vncol, cache_kT, cache_vT):
    nseq, n_pages = page_table.shape
    cur = pl.BlockSpec((1, ATTN_WIDTH, 1), lambda b, pt: (jnp.minimum(b, nseq - 1), 0, 0))
    prev = pl.BlockSpec((1, ATTN_WIDTH, 1), lambda b, pt: (jnp.maximum(b - 1, 0), 0, 0))
    any_spec = pl.BlockSpec(memory_space=pl.ANY)
    grid_spec = pltpu.PrefetchScalarGridSpec(
        num_scalar_prefetch=1,
        grid=(nseq + 1,),
        in_specs=[cur, cur, prev, any_spec, any_spec],
        out_specs=prev,
        scratch_shapes=[
            pltpu.VMEM((2, n_pages, N_HEADS, HEAD_DIM, PAGE_SIZE), F32),
            pltpu.VMEM((N_HEADS, MOBA_TOPK, 2, HEAD_DIM, PAGE_SIZE), F32),
            pltpu.VMEM((2, N_HEADS, n_pages // 2, PAGE_SIZE), F32),
            pltpu.VMEM((N_HEADS, 2 * MOBA_TOPK, 1, PAGE_SIZE), F32),
            pltpu.VMEM((2, N_HEADS, 1, PAGE_SIZE), F32),
            pltpu.VMEM((N_HEADS, HEAD_DIM, PAGE_SIZE), F32),
            pltpu.SemaphoreType.DMA((2,)),
            pltpu.SemaphoreType.DMA((1,)),
        ])
    return pl.pallas_call(
        _moba_paged_kernel,
        grid_spec=grid_spec,
        out_shape=jax.ShapeDtypeStruct((nseq, ATTN_WIDTH, 1), F32),
        compiler_params=_cparams(("arbitrary",)),
        name="moba_paged",
    )(page_table, qcol, kncol, vncol, cache_kT, cache_vT)


def _ssm_weights(a_re, a_im, log_dt, b_re, b_im, c_re, c_im, chunk, n_pow):
    t_ = chunk
    lam = lax.complex(a_re.astype(F32), a_im.astype(F32))
    dt = jnp.exp(log_dt.astype(F32))[:, None]
    ldt = lam * dt
    a_bar = jnp.exp(ldt)
    b_bar = ((a_bar - 1.0) / lam)[..., None] * lax.complex(b_re.astype(F32), b_im.astype(F32))
    c_c = lax.complex(c_re.astype(F32), c_im.astype(F32))
    taus = jnp.arange(t_ + 1, dtype=F32).astype(jnp.complex64)
    apow = jnp.exp(ldt[None] * taus[:, None, None])
    eye = jnp.eye(GROUPS_PER_SG, dtype=F32)
    sg = lambda x, axis: x.reshape(x.shape[:axis] + (N_SG, GROUPS_PER_SG) + x.shape[axis + 1:])

    ktoep = jnp.einsum('gcp,tgp,gpd->tgcd', c_c, apow[:t_], b_bar).real
    s_i = jnp.arange(t_)[:, None]
    t_i = jnp.arange(t_)[None, :]
    kst = jnp.where((s_i <= t_i)[:, :, None, None, None],
                    ktoep[jnp.clip(t_i - s_i, 0, t_ - 1)], 0.0)
    w_toep = jnp.einsum('stgjcd,jk->gsjdtkc', sg(kst, 2), eye)
    w_toep = w_toep.reshape(N_SG, t_ * LANES, t_ * LANES)

    mst = apow[:t_][::-1][..., None] * b_bar[None]
    mst = jnp.stack([mst.real, mst.imag], axis=-1)
    w_state = jnp.einsum('sgjpdr,jk->gsjdrkp', sg(mst, 1), eye)
    w_state = w_state.reshape(N_SG, t_ * LANES, 2 * SG_STATE)

    nout = c_c[None] * apow[1:][:, :, None, :]
    nout = jnp.stack([nout.real, -nout.imag], axis=-1)
    w_out = jnp.einsum('tgjcpr,jk->grjptkc', sg(nout, 1), eye)
    w_out = w_out.reshape(N_SG, 2 * SG_STATE, t_ * LANES)

    rs = jnp.arange(1, n_pow + 1, dtype=F32).astype(jnp.complex64)
    ap = jnp.exp((ldt * t_)[None] * rs[:, None, None])
    ap = jnp.stack([ap.real, ap.imag], axis=0)
    a_pow = sg(ap, 2).transpose(2, 1, 0, 3, 4).reshape(N_SG, n_pow, 2 * SG_STATE)
    return w_toep, w_state, w_out, a_pow


def _ssm_tables(a_re, a_im, log_dt, b_re, b_im, c_re, c_im, chunk, n_pow):
    t_ = chunk
    lam = lax.complex(a_re.astype(F32), a_im.astype(F32))
    ldt = lam * jnp.exp(log_dt.astype(F32))[:, None]
    a_bar = jnp.exp(ldt)
    b_bar = ((a_bar - 1.0) / lam)[..., None] * lax.complex(b_re.astype(F32), b_im.astype(F32))
    c_c = lax.complex(c_re.astype(F32), c_im.astype(F32))
    taus = jnp.arange(t_ + 1, dtype=F32).astype(jnp.complex64)
    apow = jnp.exp(ldt[None] * taus[:, None, None])
    gq, ssg, c_, p_ = GROUPS_PER_SG, N_SG, SSM_GROUP, SSM_STATE
    c4 = c_c.reshape(ssg, gq, c_, p_)
    b4 = b_bar.reshape(ssg, gq, p_, c_)
    ap4 = apow.reshape(t_ + 1, ssg, gq, p_)

    kc = jnp.einsum('sjcp,tsjp,sjpd->sjdtc', c4, ap4[:t_], b4).real
    kc = kc.reshape(ssg, LANES, t_ * c_)
    mst = jnp.einsum('tsjp,sjpd->stjdp', ap4[:t_][::-1], b4)
    mc = jnp.concatenate([mst.real, mst.imag], axis=-1).reshape(ssg, t_ * LANES, 2 * p_)
    nout = jnp.einsum('sjcp,tsjp->sjptc', c4, ap4[1:])
    nc = jnp.concatenate([nout.real, -nout.imag], axis=1).reshape(ssg, 2 * SG_STATE, t_ * c_)
    rs = jnp.arange(1, n_pow + 1, dtype=F32).astype(jnp.complex64)
    ap = jnp.exp((ldt * t_)[None] * rs[:, None, None])
    ap = ap.reshape(n_pow, ssg, SG_STATE).transpose(1, 0, 2)
    a_pow = jnp.concatenate([ap.real, ap.imag], axis=-1)
    return kc, mc, nc, a_pow


def _expand_ssm_weights(kc_ref, mc_ref, nc_ref, wt_s, ws_s, wo_s):
    t_ = SSM_CHUNK
    kw = t_ * LANES
    lg = lambda n: n.bit_length() - 1
    r = lax.broadcasted_iota(jnp.int32, (LANES, kw), 0)
    q = lax.broadcasted_iota(jnp.int32, (LANES, kw), 1)
    sh_c, sh_p = lg(SSM_GROUP), lg(SSM_STATE)
    ecol = jnp.where(((r >> sh_c) == (q >> lg(LANES))) & ((r & (SSM_GROUP - 1)) == (q & (SSM_GROUP - 1))),
                     1.0, 0.0).astype(BF16)
    e2 = jnp.where(((r >> sh_p) == (q >> lg(SG_STATE))) & ((r & (SSM_STATE - 1)) == (q & (SSM_STATE - 1))),
                   1.0, 0.0).astype(BF16)
    col_c = (q & (LANES - 1)) >> sh_c
    taps = jnp.where((r >> sh_c) == col_c, _dot(kc_ref[0].astype(BF16), ecol), 0.0).astype(BF16)
    blank = jnp.zeros((LANES, LANES), BF16)
    for s in range(t_):
        for t in range(t_):
            wt_s[s * LANES:(s + 1) * LANES, t * LANES:(t + 1) * LANES] = (
                taps[:, (t - s) * LANES:(t - s + 1) * LANES] if t >= s else blank)
    rr = lax.broadcasted_iota(jnp.int32, (kw, kw), 0)
    qq = lax.broadcasted_iota(jnp.int32, (kw, kw), 1)
    row_c = (rr & (LANES - 1)) >> sh_c
    row_p = (rr & (SG_STATE - 1)) >> sh_p
    colq_c = (qq & (LANES - 1)) >> sh_c
    colq_p = (qq & (SG_STATE - 1)) >> sh_p
    ws_s[...] = jnp.where(row_c == colq_p, _dot(mc_ref[0].astype(BF16), e2), 0.0).astype(BF16)
    wo_s[...] = jnp.where(row_p == colq_c, _dot(nc_ref[0].astype(BF16), ecol), 0.0).astype(BF16)


def _ssm_prompt_kernel(u_ref, kc_ref, mc_ref, nc_ref, ap_ref, d_ref, h0_ref,
                       y_ref, hf_ref, uc_ref, x_ref, hin_ref, wt_s, ws_s, wo_s):
    t_ = SSM_CHUNK
    nc = uc_ref.shape[0]
    ns = SG_STATE

    @pl.when(pl.program_id(1) == 0)
    def _():
        _expand_ssm_weights(kc_ref, mc_ref, nc_ref, wt_s, ws_s, wo_s)

    for t in range(t_):
        uc_ref[:, t * LANES:(t + 1) * LANES] = u_ref[pl.ds(t, nc, stride=t_), :].astype(BF16)
    uc = uc_ref[...]
    x_ref[...] = _dot(uc, ws_s[...])

    ap = ap_ref[0]
    apr, api = ap[:, :ns], ap[:, ns:]
    row = lax.broadcasted_iota(jnp.int32, (SUBLANES, 1), 0)

    def tile_body(k, carry):
        hr, hi = carry
        rows = pl.ds(pl.multiple_of(k * SUBLANES, SUBLANES), SUBLANES)
        xr = x_ref[rows, 0:ns]
        xi = x_ref[rows, ns:2 * ns]
        for d in (1, 2, 4):
            ar, ai = apr[d - 1:d], api[d - 1:d]
            sr = jnp.where(row >= d, pltpu.roll(xr, d, 0), 0.0)
            si = jnp.where(row >= d, pltpu.roll(xi, d, 0), 0.0)
            xr, xi = xr + ar * sr - ai * si, xi + ar * si + ai * sr
        outr = xr + apr * hr - api * hi
        outi = xi + apr * hi + api * hr
        hin_ref[rows, 0:ns] = jnp.where(row >= 1, pltpu.roll(outr, 1, 0), hr)
        hin_ref[rows, ns:2 * ns] = jnp.where(row >= 1, pltpu.roll(outi, 1, 0), hi)
        return outr[SUBLANES - 1:SUBLANES], outi[SUBLANES - 1:SUBLANES]

    h0 = h0_ref[0, 0]
    hr, hi = lax.fori_loop(0, nc // SUBLANES, tile_body, (h0[:, :ns], h0[:, ns:]))
    hf_ref[0, 0] = jnp.concatenate([hr, hi], axis=1)

    y = _dot(uc, wt_s[...]) + _dot(hin_ref[...].astype(BF16), wo_s[...])
    dv = d_ref[...]
    for t in range(t_):
        rows = pl.ds(t, nc, stride=t_)
        y_ref[rows, :] = y[:, t * LANES:(t + 1) * LANES] + dv * u_ref[rows, :]


def _ssm_prompt(u, kc, mc, nc_tab, a_pow, d_row, h0, bsz, seq):
    t_ = SSM_CHUNK
    nc = seq // t_
    kw = t_ * LANES
    assert kc.shape[2] == LANES and 2 * SG_STATE == kw
    wspec = lambda shape: pl.BlockSpec((1,) + shape, lambda s, b: (s, 0, 0))
    return pl.pallas_call(
        _ssm_prompt_kernel,
        grid=(N_SG, bsz),
        in_specs=[pl.BlockSpec((seq, LANES), lambda s, b: (b, s)),
                  wspec(kc.shape[1:]), wspec(mc.shape[1:]), wspec(nc_tab.shape[1:]),
                  wspec((SUBLANES, 2 * SG_STATE)),
                  pl.BlockSpec((1, LANES), lambda s, b: (0, s)),
                  pl.BlockSpec((1, 1, 1, 2 * SG_STATE), lambda s, b: (b, s, 0, 0))],
        out_specs=(pl.BlockSpec((seq, LANES), lambda s, b: (b, s)),
                   pl.BlockSpec((1, 1, 1, 2 * SG_STATE), lambda s, b: (b, s, 0, 0))),
        out_shape=(jax.ShapeDtypeStruct((bsz * seq, SSM_WIDTH), F32),
                   jax.ShapeDtypeStruct((bsz, N_SG, 1, 2 * SG_STATE), F32)),
        scratch_shapes=[pltpu.VMEM((nc, kw), BF16),
                        pltpu.VMEM((nc, 2 * SG_STATE), F32),
                        pltpu.VMEM((nc, 2 * SG_STATE), F32),
                        pltpu.VMEM((kw, kw), BF16),
                        pltpu.VMEM((kw, 2 * SG_STATE), BF16),
                        pltpu.VMEM((2 * SG_STATE, kw), BF16)],
        compiler_params=_cparams(("arbitrary", "arbitrary")),
        name="ssm_prompt",
    )(u, kc, mc, nc_tab, a_pow, d_row, h0)


def _ssm_step_kernel(u_ref, wt_ref, ws_ref, wo_ref, ap_ref, d_ref, h0_ref, y_ref, hf_ref):
    ns = SG_STATE
    hp = lax.Precision.HIGHEST
    for s in range(N_SG):
        us = u_ref[:, s * LANES:(s + 1) * LANES]
        h0 = h0_ref[s]
        x = jnp.dot(us, ws_ref[s], preferred_element_type=F32, precision=hp)
        ar, ai = ap_ref[s, 0:1, :ns], ap_ref[s, 0:1, ns:]
        hr, hi = h0[:, :ns], h0[:, ns:]
        hf_ref[s] = jnp.concatenate([x[:, :ns] + ar * hr - ai * hi,
                                     x[:, ns:] + ar * hi + ai * hr], axis=1)
        y = (jnp.dot(us, wt_ref[s], preferred_element_type=F32, precision=hp)
             + jnp.dot(h0, wo_ref[s], preferred_element_type=F32, precision=hp))
        y_ref[:, s * LANES:(s + 1) * LANES] = y + d_ref[:, s * LANES:(s + 1) * LANES] * us


def _ssm_step(u, w_toep, w_state, w_out, a_pow, d_row, h0):
    m = u.shape[0]
    full = lambda a: pl.BlockSpec(a.shape, lambda i: (0,) * a.ndim)
    args = (u, w_toep, w_state, w_out, a_pow, d_row, h0)
    shapes = [(m, SSM_WIDTH), (N_SG, m, 2 * SG_STATE)]
    return pl.pallas_call(
        _ssm_step_kernel,
        grid=(1,),
        in_specs=[full(a) for a in args],
        out_specs=tuple(pl.BlockSpec(s, lambda i, n=len(s): (0,) * n) for s in shapes),
        out_shape=tuple(jax.ShapeDtypeStruct(s, F32) for s in shapes),
        compiler_params=_cparams(("arbitrary",)),
        name="ssm_step",
    )(*args)


FFN_SPLIT = 2
FFN_CHUNK = FFN_HIDDEN // FFN_SPLIT


def _post_kernel(x_ref, o_ref, y_ref, sga_ref, sgs_ref, wap_ref, wglu_ref, bglu_ref, wsp_ref,
                 wout_ref, nffn_ref, wfi_ref, wfo_ref, nfin_ref, out_ref):
    z = jax.nn.gelu(y_ref[...])
    z = z * jax.nn.sigmoid(_dot(z.astype(BF16), wglu_ref[...]) + bglu_ref[...])
    ssm_out = _dot(z.astype(BF16), wsp_ref[...])
    attn_out = _dot(o_ref[...].astype(BF16), wap_ref[...])
    merged = sga_ref[...].astype(F32) * attn_out + sgs_ref[...].astype(F32) * ssm_out
    x1 = x_ref[...] + _dot(merged.astype(BF16), wout_ref[...])
    hf = _rms(x1, nffn_ref[...]).astype(BF16)
    acc = x1
    for c in range(FFN_SPLIT):
        lo = c * FFN_CHUNK
        a = _dot(hf, wfi_ref[:, lo:lo + FFN_CHUNK])
        g = _dot(hf, wfi_ref[:, FFN_HIDDEN + lo:FFN_HIDDEN + lo + FFN_CHUNK])
        act = (jax.nn.silu(a) * g).astype(BF16)
        acc = acc + _dot(act, wfo_ref[lo:lo + FFN_CHUNK, :])
    out_ref[...] = _rms(acc, nfin_ref[...])


def _post(x2d, o, y, sga, sgs, wap, wglu, bglu, wsp, wout, nffn, wfi, wfo, nfin, tm):
    m = x2d.shape[0]
    tok = lambda w: pl.BlockSpec((tm, w), lambda i: (i, 0))
    const = lambda a: pl.BlockSpec(a.shape, lambda i: (0,) * a.ndim, pipeline_mode=pl.Buffered(1))
    weights = (wap, wglu, bglu, wsp, wout, nffn, wfi, wfo, nfin)
    return pl.pallas_call(
        _post_kernel,
        grid=(m // tm,),
        in_specs=[tok(D_MODEL), tok(ATTN_WIDTH), tok(SSM_WIDTH), tok(D_MODEL), tok(D_MODEL)]
                 + [const(w) for w in weights],
        out_specs=tok(D_MODEL),
        out_shape=jax.ShapeDtypeStruct((m, D_MODEL), F32),
        compiler_params=_cparams(("parallel",)),
        name="post",
    )(x2d, o, y, sga, sgs, *weights)


def _rope_tables(pos):
    half = HEAD_DIM // 2
    inv = jnp.power(jnp.float32(ROPE_THETA), -2.0 * jnp.arange(half, dtype=F32) / HEAD_DIM)
    ang = pos.astype(F32)[:, None] * inv[None, :]
    cos, sin = jnp.cos(ang), jnp.sin(ang)
    reps = LANES // HEAD_DIM
    return (jnp.tile(jnp.concatenate([cos, cos], axis=1), (1, reps)),
            jnp.tile(jnp.concatenate([-sin, sin], axis=1), (1, reps)))


def _leaf_from_T(xT, bsz, seq):
    return xT.reshape(bsz, N_HEADS, HEAD_DIM, seq).transpose(0, 3, 1, 2)[None]


def _state_in(re, im):
    n = re.shape[0]
    h = jnp.concatenate([re.reshape(n, N_SG, SG_STATE), im.reshape(n, N_SG, SG_STATE)], axis=-1)
    return h.transpose(1, 0, 2)


def _state_out(h):
    n = h.shape[0]
    return (h[..., :SG_STATE].reshape(1, n, SSM_GROUPS, SSM_STATE),
            h[..., SG_STATE:].reshape(1, n, SSM_GROUPS, SSM_STATE))


def kernel(x_prompt, x_sample, cache_k, cache_v, state_ssm_re, state_ssm_im, page_table, norm_mix, w_in,
           w_attn_proj, ssm_a_re, ssm_a_im, ssm_log_dt, ssm_b_re, ssm_b_im, ssm_c_re, ssm_c_im, ssm_d, w_glu,
           b_glu, w_ssm_proj, w_out, norm_ffn, w_ffn_in, w_ffn_out, norm_final):
    assert w_in.shape[0] == 1, "single layer"
    bsz, seq = x_prompt.shape[:2]
    nseq = x_sample.shape[0]
    past_len = page_table.shape[1] * PAGE_SIZE
    assert seq % MOBA_BLOCK == 0 and past_len % MOBA_BLOCK == 0 and x_sample.shape[1] == 1

    w_in_bf = w_in[0].astype(BF16)
    post_w = (w_attn_proj[0].astype(BF16), w_glu[0].astype(BF16), b_glu, w_ssm_proj[0].astype(BF16),
              w_out[0].astype(BF16), norm_ffn, w_ffn_in[0].astype(BF16), w_ffn_out[0].astype(BF16),
              norm_final[None])
    ssm_p = (ssm_a_re[0], ssm_a_im[0], ssm_log_dt[0], ssm_b_re[0], ssm_b_im[0], ssm_c_re[0], ssm_c_im[0])

    cos_p, sin_p = _rope_tables(jnp.arange(seq, dtype=jnp.int32))
    xp2 = x_prompt.reshape(bsz * seq, D_MODEL)
    qT, kT, kbf, kmean, vT, vTb, u_p, sga_p, sgs_p = _inproj_prompt(
        xp2, norm_mix, w_in_bf, cos_p, sin_p, bsz, seq, tm=512)
    o_p = _moba_prompt(qT, kbf, kmean.reshape(bsz, seq // MOBA_BLOCK, ATTN_WIDTH), vTb, bsz, seq)
    kc, mc, nc_tab, ap = _ssm_tables(*ssm_p, chunk=SSM_CHUNK, n_pow=SUBLANES)
    h0_p = jnp.zeros((bsz, N_SG, 1, 2 * SG_STATE), F32)
    y_p, hf_p = _ssm_prompt(u_p, kc, mc, nc_tab, ap, ssm_d, h0_p, bsz, seq)
    y_prompt = _post(xp2, o_p, y_p, sga_p, sgs_p, *post_w, tm=512).reshape(bsz, seq, D_MODEL)
    new_ssm_re_p, new_ssm_im_p = _state_out(hf_p.reshape(bsz, N_SG, 2 * SG_STATE))

    cos_s, sin_s = _rope_tables(jnp.full((1,), past_len, dtype=jnp.int32))
    xs2 = x_sample.reshape(nseq, D_MODEL)
    q_s, k_s, v_s, u_s, sga_s, sgs_s = _inproj_sample(xs2, norm_mix, w_in_bf, cos_s, sin_s)
    cache_kT = cache_k[0].transpose(0, 2, 3, 1)
    cache_vT = cache_v[0].transpose(0, 2, 3, 1)
    o_s = _moba_paged(page_table, q_s[:, :, None], k_s[:, :, None], v_s[:, :, None], cache_kT, cache_vT)
    wt1, ws1, wo1, ap1 = _ssm_weights(*ssm_p, chunk=1, n_pow=1)
    h0_s = _state_in(state_ssm_re[0], state_ssm_im[0])
    y_s, hf_s = _ssm_step(u_s, wt1, ws1, wo1, ap1, ssm_d, h0_s)
    y_sample = _post(xs2, o_s.reshape(nseq, ATTN_WIDTH), y_s, sga_s, sgs_s, *post_w, tm=nseq)
    new_ssm_re_s, new_ssm_im_s = _state_out(hf_s.transpose(1, 0, 2))

    return (y_prompt, y_sample.reshape(nseq, 1, D_MODEL),
            _leaf_from_T(kT, bsz, seq), _leaf_from_T(vT, bsz, seq), new_ssm_re_p, new_ssm_im_p,
            k_s.reshape(1, nseq, 1, N_HEADS, HEAD_DIM), v_s.reshape(1, nseq, 1, N_HEADS, HEAD_DIM),
            new_ssm_re_s, new_ssm_im_s)
```

```python
import functools
import math

import jax
import jax.numpy as jnp
from jax import lax
from jax.experimental import pallas as pl
from jax.experimental.pallas import tpu as pltpu

F32 = jnp.float32
BF16 = jnp.bfloat16

D_MODEL = 1024
N_HEADS = 8
HEAD_DIM = 64
ATTN_WIDTH = N_HEADS * HEAD_DIM
MOBA_BLOCK = 256
MOBA_TOPK = 3
ROPE_THETA = 10000.0
SSM_WIDTH = 512
SSM_GROUP = 16
SSM_GROUPS = 32
SSM_STATE = 64
FFN_HIDDEN = 2816
RMS_EPS = 1e-6
PAGE_SIZE = 128

LANES = 128
SUBLANES = 8
GROUPS_PER_SG = LANES // SSM_GROUP
N_SG = SSM_GROUPS // GROUPS_PER_SG
SG_STATE = GROUPS_PER_SG * SSM_STATE
SSM_CHUNK = 8
NEG_BIG = -1e30
MOBA_HEADS_PER_STEP = 4
MOBA_LOOKAHEAD = 2
LOG2E = math.log2(math.e)
MOBA_SUM_ROWS = 16
PAGED_SCORE_UNROLL = 4
VMEM_LIMIT = 56 * 1024 * 1024


def _cparams(sem):
    return pltpu.CompilerParams(dimension_semantics=sem, vmem_limit_bytes=VMEM_LIMIT)


def _dot(a, b):
    return jnp.dot(a, b, preferred_element_type=F32)


def _rms(x, g):
    return x * lax.rsqrt(jnp.mean(x * x, axis=-1, keepdims=True) + RMS_EPS) * g


def _inproj_core(x, g, w_ref, cos, sin, q_scale):
    h = _rms(x, g).astype(BF16)
    lane = lax.broadcasted_iota(jnp.int32, (1, ATTN_WIDTH), 1)
    first_half = (lane % HEAD_DIM) < (HEAD_DIM // 2)

    def rot(t):
        partner = jnp.where(first_half,
                            pltpu.roll(t, ATTN_WIDTH - HEAD_DIM // 2, 1),
                            pltpu.roll(t, HEAD_DIM // 2, 1))
        return t * cos + partner * sin

    a = ATTN_WIDTH
    q = rot(_dot(h, w_ref[:, 0:a])) * q_scale
    k = rot(_dot(h, w_ref[:, a:2 * a]))
    v = _dot(h, w_ref[:, 2 * a:3 * a])
    u = _dot(h, w_ref[:, 3 * a:3 * a + SSM_WIDTH])
    o = 3 * a + SSM_WIDTH
    sga = jax.nn.sigmoid(_dot(h, w_ref[:, o:o + D_MODEL]))
    sgs = jax.nn.sigmoid(_dot(h, w_ref[:, o + D_MODEL:o + 2 * D_MODEL]))
    return q, k, v, u, sga, sgs


def _inproj_prompt_kernel(x_ref, g_ref, w_ref, cos_ref, sin_ref,
                          qT_ref, kT_ref, kbf_ref, kmean_ref, vT_ref, vTb_ref, u_ref, sga_ref, sgs_ref):
    reps = ATTN_WIDTH // LANES
    cos = jnp.tile(cos_ref[...], (1, reps))
    sin = jnp.tile(sin_ref[...], (1, reps))
    q, k, v, u, sga, sgs = _inproj_core(x_ref[...], g_ref[...], w_ref, cos, sin, LOG2E * HEAD_DIM ** -0.5)
    tm = q.shape[0]
    kT_ref[0] = k.T
    vT_ref[0] = v.T
    kbf_ref[...] = k.astype(BF16)
    for s in range(tm // MOBA_BLOCK):
        rows = slice(s * MOBA_BLOCK, (s + 1) * MOBA_BLOCK)
        qT_ref[0, s] = q[rows].T.astype(BF16)
        vTb_ref[0, s] = v[rows].T.astype(BF16)
        kmean_ref[0, s] = jnp.mean(k[rows], axis=0, keepdims=True)
    u_ref[...] = u
    sga_ref[...] = sga.astype(BF16)
    sgs_ref[...] = sgs.astype(BF16)


def _inproj_prompt(x2d, g, w_bf, cos, sin, bsz, seq, tm):
    m = bsz * seq
    nb = seq // MOBA_BLOCK
    tpb = seq // tm
    sub = tm // MOBA_BLOCK
    a = ATTN_WIDTH
    full = lambda shape: pl.BlockSpec(shape, lambda b, t: (0,) * len(shape))
    tok = lambda w: pl.BlockSpec((tm, w), lambda b, t: (b * tpb + t, 0))
    out_shape = (
        jax.ShapeDtypeStruct((bsz, nb, a, MOBA_BLOCK), BF16),
        jax.ShapeDtypeStruct((bsz, a, seq), F32),
        jax.ShapeDtypeStruct((m, a), BF16),
        jax.ShapeDtypeStruct((bsz, nb, 1, a), F32),
        jax.ShapeDtypeStruct((bsz, a, seq), F32),
        jax.ShapeDtypeStruct((bsz, nb, a, MOBA_BLOCK), BF16),
        jax.ShapeDtypeStruct((m, SSM_WIDTH), F32),
        jax.ShapeDtypeStruct((m, D_MODEL), BF16),
        jax.ShapeDtypeStruct((m, D_MODEL), BF16),
    )
    blk_t = pl.BlockSpec((1, sub, a, MOBA_BLOCK), lambda b, t: (b, t, 0, 0))
    lane_t = pl.BlockSpec((1, a, tm), lambda b, t: (b, 0, t))
    out_specs = (blk_t, lane_t, tok(a),
                 pl.BlockSpec((1, sub, 1, a), lambda b, t: (b, t, 0, 0)),
                 lane_t, blk_t, tok(SSM_WIDTH), tok(D_MODEL), tok(D_MODEL))
    return pl.pallas_call(
        _inproj_prompt_kernel,
        grid=(bsz, tpb),
        in_specs=[tok(D_MODEL), full((1, D_MODEL)), full(w_bf.shape),
                  pl.BlockSpec((tm, LANES), lambda b, t: (t, 0)),
                  pl.BlockSpec((tm, LANES), lambda b, t: (t, 0))],
        out_specs=out_specs,
        out_shape=out_shape,
        compiler_params=_cparams(("parallel", "parallel")),
        name="inproj_prompt",
    )(x2d, g, w_bf, cos, sin)


def _inproj_sample_kernel(x_ref, g_ref, w_ref, cos_ref, sin_ref,
                          q_ref, k_ref, v_ref, u_ref, sga_ref, sgs_ref):
    reps = ATTN_WIDTH // LANES
    cos = jnp.tile(cos_ref[...], (1, reps))
    sin = jnp.tile(sin_ref[...], (1, reps))
    q, k, v, u, sga, sgs = _inproj_core(x_ref[...], g_ref[...], w_ref, cos, sin, HEAD_DIM ** -0.5)
    q_ref[...] = q
    k_ref[...] = k
    v_ref[...] = v
    u_ref[...] = u
    sga_ref[...] = sga.astype(BF16)
    sgs_ref[...] = sgs.astype(BF16)


def _inproj_sample(x2d, g, w_bf, cos, sin):
    m = x2d.shape[0]
    a = ATTN_WIDTH
    full = lambda shape: pl.BlockSpec(shape, lambda i: (0,) * len(shape))
    shapes = [(m, a), (m, a), (m, a), (m, SSM_WIDTH), (m, D_MODEL), (m, D_MODEL)]
    dts = [F32, F32, F32, F32, BF16, BF16]
    return pl.pallas_call(
        _inproj_sample_kernel,
        grid=(1,),
        in_specs=[full(x2d.shape), full(g.shape), full(w_bf.shape), full(cos.shape), full(sin.shape)],
        out_specs=tuple(full(s) for s in shapes),
        out_shape=tuple(jax.ShapeDtypeStruct(s, d) for s, d in zip(shapes, dts)),
        compiler_params=_cparams(("arbitrary",)),
        name="inproj_sample",
    )(x2d, g, w_bf, cos, sin)


def _moba_prompt_kernel(qT_ref, k_ref, kmean_ref, vT_ref, o_ref, bias_ref, qz_s, m_s, acc_s):
    nb = qT_ref.shape[1]
    blk = MOBA_BLOCK
    nh = MOBA_HEADS_PER_STEP
    row2 = lax.broadcasted_iota(jnp.int32, (nh * HEAD_DIM, 1), 0)
    lane_km = lax.broadcasted_iota(jnp.int32, (1, nh * HEAD_DIM), 1)
    blk_row = lax.broadcasted_iota(jnp.int32, (nb, blk), 0)
    key_i = lax.broadcasted_iota(jnp.int32, (blk, blk), 0)
    qry_i = lax.broadcasted_iota(jnp.int32, (blk, blk), 1)
    causal = key_i <= qry_i
    in_head = [(row2 >= hh * HEAD_DIM) & (row2 < (hh + 1) * HEAD_DIM) for hh in range(nh)]
    hrows = [slice(hh * HEAD_DIM, (hh + 1) * HEAD_DIM) for hh in range(nh)]

    km_all = jnp.concatenate(
        [jnp.where((lane_km >= hh * HEAD_DIM) & (lane_km < (hh + 1) * HEAD_DIM), kmean_ref[0], 0.0)
         for hh in range(nh)], axis=0)
    for hh in range(nh):
        bias_ref[hh, 0] = jnp.full((nb, blk), NEG_BIG, F32)
    for i in range(1, nb):
        sb_all = jnp.dot(km_all, qT_ref[0, i].astype(F32), preferred_element_type=F32,
                         precision=lax.Precision.HIGHEST)
        for hh in range(nh):
            sb = jnp.where(blk_row < i, sb_all[hh * nb:(hh + 1) * nb], -jnp.inf)
            sel = jnp.zeros((nb, blk), dtype=jnp.bool_)
            for _r in range(min(MOBA_TOPK, i)):
                mx = jnp.max(sb, axis=0, keepdims=True)
                first = jnp.min(jnp.where(sb == mx, blk_row, nb), axis=0, keepdims=True)
                pick = (blk_row == first) & (mx > -jnp.inf)
                sel = sel | pick
                sb = jnp.where(pick, -jnp.inf, sb)
            bias_ref[hh, i] = jnp.where(sel, 0.0, NEG_BIG)

    n_items = nb // 2
    ones_rows = jnp.ones((MOBA_SUM_ROWS, 2 * blk), BF16)

    def couple(i, _):
        iq = (i, nb - 1 - i)
        n_first = (i + 1) // 2

        def diag_scores(x):
            q_pair = qT_ref[0, iq[x]]
            k_own = k_ref[pl.ds(pl.multiple_of(iq[x] * blk, blk), blk), :]
            out = []
            for hh in range(nh):
                qz = jnp.where(in_head[hh], q_pair, jnp.zeros_like(q_pair))
                qz_s[x, hh] = qz
                out.append(_dot(k_own, qz))
            return out

        def diag_absorb(x, s_own):
            for hh in range(nh):
                s = jnp.where(causal, s_own[hh], NEG_BIG)
                m0 = jnp.max(s, axis=0, keepdims=True)
                p = jnp.exp2(s - m0)
                m_s[x, hh] = m0
                vv = jnp.concatenate([vT_ref[0, iq[x], hrows[hh], :], ones_rows[:, :blk]], axis=0)
                acc_s[x, hh] = _dot(vv, p.astype(BF16))

        def item_params(k):
            first = k < n_first
            x = jnp.where(first, 0, 1)
            return x, jnp.where(first, iq[0], iq[1]), jnp.where(first, k, k - n_first)

        def item_scores(k):
            x, _, jp = item_params(k)
            kk = k_ref[pl.ds(pl.multiple_of(2 * jp * blk, blk), 2 * blk), :]
            out = []
            for hh in range(nh):
                s = _dot(kk, qz_s[x, hh])
                out.append((s, jnp.max(s[:blk], axis=0, keepdims=True), jnp.max(s[blk:], axis=0, keepdims=True)))
            return out

        def item_absorb(k, sc):
            x, qb, jp = item_params(k)
            j0 = 2 * jp
            for hh in range(nh):
                s, cma, cmb = sc[hh]
                ba = bias_ref[hh, qb, pl.ds(j0, 1), :]
                bb = bias_ref[hh, qb, pl.ds(j0 + 1, 1), :]
                m = m_s[x, hh]
                m_new = jnp.maximum(m, jnp.maximum(cma + ba, cmb + bb))
                alpha = jnp.exp2(m - m_new)
                pa = jnp.exp2(s[:blk] - (m_new - ba))
                pb = jnp.exp2(s[blk:] - (m_new - bb))
                m_s[x, hh] = m_new
                pp = jnp.concatenate([pa, pb], axis=0).astype(BF16)
                vv = jnp.concatenate([vT_ref[0, j0, hrows[hh], :], vT_ref[0, j0 + 1, hrows[hh], :]], axis=1)
                vv = jnp.concatenate([vv, ones_rows], axis=0)
                acc_s[x, hh] = alpha * acc_s[x, hh] + _dot(vv, pp)

        s_diag = [diag_scores(0), diag_scores(1)]
        pending = [item_scores(k) for k in range(min(MOBA_LOOKAHEAD, n_items))]
        diag_absorb(0, s_diag[0])
        diag_absorb(1, s_diag[1])
        for k in range(n_items):
            if k + MOBA_LOOKAHEAD < n_items:
                pending.append(item_scores(k + MOBA_LOOKAHEAD))
            item_absorb(k, pending[k])
        for x in range(2):
            oT = jnp.concatenate([acc_s[x, hh, 0:HEAD_DIM, :] / acc_s[x, hh, HEAD_DIM:HEAD_DIM + 1, :]
                                  for hh in range(nh)], axis=0)
            o_ref[pl.ds(pl.multiple_of(iq[x] * blk, blk), blk), :] = oT.T.astype(o_ref.dtype)
        return 0

    lax.fori_loop(0, nb // 2, couple, 0)


def _moba_prompt(qT, kbf, kmean, vTb, bsz, seq):
    nb = seq // MOBA_BLOCK
    hp = N_HEADS // MOBA_HEADS_PER_STEP
    pair = MOBA_HEADS_PER_STEP * HEAD_DIM
    return pl.pallas_call(
        _moba_prompt_kernel,
        grid=(bsz, hp),
        in_specs=[pl.BlockSpec((1, nb, pair, MOBA_BLOCK), lambda b, h: (b, 0, h, 0)),
                  pl.BlockSpec((seq, pair), lambda b, h: (b, h)),
                  pl.BlockSpec((1, nb, pair), lambda b, h: (b, 0, h)),
                  pl.BlockSpec((1, nb, pair, MOBA_BLOCK), lambda b, h: (b, 0, h, 0))],
        out_specs=pl.BlockSpec((seq, pair), lambda b, h: (b, h)),
        out_shape=jax.ShapeDtypeStruct((bsz * seq, ATTN_WIDTH), BF16),
        scratch_shapes=[pltpu.VMEM((MOBA_HEADS_PER_STEP, nb, nb, MOBA_BLOCK), F32),
                        pltpu.VMEM((2, MOBA_HEADS_PER_STEP, pair, MOBA_BLOCK), BF16),
                        pltpu.VMEM((2, MOBA_HEADS_PER_STEP, 1, MOBA_BLOCK), F32),
                        pltpu.VMEM((2, MOBA_HEADS_PER_STEP, HEAD_DIM + MOBA_SUM_ROWS, MOBA_BLOCK), F32)],
        compiler_params=_cparams(("parallel", "parallel")),
        name="moba_prompt",
    )(qT, kbf, kmean, vTb)


def _moba_paged_kernel(pt_ref, qcol_ref, kn_ref, vn_ref, ck_ref, cv_ref, o_ref,
                       kbuf, vbuf, s_ref, psel_ref, stat_ref, qb_ref, ksem, vsem):
    b = pl.program_id(0)
    nseq = pl.num_programs(0) - 1
    n_pages = kbuf.shape[1]
    nblk = n_pages // 2
    slot = b % 2
    n_v = N_HEADS * MOBA_TOPK * 2

    def k_copy(seq_i, sl, p):
        return pltpu.make_async_copy(ck_ref.at[pt_ref[seq_i, p]], kbuf.at[sl, p], ksem.at[sl])

    def start_k(seq_i, sl):
        def body(p, _):
            k_copy(seq_i, sl, p).start()
            return 0
        lax.fori_loop(0, n_pages, body, 0)

    def v_copy(page, h, r, par):
        return pltpu.make_async_copy(cv_ref.at[page, h], vbuf.at[h, r, par], vsem.at[0])

    @pl.when(b == 0)
    def _():
        start_k(0, 0)

    @pl.when(b < nseq)
    def _():
        def wait_body(p, _):
            k_copy(b, slot, p).wait()
            return 0
        lax.fori_loop(0, n_pages, wait_body, 0)

    @pl.when(b + 1 < nseq)
    def _():
        start_k(b + 1, 1 - slot)

    @pl.when(b >= 1)
    def _():
        for h in range(N_HEADS):
            for r in range(MOBA_TOPK):
                for par in range(2):
                    v_copy(0, h, r, par).wait()
        vn = vn_ref[0]
        for h in range(N_HEADS):
            acc = jnp.zeros((HEAD_DIM, PAGE_SIZE), F32)
            for r in range(MOBA_TOPK):
                for par in range(2):
                    acc = acc + vbuf[h, r, par] * psel_ref[h, 2 * r + par]
            p_own = stat_ref[0, h][:, 0:1]
            l = stat_ref[1, h][:, 0:1]
            o_h = jnp.sum(acc, axis=1, keepdims=True) + p_own * vn[:, h:h + 1]
            o_ref[0, :, h:h + 1] = o_h / l

    @pl.when(b < nseq)
    def _():
        qcol = qcol_ref[0]
        own = jnp.sum(qcol * kn_ref[0], axis=0, keepdims=True)
        blk_i = lax.broadcasted_iota(jnp.int32, (nblk, 1), 0)
        for h in range(N_HEADS):
            qb_ref[h] = jnp.broadcast_to(qcol[:, h:h + 1], (HEAD_DIM, PAGE_SIZE))
        for h in range(N_HEADS):
            def score_body(g, _):
                qb = qb_ref[h]
                for dn in range(PAGED_SCORE_UNROLL):
                    n = g * PAGED_SCORE_UNROLL + dn
                    for par in range(2):
                        kt = kbuf[slot, 2 * n + par, h]
                        s_ref[par, h, pl.ds(n, 1), :] = jnp.sum(kt * qb, axis=0, keepdims=True)
                return 0
            lax.fori_loop(0, nblk // PAGED_SCORE_UNROLL, score_body, 0)

        for h in range(N_HEADS):
            s0 = s_ref[0, h]
            s1 = s_ref[1, h]
            bs = jnp.sum(s0 + s1, axis=1, keepdims=True)
            sel = jnp.zeros((nblk, 1), dtype=jnp.bool_)
            picks = []
            for r in range(MOBA_TOPK):
                mx = jnp.max(bs, axis=0, keepdims=True)
                first = jnp.min(jnp.where(bs == mx, blk_i, nblk), axis=0, keepdims=True)
                pick = (blk_i == first) & (mx > -jnp.inf)
                sel = sel | pick
                bs = jnp.where(pick, -jnp.inf, bs)
                blk_id = jnp.max(jnp.where(pick, blk_i, 0))
                for par in range(2):
                    v_copy(pt_ref[b, 2 * blk_id + par], h, r, par).start()
                picks.append(blk_id)
            s_own = own[:, h:h + 1]
            sm0 = jnp.where(sel, s0, NEG_BIG)
            sm1 = jnp.where(sel, s1, NEG_BIG)
            mx = jnp.maximum(jnp.max(jnp.max(jnp.maximum(sm0, sm1), axis=1, keepdims=True),
                                     axis=0, keepdims=True), s_own)
            p0 = jnp.exp(sm0 - mx)
            p1 = jnp.exp(sm1 - mx)
            p_own = jnp.exp(s_own - mx)
            l = jnp.sum(jnp.sum(p0 + p1, axis=1, keepdims=True), axis=0, keepdims=True) + p_own
            s_ref[0, h] = p0
            s_ref[1, h] = p1
            for r in range(MOBA_TOPK):
                for par in range(2):
                    psel_ref[h, 2 * r + par] = s_ref[par, h, pl.ds(picks[r], 1), :]
            stat_ref[0, h] = jnp.broadcast_to(p_own, (1, PAGE_SIZE))
            stat_ref[1, h] = jnp.broadcast_to(l, (1, PAGE_SIZE))


def _moba_paged(page_table, qcol, kncol, vncol, cache_kT, cache_vT):
    nseq, n_pages = page_table.shape
    cur = pl.BlockSpec((1, HEAD_DIM, N_HEADS), lambda b, pt: (jnp.minimum(b, nseq - 1), 0, 0))
    prev = pl.BlockSpec((1, HEAD_DIM, N_HEADS), lambda b, pt: (jnp.maximum(b - 1, 0), 0, 0))
    any_spec = pl.BlockSpec(memory_space=pl.ANY)
    grid_spec = pltpu.PrefetchScalarGridSpec(
        num_scalar_prefetch=1,
        grid=(nseq + 1,),
        in_specs=[cur, cur, prev, any_spec, any_spec],
        out_specs=prev,
        scratch_shapes=[
            pltpu.VMEM((2, n_pages, N_HEADS, HEAD_DIM, PAGE_SIZE), F32),
            pltpu.VMEM((N_HEADS, MOBA_TOPK, 2, HEAD_DIM, PAGE_SIZE), F32),
            pltpu.VMEM((2, N_HEADS, n_pages // 2, PAGE_SIZE), F32),
            pltpu.VMEM((N_HEADS, 2 * MOBA_TOPK, 1, PAGE_SIZE), F32),
            pltpu.VMEM((2, N_HEADS, 1, PAGE_SIZE), F32),
            pltpu.VMEM((N_HEADS, HEAD_DIM, PAGE_SIZE), F32),
            pltpu.SemaphoreType.DMA((2,)),
            pltpu.SemaphoreType.DMA((1,)),
        ])
    return pl.pallas_call(
        _moba_paged_kernel,
        grid_spec=grid_spec,
        out_shape=jax.ShapeDtypeStruct((nseq, HEAD_DIM, N_HEADS), F32),
        compiler_params=_cparams(("arbitrary",)),
        name="moba_paged",
    )(page_table, qcol, kncol, vncol, cache_kT, cache_vT)


def _ssm_tables(a_re, a_im, log_dt, b_re, b_im, c_re, c_im, chunk, n_pow):
    t_ = chunk
    lam = lax.complex(a_re.astype(F32), a_im.astype(F32))
    ldt = lam * jnp.exp(log_dt.astype(F32))[:, None]
    a_bar = jnp.exp(ldt)
    b_bar = ((a_bar - 1.0) / lam)[..., None] * lax.complex(b_re.astype(F32), b_im.astype(F32))
    c_c = lax.complex(c_re.astype(F32), c_im.astype(F32))
    taus = jnp.arange(t_ + 1, dtype=F32).astype(jnp.complex64)
    apow = jnp.exp(ldt[None] * taus[:, None, None])
    gq, ssg, c_, p_ = GROUPS_PER_SG, N_SG, SSM_GROUP, SSM_STATE
    c4 = c_c.reshape(ssg, gq, c_, p_)
    b4 = b_bar.reshape(ssg, gq, p_, c_)
    ap4 = apow.reshape(t_ + 1, ssg, gq, p_)

    kc = jnp.einsum('sjcp,tsjp,sjpd->sjdtc', c4, ap4[:t_], b4).real
    kc = kc.reshape(ssg, LANES, t_ * c_)
    mst = jnp.einsum('tsjp,sjpd->stjdp', ap4[:t_][::-1], b4)
    mc = jnp.concatenate([mst.real, mst.imag], axis=-1).reshape(ssg, t_ * LANES, 2 * p_)
    nout = jnp.einsum('sjcp,tsjp->sjptc', c4, ap4[1:])
    nc = jnp.concatenate([nout.real, -nout.imag], axis=1).reshape(ssg, 2 * SG_STATE, t_ * c_)
    rs = jnp.arange(1, n_pow + 1, dtype=F32).astype(jnp.complex64)
    ap = jnp.exp((ldt * t_)[None] * rs[:, None, None])
    ap = ap.reshape(n_pow, ssg, SG_STATE).transpose(1, 0, 2)
    a_pow = jnp.concatenate([ap.real, ap.imag], axis=-1)
    a1 = a_bar.reshape(ssg, 1, SG_STATE)
    a_one = jnp.concatenate([a1.real, a1.imag], axis=-1)
    return kc, mc, nc, a_pow, a_one


def _expand_ssm_weights(kc_ref, mc_ref, nc_ref, wt_s, ws_s, wo_s):
    t_ = SSM_CHUNK
    kw = t_ * LANES
    lg = lambda n: n.bit_length() - 1
    r = lax.broadcasted_iota(jnp.int32, (LANES, kw), 0)
    q = lax.broadcasted_iota(jnp.int32, (LANES, kw), 1)
    sh_c, sh_p = lg(SSM_GROUP), lg(SSM_STATE)
    ecol = jnp.where(((r >> sh_c) == (q >> lg(LANES))) & ((r & (SSM_GROUP - 1)) == (q & (SSM_GROUP - 1))),
                     1.0, 0.0).astype(BF16)
    e2 = jnp.where(((r >> sh_p) == (q >> lg(SG_STATE))) & ((r & (SSM_STATE - 1)) == (q & (SSM_STATE - 1))),
                   1.0, 0.0).astype(BF16)
    col_c = (q & (LANES - 1)) >> sh_c
    taps = jnp.where((r >> sh_c) == col_c, _dot(kc_ref[0].astype(BF16), ecol), 0.0).astype(BF16)
    blank = jnp.zeros((LANES, LANES), BF16)
    for s in range(t_):
        for t in range(t_):
            wt_s[s * LANES:(s + 1) * LANES, t * LANES:(t + 1) * LANES] = (
                taps[:, (t - s) * LANES:(t - s + 1) * LANES] if t >= s else blank)
    rr = lax.broadcasted_iota(jnp.int32, (kw, kw), 0)
    qq = lax.broadcasted_iota(jnp.int32, (kw, kw), 1)
    row_c = (rr & (LANES - 1)) >> sh_c
    row_p = (rr & (SG_STATE - 1)) >> sh_p
    colq_c = (qq & (LANES - 1)) >> sh_c
    colq_p = (qq & (SG_STATE - 1)) >> sh_p
    ws_s[...] = jnp.where(row_c == colq_p, _dot(mc_ref[0].astype(BF16), e2), 0.0).astype(BF16)
    wo_s[...] = jnp.where(row_p == colq_c, _dot(nc_ref[0].astype(BF16), ecol), 0.0).astype(BF16)


def _ssm_prompt_kernel(u_ref, kc_ref, mc_ref, nc_ref, ap_ref, d_ref, h0_ref,
                       y_ref, hf_ref, uc_ref, x_ref, hin_ref, wt_s, ws_s, wo_s):
    t_ = SSM_CHUNK
    nc = uc_ref.shape[0]
    ns = SG_STATE

    @pl.when(pl.program_id(1) == 0)
    def _():
        _expand_ssm_weights(kc_ref, mc_ref, nc_ref, wt_s, ws_s, wo_s)

    for t in range(t_):
        uc_ref[:, t * LANES:(t + 1) * LANES] = u_ref[pl.ds(t, nc, stride=t_), :].astype(BF16)
    uc = uc_ref[...]
    x_ref[...] = _dot(uc, ws_s[...])

    ap = ap_ref[0]
    apr, api = ap[:, :ns], ap[:, ns:]
    row = lax.broadcasted_iota(jnp.int32, (SUBLANES, 1), 0)

    def tile_body(k, carry):
        hr, hi = carry
        rows = pl.ds(k * SUBLANES, SUBLANES)
        xr = x_ref[rows, 0:ns]
        xi = x_ref[rows, ns:2 * ns]
        for d in (1, 2, 4):
            ar, ai = apr[d - 1:d], api[d - 1:d]
            sr = jnp.where(row >= d, pltpu.roll(xr, d, 0), 0.0)
            si = jnp.where(row >= d, pltpu.roll(xi, d, 0), 0.0)
            xr, xi = xr + ar * sr - ai * si, xi + ar * si + ai * sr
        outr = xr + apr * hr - api * hi
        outi = xi + apr * hi + api * hr
        hin_ref[rows, 0:ns] = jnp.where(row >= 1, pltpu.roll(outr, 1, 0), hr)
        hin_ref[rows, ns:2 * ns] = jnp.where(row >= 1, pltpu.roll(outi, 1, 0), hi)
        return outr[SUBLANES - 1:SUBLANES], outi[SUBLANES - 1:SUBLANES]

    y_intra = _dot(uc, wt_s[...])
    h0 = h0_ref[0, 0]
    carry = (h0[:, :ns], h0[:, ns:])
    for k in range(nc // SUBLANES):
        carry = tile_body(k, carry)
    hf_ref[0, 0] = jnp.concatenate(carry, axis=1)

    y = y_intra + _dot(hin_ref[...].astype(BF16), wo_s[...])
    dv = d_ref[...]
    for t in range(t_):
        rows = pl.ds(t, nc, stride=t_)
        y_ref[rows, :] = y[:, t * LANES:(t + 1) * LANES] + dv * u_ref[rows, :]


def _ssm_prompt(u, kc, mc, nc_tab, a_pow, d_row, h0, bsz, seq):
    t_ = SSM_CHUNK
    nc = seq // t_
    kw = t_ * LANES
    assert kc.shape[2] == LANES and 2 * SG_STATE == kw
    wspec = lambda shape: pl.BlockSpec((1,) + shape, lambda s, b: (s, 0, 0))
    return pl.pallas_call(
        _ssm_prompt_kernel,
        grid=(N_SG, bsz),
        in_specs=[pl.BlockSpec((seq, LANES), lambda s, b: (b, s)),
                  wspec(kc.shape[1:]), wspec(mc.shape[1:]), wspec(nc_tab.shape[1:]),
                  wspec((SUBLANES, 2 * SG_STATE)),
                  pl.BlockSpec((1, LANES), lambda s, b: (0, s)),
                  pl.BlockSpec((1, 1, 1, 2 * SG_STATE), lambda s, b: (b, s, 0, 0))],
        out_specs=(pl.BlockSpec((seq, LANES), lambda s, b: (b, s)),
                   pl.BlockSpec((1, 1, 1, 2 * SG_STATE), lambda s, b: (b, s, 0, 0))),
        out_shape=(jax.ShapeDtypeStruct((bsz * seq, SSM_WIDTH), F32),
                   jax.ShapeDtypeStruct((bsz, N_SG, 1, 2 * SG_STATE), F32)),
        scratch_shapes=[pltpu.VMEM((nc, kw), BF16),
                        pltpu.VMEM((nc, 2 * SG_STATE), F32),
                        pltpu.VMEM((nc, 2 * SG_STATE), F32),
                        pltpu.VMEM((kw, kw), BF16),
                        pltpu.VMEM((kw, 2 * SG_STATE), BF16),
                        pltpu.VMEM((2 * SG_STATE, kw), BF16)],
        compiler_params=_cparams(("arbitrary", "arbitrary")),
        name="ssm_prompt",
    )(u, kc, mc, nc_tab, a_pow, d_row, h0)


def _ssm_step_kernel(u_ref, kc_ref, mc_ref, nc_ref, a1_ref, d_ref, h0_ref, y_ref, hf_ref):
    ns = SG_STATE
    kw = 2 * ns
    lg = lambda n: n.bit_length() - 1
    sh_c, sh_p = lg(SSM_GROUP), lg(SSM_STATE)
    iota = lambda shape, d: lax.broadcasted_iota(jnp.int32, shape, d)
    r1, q1 = iota((LANES, LANES), 0), iota((LANES, LANES), 1)
    e_c = jnp.where((r1 < SSM_GROUP) & ((r1 & (SSM_GROUP - 1)) == (q1 & (SSM_GROUP - 1))), 1.0, 0.0).astype(BF16)
    r2, q2 = iota((LANES, kw), 0), iota((LANES, kw), 1)
    e_p = jnp.where(((r2 >> sh_p) == (q2 >> lg(ns))) & ((r2 & (SSM_STATE - 1)) == (q2 & (SSM_STATE - 1))),
                    1.0, 0.0).astype(BF16)
    same_t = (r1 >> sh_c) == (q1 >> sh_c)
    same_s = (r2 >> sh_c) == ((q2 & (ns - 1)) >> sh_p)
    r3, q3 = iota((kw, LANES), 0), iota((kw, LANES), 1)
    same_o = ((r3 & (ns - 1)) >> sh_p) == (q3 >> sh_c)
    last = (SSM_CHUNK - 1) * LANES
    for s in range(N_SG):
        wt = jnp.where(same_t, _dot(kc_ref[s].astype(BF16), e_c), 0.0).astype(BF16)
        ws = jnp.where(same_s, _dot(mc_ref[s, last:last + LANES, :].astype(BF16), e_p), 0.0).astype(BF16)
        wo = jnp.where(same_o, _dot(nc_ref[s].astype(BF16), e_c), 0.0).astype(BF16)
        us = u_ref[:, s * LANES:(s + 1) * LANES]
        ub = us.astype(BF16)
        h0 = h0_ref[s]
        x = _dot(ub, ws)
        ar, ai = a1_ref[s, :, :ns], a1_ref[s, :, ns:]
        hr, hi = h0[:, :ns], h0[:, ns:]
        hf_ref[s] = jnp.concatenate([x[:, :ns] + ar * hr - ai * hi,
                                     x[:, ns:] + ar * hi + ai * hr], axis=1)
        y = _dot(ub, wt) + _dot(h0.astype(BF16), wo)
        y_ref[:, s * LANES:(s + 1) * LANES] = y + d_ref[:, s * LANES:(s + 1) * LANES] * us


def _ssm_step(u, kc, mc, nc_tab, a_one, d_row, h0):
    m = u.shape[0]
    full = lambda a: pl.BlockSpec(a.shape, lambda i: (0,) * a.ndim)
    args = (u, kc, mc, nc_tab, a_one, d_row, h0)
    shapes = [(m, SSM_WIDTH), (N_SG, m, 2 * SG_STATE)]
    return pl.pallas_call(
        _ssm_step_kernel,
        grid=(1,),
        in_specs=[full(a) for a in args],
        out_specs=tuple(pl.BlockSpec(s, lambda i, n=len(s): (0,) * n) for s in shapes),
        out_shape=tuple(jax.ShapeDtypeStruct(s, F32) for s in shapes),
        compiler_params=_cparams(("arbitrary",)),
        name="ssm_step",
    )(*args)


FFN_SPLIT = 2
FFN_CHUNK = FFN_HIDDEN // FFN_SPLIT


def _post_kernel(x_ref, o_ref, y_ref, sga_ref, sgs_ref, wap_ref, wglu_ref, bglu_ref, wsp_ref,
                 wout_ref, nffn_ref, wfi_ref, wfo_ref, nfin_ref, out_ref):
    z = jax.nn.gelu(y_ref[...])
    z = z * jax.nn.sigmoid(_dot(z.astype(BF16), wglu_ref[...]) + bglu_ref[...])
    ssm_out = _dot(z.astype(BF16), wsp_ref[...])
    attn_out = _dot(o_ref[...].astype(BF16), wap_ref[...])
    merged = sga_ref[...].astype(F32) * attn_out + sgs_ref[...].astype(F32) * ssm_out
    x1 = x_ref[...] + _dot(merged.astype(BF16), wout_ref[...])
    hf = _rms(x1, nffn_ref[...]).astype(BF16)
    acc = x1
    for c in range(FFN_SPLIT):
        lo = c * FFN_CHUNK
        a = _dot(hf, wfi_ref[:, lo:lo + FFN_CHUNK])
        g = _dot(hf, wfi_ref[:, FFN_HIDDEN + lo:FFN_HIDDEN + lo + FFN_CHUNK])
        act = (jax.nn.silu(a) * g).astype(BF16)
        acc = acc + _dot(act, wfo_ref[lo:lo + FFN_CHUNK, :])
    out_ref[...] = _rms(acc, nfin_ref[...])


def _post(x2d, o, y, sga, sgs, wap, wglu, bglu, wsp, wout, nffn, wfi, wfo, nfin, tm):
    m = x2d.shape[0]
    tok = lambda w: pl.BlockSpec((tm, w), lambda i: (i, 0))
    const = lambda a: pl.BlockSpec(a.shape, lambda i: (0,) * a.ndim, pipeline_mode=pl.Buffered(1))
    weights = (wap, wglu, bglu, wsp, wout, nffn, wfi, wfo, nfin)
    return pl.pallas_call(
        _post_kernel,
        grid=(m // tm,),
        in_specs=[tok(D_MODEL), tok(ATTN_WIDTH), tok(SSM_WIDTH), tok(D_MODEL), tok(D_MODEL)]
                 + [const(w) for w in weights],
        out_specs=tok(D_MODEL),
        out_shape=jax.ShapeDtypeStruct((m, D_MODEL), F32),
        compiler_params=_cparams(("parallel",)),
        name="post",
    )(x2d, o, y, sga, sgs, *weights)


def _rope_tables(pos):
    half = HEAD_DIM // 2
    inv = jnp.power(jnp.float32(ROPE_THETA), -2.0 * jnp.arange(half, dtype=F32) / HEAD_DIM)
    ang = pos.astype(F32)[:, None] * inv[None, :]
    cos, sin = jnp.cos(ang), jnp.sin(ang)
    reps = LANES // HEAD_DIM
    return (jnp.tile(jnp.concatenate([cos, cos], axis=1), (1, reps)),
            jnp.tile(jnp.concatenate([-sin, sin], axis=1), (1, reps)))


def _leaf_from_T(xT, bsz, seq):
    return xT.reshape(bsz, N_HEADS, HEAD_DIM, seq).transpose(0, 3, 1, 2)[None]


def _state_in(re, im):
    n = re.shape[0]
    h = jnp.concatenate([re.reshape(n, N_SG, SG_STATE), im.reshape(n, N_SG, SG_STATE)], axis=-1)
    return h.transpose(1, 0, 2)


def _state_out(h):
    n = h.shape[0]
    return (h[..., :SG_STATE].reshape(1, n, SSM_GROUPS, SSM_STATE),
            h[..., SG_STATE:].reshape(1, n, SSM_GROUPS, SSM_STATE))


def kernel(x_prompt, x_sample, cache_k, cache_v, state_ssm_re, state_ssm_im, page_table, norm_mix, w_in,
           w_attn_proj, ssm_a_re, ssm_a_im, ssm_log_dt, ssm_b_re, ssm_b_im, ssm_c_re, ssm_c_im, ssm_d, w_glu,
           b_glu, w_ssm_proj, w_out, norm_ffn, w_ffn_in, w_ffn_out, norm_final):
    assert w_in.shape[0] == 1, "single layer"
    bsz, seq = x_prompt.shape[:2]
    nseq = x_sample.shape[0]
    past_len = page_table.shape[1] * PAGE_SIZE
    assert seq % MOBA_BLOCK == 0 and past_len % MOBA_BLOCK == 0 and x_sample.shape[1] == 1

    w_in_bf = w_in[0].astype(BF16)
    post_w = (w_attn_proj[0].astype(BF16), w_glu[0].astype(BF16), b_glu, w_ssm_proj[0].astype(BF16),
              w_out[0].astype(BF16), norm_ffn, w_ffn_in[0].astype(BF16), w_ffn_out[0].astype(BF16),
              norm_final[None])
    ssm_p = (ssm_a_re[0], ssm_a_im[0], ssm_log_dt[0], ssm_b_re[0], ssm_b_im[0], ssm_c_re[0], ssm_c_im[0])

    cos_p, sin_p = _rope_tables(jnp.arange(seq, dtype=jnp.int32))
    xp2 = x_prompt.reshape(bsz * seq, D_MODEL)
    qT, kT, kbf, kmean, vT, vTb, u_p, sga_p, sgs_p = _inproj_prompt(
        xp2, norm_mix, w_in_bf, cos_p, sin_p, bsz, seq, tm=512)
    o_p = _moba_prompt(qT, kbf, kmean.reshape(bsz, seq // MOBA_BLOCK, ATTN_WIDTH), vTb, bsz, seq)
    kc, mc, nc_tab, ap, a_one = _ssm_tables(*ssm_p, chunk=SSM_CHUNK, n_pow=SUBLANES)
    h0_p = jnp.zeros((bsz, N_SG, 1, 2 * SG_STATE), F32)
    y_p, hf_p = _ssm_prompt(u_p, kc, mc, nc_tab, ap, ssm_d, h0_p, bsz, seq)
    y_prompt = _post(xp2, o_p, y_p, sga_p, sgs_p, *post_w, tm=512).reshape(bsz, seq, D_MODEL)
    new_ssm_re_p, new_ssm_im_p = _state_out(hf_p.reshape(bsz, N_SG, 2 * SG_STATE))

    cos_s, sin_s = _rope_tables(jnp.full((1,), past_len, dtype=jnp.int32))
    xs2 = x_sample.reshape(nseq, D_MODEL)
    q_s, k_s, v_s, u_s, sga_s, sgs_s = _inproj_sample(xs2, norm_mix, w_in_bf, cos_s, sin_s)
    cache_kT = cache_k[0].transpose(0, 2, 3, 1)
    cache_vT = cache_v[0].transpose(0, 2, 3, 1)
    dh = lambda t: t.reshape(nseq, N_HEADS, HEAD_DIM).transpose(0, 2, 1)
    o_s = _moba_paged(page_table, dh(q_s), dh(k_s), dh(v_s), cache_kT, cache_vT).transpose(0, 2, 1)
    h0_s = _state_in(state_ssm_re[0], state_ssm_im[0])
    y_s, hf_s = _ssm_step(u_s, kc, mc, nc_tab, a_one, ssm_d, h0_s)
    y_sample = _post(xs2, o_s.reshape(nseq, ATTN_WIDTH), y_s, sga_s, sgs_s, *post_w, tm=nseq)
    new_ssm_re_s, new_ssm_im_s = _state_out(hf_s.transpose(1, 0, 2))

    return (y_prompt, y_sample.reshape(nseq, 1, D_MODEL),
            _leaf_from_T(kT, bsz, seq), _leaf_from_T(vT, bsz, seq), new_ssm_re_p, new_ssm_im_p,
            k_s.reshape(1, nseq, 1, N_HEADS, HEAD_DIM), v_s.reshape(1, nseq, 1, N_HEADS, HEAD_DIM),
            new_ssm_re_s, new_ssm_im_s)
```

```python
import functools
import math

import jax
import jax.numpy as jnp
from jax import lax
from jax.experimental import pallas as pl
from jax.experimental.pallas import tpu as pltpu

F32 = jnp.float32
BF16 = jnp.bfloat16

D_MODEL = 1024
N_HEADS = 8
HEAD_DIM = 64
ATTN_WIDTH = N_HEADS * HEAD_DIM
MOBA_BLOCK = 256
MOBA_TOPK = 3
ROPE_THETA = 10000.0
SSM_WIDTH = 512
SSM_GROUP = 16
SSM_GROUPS = 32
SSM_STATE = 64
FFN_HIDDEN = 2816
RMS_EPS = 1e-6
PAGE_SIZE = 128

LANES = 128
SUBLANES = 8
BF16_SUBLANES = 16
GROUPS_PER_SG = LANES // SSM_GROUP
N_SG = SSM_GROUPS // GROUPS_PER_SG
SG_STATE = GROUPS_PER_SG * SSM_STATE
SSM_CHUNK = 8
NEG_BIG = -1e30
MOBA_HEADS_PER_STEP = 4
MOBA_LOOKAHEAD = 2
LOG2E = math.log2(math.e)
MOBA_SUM_ROWS = 16
PAGED_SCORE_UNROLL = 4
VMEM_LIMIT = 56 * 1024 * 1024


def _cparams(sem):
    return pltpu.CompilerParams(dimension_semantics=sem, vmem_limit_bytes=VMEM_LIMIT)


def _dot(a, b):
    return jnp.dot(a, b, preferred_element_type=F32)


def _rms(x, g):
    return x * lax.rsqrt(jnp.mean(x * x, axis=-1, keepdims=True) + RMS_EPS) * g


def _inproj_core(x, g, w_ref, cos, sin, q_scale):
    h = _rms(x, g).astype(BF16)
    lane = lax.broadcasted_iota(jnp.int32, (1, ATTN_WIDTH), 1)
    first_half = (lane % HEAD_DIM) < (HEAD_DIM // 2)

    def rot(t):
        partner = jnp.where(first_half,
                            pltpu.roll(t, ATTN_WIDTH - HEAD_DIM // 2, 1),
                            pltpu.roll(t, HEAD_DIM // 2, 1))
        return t * cos + partner * sin

    a = ATTN_WIDTH
    q = rot(_dot(h, w_ref[:, 0:a])) * q_scale
    k = rot(_dot(h, w_ref[:, a:2 * a]))
    v = _dot(h, w_ref[:, 2 * a:3 * a])
    u = _dot(h, w_ref[:, 3 * a:3 * a + SSM_WIDTH])
    o = 3 * a + SSM_WIDTH
    sga = jax.nn.sigmoid(_dot(h, w_ref[:, o:o + D_MODEL]))
    sgs = jax.nn.sigmoid(_dot(h, w_ref[:, o + D_MODEL:o + 2 * D_MODEL]))
    return q, k, v, u, sga, sgs


def _inproj_prompt_kernel(x_ref, g_ref, w_ref, cos_ref, sin_ref, *rest):
    n_cast = (len(rest) - 9) // 2
    cast_in = rest[:n_cast]
    qT_ref, kT_ref, kbf_ref, kmean_ref, vT_ref, vTb_ref, u_ref, sga_ref, sgs_ref = rest[n_cast:n_cast + 9]
    cast_out = rest[n_cast + 9:]
    for wi_ref, wo_ref in zip(cast_in, cast_out):
        wo_ref[...] = wi_ref[...].astype(BF16)
    reps = ATTN_WIDTH // LANES
    cos = jnp.tile(cos_ref[...], (1, reps))
    sin = jnp.tile(sin_ref[...], (1, reps))
    q, k, v, u, sga, sgs = _inproj_core(x_ref[...], g_ref[...], w_ref, cos, sin, LOG2E * HEAD_DIM ** -0.5)
    tm = q.shape[0]
    kT_ref[0] = k.T
    vT_ref[0] = v.T
    kbf_ref[...] = k.astype(BF16)
    for s in range(tm // MOBA_BLOCK):
        rows = slice(s * MOBA_BLOCK, (s + 1) * MOBA_BLOCK)
        qT_ref[0, s] = q[rows].T.astype(BF16)
        vTb_ref[0, s] = v[rows].T.astype(BF16)
        kmean_ref[0, s] = jnp.mean(k[rows], axis=0, keepdims=True)
    u_ref[...] = u
    sga_ref[...] = sga.astype(BF16)
    sgs_ref[...] = sgs.astype(BF16)


def _inproj_prompt(x2d, g, w_bf, cos, sin, cast_ws, bsz, seq, tm):
    m = bsz * seq
    nb = seq // MOBA_BLOCK
    tpb = seq // tm
    sub = tm // MOBA_BLOCK
    a = ATTN_WIDTH
    full = lambda shape: pl.BlockSpec(shape, lambda b, t: (0,) * len(shape))
    tok = lambda w: pl.BlockSpec((tm, w), lambda b, t: (b * tpb + t, 0))
    out_shape = (
        jax.ShapeDtypeStruct((bsz, nb, a, MOBA_BLOCK), BF16),
        jax.ShapeDtypeStruct((bsz, a, seq), F32),
        jax.ShapeDtypeStruct((m, a), BF16),
        jax.ShapeDtypeStruct((bsz, nb, 1, a), F32),
        jax.ShapeDtypeStruct((bsz, a, seq), F32),
        jax.ShapeDtypeStruct((bsz, nb, a, MOBA_BLOCK), BF16),
        jax.ShapeDtypeStruct((m, SSM_WIDTH), F32),
        jax.ShapeDtypeStruct((m, D_MODEL), BF16),
        jax.ShapeDtypeStruct((m, D_MODEL), BF16),
    )
    blk_t = pl.BlockSpec((1, sub, a, MOBA_BLOCK), lambda b, t: (b, t, 0, 0))
    lane_t = pl.BlockSpec((1, a, tm), lambda b, t: (b, 0, t))
    out_specs = (blk_t, lane_t, tok(a),
                 pl.BlockSpec((1, sub, 1, a), lambda b, t: (b, t, 0, 0)),
                 lane_t, blk_t, tok(SSM_WIDTH), tok(D_MODEL), tok(D_MODEL))
    n_steps = bsz * tpb
    cast_specs = []
    for w in cast_ws:
        rows = w.shape[0]
        nblk = n_steps
        while rows % nblk or (rows // nblk) % BF16_SUBLANES:
            nblk //= 2
        cast_specs.append(pl.BlockSpec((rows // nblk, w.shape[1]),
                                       lambda b, t, n=nblk: (jnp.minimum(b * tpb + t, n - 1), 0)))
    return pl.pallas_call(
        _inproj_prompt_kernel,
        grid=(bsz, tpb),
        in_specs=[tok(D_MODEL), full((1, D_MODEL)), full(w_bf.shape),
                  pl.BlockSpec((tm, LANES), lambda b, t: (t, 0)),
                  pl.BlockSpec((tm, LANES), lambda b, t: (t, 0))] + cast_specs,
        out_specs=out_specs + tuple(cast_specs),
        out_shape=out_shape + tuple(jax.ShapeDtypeStruct(w.shape, BF16) for w in cast_ws),
        compiler_params=_cparams(("arbitrary", "arbitrary")),
        name="inproj_prompt",
    )(x2d, g, w_bf, cos, sin, *cast_ws)


def _inproj_sample_kernel(x_ref, g_ref, w_ref, cos_ref, sin_ref,
                          q_ref, k_ref, v_ref, u_ref, sga_ref, sgs_ref, wb_ref):
    step = ATTN_WIDTH
    for c in range(w_ref.shape[1] // step):
        wb_ref[:, c * step:(c + 1) * step] = w_ref[:, c * step:(c + 1) * step].astype(BF16)
    reps = ATTN_WIDTH // LANES
    cos = jnp.tile(cos_ref[...], (1, reps))
    sin = jnp.tile(sin_ref[...], (1, reps))
    q, k, v, u, sga, sgs = _inproj_core(x_ref[...], g_ref[...], wb_ref, cos, sin, HEAD_DIM ** -0.5)
    q_ref[...] = q
    k_ref[...] = k
    v_ref[...] = v
    u_ref[...] = u
    sga_ref[...] = sga.astype(BF16)
    sgs_ref[...] = sgs.astype(BF16)


def _inproj_sample(x2d, g, w_f32, cos, sin):
    m = x2d.shape[0]
    a = ATTN_WIDTH
    full = lambda shape, **kw: pl.BlockSpec(shape, lambda i: (0,) * len(shape), **kw)
    shapes = [(m, a), (m, a), (m, a), (m, SSM_WIDTH), (m, D_MODEL), (m, D_MODEL), w_f32.shape]
    dts = [F32, F32, F32, F32, BF16, BF16, BF16]
    return pl.pallas_call(
        _inproj_sample_kernel,
        grid=(1,),
        in_specs=[full(x2d.shape), full(g.shape), full(w_f32.shape, pipeline_mode=pl.Buffered(1)),
                  full(cos.shape), full(sin.shape)],
        out_specs=tuple(full(s) for s in shapes),
        out_shape=tuple(jax.ShapeDtypeStruct(s, d) for s, d in zip(shapes, dts)),
        compiler_params=_cparams(("arbitrary",)),
        name="inproj_sample",
    )(x2d, g, w_f32, cos, sin)


def _moba_prompt_kernel(qT_ref, k_ref, kmean_ref, vT_ref, o_ref, bias_ref, qz_s, m_s, acc_s):
    nb = qT_ref.shape[1]
    blk = MOBA_BLOCK
    nh = MOBA_HEADS_PER_STEP
    row2 = lax.broadcasted_iota(jnp.int32, (nh * HEAD_DIM, 1), 0)
    lane_km = lax.broadcasted_iota(jnp.int32, (1, nh * HEAD_DIM), 1)
    blk_row = lax.broadcasted_iota(jnp.int32, (nb, blk), 0)
    key_i = lax.broadcasted_iota(jnp.int32, (blk, blk), 0)
    qry_i = lax.broadcasted_iota(jnp.int32, (blk, blk), 1)
    causal = key_i <= qry_i
    in_head = [(row2 >= hh * HEAD_DIM) & (row2 < (hh + 1) * HEAD_DIM) for hh in range(nh)]
    hrows = [slice(hh * HEAD_DIM, (hh + 1) * HEAD_DIM) for hh in range(nh)]

    km_all = jnp.concatenate(
        [jnp.where((lane_km >= hh * HEAD_DIM) & (lane_km < (hh + 1) * HEAD_DIM), kmean_ref[0], 0.0)
         for hh in range(nh)], axis=0)
    for hh in range(nh):
        bias_ref[hh, 0] = jnp.full((nb, blk), NEG_BIG, F32)
    for i in range(1, nb):
        sb_all = jnp.dot(km_all, qT_ref[0, i].astype(F32), preferred_element_type=F32,
                         precision=lax.Precision.HIGHEST)
        for hh in range(nh):
            sb = jnp.where(blk_row < i, sb_all[hh * nb:(hh + 1) * nb], -jnp.inf)
            sel = jnp.zeros((nb, blk), dtype=jnp.bool_)
            for _r in range(min(MOBA_TOPK, i)):
                mx = jnp.max(sb, axis=0, keepdims=True)
                first = jnp.min(jnp.where(sb == mx, blk_row, nb), axis=0, keepdims=True)
                pick = (blk_row == first) & (mx > -jnp.inf)
                sel = sel | pick
                sb = jnp.where(pick, -jnp.inf, sb)
            bias_ref[hh, i] = jnp.where(sel, 0.0, NEG_BIG)

    n_items = nb // 2
    ones_rows = jnp.ones((MOBA_SUM_ROWS, 2 * blk), BF16)

    def couple(i, _):
        iq = (i, nb - 1 - i)
        n_first = (i + 1) // 2

        def diag_scores(x):
            q_pair = qT_ref[0, iq[x]]
            k_own = k_ref[pl.ds(pl.multiple_of(iq[x] * blk, blk), blk), :]
            out = []
            for hh in range(nh):
                qz = jnp.where(in_head[hh], q_pair, jnp.zeros_like(q_pair))
                qz_s[x, hh] = qz
                out.append(_dot(k_own, qz))
            return out

        def diag_absorb(x, s_own):
            for hh in range(nh):
                s = jnp.where(causal, s_own[hh], NEG_BIG)
                m0 = jnp.max(s, axis=0, keepdims=True)
                p = jnp.exp2(s - m0)
                m_s[x, hh] = m0
                vv = jnp.concatenate([vT_ref[0, iq[x], hrows[hh], :], ones_rows[:, :blk]], axis=0)
                acc_s[x, hh] = _dot(vv, p.astype(BF16))

        def item_params(k):
            first = k < n_first
            x = jnp.where(first, 0, 1)
            return x, jnp.where(first, iq[0], iq[1]), jnp.where(first, k, k - n_first)

        def item_scores(k):
            x, _, jp = item_params(k)
            kk = k_ref[pl.ds(pl.multiple_of(2 * jp * blk, blk), 2 * blk), :]
            out = []
            for hh in range(nh):
                s = _dot(kk, qz_s[x, hh])
                out.append((s, jnp.max(s[:blk], axis=0, keepdims=True), jnp.max(s[blk:], axis=0, keepdims=True)))
            return out

        def item_absorb(k, sc):
            x, qb, jp = item_params(k)
            j0 = 2 * jp
            for hh in range(nh):
                s, cma, cmb = sc[hh]
                ba = bias_ref[hh, qb, pl.ds(j0, 1), :]
                bb = bias_ref[hh, qb, pl.ds(j0 + 1, 1), :]
                m = m_s[x, hh]
                m_new = jnp.maximum(m, jnp.maximum(cma + ba, cmb + bb))
                alpha = jnp.exp2(m - m_new)
                pa = jnp.exp2(s[:blk] - (m_new - ba))
                pb = jnp.exp2(s[blk:] - (m_new - bb))
                m_s[x, hh] = m_new
                pp = jnp.concatenate([pa, pb], axis=0).astype(BF16)
                vv = jnp.concatenate([vT_ref[0, j0, hrows[hh], :], vT_ref[0, j0 + 1, hrows[hh], :]], axis=1)
                vv = jnp.concatenate([vv, ones_rows], axis=0)
                acc_s[x, hh] = alpha * acc_s[x, hh] + _dot(vv, pp)

        s_diag = [diag_scores(0), diag_scores(1)]
        pending = [item_scores(k) for k in range(min(MOBA_LOOKAHEAD, n_items))]
        diag_absorb(0, s_diag[0])
        diag_absorb(1, s_diag[1])
        for k in range(n_items):
            if k + MOBA_LOOKAHEAD < n_items:
                pending.append(item_scores(k + MOBA_LOOKAHEAD))
            item_absorb(k, pending[k])
        for x in range(2):
            oT = jnp.concatenate([acc_s[x, hh, 0:HEAD_DIM, :] / acc_s[x, hh, HEAD_DIM:HEAD_DIM + 1, :]
                                  for hh in range(nh)], axis=0)
            o_ref[pl.ds(pl.multiple_of(iq[x] * blk, blk), blk), :] = oT.T.astype(o_ref.dtype)
        return 0

    lax.fori_loop(0, nb // 2, couple, 0)


def _moba_prompt(qT, kbf, kmean, vTb, bsz, seq):
    nb = seq // MOBA_BLOCK
    hp = N_HEADS // MOBA_HEADS_PER_STEP
    pair = MOBA_HEADS_PER_STEP * HEAD_DIM
    return pl.pallas_call(
        _moba_prompt_kernel,
        grid=(bsz, hp),
        in_specs=[pl.BlockSpec((1, nb, pair, MOBA_BLOCK), lambda b, h: (b, 0, h, 0)),
                  pl.BlockSpec((seq, pair), lambda b, h: (b, h)),
                  pl.BlockSpec((1, nb, pair), lambda b, h: (b, 0, h)),
                  pl.BlockSpec((1, nb, pair, MOBA_BLOCK), lambda b, h: (b, 0, h, 0))],
        out_specs=pl.BlockSpec((seq, pair), lambda b, h: (b, h)),
        out_shape=jax.ShapeDtypeStruct((bsz * seq, ATTN_WIDTH), BF16),
        scratch_shapes=[pltpu.VMEM((MOBA_HEADS_PER_STEP, nb, nb, MOBA_BLOCK), F32),
                        pltpu.VMEM((2, MOBA_HEADS_PER_STEP, pair, MOBA_BLOCK), BF16),
                        pltpu.VMEM((2, MOBA_HEADS_PER_STEP, 1, MOBA_BLOCK), F32),
                        pltpu.VMEM((2, MOBA_HEADS_PER_STEP, HEAD_DIM + MOBA_SUM_ROWS, MOBA_BLOCK), F32)],
        compiler_params=_cparams(("parallel", "parallel")),
        name="moba_prompt",
    )(qT, kbf, kmean, vTb)


def _moba_paged_kernel(pt_ref, qcol_ref, kn_ref, vn_ref, ck_ref, cv_ref, o_ref,
                       kbuf, vbuf, s_ref, psel_ref, stat_ref, qb_ref, ksem, vsem):
    b = pl.program_id(0)
    nseq = pl.num_programs(0) - 1
    n_pages = kbuf.shape[1]
    nblk = n_pages // 2
    slot = b % 2
    n_v = N_HEADS * MOBA_TOPK * 2

    def k_copy(seq_i, sl, p):
        return pltpu.make_async_copy(ck_ref.at[pt_ref[seq_i, p]], kbuf.at[sl, p], ksem.at[sl])

    def start_k(seq_i, sl):
        def body(p, _):
            k_copy(seq_i, sl, p).start()
            return 0
        lax.fori_loop(0, n_pages, body, 0)

    def v_copy(page, h, r, par):
        return pltpu.make_async_copy(cv_ref.at[page, h], vbuf.at[h, r, par], vsem.at[0])

    @pl.when(b == 0)
    def _():
        start_k(0, 0)

    @pl.when(b < nseq)
    def _():
        def wait_body(p, _):
            k_copy(b, slot, p).wait()
            return 0
        lax.fori_loop(0, n_pages, wait_body, 0)

    @pl.when(b + 1 < nseq)
    def _():
        start_k(b + 1, 1 - slot)

    @pl.when(b >= 1)
    def _():
        for h in range(N_HEADS):
            for r in range(MOBA_TOPK):
                for par in range(2):
                    v_copy(0, h, r, par).wait()
        vn = vn_ref[0]
        for h in range(N_HEADS):
            acc = jnp.zeros((HEAD_DIM, PAGE_SIZE), F32)
            for r in range(MOBA_TOPK):
                for par in range(2):
                    acc = acc + vbuf[h, r, par] * psel_ref[h, 2 * r + par]
            p_own = stat_ref[0, h][:, 0:1]
            l = stat_ref[1, h][:, 0:1]
            o_h = jnp.sum(acc, axis=1, keepdims=True) + p_own * vn[:, h:h + 1]
            o_ref[0, :, h:h + 1] = o_h / l

    @pl.when(b < nseq)
    def _():
        qcol = qcol_ref[0]
        own = jnp.sum(qcol * kn_ref[0], axis=0, keepdims=True)
        blk_i = lax.broadcasted_iota(jnp.int32, (nblk, 1), 0)
        for h in range(N_HEADS):
            qb_ref[h] = jnp.broadcast_to(qcol[:, h:h + 1], (HEAD_DIM, PAGE_SIZE))
        for h in range(N_HEADS):
            def score_body(g, _):
                qb = qb_ref[h]
                for dn in range(PAGED_SCORE_UNROLL):
                    n = g * PAGED_SCORE_UNROLL + dn
                    for par in range(2):
                        kt = kbuf[slot, 2 * n + par, h]
                        s_ref[par, h, pl.ds(n, 1), :] = jnp.sum(kt * qb, axis=0, keepdims=True)
                return 0
            lax.fori_loop(0, nblk // PAGED_SCORE_UNROLL, score_body, 0)

        for h in range(N_HEADS):
            s0 = s_ref[0, h]
            s1 = s_ref[1, h]
            bs = jnp.sum(s0 + s1, axis=1, keepdims=True)
            sel = jnp.zeros((nblk, 1), dtype=jnp.bool_)
            picks = []
            for r in range(MOBA_TOPK):
                mx = jnp.max(bs, axis=0, keepdims=True)
                first = jnp.min(jnp.where(bs == mx, blk_i, nblk), axis=0, keepdims=True)
                pick = (blk_i == first) & (mx > -jnp.inf)
                sel = sel | pick
                bs = jnp.where(pick, -jnp.inf, bs)
                blk_id = jnp.max(jnp.where(pick, blk_i, 0))
                for par in range(2):
                    v_copy(pt_ref[b, 2 * blk_id + par], h, r, par).start()
                picks.append(blk_id)
            s_own = own[:, h:h + 1]
            sm0 = jnp.where(sel, s0, NEG_BIG)
            sm1 = jnp.where(sel, s1, NEG_BIG)
            mx = jnp.maximum(jnp.max(jnp.max(jnp.maximum(sm0, sm1), axis=1, keepdims=True),
                                     axis=0, keepdims=True), s_own)
            p0 = jnp.exp(sm0 - mx)
            p1 = jnp.exp(sm1 - mx)
            p_own = jnp.exp(s_own - mx)
            l = jnp.sum(jnp.sum(p0 + p1, axis=1, keepdims=True), axis=0, keepdims=True) + p_own
            s_ref[0, h] = p0
            s_ref[1, h] = p1
            for r in range(MOBA_TOPK):
                for par in range(2):
                    psel_ref[h, 2 * r + par] = s_ref[par, h, pl.ds(picks[r], 1), :]
            stat_ref[0, h] = jnp.broadcast_to(p_own, (1, PAGE_SIZE))
            stat_ref[1, h] = jnp.broadcast_to(l, (1, PAGE_SIZE))


def _moba_paged(page_table, qcol, kncol, vncol, cache_kT, cache_vT):
    nseq, n_pages = page_table.shape
    cur = pl.BlockSpec((1, HEAD_DIM, N_HEADS), lambda b, pt: (jnp.minimum(b, nseq - 1), 0, 0))
    prev = pl.BlockSpec((1, HEAD_DIM, N_HEADS), lambda b, pt: (jnp.maximum(b - 1, 0), 0, 0))
    any_spec = pl.BlockSpec(memory_space=pl.ANY)
    grid_spec = pltpu.PrefetchScalarGridSpec(
        num_scalar_prefetch=1,
        grid=(nseq + 1,),
        in_specs=[cur, cur, prev, any_spec, any_spec],
        out_specs=prev,
        scratch_shapes=[
            pltpu.VMEM((2, n_pages, N_HEADS, HEAD_DIM, PAGE_SIZE), F32),
            pltpu.VMEM((N_HEADS, MOBA_TOPK, 2, HEAD_DIM, PAGE_SIZE), F32),
            pltpu.VMEM((2, N_HEADS, n_pages // 2, PAGE_SIZE), F32),
            pltpu.VMEM((N_HEADS, 2 * MOBA_TOPK, 1, PAGE_SIZE), F32),
            pltpu.VMEM((2, N_HEADS, 1, PAGE_SIZE), F32),
            pltpu.VMEM((N_HEADS, HEAD_DIM, PAGE_SIZE), F32),
            pltpu.SemaphoreType.DMA((2,)),
            pltpu.SemaphoreType.DMA((1,)),
        ])
    return pl.pallas_call(
        _moba_paged_kernel,
        grid_spec=grid_spec,
        out_shape=jax.ShapeDtypeStruct((nseq, HEAD_DIM, N_HEADS), F32),
        compiler_params=_cparams(("arbitrary",)),
        name="moba_paged",
    )(page_table, qcol, kncol, vncol, cache_kT, cache_vT)


def _ssm_tables(a_re, a_im, log_dt, b_re, b_im, c_re, c_im, chunk, n_pow):
    t_ = chunk
    lam = lax.complex(a_re.astype(F32), a_im.astype(F32))
    ldt = lam * jnp.exp(log_dt.astype(F32))[:, None]
    a_bar = jnp.exp(ldt)
    b_bar = ((a_bar - 1.0) / lam)[..., None] * lax.complex(b_re.astype(F32), b_im.astype(F32))
    c_c = lax.complex(c_re.astype(F32), c_im.astype(F32))
    taus = jnp.arange(t_ + 1, dtype=F32).astype(jnp.complex64)
    apow = jnp.exp(ldt[None] * taus[:, None, None])
    gq, ssg, c_, p_ = GROUPS_PER_SG, N_SG, SSM_GROUP, SSM_STATE
    c4 = c_c.reshape(ssg, gq, c_, p_)
    b4 = b_bar.reshape(ssg, gq, p_, c_)
    ap4 = apow.reshape(t_ + 1, ssg, gq, p_)

    kc = jnp.einsum('sjcp,tsjp,sjpd->sjdtc', c4, ap4[:t_], b4).real
    kc = kc.reshape(ssg, LANES, t_ * c_)
    mst = jnp.einsum('tsjp,sjpd->stjdp', ap4[:t_][::-1], b4)
    mc = jnp.concatenate([mst.real, mst.imag], axis=-1).reshape(ssg, t_ * LANES, 2 * p_)
    nout = jnp.einsum('sjcp,tsjp->sjptc', c4, ap4[1:])
    nc = jnp.concatenate([nout.real, -nout.imag], axis=1).reshape(ssg, 2 * SG_STATE, t_ * c_)
    rs = jnp.arange(1, n_pow + 1, dtype=F32).astype(jnp.complex64)
    ap = jnp.exp((ldt * t_)[None] * rs[:, None, None])
    ap = ap.reshape(n_pow, ssg, SG_STATE).transpose(1, 0, 2)
    a_pow = jnp.concatenate([ap.real, ap.imag], axis=-1)
    a1 = a_bar.reshape(ssg, 1, SG_STATE)
    a_one = jnp.concatenate([a1.real, a1.imag], axis=-1)
    return kc, mc, nc, a_pow, a_one


def _expand_ssm_weights(kc_ref, mc_ref, nc_ref, wt_s, ws_s, wo_s):
    t_ = SSM_CHUNK
    kw = t_ * LANES
    lg = lambda n: n.bit_length() - 1
    r = lax.broadcasted_iota(jnp.int32, (LANES, kw), 0)
    q = lax.broadcasted_iota(jnp.int32, (LANES, kw), 1)
    sh_c, sh_p = lg(SSM_GROUP), lg(SSM_STATE)
    ecol = jnp.where(((r >> sh_c) == (q >> lg(LANES))) & ((r & (SSM_GROUP - 1)) == (q & (SSM_GROUP - 1))),
                     1.0, 0.0).astype(BF16)
    e2 = jnp.where(((r >> sh_p) == (q >> lg(SG_STATE))) & ((r & (SSM_STATE - 1)) == (q & (SSM_STATE - 1))),
                   1.0, 0.0).astype(BF16)
    col_c = (q & (LANES - 1)) >> sh_c
    taps = jnp.where((r >> sh_c) == col_c, _dot(kc_ref[0].astype(BF16), ecol), 0.0).astype(BF16)
    blank = jnp.zeros((LANES, LANES), BF16)
    for s in range(t_):
        for t in range(t_):
            wt_s[s * LANES:(s + 1) * LANES, t * LANES:(t + 1) * LANES] = (
                taps[:, (t - s) * LANES:(t - s + 1) * LANES] if t >= s else blank)
    rr = lax.broadcasted_iota(jnp.int32, (kw, kw), 0)
    qq = lax.broadcasted_iota(jnp.int32, (kw, kw), 1)
    row_c = (rr & (LANES - 1)) >> sh_c
    row_p = (rr & (SG_STATE - 1)) >> sh_p
    colq_c = (qq & (LANES - 1)) >> sh_c
    colq_p = (qq & (SG_STATE - 1)) >> sh_p
    ws_s[...] = jnp.where(row_c == colq_p, _dot(mc_ref[0].astype(BF16), e2), 0.0).astype(BF16)
    wo_s[...] = jnp.where(row_p == colq_c, _dot(nc_ref[0].astype(BF16), ecol), 0.0).astype(BF16)


def _ssm_prompt_kernel(u_ref, kc_ref, mc_ref, nc_ref, ap_ref, d_ref, h0_ref,
                       y_ref, hf_ref, uc_ref, x_ref, hin_ref, wt_s, ws_s, wo_s):
    t_ = SSM_CHUNK
    nc = uc_ref.shape[0]
    ns = SG_STATE

    @pl.when(pl.program_id(1) == 0)
    def _():
        _expand_ssm_weights(kc_ref, mc_ref, nc_ref, wt_s, ws_s, wo_s)

    for t in range(t_):
        uc_ref[:, t * LANES:(t + 1) * LANES] = u_ref[pl.ds(t, nc, stride=t_), :].astype(BF16)
    uc = uc_ref[...]
    x_ref[...] = _dot(uc, ws_s[...])

    ap = ap_ref[0]
    apr, api = ap[:, :ns], ap[:, ns:]
    row = lax.broadcasted_iota(jnp.int32, (SUBLANES, 1), 0)

    def tile_body(k, carry):
        hr, hi = carry
        rows = pl.ds(k * SUBLANES, SUBLANES)
        xr = x_ref[rows, 0:ns]
        xi = x_ref[rows, ns:2 * ns]
        for d in (1, 2, 4):
            ar, ai = apr[d - 1:d], api[d - 1:d]
            sr = jnp.where(row >= d, pltpu.roll(xr, d, 0), 0.0)
            si = jnp.where(row >= d, pltpu.roll(xi, d, 0), 0.0)
            xr, xi = xr + ar * sr - ai * si, xi + ar * si + ai * sr
        outr = xr + apr * hr - api * hi
        outi = xi + apr * hi + api * hr
        hin_ref[rows, 0:ns] = jnp.where(row >= 1, pltpu.roll(outr, 1, 0), hr)
        hin_ref[rows, ns:2 * ns] = jnp.where(row >= 1, pltpu.roll(outi, 1, 0), hi)
        return outr[SUBLANES - 1:SUBLANES], outi[SUBLANES - 1:SUBLANES]

    y_intra = _dot(uc, wt_s[...])
    h0 = h0_ref[0, 0]
    carry = (h0[:, :ns], h0[:, ns:])
    for k in range(nc // SUBLANES):
        carry = tile_body(k, carry)
    hf_ref[0, 0] = jnp.concatenate(carry, axis=1)

    y = y_intra + _dot(hin_ref[...].astype(BF16), wo_s[...])
    dv = d_ref[...]
    for t in range(t_):
        rows = pl.ds(t, nc, stride=t_)
        y_ref[rows, :] = y[:, t * LANES:(t + 1) * LANES] + dv * u_ref[rows, :]


def _ssm_prompt(u, kc, mc, nc_tab, a_pow, d_row, h0, bsz, seq):
    t_ = SSM_CHUNK
    nc = seq // t_
    kw = t_ * LANES
    assert kc.shape[2] == LANES and 2 * SG_STATE == kw
    wspec = lambda shape: pl.BlockSpec((1,) + shape, lambda s, b: (s, 0, 0))
    return pl.pallas_call(
        _ssm_prompt_kernel,
        grid=(N_SG, bsz),
        in_specs=[pl.BlockSpec((seq, LANES), lambda s, b: (b, s)),
                  wspec(kc.shape[1:]), wspec(mc.shape[1:]), wspec(nc_tab.shape[1:]),
                  wspec((SUBLANES, 2 * SG_STATE)),
                  pl.BlockSpec((1, LANES), lambda s, b: (0, s)),
                  pl.BlockSpec((1, 1, 1, 2 * SG_STATE), lambda s, b: (b, s, 0, 0))],
        out_specs=(pl.BlockSpec((seq, LANES), lambda s, b: (b, s)),
                   pl.BlockSpec((1, 1, 1, 2 * SG_STATE), lambda s, b: (b, s, 0, 0))),
        out_shape=(jax.ShapeDtypeStruct((bsz * seq, SSM_WIDTH), F32),
                   jax.ShapeDtypeStruct((bsz, N_SG, 1, 2 * SG_STATE), F32)),
        scratch_shapes=[pltpu.VMEM((nc, kw), BF16),
                        pltpu.VMEM((nc, 2 * SG_STATE), F32),
                        pltpu.VMEM((nc, 2 * SG_STATE), F32),
                        pltpu.VMEM((kw, kw), BF16),
                        pltpu.VMEM((kw, 2 * SG_STATE), BF16),
                        pltpu.VMEM((2 * SG_STATE, kw), BF16)],
        compiler_params=_cparams(("arbitrary", "arbitrary")),
        name="ssm_prompt",
    )(u, kc, mc, nc_tab, a_pow, d_row, h0)


def _ssm_step_kernel(u_ref, kc_ref, mc_ref, nc_ref, a1_ref, d_ref, h0_ref, y_ref, hf_ref):
    ns = SG_STATE
    kw = 2 * ns
    lg = lambda n: n.bit_length() - 1
    sh_c, sh_p = lg(SSM_GROUP), lg(SSM_STATE)
    iota = lambda shape, d: lax.broadcasted_iota(jnp.int32, shape, d)
    r1, q1 = iota((LANES, LANES), 0), iota((LANES, LANES), 1)
    e_c = jnp.where((r1 < SSM_GROUP) & ((r1 & (SSM_GROUP - 1)) == (q1 & (SSM_GROUP - 1))), 1.0, 0.0).astype(BF16)
    r2, q2 = iota((LANES, kw), 0), iota((LANES, kw), 1)
    e_p = jnp.where(((r2 >> sh_p) == (q2 >> lg(ns))) & ((r2 & (SSM_STATE - 1)) == (q2 & (SSM_STATE - 1))),
                    1.0, 0.0).astype(BF16)
    same_t = (r1 >> sh_c) == (q1 >> sh_c)
    same_s = (r2 >> sh_c) == ((q2 & (ns - 1)) >> sh_p)
    r3, q3 = iota((kw, LANES), 0), iota((kw, LANES), 1)
    same_o = ((r3 & (ns - 1)) >> sh_p) == (q3 >> sh_c)
    last = (SSM_CHUNK - 1) * LANES
    for s in range(N_SG):
        wt = jnp.where(same_t, _dot(kc_ref[s].astype(BF16), e_c), 0.0).astype(BF16)
        ws = jnp.where(same_s, _dot(mc_ref[s, last:last + LANES, :].astype(BF16), e_p), 0.0).astype(BF16)
        wo = jnp.where(same_o, _dot(nc_ref[s].astype(BF16), e_c), 0.0).astype(BF16)
        us = u_ref[:, s * LANES:(s + 1) * LANES]
        ub = us.astype(BF16)
        h0 = h0_ref[s]
        x = _dot(ub, ws)
        ar, ai = a1_ref[s, :, :ns], a1_ref[s, :, ns:]
        hr, hi = h0[:, :ns], h0[:, ns:]
        hf_ref[s] = jnp.concatenate([x[:, :ns] + ar * hr - ai * hi,
                                     x[:, ns:] + ar * hi + ai * hr], axis=1)
        y = _dot(ub, wt) + _dot(h0.astype(BF16), wo)
        y_ref[:, s * LANES:(s + 1) * LANES] = y + d_ref[:, s * LANES:(s + 1) * LANES] * us


def _ssm_step(u, kc, mc, nc_tab, a_one, d_row, h0):
    m = u.shape[0]
    full = lambda a: pl.BlockSpec(a.shape, lambda i: (0,) * a.ndim)
    args = (u, kc, mc, nc_tab, a_one, d_row, h0)
    shapes = [(m, SSM_WIDTH), (N_SG, m, 2 * SG_STATE)]
    return pl.pallas_call(
        _ssm_step_kernel,
        grid=(1,),
        in_specs=[full(a) for a in args],
        out_specs=tuple(pl.BlockSpec(s, lambda i, n=len(s): (0,) * n) for s in shapes),
        out_shape=tuple(jax.ShapeDtypeStruct(s, F32) for s in shapes),
        compiler_params=_cparams(("arbitrary",)),
        name="ssm_step",
    )(*args)


FFN_SPLIT = 2
FFN_CHUNK = FFN_HIDDEN // FFN_SPLIT


def _post_kernel(x_ref, o_ref, y_ref, sga_ref, sgs_ref, wap_ref, wglu_ref, bglu_ref, wsp_ref,
                 wout_ref, nffn_ref, wfi_ref, wfo_ref, nfin_ref, out_ref):
    z = jax.nn.gelu(y_ref[...])
    z = z * jax.nn.sigmoid(_dot(z.astype(BF16), wglu_ref[...]) + bglu_ref[...])
    ssm_out = _dot(z.astype(BF16), wsp_ref[...])
    attn_out = _dot(o_ref[...].astype(BF16), wap_ref[...])
    merged = sga_ref[...].astype(F32) * attn_out + sgs_ref[...].astype(F32) * ssm_out
    x1 = x_ref[...] + _dot(merged.astype(BF16), wout_ref[...])
    hf = _rms(x1, nffn_ref[...]).astype(BF16)
    acc = x1
    for c in range(FFN_SPLIT):
        lo = c * FFN_CHUNK
        a = _dot(hf, wfi_ref[:, lo:lo + FFN_CHUNK])
        g = _dot(hf, wfi_ref[:, FFN_HIDDEN + lo:FFN_HIDDEN + lo + FFN_CHUNK])
        act = (jax.nn.silu(a) * g).astype(BF16)
        acc = acc + _dot(act, wfo_ref[lo:lo + FFN_CHUNK, :])
    out_ref[...] = _rms(acc, nfin_ref[...])


def _post(x2d, o, y, sga, sgs, wap, wglu, bglu, wsp, wout, nffn, wfi, wfo, nfin, tm):
    m = x2d.shape[0]
    tok = lambda w: pl.BlockSpec((tm, w), lambda i: (i, 0))
    const = lambda a: pl.BlockSpec(a.shape, lambda i: (0,) * a.ndim, pipeline_mode=pl.Buffered(1))
    weights = (wap, wglu, bglu, wsp, wout, nffn, wfi, wfo, nfin)
    return pl.pallas_call(
        _post_kernel,
        grid=(m // tm,),
        in_specs=[tok(D_MODEL), tok(ATTN_WIDTH), tok(SSM_WIDTH), tok(D_MODEL), tok(D_MODEL)]
                 + [const(w) for w in weights],
        out_specs=tok(D_MODEL),
        out_shape=jax.ShapeDtypeStruct((m, D_MODEL), F32),
        compiler_params=_cparams(("parallel",)),
        name="post",
    )(x2d, o, y, sga, sgs, *weights)


def _rope_tables(pos):
    half = HEAD_DIM // 2
    inv = jnp.power(jnp.float32(ROPE_THETA), -2.0 * jnp.arange(half, dtype=F32) / HEAD_DIM)
    ang = pos.astype(F32)[:, None] * inv[None, :]
    cos, sin = jnp.cos(ang), jnp.sin(ang)
    reps = LANES // HEAD_DIM
    return (jnp.tile(jnp.concatenate([cos, cos], axis=1), (1, reps)),
            jnp.tile(jnp.concatenate([-sin, sin], axis=1), (1, reps)))


def _leaf_from_T(xT, bsz, seq):
    return xT.reshape(bsz, N_HEADS, HEAD_DIM, seq).transpose(0, 3, 1, 2)[None]


def _state_in(re, im):
    n = re.shape[0]
    h = jnp.concatenate([re.reshape(n, N_SG, SG_STATE), im.reshape(n, N_SG, SG_STATE)], axis=-1)
    return h.transpose(1, 0, 2)


def _state_out(h):
    n = h.shape[0]
    return (h[..., :SG_STATE].reshape(1, n, SSM_GROUPS, SSM_STATE),
            h[..., SG_STATE:].reshape(1, n, SSM_GROUPS, SSM_STATE))


def kernel(x_prompt, x_sample, cache_k, cache_v, state_ssm_re, state_ssm_im, page_table, norm_mix, w_in,
           w_attn_proj, ssm_a_re, ssm_a_im, ssm_log_dt, ssm_b_re, ssm_b_im, ssm_c_re, ssm_c_im, ssm_d, w_glu,
           b_glu, w_ssm_proj, w_out, norm_ffn, w_ffn_in, w_ffn_out, norm_final):
    assert w_in.shape[0] == 1, "single layer"
    bsz, seq = x_prompt.shape[:2]
    nseq = x_sample.shape[0]
    past_len = page_table.shape[1] * PAGE_SIZE
    assert seq % MOBA_BLOCK == 0 and past_len % MOBA_BLOCK == 0 and x_sample.shape[1] == 1

    ssm_p = (ssm_a_re[0], ssm_a_im[0], ssm_log_dt[0], ssm_b_re[0], ssm_b_im[0], ssm_c_re[0], ssm_c_im[0])

    cos_s, sin_s = _rope_tables(jnp.full((1,), past_len, dtype=jnp.int32))
    xs2 = x_sample.reshape(nseq, D_MODEL)
    q_s, k_s, v_s, u_s, sga_s, sgs_s, w_in_bf = _inproj_sample(xs2, norm_mix, w_in[0], cos_s, sin_s)

    cos_p, sin_p = _rope_tables(jnp.arange(seq, dtype=jnp.int32))
    xp2 = x_prompt.reshape(bsz * seq, D_MODEL)
    cast_ws = (w_attn_proj[0], w_glu[0], w_ssm_proj[0], w_out[0], w_ffn_in[0], w_ffn_out[0])
    (qT, kT, kbf, kmean, vT, vTb, u_p, sga_p, sgs_p,
     wap_bf, wglu_bf, wsp_bf, wout_bf, wfi_bf, wfo_bf) = _inproj_prompt(
        xp2, norm_mix, w_in_bf, cos_p, sin_p, cast_ws, bsz, seq, tm=512)
    post_w = (wap_bf, wglu_bf, b_glu, wsp_bf, wout_bf, norm_ffn, wfi_bf, wfo_bf, norm_final[None])
    o_p = _moba_prompt(qT, kbf, kmean.reshape(bsz, seq // MOBA_BLOCK, ATTN_WIDTH), vTb, bsz, seq)
    kc, mc, nc_tab, ap, a_one = _ssm_tables(*ssm_p, chunk=SSM_CHUNK, n_pow=SUBLANES)
    h0_p = jnp.zeros((bsz, N_SG, 1, 2 * SG_STATE), F32)
    y_p, hf_p = _ssm_prompt(u_p, kc, mc, nc_tab, ap, ssm_d, h0_p, bsz, seq)
    y_prompt = _post(xp2, o_p, y_p, sga_p, sgs_p, *post_w, tm=512).reshape(bsz, seq, D_MODEL)
    new_ssm_re_p, new_ssm_im_p = _state_out(hf_p.reshape(bsz, N_SG, 2 * SG_STATE))

    cache_kT = cache_k[0].transpose(0, 2, 3, 1)
    cache_vT = cache_v[0].transpose(0, 2, 3, 1)
    dh = lambda t: t.reshape(nseq, N_HEADS, HEAD_DIM).transpose(0, 2, 1)
    o_s = _moba_paged(page_table, dh(q_s), dh(k_s), dh(v_s), cache_kT, cache_vT).transpose(0, 2, 1)
    h0_s = _state_in(state_ssm_re[0], state_ssm_im[0])
    y_s, hf_s = _ssm_step(u_s, kc, mc, nc_tab, a_one, ssm_d, h0_s)
    y_sample = _post(xs2, o_s.reshape(nseq, ATTN_WIDTH), y_s, sga_s, sgs_s, *post_w, tm=nseq)
    new_ssm_re_s, new_ssm_im_s = _state_out(hf_s.transpose(1, 0, 2))

    return (y_prompt, y_sample.reshape(nseq, 1, D_MODEL),
            _leaf_from_T(kT, bsz, seq), _leaf_from_T(vT, bsz, seq), new_ssm_re_p, new_ssm_im_p,
            k_s.reshape(1, nseq, 1, N_HEADS, HEAD_DIM), v_s.reshape(1, nseq, 1, N_HEADS, HEAD_DIM),
            new_ssm_re_s, new_ssm_im_s)
```

```python
import functools
import math

import jax
import jax.numpy as jnp
from jax import lax
from jax.experimental import pallas as pl
from jax.experimental.pallas import tpu as pltpu

F32 = jnp.float32
BF16 = jnp.bfloat16

D_MODEL = 1024
N_HEADS = 8
HEAD_DIM = 64
ATTN_WIDTH = N_HEADS * HEAD_DIM
MOBA_BLOCK = 256
MOBA_TOPK = 3
ROPE_THETA = 10000.0
SSM_WIDTH = 512
SSM_GROUP = 16
SSM_GROUPS = 32
SSM_STATE = 64
FFN_HIDDEN = 2816
RMS_EPS = 1e-6
PAGE_SIZE = 128

LANES = 128
SUBLANES = 8
BF16_SUBLANES = 16
GROUPS_PER_SG = LANES // SSM_GROUP
N_SG = SSM_GROUPS // GROUPS_PER_SG
SG_STATE = GROUPS_PER_SG * SSM_STATE
SSM_CHUNK = 8
NEG_BIG = -1e30
MOBA_HEADS_PER_STEP = 4
MOBA_LOOKAHEAD = 2
LOG2E = math.log2(math.e)
MOBA_SUM_ROWS = 16
PAGED_SCORE_UNROLL = 4
VMEM_LIMIT = 56 * 1024 * 1024


def _cparams(sem):
    return pltpu.CompilerParams(dimension_semantics=sem, vmem_limit_bytes=VMEM_LIMIT)


def _dot(a, b):
    return jnp.dot(a, b, preferred_element_type=F32)


def _rms(x, g):
    return x * lax.rsqrt(jnp.mean(x * x, axis=-1, keepdims=True) + RMS_EPS) * g


def _inproj_core(x, g, w_ref, cos, sin, q_scale):
    h = _rms(x, g).astype(BF16)
    lane = lax.broadcasted_iota(jnp.int32, (1, ATTN_WIDTH), 1)
    first_half = (lane % HEAD_DIM) < (HEAD_DIM // 2)

    def rot(t):
        partner = jnp.where(first_half,
                            pltpu.roll(t, ATTN_WIDTH - HEAD_DIM // 2, 1),
                            pltpu.roll(t, HEAD_DIM // 2, 1))
        return t * cos + partner * sin

    a = ATTN_WIDTH
    q = rot(_dot(h, w_ref[:, 0:a])) * q_scale
    k = rot(_dot(h, w_ref[:, a:2 * a]))
    v = _dot(h, w_ref[:, 2 * a:3 * a])
    u = _dot(h, w_ref[:, 3 * a:3 * a + SSM_WIDTH])
    o = 3 * a + SSM_WIDTH
    sga = jax.nn.sigmoid(_dot(h, w_ref[:, o:o + D_MODEL]))
    sgs = jax.nn.sigmoid(_dot(h, w_ref[:, o + D_MODEL:o + 2 * D_MODEL]))
    return q, k, v, u, sga, sgs


def _inproj_prompt_kernel(x_ref, g_ref, w_ref, cos_ref, sin_ref, *rest):
    n_cast = (len(rest) - 9) // 2
    cast_in = rest[:n_cast]
    qT_ref, kT_ref, kbf_ref, kmean_ref, vT_ref, vTb_ref, u_ref, sga_ref, sgs_ref = rest[n_cast:n_cast + 9]
    cast_out = rest[n_cast + 9:]
    for wi_ref, wo_ref in zip(cast_in, cast_out):
        wo_ref[...] = wi_ref[...].astype(BF16)
    reps = ATTN_WIDTH // LANES
    cos = jnp.tile(cos_ref[...], (1, reps))
    sin = jnp.tile(sin_ref[...], (1, reps))
    q, k, v, u, sga, sgs = _inproj_core(x_ref[...], g_ref[...], w_ref, cos, sin, LOG2E * HEAD_DIM ** -0.5)
    tm = q.shape[0]
    kT_ref[0] = k.T
    vT_ref[0] = v.T
    kbf_ref[...] = k.astype(BF16)
    for s in range(tm // MOBA_BLOCK):
        rows = slice(s * MOBA_BLOCK, (s + 1) * MOBA_BLOCK)
        qT_ref[0, s] = q[rows].T.astype(BF16)
        vTb_ref[0, s] = v[rows].T.astype(BF16)
        kmean_ref[0, s] = jnp.mean(k[rows], axis=0, keepdims=True)
    u_ref[...] = u
    sga_ref[...] = sga.astype(BF16)
    sgs_ref[...] = sgs.astype(BF16)


def _inproj_prompt(x2d, g, w_bf, cos, sin, cast_ws, bsz, seq, tm):
    m = bsz * seq
    nb = seq // MOBA_BLOCK
    tpb = seq // tm
    sub = tm // MOBA_BLOCK
    a = ATTN_WIDTH
    full = lambda shape: pl.BlockSpec(shape, lambda b, t: (0,) * len(shape))
    tok = lambda w: pl.BlockSpec((tm, w), lambda b, t: (b * tpb + t, 0))
    out_shape = (
        jax.ShapeDtypeStruct((bsz, nb, a, MOBA_BLOCK), BF16),
        jax.ShapeDtypeStruct((bsz, a, seq), F32),
        jax.ShapeDtypeStruct((m, a), BF16),
        jax.ShapeDtypeStruct((bsz, nb, 1, a), F32),
        jax.ShapeDtypeStruct((bsz, a, seq), F32),
        jax.ShapeDtypeStruct((bsz, nb, a, MOBA_BLOCK), BF16),
        jax.ShapeDtypeStruct((m, SSM_WIDTH), F32),
        jax.ShapeDtypeStruct((m, D_MODEL), BF16),
        jax.ShapeDtypeStruct((m, D_MODEL), BF16),
    )
    blk_t = pl.BlockSpec((1, sub, a, MOBA_BLOCK), lambda b, t: (b, t, 0, 0))
    lane_t = pl.BlockSpec((1, a, tm), lambda b, t: (b, 0, t))
    out_specs = (blk_t, lane_t, tok(a),
                 pl.BlockSpec((1, sub, 1, a), lambda b, t: (b, t, 0, 0)),
                 lane_t, blk_t, tok(SSM_WIDTH), tok(D_MODEL), tok(D_MODEL))
    n_steps = bsz * tpb
    cast_specs = []
    for w in cast_ws:
        rows = w.shape[0]
        nblk = n_steps
        while rows % nblk or (rows // nblk) % BF16_SUBLANES:
            nblk //= 2
        cast_specs.append(pl.BlockSpec((rows // nblk, w.shape[1]),
                                       lambda b, t, n=nblk: (jnp.minimum(b * tpb + t, n - 1), 0)))
    return pl.pallas_call(
        _inproj_prompt_kernel,
        grid=(bsz, tpb),
        in_specs=[tok(D_MODEL), full((1, D_MODEL)), full(w_bf.shape),
                  pl.BlockSpec((tm, LANES), lambda b, t: (t, 0)),
                  pl.BlockSpec((tm, LANES), lambda b, t: (t, 0))] + cast_specs,
        out_specs=out_specs + tuple(cast_specs),
        out_shape=out_shape + tuple(jax.ShapeDtypeStruct(w.shape, BF16) for w in cast_ws),
        compiler_params=_cparams(("arbitrary", "arbitrary")),
        name="inproj_prompt",
    )(x2d, g, w_bf, cos, sin, *cast_ws)


def _inproj_sample_kernel(x_ref, g_ref, w_ref, cos_ref, sin_ref,
                          q_ref, k_ref, v_ref, u_ref, sga_ref, sgs_ref, wb_ref):
    step = ATTN_WIDTH
    for c in range(w_ref.shape[1] // step):
        wb_ref[:, c * step:(c + 1) * step] = w_ref[:, c * step:(c + 1) * step].astype(BF16)
    reps = ATTN_WIDTH // LANES
    cos = jnp.tile(cos_ref[...], (1, reps))
    sin = jnp.tile(sin_ref[...], (1, reps))
    q, k, v, u, sga, sgs = _inproj_core(x_ref[...], g_ref[...], wb_ref, cos, sin, HEAD_DIM ** -0.5)
    q_ref[...] = q
    k_ref[...] = k
    v_ref[...] = v
    u_ref[...] = u
    sga_ref[...] = sga.astype(BF16)
    sgs_ref[...] = sgs.astype(BF16)


def _inproj_sample(x2d, g, w_f32, cos, sin):
    m = x2d.shape[0]
    a = ATTN_WIDTH
    full = lambda shape, **kw: pl.BlockSpec(shape, lambda i: (0,) * len(shape), **kw)
    shapes = [(m, a), (m, a), (m, a), (m, SSM_WIDTH), (m, D_MODEL), (m, D_MODEL), w_f32.shape]
    dts = [F32, F32, F32, F32, BF16, BF16, BF16]
    return pl.pallas_call(
        _inproj_sample_kernel,
        grid=(1,),
        in_specs=[full(x2d.shape), full(g.shape), full(w_f32.shape, pipeline_mode=pl.Buffered(1)),
                  full(cos.shape), full(sin.shape)],
        out_specs=tuple(full(s) for s in shapes),
        out_shape=tuple(jax.ShapeDtypeStruct(s, d) for s, d in zip(shapes, dts)),
        compiler_params=_cparams(("arbitrary",)),
        name="inproj_sample",
    )(x2d, g, w_f32, cos, sin)


def _moba_prompt_kernel(qT_ref, k_ref, kmean_ref, vT_ref, o_ref, bias_ref, qz_s, m_s, acc_s):
    nb = qT_ref.shape[1]
    blk = MOBA_BLOCK
    nh = MOBA_HEADS_PER_STEP
    row2 = lax.broadcasted_iota(jnp.int32, (nh * HEAD_DIM, 1), 0)
    lane_km = lax.broadcasted_iota(jnp.int32, (1, nh * HEAD_DIM), 1)
    blk_row = lax.broadcasted_iota(jnp.int32, (nb, blk), 0)
    key_i = lax.broadcasted_iota(jnp.int32, (blk, blk), 0)
    qry_i = lax.broadcasted_iota(jnp.int32, (blk, blk), 1)
    causal = key_i <= qry_i
    in_head = [(row2 >= hh * HEAD_DIM) & (row2 < (hh + 1) * HEAD_DIM) for hh in range(nh)]
    hrows = [slice(hh * HEAD_DIM, (hh + 1) * HEAD_DIM) for hh in range(nh)]

    km_all = jnp.concatenate(
        [jnp.where((lane_km >= hh * HEAD_DIM) & (lane_km < (hh + 1) * HEAD_DIM), kmean_ref[0], 0.0)
         for hh in range(nh)], axis=0)
    km_terms = []
    rest = km_all
    for _ in range(3):
        term = rest.astype(BF16)
        km_terms.append(term)
        rest = rest - term.astype(F32)
    km_split = jnp.concatenate(km_terms, axis=0)
    for hh in range(nh):
        bias_ref[hh, 0] = jnp.full((nb, blk), NEG_BIG, F32)
    for i in range(1, nb):
        sb3 = _dot(km_split, qT_ref[0, i])
        sb_all = (sb3[0:nh * nb] + sb3[nh * nb:2 * nh * nb]) + sb3[2 * nh * nb:]
        for hh in range(nh):
            sb = jnp.where(blk_row < i, sb_all[hh * nb:(hh + 1) * nb], -jnp.inf)
            bias = jnp.full((nb, blk), NEG_BIG, F32)
            for _r in range(min(MOBA_TOPK, i)):
                mx = jnp.max(sb, axis=0, keepdims=True)
                first = jnp.min(jnp.where(sb == mx, blk_row, nb), axis=0, keepdims=True)
                pick = blk_row == first
                bias = jnp.where(pick, 0.0, bias)
                sb = jnp.where(pick, -jnp.inf, sb)
            bias_ref[hh, i] = bias

    n_items = nb // 2
    ones_rows = jnp.ones((MOBA_SUM_ROWS, 2 * blk), BF16)

    def couple(i, _):
        iq = (i, nb - 1 - i)
        n_first = (i + 1) // 2

        def diag_scores(x):
            q_pair = qT_ref[0, iq[x]]
            k_own = k_ref[pl.ds(pl.multiple_of(iq[x] * blk, blk), blk), :]
            out = []
            for hh in range(nh):
                qz = jnp.where(in_head[hh], q_pair, jnp.zeros_like(q_pair))
                qz_s[x, hh] = qz
                out.append(_dot(k_own, qz))
            return out

        def diag_absorb(x, s_own):
            for hh in range(nh):
                s = jnp.where(causal, s_own[hh], NEG_BIG)
                m0 = jnp.max(s, axis=0, keepdims=True)
                p = jnp.exp2((s - m0).astype(BF16))
                m_s[x, hh] = m0
                vv = jnp.concatenate([vT_ref[0, iq[x], hrows[hh], :], ones_rows[:, :blk]], axis=0)
                acc_s[x, hh] = _dot(vv, p)

        def item_params(k):
            first = k < n_first
            x = jnp.where(first, 0, 1)
            return x, jnp.where(first, iq[0], iq[1]), jnp.where(first, k, k - n_first)

        def item_scores(k):
            x, _, jp = item_params(k)
            kk = k_ref[pl.ds(pl.multiple_of(2 * jp * blk, blk), 2 * blk), :]
            out = []
            for hh in range(nh):
                s = _dot(kk, qz_s[x, hh])
                out.append((s, jnp.max(s[:blk], axis=0, keepdims=True), jnp.max(s[blk:], axis=0, keepdims=True)))
            return out

        def item_absorb(k, sc):
            x, qb, jp = item_params(k)
            j0 = 2 * jp
            for hh in range(nh):
                s, cma, cmb = sc[hh]
                ba = bias_ref[hh, qb, pl.ds(j0, 1), :]
                bb = bias_ref[hh, qb, pl.ds(j0 + 1, 1), :]
                m = m_s[x, hh]
                m_new = jnp.maximum(m, jnp.maximum(cma + ba, cmb + bb))
                alpha = jnp.exp2(m - m_new)
                pa = jnp.exp2((s[:blk] - (m_new - ba)).astype(BF16))
                pb = jnp.exp2((s[blk:] - (m_new - bb)).astype(BF16))
                m_s[x, hh] = m_new
                pp = jnp.concatenate([pa, pb], axis=0)
                vv = jnp.concatenate([vT_ref[0, j0, hrows[hh], :], vT_ref[0, j0 + 1, hrows[hh], :]], axis=1)
                vv = jnp.concatenate([vv, ones_rows], axis=0)
                acc_s[x, hh] = alpha * acc_s[x, hh] + _dot(vv, pp)

        s_diag = [diag_scores(0), diag_scores(1)]
        pending = [item_scores(k) for k in range(min(MOBA_LOOKAHEAD, n_items))]
        diag_absorb(0, s_diag[0])
        diag_absorb(1, s_diag[1])
        for k in range(n_items):
            if k + MOBA_LOOKAHEAD < n_items:
                pending.append(item_scores(k + MOBA_LOOKAHEAD))
            item_absorb(k, pending[k])
        for x in range(2):
            oT = jnp.concatenate([acc_s[x, hh, 0:HEAD_DIM, :] / acc_s[x, hh, HEAD_DIM:HEAD_DIM + 1, :]
                                  for hh in range(nh)], axis=0)
            o_ref[pl.ds(pl.multiple_of(iq[x] * blk, blk), blk), :] = oT.T.astype(o_ref.dtype)
        return 0

    lax.fori_loop(0, nb // 2, couple, 0)


def _moba_prompt(qT, kbf, kmean, vTb, bsz, seq):
    nb = seq // MOBA_BLOCK
    hp = N_HEADS // MOBA_HEADS_PER_STEP
    pair = MOBA_HEADS_PER_STEP * HEAD_DIM
    return pl.pallas_call(
        _moba_prompt_kernel,
        grid=(bsz, hp),
        in_specs=[pl.BlockSpec((1, nb, pair, MOBA_BLOCK), lambda b, h: (b, 0, h, 0)),
                  pl.BlockSpec((seq, pair), lambda b, h: (b, h)),
                  pl.BlockSpec((1, nb, pair), lambda b, h: (b, 0, h)),
                  pl.BlockSpec((1, nb, pair, MOBA_BLOCK), lambda b, h: (b, 0, h, 0))],
        out_specs=pl.BlockSpec((seq, pair), lambda b, h: (b, h)),
        out_shape=jax.ShapeDtypeStruct((bsz * seq, ATTN_WIDTH), BF16),
        scratch_shapes=[pltpu.VMEM((MOBA_HEADS_PER_STEP, nb, nb, MOBA_BLOCK), F32),
                        pltpu.VMEM((2, MOBA_HEADS_PER_STEP, pair, MOBA_BLOCK), BF16),
                        pltpu.VMEM((2, MOBA_HEADS_PER_STEP, 1, MOBA_BLOCK), F32),
                        pltpu.VMEM((2, MOBA_HEADS_PER_STEP, HEAD_DIM + MOBA_SUM_ROWS, MOBA_BLOCK), F32)],
        compiler_params=_cparams(("parallel", "parallel")),
        name="moba_prompt",
    )(qT, kbf, kmean, vTb)


def _moba_paged_kernel(pt_ref, qcol_ref, kn_ref, vn_ref, ck_ref, cv_ref, o_ref,
                       kbuf, vbuf, s_ref, psel_ref, stat_ref, qb_ref, ksem, vsem):
    b = pl.program_id(0)
    nseq = pl.num_programs(0) - 1
    n_pages = kbuf.shape[1]
    nblk = n_pages // 2
    slot = b % 2
    n_v = N_HEADS * MOBA_TOPK * 2

    def k_copy(seq_i, sl, p):
        return pltpu.make_async_copy(ck_ref.at[pt_ref[seq_i, p]], kbuf.at[sl, p], ksem.at[sl])

    def start_k(seq_i, sl):
        def body(p, _):
            k_copy(seq_i, sl, p).start()
            return 0
        lax.fori_loop(0, n_pages, body, 0)

    def v_copy(page, h, r, par):
        return pltpu.make_async_copy(cv_ref.at[page, h], vbuf.at[h, r, par], vsem.at[0])

    @pl.when(b == 0)
    def _():
        start_k(0, 0)

    @pl.when(b < nseq)
    def _():
        def wait_body(p, _):
            k_copy(b, slot, p).wait()
            return 0
        lax.fori_loop(0, n_pages, wait_body, 0)

    @pl.when(b + 1 < nseq)
    def _():
        start_k(b + 1, 1 - slot)

    @pl.when(b >= 1)
    def _():
        for h in range(N_HEADS):
            for r in range(MOBA_TOPK):
                for par in range(2):
                    v_copy(0, h, r, par).wait()
        vn = vn_ref[0]
        for h in range(N_HEADS):
            acc = jnp.zeros((HEAD_DIM, PAGE_SIZE), F32)
            for r in range(MOBA_TOPK):
                for par in range(2):
                    acc = acc + vbuf[h, r, par] * psel_ref[h, 2 * r + par]
            p_own = stat_ref[0, h][:, 0:1]
            l = stat_ref[1, h][:, 0:1]
            o_h = jnp.sum(acc, axis=1, keepdims=True) + p_own * vn[:, h:h + 1]
            o_ref[0, :, h:h + 1] = o_h / l

    @pl.when(b < nseq)
    def _():
        qcol = qcol_ref[0]
        own = jnp.sum(qcol * kn_ref[0], axis=0, keepdims=True)
        blk_i = lax.broadcasted_iota(jnp.int32, (nblk, 1), 0)
        for h in range(N_HEADS):
            qb_ref[h] = jnp.broadcast_to(qcol[:, h:h + 1], (HEAD_DIM, PAGE_SIZE))
        for h in range(N_HEADS):
            def score_body(g, _):
                qb = qb_ref[h]
                for dn in range(PAGED_SCORE_UNROLL):
                    n = g * PAGED_SCORE_UNROLL + dn
                    for par in range(2):
                        kt = kbuf[slot, 2 * n + par, h]
                        s_ref[par, h, pl.ds(n, 1), :] = jnp.sum(kt * qb, axis=0, keepdims=True)
                return 0
            lax.fori_loop(0, nblk // PAGED_SCORE_UNROLL, score_body, 0)

        for h in range(N_HEADS):
            s0 = s_ref[0, h]
            s1 = s_ref[1, h]
            bs = jnp.sum(s0 + s1, axis=1, keepdims=True)
            sel = jnp.zeros((nblk, 1), dtype=jnp.bool_)
            picks = []
            for r in range(MOBA_TOPK):
                mx = jnp.max(bs, axis=0, keepdims=True)
                first = jnp.min(jnp.where(bs == mx, blk_i, nblk), axis=0, keepdims=True)
                pick = blk_i == first
                sel = sel | pick
                bs = jnp.where(pick, -jnp.inf, bs)
                blk_id = jnp.max(first)
                for par in range(2):
                    v_copy(pt_ref[b, 2 * blk_id + par], h, r, par).start()
                picks.append(blk_id)
            s_own = own[:, h:h + 1]
            sm0 = jnp.where(sel, s0, NEG_BIG)
            sm1 = jnp.where(sel, s1, NEG_BIG)
            mx = jnp.maximum(jnp.max(jnp.max(jnp.maximum(sm0, sm1), axis=1, keepdims=True),
                                     axis=0, keepdims=True), s_own)
            p0 = jnp.exp(sm0 - mx)
            p1 = jnp.exp(sm1 - mx)
            p_own = jnp.exp(s_own - mx)
            l = jnp.sum(jnp.sum(p0 + p1, axis=1, keepdims=True), axis=0, keepdims=True) + p_own
            s_ref[0, h] = p0
            s_ref[1, h] = p1
            for r in range(MOBA_TOPK):
                for par in range(2):
                    psel_ref[h, 2 * r + par] = s_ref[par, h, pl.ds(picks[r], 1), :]
            stat_ref[0, h] = jnp.broadcast_to(p_own, (1, PAGE_SIZE))
            stat_ref[1, h] = jnp.broadcast_to(l, (1, PAGE_SIZE))


def _moba_paged(page_table, qcol, kncol, vncol, cache_kT, cache_vT):
    nseq, n_pages = page_table.shape
    assert n_pages // 2 >= MOBA_TOPK, "every pick must find an unpicked cached block"
    cur = pl.BlockSpec((1, HEAD_DIM, N_HEADS), lambda b, pt: (jnp.minimum(b, nseq - 1), 0, 0))
    prev = pl.BlockSpec((1, HEAD_DIM, N_HEADS), lambda b, pt: (jnp.maximum(b - 1, 0), 0, 0))
    any_spec = pl.BlockSpec(memory_space=pl.ANY)
    grid_spec = pltpu.PrefetchScalarGridSpec(
        num_scalar_prefetch=1,
        grid=(nseq + 1,),
        in_specs=[cur, cur, prev, any_spec, any_spec],
        out_specs=prev,
        scratch_shapes=[
            pltpu.VMEM((2, n_pages, N_HEADS, HEAD_DIM, PAGE_SIZE), F32),
            pltpu.VMEM((N_HEADS, MOBA_TOPK, 2, HEAD_DIM, PAGE_SIZE), F32),
            pltpu.VMEM((2, N_HEADS, n_pages // 2, PAGE_SIZE), F32),
            pltpu.VMEM((N_HEADS, 2 * MOBA_TOPK, 1, PAGE_SIZE), F32),
            pltpu.VMEM((2, N_HEADS, 1, PAGE_SIZE), F32),
            pltpu.VMEM((N_HEADS, HEAD_DIM, PAGE_SIZE), F32),
            pltpu.SemaphoreType.DMA((2,)),
            pltpu.SemaphoreType.DMA((1,)),
        ])
    return pl.pallas_call(
        _moba_paged_kernel,
        grid_spec=grid_spec,
        out_shape=jax.ShapeDtypeStruct((nseq, HEAD_DIM, N_HEADS), F32),
        compiler_params=_cparams(("arbitrary",)),
        name="moba_paged",
    )(page_table, qcol, kncol, vncol, cache_kT, cache_vT)


def _ssm_tables(a_re, a_im, log_dt, b_re, b_im, c_re, c_im, chunk, n_pow):
    t_ = chunk
    lam = lax.complex(a_re.astype(F32), a_im.astype(F32))
    ldt = lam * jnp.exp(log_dt.astype(F32))[:, None]
    a_bar = jnp.exp(ldt)
    b_bar = ((a_bar - 1.0) / lam)[..., None] * lax.complex(b_re.astype(F32), b_im.astype(F32))
    c_c = lax.complex(c_re.astype(F32), c_im.astype(F32))
    taus = jnp.arange(t_ + 1, dtype=F32).astype(jnp.complex64)
    apow = jnp.exp(ldt[None] * taus[:, None, None])
    gq, ssg, c_, p_ = GROUPS_PER_SG, N_SG, SSM_GROUP, SSM_STATE
    c4 = c_c.reshape(ssg, gq, c_, p_)
    b4 = b_bar.reshape(ssg, gq, p_, c_)
    ap4 = apow.reshape(t_ + 1, ssg, gq, p_)

    kc = jnp.einsum('sjcp,tsjp,sjpd->sjdtc', c4, ap4[:t_], b4).real
    kc = kc.reshape(ssg, LANES, t_ * c_)
    mst = jnp.einsum('tsjp,sjpd->stjdp', ap4[:t_][::-1], b4)
    mc = jnp.concatenate([mst.real, mst.imag], axis=-1).reshape(ssg, t_ * LANES, 2 * p_)
    nout = jnp.einsum('sjcp,tsjp->sjptc', c4, ap4[1:])
    nc = jnp.concatenate([nout.real, -nout.imag], axis=1).reshape(ssg, 2 * SG_STATE, t_ * c_)
    rs = jnp.arange(1, n_pow + 1, dtype=F32).astype(jnp.complex64)
    ap = jnp.exp((ldt * t_)[None] * rs[:, None, None])
    ap = ap.reshape(n_pow, ssg, SG_STATE).transpose(1, 0, 2)
    a_pow = jnp.concatenate([ap.real, ap.imag], axis=-1)
    a1 = a_bar.reshape(ssg, 1, SG_STATE)
    a_one = jnp.concatenate([a1.real, a1.imag], axis=-1)
    return kc, mc, nc, a_pow, a_one


def _expand_ssm_weights(kc_ref, mc_ref, nc_ref, wt_s, ws_s, wo_s):
    t_ = SSM_CHUNK
    kw = t_ * LANES
    lg = lambda n: n.bit_length() - 1
    r = lax.broadcasted_iota(jnp.int32, (LANES, kw), 0)
    q = lax.broadcasted_iota(jnp.int32, (LANES, kw), 1)
    sh_c, sh_p = lg(SSM_GROUP), lg(SSM_STATE)
    ecol = jnp.where(((r >> sh_c) == (q >> lg(LANES))) & ((r & (SSM_GROUP - 1)) == (q & (SSM_GROUP - 1))),
                     1.0, 0.0).astype(BF16)
    e2 = jnp.where(((r >> sh_p) == (q >> lg(SG_STATE))) & ((r & (SSM_STATE - 1)) == (q & (SSM_STATE - 1))),
                   1.0, 0.0).astype(BF16)
    col_c = (q & (LANES - 1)) >> sh_c
    taps = jnp.where((r >> sh_c) == col_c, _dot(kc_ref[0].astype(BF16), ecol), 0.0).astype(BF16)
    blank = jnp.zeros((LANES, LANES), BF16)
    for s in range(t_):
        for t in range(t_):
            wt_s[s * LANES:(s + 1) * LANES, t * LANES:(t + 1) * LANES] = (
                taps[:, (t - s) * LANES:(t - s + 1) * LANES] if t >= s else blank)
    rr = lax.broadcasted_iota(jnp.int32, (kw, kw), 0)
    qq = lax.broadcasted_iota(jnp.int32, (kw, kw), 1)
    row_c = (rr & (LANES - 1)) >> sh_c
    row_p = (rr & (SG_STATE - 1)) >> sh_p
    colq_c = (qq & (LANES - 1)) >> sh_c
    colq_p = (qq & (SG_STATE - 1)) >> sh_p
    ws_s[...] = jnp.where(row_c == colq_p, _dot(mc_ref[0].astype(BF16), e2), 0.0).astype(BF16)
    wo_s[...] = jnp.where(row_p == colq_c, _dot(nc_ref[0].astype(BF16), ecol), 0.0).astype(BF16)


def _ssm_prompt_kernel(u_ref, kc_ref, mc_ref, nc_ref, ap_ref, d_ref, h0_ref,
                       y_ref, hf_ref, uc_ref, x_ref, hin_ref, wt_s, ws_s, wo_s):
    t_ = SSM_CHUNK
    nc = uc_ref.shape[0]
    ns = SG_STATE

    @pl.when(pl.program_id(1) == 0)
    def _():
        _expand_ssm_weights(kc_ref, mc_ref, nc_ref, wt_s, ws_s, wo_s)

    for t in range(t_):
        uc_ref[:, t * LANES:(t + 1) * LANES] = u_ref[pl.ds(t, nc, stride=t_), :].astype(BF16)
    uc = uc_ref[...]
    x_ref[...] = _dot(uc, ws_s[...])

    ap = ap_ref[0]
    apr, api = ap[:, :ns], ap[:, ns:]
    row = lax.broadcasted_iota(jnp.int32, (SUBLANES, 1), 0)

    def tile_body(k, carry):
        hr, hi = carry
        rows = pl.ds(k * SUBLANES, SUBLANES)
        xr = x_ref[rows, 0:ns]
        xi = x_ref[rows, ns:2 * ns]
        for d in (1, 2, 4):
            ar, ai = apr[d - 1:d], api[d - 1:d]
            sr = jnp.where(row >= d, pltpu.roll(xr, d, 0), 0.0)
            si = jnp.where(row >= d, pltpu.roll(xi, d, 0), 0.0)
            xr, xi = xr + ar * sr - ai * si, xi + ar * si + ai * sr
        outr = xr + apr * hr - api * hi
        outi = xi + apr * hi + api * hr
        hin_ref[rows, 0:ns] = jnp.where(row >= 1, pltpu.roll(outr, 1, 0), hr)
        hin_ref[rows, ns:2 * ns] = jnp.where(row >= 1, pltpu.roll(outi, 1, 0), hi)
        return outr[SUBLANES - 1:SUBLANES], outi[SUBLANES - 1:SUBLANES]

    y_intra = _dot(uc, wt_s[...])
    h0 = h0_ref[0, 0]
    carry = (h0[:, :ns], h0[:, ns:])
    for k in range(nc // SUBLANES):
        carry = tile_body(k, carry)
    hf_ref[0, 0] = jnp.concatenate(carry, axis=1)

    y = y_intra + _dot(hin_ref[...].astype(BF16), wo_s[...])
    dv = d_ref[...]
    for t in range(t_):
        rows = pl.ds(t, nc, stride=t_)
        y_ref[rows, :] = y[:, t * LANES:(t + 1) * LANES] + dv * u_ref[rows, :]


def _ssm_prompt(u, kc, mc, nc_tab, a_pow, d_row, h0, bsz, seq):
    t_ = SSM_CHUNK
    nc = seq // t_
    kw = t_ * LANES
    assert kc.shape[2] == LANES and 2 * SG_STATE == kw
    wspec = lambda shape: pl.BlockSpec((1,) + shape, lambda s, b: (s, 0, 0))
    return pl.pallas_call(
        _ssm_prompt_kernel,
        grid=(N_SG, bsz),
        in_specs=[pl.BlockSpec((seq, LANES), lambda s, b: (b, s)),
                  wspec(kc.shape[1:]), wspec(mc.shape[1:]), wspec(nc_tab.shape[1:]),
                  wspec((SUBLANES, 2 * SG_STATE)),
                  pl.BlockSpec((1, LANES), lambda s, b: (0, s)),
                  pl.BlockSpec((1, 1, 1, 2 * SG_STATE), lambda s, b: (b, s, 0, 0))],
        out_specs=(pl.BlockSpec((seq, LANES), lambda s, b: (b, s)),
                   pl.BlockSpec((1, 1, 1, 2 * SG_STATE), lambda s, b: (b, s, 0, 0))),
        out_shape=(jax.ShapeDtypeStruct((bsz * seq, SSM_WIDTH), F32),
                   jax.ShapeDtypeStruct((bsz, N_SG, 1, 2 * SG_STATE), F32)),
        scratch_shapes=[pltpu.VMEM((nc, kw), BF16),
                        pltpu.VMEM((nc, 2 * SG_STATE), F32),
                        pltpu.VMEM((nc, 2 * SG_STATE), F32),
                        pltpu.VMEM((kw, kw), BF16),
                        pltpu.VMEM((kw, 2 * SG_STATE), BF16),
                        pltpu.VMEM((2 * SG_STATE, kw), BF16)],
        compiler_params=_cparams(("arbitrary", "arbitrary")),
        name="ssm_prompt",
    )(u, kc, mc, nc_tab, a_pow, d_row, h0)


def _ssm_step_kernel(u_ref, kc_ref, mc_ref, nc_ref, a1_ref, d_ref, h0_ref, y_ref, hf_ref):
    ns = SG_STATE
    kw = 2 * ns
    lg = lambda n: n.bit_length() - 1
    sh_c, sh_p = lg(SSM_GROUP), lg(SSM_STATE)
    iota = lambda shape, d: lax.broadcasted_iota(jnp.int32, shape, d)
    r1, q1 = iota((LANES, LANES), 0), iota((LANES, LANES), 1)
    e_c = jnp.where((r1 < SSM_GROUP) & ((r1 & (SSM_GROUP - 1)) == (q1 & (SSM_GROUP - 1))), 1.0, 0.0).astype(BF16)
    r2, q2 = iota((LANES, kw), 0), iota((LANES, kw), 1)
    e_p = jnp.where(((r2 >> sh_p) == (q2 >> lg(ns))) & ((r2 & (SSM_STATE - 1)) == (q2 & (SSM_STATE - 1))),
                    1.0, 0.0).astype(BF16)
    same_t = (r1 >> sh_c) == (q1 >> sh_c)
    same_s = (r2 >> sh_c) == ((q2 & (ns - 1)) >> sh_p)
    r3, q3 = iota((kw, LANES), 0), iota((kw, LANES), 1)
    same_o = ((r3 & (ns - 1)) >> sh_p) == (q3 >> sh_c)
    last = (SSM_CHUNK - 1) * LANES
    for s in range(N_SG):
        wt = jnp.where(same_t, _dot(kc_ref[s].astype(BF16), e_c), 0.0).astype(BF16)
        ws = jnp.where(same_s, _dot(mc_ref[s, last:last + LANES, :].astype(BF16), e_p), 0.0).astype(BF16)
        wo = jnp.where(same_o, _dot(nc_ref[s].astype(BF16), e_c), 0.0).astype(BF16)
        us = u_ref[:, s * LANES:(s + 1) * LANES]
        ub = us.astype(BF16)
        h0 = h0_ref[s]
        x = _dot(ub, ws)
        ar, ai = a1_ref[s, :, :ns], a1_ref[s, :, ns:]
        hr, hi = h0[:, :ns], h0[:, ns:]
        hf_ref[s] = jnp.concatenate([x[:, :ns] + ar * hr - ai * hi,
                                     x[:, ns:] + ar * hi + ai * hr], axis=1)
        y = _dot(ub, wt) + _dot(h0.astype(BF16), wo)
        y_ref[:, s * LANES:(s + 1) * LANES] = y + d_ref[:, s * LANES:(s + 1) * LANES] * us


def _ssm_step(u, kc, mc, nc_tab, a_one, d_row, h0):
    m = u.shape[0]
    full = lambda a: pl.BlockSpec(a.shape, lambda i: (0,) * a.ndim)
    args = (u, kc, mc, nc_tab, a_one, d_row, h0)
    shapes = [(m, SSM_WIDTH), (N_SG, m, 2 * SG_STATE)]
    return pl.pallas_call(
        _ssm_step_kernel,
        grid=(1,),
        in_specs=[full(a) for a in args],
        out_specs=tuple(pl.BlockSpec(s, lambda i, n=len(s): (0,) * n) for s in shapes),
        out_shape=tuple(jax.ShapeDtypeStruct(s, F32) for s in shapes),
        compiler_params=_cparams(("arbitrary",)),
        name="ssm_step",
    )(*args)


FFN_SPLIT = 2
FFN_CHUNK = FFN_HIDDEN // FFN_SPLIT


def _post_kernel(x_ref, o_ref, y_ref, sga_ref, sgs_ref, wap_ref, wglu_ref, bglu_ref, wsp_ref,
                 wout_ref, nffn_ref, wfi_ref, wfo_ref, nfin_ref, out_ref):
    z = jax.nn.gelu(y_ref[...])
    z = z * jax.nn.sigmoid(_dot(z.astype(BF16), wglu_ref[...]) + bglu_ref[...])
    ssm_out = _dot(z.astype(BF16), wsp_ref[...])
    attn_out = _dot(o_ref[...].astype(BF16), wap_ref[...])
    merged = sga_ref[...].astype(F32) * attn_out + sgs_ref[...].astype(F32) * ssm_out
    x1 = x_ref[...] + _dot(merged.astype(BF16), wout_ref[...])
    hf = _rms(x1, nffn_ref[...]).astype(BF16)
    acc = x1
    for c in range(FFN_SPLIT):
        lo = c * FFN_CHUNK
        a = _dot(hf, wfi_ref[:, lo:lo + FFN_CHUNK])
        g = _dot(hf, wfi_ref[:, FFN_HIDDEN + lo:FFN_HIDDEN + lo + FFN_CHUNK])
        act = (jax.nn.silu(a) * g).astype(BF16)
        acc = acc + _dot(act, wfo_ref[lo:lo + FFN_CHUNK, :])
    out_ref[...] = _rms(acc, nfin_ref[...])


def _post(x2d, o, y, sga, sgs, wap, wglu, bglu, wsp, wout, nffn, wfi, wfo, nfin, tm):
    m = x2d.shape[0]
    tok = lambda w: pl.BlockSpec((tm, w), lambda i: (i, 0))
    const = lambda a: pl.BlockSpec(a.shape, lambda i: (0,) * a.ndim, pipeline_mode=pl.Buffered(1))
    weights = (wap, wglu, bglu, wsp, wout, nffn, wfi, wfo, nfin)
    return pl.pallas_call(
        _post_kernel,
        grid=(m // tm,),
        in_specs=[tok(D_MODEL), tok(ATTN_WIDTH), tok(SSM_WIDTH), tok(D_MODEL), tok(D_MODEL)]
                 + [const(w) for w in weights],
        out_specs=tok(D_MODEL),
        out_shape=jax.ShapeDtypeStruct((m, D_MODEL), F32),
        compiler_params=_cparams(("parallel",)),
        name="post",
    )(x2d, o, y, sga, sgs, *weights)


def _rope_tables(pos):
    half = HEAD_DIM // 2
    inv = jnp.power(jnp.float32(ROPE_THETA), -2.0 * jnp.arange(half, dtype=F32) / HEAD_DIM)
    ang = pos.astype(F32)[:, None] * inv[None, :]
    cos, sin = jnp.cos(ang), jnp.sin(ang)
    reps = LANES // HEAD_DIM
    return (jnp.tile(jnp.concatenate([cos, cos], axis=1), (1, reps)),
            jnp.tile(jnp.concatenate([-sin, sin], axis=1), (1, reps)))


def _leaf_from_T(xT, bsz, seq):
    return xT.reshape(bsz, N_HEADS, HEAD_DIM, seq).transpose(0, 3, 1, 2)[None]


def _state_in(re, im):
    n = re.shape[0]
    h = jnp.concatenate([re.reshape(n, N_SG, SG_STATE), im.reshape(n, N_SG, SG_STATE)], axis=-1)
    return h.transpose(1, 0, 2)


def _state_out(h):
    n = h.shape[0]
    return (h[..., :SG_STATE].reshape(1, n, SSM_GROUPS, SSM_STATE),
            h[..., SG_STATE:].reshape(1, n, SSM_GROUPS, SSM_STATE))


def kernel(x_prompt, x_sample, cache_k, cache_v, state_ssm_re, state_ssm_im, page_table, norm_mix, w_in,
           w_attn_proj, ssm_a_re, ssm_a_im, ssm_log_dt, ssm_b_re, ssm_b_im, ssm_c_re, ssm_c_im, ssm_d, w_glu,
           b_glu, w_ssm_proj, w_out, norm_ffn, w_ffn_in, w_ffn_out, norm_final):
    assert w_in.shape[0] == 1, "single layer"
    bsz, seq = x_prompt.shape[:2]
    nseq = x_sample.shape[0]
    past_len = page_table.shape[1] * PAGE_SIZE
    assert seq % MOBA_BLOCK == 0 and past_len % MOBA_BLOCK == 0 and x_sample.shape[1] == 1

    ssm_p = (ssm_a_re[0], ssm_a_im[0], ssm_log_dt[0], ssm_b_re[0], ssm_b_im[0], ssm_c_re[0], ssm_c_im[0])

    cos_s, sin_s = _rope_tables(jnp.full((1,), past_len, dtype=jnp.int32))
    xs2 = x_sample.reshape(nseq, D_MODEL)
    q_s, k_s, v_s, u_s, sga_s, sgs_s, w_in_bf = _inproj_sample(xs2, norm_mix, w_in[0], cos_s, sin_s)

    cos_p, sin_p = _rope_tables(jnp.arange(seq, dtype=jnp.int32))
    xp2 = x_prompt.reshape(bsz * seq, D_MODEL)
    cast_ws = (w_attn_proj[0], w_glu[0], w_ssm_proj[0], w_out[0], w_ffn_in[0], w_ffn_out[0])
    (qT, kT, kbf, kmean, vT, vTb, u_p, sga_p, sgs_p,
     wap_bf, wglu_bf, wsp_bf, wout_bf, wfi_bf, wfo_bf) = _inproj_prompt(
        xp2, norm_mix, w_in_bf, cos_p, sin_p, cast_ws, bsz, seq, tm=512)
    post_w = (wap_bf, wglu_bf, b_glu, wsp_bf, wout_bf, norm_ffn, wfi_bf, wfo_bf, norm_final[None])
    o_p = _moba_prompt(qT, kbf, kmean.reshape(bsz, seq // MOBA_BLOCK, ATTN_WIDTH), vTb, bsz, seq)
    kc, mc, nc_tab, ap, a_one = _ssm_tables(*ssm_p, chunk=SSM_CHUNK, n_pow=SUBLANES)
    h0_p = jnp.zeros((bsz, N_SG, 1, 2 * SG_STATE), F32)
    y_p, hf_p = _ssm_prompt(u_p, kc, mc, nc_tab, ap, ssm_d, h0_p, bsz, seq)
    y_prompt = _post(xp2, o_p, y_p, sga_p, sgs_p, *post_w, tm=512).reshape(bsz, seq, D_MODEL)
    new_ssm_re_p, new_ssm_im_p = _state_out(hf_p.reshape(bsz, N_SG, 2 * SG_STATE))

    cache_kT = cache_k[0].transpose(0, 2, 3, 1)
    cache_vT = cache_v[0].transpose(0, 2, 3, 1)
    dh = lambda t: t.reshape(nseq, N_HEADS, HEAD_DIM).transpose(0, 2, 1)
    o_s = _moba_paged(page_table, dh(q_s), dh(k_s), dh(v_s), cache_kT, cache_vT).transpose(0, 2, 1)
    h0_s = _state_in(state_ssm_re[0], state_ssm_im[0])
    y_s, hf_s = _ssm_step(u_s, kc, mc, nc_tab, a_one, ssm_d, h0_s)
    y_sample = _post(xs2, o_s.reshape(nseq, ATTN_WIDTH), y_s, sga_s, sgs_s, *post_w, tm=nseq)
    new_ssm_re_s, new_ssm_im_s = _state_out(hf_s.transpose(1, 0, 2))

    return (y_prompt, y_sample.reshape(nseq, 1, D_MODEL),
            _leaf_from_T(kT, bsz, seq), _leaf_from_T(vT, bsz, seq), new_ssm_re_p, new_ssm_im_p,
            k_s.reshape(1, nseq, 1, N_HEADS, HEAD_DIM), v_s.reshape(1, nseq, 1, N_HEADS, HEAD_DIM),
            new_ssm_re_s, new_ssm_im_s)
```

```python
import functools
import math

import jax
import jax.numpy as jnp
from jax import lax
from jax.experimental import pallas as pl
from jax.experimental.pallas import tpu as pltpu

F32 = jnp.float32
BF16 = jnp.bfloat16

D_MODEL = 1024
N_HEADS = 8
HEAD_DIM = 64
ATTN_WIDTH = N_HEADS * HEAD_DIM
MOBA_BLOCK = 256
MOBA_TOPK = 3
ROPE_THETA = 10000.0
SSM_WIDTH = 512
SSM_GROUP = 16
SSM_GROUPS = 32
SSM_STATE = 64
FFN_HIDDEN = 2816
RMS_EPS = 1e-6
PAGE_SIZE = 128

LANES = 128
SUBLANES = 8
BF16_SUBLANES = 16
GROUPS_PER_SG = LANES // SSM_GROUP
N_SG = SSM_GROUPS // GROUPS_PER_SG
SG_STATE = GROUPS_PER_SG * SSM_STATE
SSM_CHUNK = 8
NEG_BIG = -1e30
MOBA_HEADS_PER_STEP = 4
MOBA_LOOKAHEAD = 2
LOG2E = math.log2(math.e)
MOBA_SUM_ROWS = 16
PAGED_SCORE_UNROLL = 4
VMEM_LIMIT = 56 * 1024 * 1024


def _cparams(sem):
    return pltpu.CompilerParams(dimension_semantics=sem, vmem_limit_bytes=VMEM_LIMIT)


def _dot(a, b):
    return jnp.dot(a, b, preferred_element_type=F32)


def _rms(x, g):
    return x * lax.rsqrt(jnp.mean(x * x, axis=-1, keepdims=True) + RMS_EPS) * g


def _inproj_core(x, g, w_ref, cos, sin, q_scale):
    h = _rms(x, g).astype(BF16)
    lane = lax.broadcasted_iota(jnp.int32, (1, ATTN_WIDTH), 1)
    first_half = (lane % HEAD_DIM) < (HEAD_DIM // 2)

    def rot(t):
        partner = jnp.where(first_half,
                            pltpu.roll(t, ATTN_WIDTH - HEAD_DIM // 2, 1),
                            pltpu.roll(t, HEAD_DIM // 2, 1))
        return t * cos + partner * sin

    a = ATTN_WIDTH
    q = rot(_dot(h, w_ref[:, 0:a])) * q_scale
    k = rot(_dot(h, w_ref[:, a:2 * a]))
    v = _dot(h, w_ref[:, 2 * a:3 * a])
    u = _dot(h, w_ref[:, 3 * a:3 * a + SSM_WIDTH])
    o = 3 * a + SSM_WIDTH
    sga = jax.nn.sigmoid(_dot(h, w_ref[:, o:o + D_MODEL]))
    sgs = jax.nn.sigmoid(_dot(h, w_ref[:, o + D_MODEL:o + 2 * D_MODEL]))
    return q, k, v, u, sga, sgs


def _inproj_prompt_kernel(x_ref, g_ref, w_ref, cos_ref, sin_ref, *rest):
    n_cast = (len(rest) - 9) // 2
    cast_in = rest[:n_cast]
    qT_ref, kT_ref, kbf_ref, kmean_ref, vT_ref, vTb_ref, u_ref, sga_ref, sgs_ref = rest[n_cast:n_cast + 9]
    cast_out = rest[n_cast + 9:]
    for wi_ref, wo_ref in zip(cast_in, cast_out):
        wo_ref[...] = wi_ref[...].astype(BF16)
    reps = ATTN_WIDTH // LANES
    cos = jnp.tile(cos_ref[...], (1, reps))
    sin = jnp.tile(sin_ref[...], (1, reps))
    q, k, v, u, sga, sgs = _inproj_core(x_ref[...], g_ref[...], w_ref, cos, sin, LOG2E * HEAD_DIM ** -0.5)
    tm = q.shape[0]
    kT_ref[0] = k.T
    vT_ref[0] = v.T
    kbf_ref[...] = k.astype(BF16)
    for s in range(tm // MOBA_BLOCK):
        rows = slice(s * MOBA_BLOCK, (s + 1) * MOBA_BLOCK)
        qT_ref[0, s] = q[rows].T.astype(BF16)
        vTb_ref[0, s] = v[rows].T.astype(BF16)
        kmean_ref[0, s] = jnp.mean(k[rows], axis=0, keepdims=True)
    u_ref[...] = u
    sga_ref[...] = sga.astype(BF16)
    sgs_ref[...] = sgs.astype(BF16)


def _inproj_prompt(x2d, g, w_bf, cos, sin, cast_ws, bsz, seq, tm):
    m = bsz * seq
    nb = seq // MOBA_BLOCK
    tpb = seq // tm
    sub = tm // MOBA_BLOCK
    a = ATTN_WIDTH
    full = lambda shape: pl.BlockSpec(shape, lambda b, t: (0,) * len(shape))
    tok = lambda w: pl.BlockSpec((tm, w), lambda b, t: (b * tpb + t, 0))
    out_shape = (
        jax.ShapeDtypeStruct((bsz, nb, a, MOBA_BLOCK), BF16),
        jax.ShapeDtypeStruct((bsz, a, seq), F32),
        jax.ShapeDtypeStruct((m, a), BF16),
        jax.ShapeDtypeStruct((bsz, nb, 1, a), F32),
        jax.ShapeDtypeStruct((bsz, a, seq), F32),
        jax.ShapeDtypeStruct((bsz, nb, a, MOBA_BLOCK), BF16),
        jax.ShapeDtypeStruct((m, SSM_WIDTH), F32),
        jax.ShapeDtypeStruct((m, D_MODEL), BF16),
        jax.ShapeDtypeStruct((m, D_MODEL), BF16),
    )
    blk_t = pl.BlockSpec((1, sub, a, MOBA_BLOCK), lambda b, t: (b, t, 0, 0))
    lane_t = pl.BlockSpec((1, a, tm), lambda b, t: (b, 0, t))
    out_specs = (blk_t, lane_t, tok(a),
                 pl.BlockSpec((1, sub, 1, a), lambda b, t: (b, t, 0, 0)),
                 lane_t, blk_t, tok(SSM_WIDTH), tok(D_MODEL), tok(D_MODEL))
    n_steps = bsz * tpb
    cast_specs = []
    for w in cast_ws:
        rows = w.shape[0]
        nblk = n_steps
        while rows % nblk or (rows // nblk) % BF16_SUBLANES:
            nblk //= 2
        cast_specs.append(pl.BlockSpec((rows // nblk, w.shape[1]),
                                       lambda b, t, n=nblk: (jnp.minimum(b * tpb + t, n - 1), 0)))
    return pl.pallas_call(
        _inproj_prompt_kernel,
        grid=(bsz, tpb),
        in_specs=[tok(D_MODEL), full((1, D_MODEL)), full(w_bf.shape),
                  pl.BlockSpec((tm, LANES), lambda b, t: (t, 0)),
                  pl.BlockSpec((tm, LANES), lambda b, t: (t, 0))] + cast_specs,
        out_specs=out_specs + tuple(cast_specs),
        out_shape=out_shape + tuple(jax.ShapeDtypeStruct(w.shape, BF16) for w in cast_ws),
        compiler_params=_cparams(("arbitrary", "arbitrary")),
        name="inproj_prompt",
    )(x2d, g, w_bf, cos, sin, *cast_ws)


def _inproj_sample_kernel(x_ref, g_ref, w_ref, cos_ref, sin_ref,
                          q_ref, k_ref, v_ref, u_ref, sga_ref, sgs_ref, wb_ref):
    step = ATTN_WIDTH
    for c in range(w_ref.shape[1] // step):
        wb_ref[:, c * step:(c + 1) * step] = w_ref[:, c * step:(c + 1) * step].astype(BF16)
    reps = ATTN_WIDTH // LANES
    cos = jnp.tile(cos_ref[...], (1, reps))
    sin = jnp.tile(sin_ref[...], (1, reps))
    q, k, v, u, sga, sgs = _inproj_core(x_ref[...], g_ref[...], wb_ref, cos, sin, HEAD_DIM ** -0.5)
    q_ref[...] = q
    k_ref[...] = k
    v_ref[...] = v
    u_ref[...] = u
    sga_ref[...] = sga.astype(BF16)
    sgs_ref[...] = sgs.astype(BF16)


def _inproj_sample(x2d, g, w_f32, cos, sin):
    m = x2d.shape[0]
    a = ATTN_WIDTH
    full = lambda shape, **kw: pl.BlockSpec(shape, lambda i: (0,) * len(shape), **kw)
    shapes = [(m, a), (m, a), (m, a), (m, SSM_WIDTH), (m, D_MODEL), (m, D_MODEL), w_f32.shape]
    dts = [F32, F32, F32, F32, BF16, BF16, BF16]
    return pl.pallas_call(
        _inproj_sample_kernel,
        grid=(1,),
        in_specs=[full(x2d.shape), full(g.shape), full(w_f32.shape, pipeline_mode=pl.Buffered(1)),
                  full(cos.shape), full(sin.shape)],
        out_specs=tuple(full(s) for s in shapes),
        out_shape=tuple(jax.ShapeDtypeStruct(s, d) for s, d in zip(shapes, dts)),
        compiler_params=_cparams(("arbitrary",)),
        name="inproj_sample",
    )(x2d, g, w_f32, cos, sin)


def _moba_prompt_kernel(qT_ref, k_ref, kmean_ref, vT_ref, o_ref, bias_ref, qz_s, m_s, acc_s):
    nb = qT_ref.shape[1]
    blk = MOBA_BLOCK
    nh = MOBA_HEADS_PER_STEP
    row2 = lax.broadcasted_iota(jnp.int32, (nh * HEAD_DIM, 1), 0)
    lane_km = lax.broadcasted_iota(jnp.int32, (1, nh * HEAD_DIM), 1)
    blk_row = lax.broadcasted_iota(jnp.int32, (nb, blk), 0)
    key_i = lax.broadcasted_iota(jnp.int32, (blk, blk), 0)
    qry_i = lax.broadcasted_iota(jnp.int32, (blk, blk), 1)
    causal = key_i <= qry_i
    in_head = [(row2 >= hh * HEAD_DIM) & (row2 < (hh + 1) * HEAD_DIM) for hh in range(nh)]
    hrows = [slice(hh * HEAD_DIM, (hh + 1) * HEAD_DIM) for hh in range(nh)]

    km_all = jnp.concatenate(
        [jnp.where((lane_km >= hh * HEAD_DIM) & (lane_km < (hh + 1) * HEAD_DIM), kmean_ref[0], 0.0)
         for hh in range(nh)], axis=0)
    km_terms = []
    rest = km_all
    for _ in range(3):
        term = rest.astype(BF16)
        km_terms.append(term)
        rest = rest - term.astype(F32)
    km_split = jnp.concatenate(km_terms, axis=0)
    for hh in range(nh):
        bias_ref[hh, 0] = jnp.full((nb, blk), NEG_BIG, F32)
    for i in range(1, nb):
        sb3 = _dot(km_split, qT_ref[0, i])
        sb_all = (sb3[0:nh * nb] + sb3[nh * nb:2 * nh * nb]) + sb3[2 * nh * nb:]
        for hh in range(nh):
            sb = jnp.where(blk_row < i, sb_all[hh * nb:(hh + 1) * nb], -jnp.inf)
            bias = jnp.full((nb, blk), NEG_BIG, F32)
            for _r in range(min(MOBA_TOPK, i)):
                mx = jnp.max(sb, axis=0, keepdims=True)
                first = jnp.min(jnp.where(sb == mx, blk_row, nb), axis=0, keepdims=True)
                pick = blk_row == first
                bias = jnp.where(pick, 0.0, bias)
                sb = jnp.where(pick, -jnp.inf, sb)
            bias_ref[hh, i] = bias

    n_items = nb // 2
    ones_rows = jnp.ones((MOBA_SUM_ROWS, 2 * blk), BF16)

    def couple(i, _):
        iq = (i, nb - 1 - i)
        n_first = (i + 1) // 2

        def diag_scores(x):
            q_pair = qT_ref[0, iq[x]]
            k_own = k_ref[pl.ds(pl.multiple_of(iq[x] * blk, blk), blk), :]
            out = []
            for hh in range(nh):
                qz = jnp.where(in_head[hh], q_pair, jnp.zeros_like(q_pair))
                qz_s[x, hh] = qz
                out.append(_dot(k_own, qz))
            return out

        def diag_absorb(x, s_own):
            for hh in range(nh):
                s = jnp.where(causal, s_own[hh], NEG_BIG)
                m0 = jnp.max(s, axis=0, keepdims=True)
                p = jnp.exp2(s - m0)
                m_s[x, hh] = m0
                vv = jnp.concatenate([vT_ref[0, iq[x], hrows[hh], :], ones_rows[:, :blk]], axis=0)
                acc_s[x, hh] = _dot(vv, p.astype(BF16))

        def item_params(k):
            first = k < n_first
            x = jnp.where(first, 0, 1)
            return x, jnp.where(first, iq[0], iq[1]), jnp.where(first, k, k - n_first)

        def item_scores(k):
            x, _, jp = item_params(k)
            kk = k_ref[pl.ds(pl.multiple_of(2 * jp * blk, blk), 2 * blk), :]
            out = []
            for hh in range(nh):
                s = _dot(kk, qz_s[x, hh])
                out.append((s, jnp.max(s[:blk], axis=0, keepdims=True), jnp.max(s[blk:], axis=0, keepdims=True)))
            return out

        def item_absorb(k, sc):
            x, qb, jp = item_params(k)
            j0 = 2 * jp
            for hh in range(nh):
                s, cma, cmb = sc[hh]
                ba = bias_ref[hh, qb, pl.ds(j0, 1), :]
                bb = bias_ref[hh, qb, pl.ds(j0 + 1, 1), :]
                m = m_s[x, hh]
                m_new = jnp.maximum(m, jnp.maximum(cma + ba, cmb + bb))
                alpha = jnp.exp2(m - m_new)
                pa = jnp.exp2(s[:blk] - (m_new - ba))
                pb = jnp.exp2(s[blk:] - (m_new - bb))
                m_s[x, hh] = m_new
                pp = jnp.concatenate([pa, pb], axis=0).astype(BF16)
                vv = jnp.concatenate([vT_ref[0, j0, hrows[hh], :], vT_ref[0, j0 + 1, hrows[hh], :]], axis=1)
                vv = jnp.concatenate([vv, ones_rows], axis=0)
                acc_s[x, hh] = alpha * acc_s[x, hh] + _dot(vv, pp)

        s_diag = [diag_scores(0), diag_scores(1)]
        pending = [item_scores(k) for k in range(min(MOBA_LOOKAHEAD, n_items))]
        diag_absorb(0, s_diag[0])
        diag_absorb(1, s_diag[1])
        for k in range(n_items):
            if k + MOBA_LOOKAHEAD < n_items:
                pending.append(item_scores(k + MOBA_LOOKAHEAD))
            item_absorb(k, pending[k])
        for x in range(2):
            oT = jnp.concatenate([acc_s[x, hh, 0:HEAD_DIM, :] / acc_s[x, hh, HEAD_DIM:HEAD_DIM + 1, :]
                                  for hh in range(nh)], axis=0)
            o_ref[pl.ds(pl.multiple_of(iq[x] * blk, blk), blk), :] = oT.T.astype(o_ref.dtype)
        return 0

    lax.fori_loop(0, nb // 2, couple, 0)


def _moba_prompt(qT, kbf, kmean, vTb, bsz, seq):
    nb = seq // MOBA_BLOCK
    hp = N_HEADS // MOBA_HEADS_PER_STEP
    pair = MOBA_HEADS_PER_STEP * HEAD_DIM
    return pl.pallas_call(
        _moba_prompt_kernel,
        grid=(bsz, hp),
        in_specs=[pl.BlockSpec((1, nb, pair, MOBA_BLOCK), lambda b, h: (b, 0, h, 0)),
                  pl.BlockSpec((seq, pair), lambda b, h: (b, h)),
                  pl.BlockSpec((1, nb, pair), lambda b, h: (b, 0, h)),
                  pl.BlockSpec((1, nb, pair, MOBA_BLOCK), lambda b, h: (b, 0, h, 0))],
        out_specs=pl.BlockSpec((seq, pair), lambda b, h: (b, h)),
        out_shape=jax.ShapeDtypeStruct((bsz * seq, ATTN_WIDTH), BF16),
        scratch_shapes=[pltpu.VMEM((MOBA_HEADS_PER_STEP, nb, nb, MOBA_BLOCK), F32),
                        pltpu.VMEM((2, MOBA_HEADS_PER_STEP, pair, MOBA_BLOCK), BF16),
                        pltpu.VMEM((2, MOBA_HEADS_PER_STEP, 1, MOBA_BLOCK), F32),
                        pltpu.VMEM((2, MOBA_HEADS_PER_STEP, HEAD_DIM + MOBA_SUM_ROWS, MOBA_BLOCK), F32)],
        compiler_params=_cparams(("parallel", "parallel")),
        name="moba_prompt",
    )(qT, kbf, kmean, vTb)


def _moba_paged_kernel(pt_ref, qcol_ref, kn_ref, vn_ref, ck_ref, cv_ref, o_ref,
                       kbuf, vbuf, s_ref, psel_ref, stat_ref, qb_ref, ksem, vsem):
    b = pl.program_id(0)
    nseq = pl.num_programs(0) - 1
    hpages = kbuf.shape[1]
    nblk = hpages
    hblk = nblk // 2
    cur_slot = 2 * (b % 2)
    nxt_slot = 2 - cur_slot

    def k_copy(seq_i, half, sl, p):
        return pltpu.make_async_copy(ck_ref.at[pt_ref[seq_i, half * hpages + p]], kbuf.at[sl, p], ksem.at[sl])

    def start_k(seq_i, half, sl):
        def body(p, _):
            k_copy(seq_i, half, sl, p).start()
            return 0
        lax.fori_loop(0, hpages, body, 0)

    def wait_k(seq_i, half, sl):
        def body(p, _):
            k_copy(seq_i, half, sl, p).wait()
            return 0
        lax.fori_loop(0, hpages, body, 0)

    def v_copy(page, h, r, par):
        return pltpu.make_async_copy(cv_ref.at[page, h], vbuf.at[h, r, par], vsem.at[0])

    @pl.when(b == 0)
    def _():
        start_k(0, 0, 0)
        start_k(0, 1, 1)

    @pl.when(b >= 1)
    def _():
        for h in range(N_HEADS):
            for r in range(MOBA_TOPK):
                for par in range(2):
                    v_copy(0, h, r, par).wait()
        vn = vn_ref[0]
        for h in range(N_HEADS):
            acc = jnp.zeros((HEAD_DIM, PAGE_SIZE), F32)
            for r in range(MOBA_TOPK):
                for par in range(2):
                    acc = acc + vbuf[h, r, par] * psel_ref[h, 2 * r + par]
            p_own = stat_ref[0, h][:, 0:1]
            l = stat_ref[1, h][:, 0:1]
            o_h = jnp.sum(acc, axis=1, keepdims=True) + p_own * vn[:, h:h + 1]
            o_ref[0, :, h:h + 1] = o_h / l

    @pl.when(b < nseq)
    def _():
        qcol = qcol_ref[0]
        own = jnp.sum(qcol * kn_ref[0], axis=0, keepdims=True)
        blk_i = lax.broadcasted_iota(jnp.int32, (nblk, 1), 0)
        for h in range(N_HEADS):
            qb_ref[h] = jnp.broadcast_to(qcol[:, h:h + 1], (HEAD_DIM, PAGE_SIZE))
        for half in range(2):
            wait_k(b, half, cur_slot + half)

            @pl.when(b + 1 < nseq)
            def _():
                start_k(b + 1, half, nxt_slot + half)

            for h in range(N_HEADS):
                def score_body(g, _):
                    qb = qb_ref[h]
                    for dn in range(PAGED_SCORE_UNROLL):
                        n = g * PAGED_SCORE_UNROLL + dn
                        for par in range(2):
                            kt = kbuf[cur_slot + half, 2 * n + par, h]
                            s_ref[par, h, pl.ds(half * hblk + n, 1), :] = jnp.sum(kt * qb, axis=0, keepdims=True)
                    return 0
                lax.fori_loop(0, hblk // PAGED_SCORE_UNROLL, score_body, 0)

        for h in range(N_HEADS):
            s0 = s_ref[0, h]
            s1 = s_ref[1, h]
            bs = jnp.sum(s0 + s1, axis=1, keepdims=True)
            sel = jnp.zeros((nblk, 1), dtype=jnp.bool_)
            picks = []
            for r in range(MOBA_TOPK):
                mx = jnp.max(bs, axis=0, keepdims=True)
                first = jnp.min(jnp.where(bs == mx, blk_i, nblk), axis=0, keepdims=True)
                pick = blk_i == first
                sel = sel | pick
                bs = jnp.where(pick, -jnp.inf, bs)
                blk_id = jnp.max(first)
                for par in range(2):
                    v_copy(pt_ref[b, 2 * blk_id + par], h, r, par).start()
                picks.append(blk_id)
            s_own = own[:, h:h + 1]
            sm0 = jnp.where(sel, s0, NEG_BIG)
            sm1 = jnp.where(sel, s1, NEG_BIG)
            mx = jnp.maximum(jnp.max(jnp.max(jnp.maximum(sm0, sm1), axis=1, keepdims=True),
                                     axis=0, keepdims=True), s_own)
            p0 = jnp.exp(sm0 - mx)
            p1 = jnp.exp(sm1 - mx)
            p_own = jnp.exp(s_own - mx)
            l = jnp.sum(jnp.sum(p0 + p1, axis=1, keepdims=True), axis=0, keepdims=True) + p_own
            s_ref[0, h] = p0
            s_ref[1, h] = p1
            for r in range(MOBA_TOPK):
                for par in range(2):
                    psel_ref[h, 2 * r + par] = s_ref[par, h, pl.ds(picks[r], 1), :]
            stat_ref[0, h] = jnp.broadcast_to(p_own, (1, PAGE_SIZE))
            stat_ref[1, h] = jnp.broadcast_to(l, (1, PAGE_SIZE))


def _moba_paged(page_table, qcol, kncol, vncol, cache_kT, cache_vT):
    nseq, n_pages = page_table.shape
    assert n_pages // 2 >= MOBA_TOPK, "every pick must find an unpicked cached block"
    cur = pl.BlockSpec((1, HEAD_DIM, N_HEADS), lambda b, pt: (jnp.minimum(b, nseq - 1), 0, 0))
    prev = pl.BlockSpec((1, HEAD_DIM, N_HEADS), lambda b, pt: (jnp.maximum(b - 1, 0), 0, 0))
    any_spec = pl.BlockSpec(memory_space=pl.ANY)
    grid_spec = pltpu.PrefetchScalarGridSpec(
        num_scalar_prefetch=1,
        grid=(nseq + 1,),
        in_specs=[cur, cur, prev, any_spec, any_spec],
        out_specs=prev,
        scratch_shapes=[
            pltpu.VMEM((4, n_pages // 2, N_HEADS, HEAD_DIM, PAGE_SIZE), F32),
            pltpu.VMEM((N_HEADS, MOBA_TOPK, 2, HEAD_DIM, PAGE_SIZE), F32),
            pltpu.VMEM((2, N_HEADS, n_pages // 2, PAGE_SIZE), F32),
            pltpu.VMEM((N_HEADS, 2 * MOBA_TOPK, 1, PAGE_SIZE), F32),
            pltpu.VMEM((2, N_HEADS, 1, PAGE_SIZE), F32),
            pltpu.VMEM((N_HEADS, HEAD_DIM, PAGE_SIZE), F32),
            pltpu.SemaphoreType.DMA((4,)),
            pltpu.SemaphoreType.DMA((1,)),
        ])
    return pl.pallas_call(
        _moba_paged_kernel,
        grid_spec=grid_spec,
        out_shape=jax.ShapeDtypeStruct((nseq, HEAD_DIM, N_HEADS), F32),
        compiler_params=_cparams(("arbitrary",)),
        name="moba_paged",
    )(page_table, qcol, kncol, vncol, cache_kT, cache_vT)


def _ssm_tables(a_re, a_im, log_dt, b_re, b_im, c_re, c_im, chunk, n_pow):
    t_ = chunk
    lam = lax.complex(a_re.astype(F32), a_im.astype(F32))
    ldt = lam * jnp.exp(log_dt.astype(F32))[:, None]
    a_bar = jnp.exp(ldt)
    b_bar = ((a_bar - 1.0) / lam)[..., None] * lax.complex(b_re.astype(F32), b_im.astype(F32))
    c_c = lax.complex(c_re.astype(F32), c_im.astype(F32))
    taus = jnp.arange(t_ + 1, dtype=F32).astype(jnp.complex64)
    apow = jnp.exp(ldt[None] * taus[:, None, None])
    gq, ssg, c_, p_ = GROUPS_PER_SG, N_SG, SSM_GROUP, SSM_STATE
    c4 = c_c.reshape(ssg, gq, c_, p_)
    b4 = b_bar.reshape(ssg, gq, p_, c_)
    ap4 = apow.reshape(t_ + 1, ssg, gq, p_)

    kc = jnp.einsum('sjcp,tsjp,sjpd->sjdtc', c4, ap4[:t_], b4).real
    kc = kc.reshape(ssg, LANES, t_ * c_)
    mst = jnp.einsum('tsjp,sjpd->stjdp', ap4[:t_][::-1], b4)
    mc = jnp.concatenate([mst.real, mst.imag], axis=-1).reshape(ssg, t_ * LANES, 2 * p_)
    nout = jnp.einsum('sjcp,tsjp->sjptc', c4, ap4[1:])
    nc = jnp.concatenate([nout.real, -nout.imag], axis=1).reshape(ssg, 2 * SG_STATE, t_ * c_)
    rs = jnp.arange(1, n_pow + 1, dtype=F32).astype(jnp.complex64)
    ap = jnp.exp((ldt * t_)[None] * rs[:, None, None])
    ap = ap.reshape(n_pow, ssg, SG_STATE).transpose(1, 0, 2)
    a_pow = jnp.concatenate([ap.real, ap.imag], axis=-1)
    a1 = a_bar.reshape(ssg, 1, SG_STATE)
    a_one = jnp.concatenate([a1.real, a1.imag], axis=-1)
    return kc, mc, nc, a_pow, a_one


def _expand_ssm_weights(kc_ref, mc_ref, nc_ref, wt_s, ws_s, wo_s):
    t_ = SSM_CHUNK
    kw = t_ * LANES
    lg = lambda n: n.bit_length() - 1
    r = lax.broadcasted_iota(jnp.int32, (LANES, kw), 0)
    q = lax.broadcasted_iota(jnp.int32, (LANES, kw), 1)
    sh_c, sh_p = lg(SSM_GROUP), lg(SSM_STATE)
    ecol = jnp.where(((r >> sh_c) == (q >> lg(LANES))) & ((r & (SSM_GROUP - 1)) == (q & (SSM_GROUP - 1))),
                     1.0, 0.0).astype(BF16)
    e2 = jnp.where(((r >> sh_p) == (q >> lg(SG_STATE))) & ((r & (SSM_STATE - 1)) == (q & (SSM_STATE - 1))),
                   1.0, 0.0).astype(BF16)
    col_c = (q & (LANES - 1)) >> sh_c
    taps = jnp.where((r >> sh_c) == col_c, _dot(kc_ref[0].astype(BF16), ecol), 0.0).astype(BF16)
    blank = jnp.zeros((LANES, LANES), BF16)
    for s in range(t_):
        for t in range(t_):
            wt_s[s * LANES:(s + 1) * LANES, t * LANES:(t + 1) * LANES] = (
                taps[:, (t - s) * LANES:(t - s + 1) * LANES] if t >= s else blank)
    rr = lax.broadcasted_iota(jnp.int32, (kw, kw), 0)
    qq = lax.broadcasted_iota(jnp.int32, (kw, kw), 1)
    row_c = (rr & (LANES - 1)) >> sh_c
    row_p = (rr & (SG_STATE - 1)) >> sh_p
    colq_c = (qq & (LANES - 1)) >> sh_c
    colq_p = (qq & (SG_STATE - 1)) >> sh_p
    ws_s[...] = jnp.where(row_c == colq_p, _dot(mc_ref[0].astype(BF16), e2), 0.0).astype(BF16)
    wo_s[...] = jnp.where(row_p == colq_c, _dot(nc_ref[0].astype(BF16), ecol), 0.0).astype(BF16)


def _ssm_prompt_kernel(u_ref, kc_ref, mc_ref, nc_ref, ap_ref, d_ref, h0_ref,
                       y_ref, hf_ref, uc_ref, x_ref, hin_ref, wt_s, ws_s, wo_s):
    t_ = SSM_CHUNK
    nc = uc_ref.shape[0]
    ns = SG_STATE

    @pl.when(pl.program_id(1) == 0)
    def _():
        _expand_ssm_weights(kc_ref, mc_ref, nc_ref, wt_s, ws_s, wo_s)

    for t in range(t_):
        uc_ref[:, t * LANES:(t + 1) * LANES] = u_ref[pl.ds(t, nc, stride=t_), :].astype(BF16)
    uc = uc_ref[...]
    x_ref[...] = _dot(uc, ws_s[...])

    ap = ap_ref[0]
    apr, api = ap[:, :ns], ap[:, ns:]
    row = lax.broadcasted_iota(jnp.int32, (SUBLANES, 1), 0)

    def tile_body(k, carry):
        hr, hi = carry
        rows = pl.ds(k * SUBLANES, SUBLANES)
        xr = x_ref[rows, 0:ns]
        xi = x_ref[rows, ns:2 * ns]
        for d in (1, 2, 4):
            ar, ai = apr[d - 1:d], api[d - 1:d]
            sr = jnp.where(row >= d, pltpu.roll(xr, d, 0), 0.0)
            si = jnp.where(row >= d, pltpu.roll(xi, d, 0), 0.0)
            xr, xi = xr + ar * sr - ai * si, xi + ar * si + ai * sr
        outr = xr + apr * hr - api * hi
        outi = xi + apr * hi + api * hr
        hin_ref[rows, 0:ns] = jnp.where(row >= 1, pltpu.roll(outr, 1, 0), hr)
        hin_ref[rows, ns:2 * ns] = jnp.where(row >= 1, pltpu.roll(outi, 1, 0), hi)
        return outr[SUBLANES - 1:SUBLANES], outi[SUBLANES - 1:SUBLANES]

    y_intra = _dot(uc, wt_s[...])
    h0 = h0_ref[0, 0]
    carry = (h0[:, :ns], h0[:, ns:])
    for k in range(nc // SUBLANES):
        carry = tile_body(k, carry)
    hf_ref[0, 0] = jnp.concatenate(carry, axis=1)

    y = y_intra + _dot(hin_ref[...].astype(BF16), wo_s[...])
    dv = d_ref[...]
    for t in range(t_):
        rows = pl.ds(t, nc, stride=t_)
        y_ref[rows, :] = y[:, t * LANES:(t + 1) * LANES] + dv * u_ref[rows, :]


def _ssm_prompt(u, kc, mc, nc_tab, a_pow, d_row, h0, bsz, seq):
    t_ = SSM_CHUNK
    nc = seq // t_
    kw = t_ * LANES
    assert kc.shape[2] == LANES and 2 * SG_STATE == kw
    wspec = lambda shape: pl.BlockSpec((1,) + shape, lambda s, b: (s, 0, 0))
    return pl.pallas_call(
        _ssm_prompt_kernel,
        grid=(N_SG, bsz),
        in_specs=[pl.BlockSpec((seq, LANES), lambda s, b: (b, s)),
                  wspec(kc.shape[1:]), wspec(mc.shape[1:]), wspec(nc_tab.shape[1:]),
                  wspec((SUBLANES, 2 * SG_STATE)),
                  pl.BlockSpec((1, LANES), lambda s, b: (0, s)),
                  pl.BlockSpec((1, 1, 1, 2 * SG_STATE), lambda s, b: (b, s, 0, 0))],
        out_specs=(pl.BlockSpec((seq, LANES), lambda s, b: (b, s)),
                   pl.BlockSpec((1, 1, 1, 2 * SG_STATE), lambda s, b: (b, s, 0, 0))),
        out_shape=(jax.ShapeDtypeStruct((bsz * seq, SSM_WIDTH), F32),
                   jax.ShapeDtypeStruct((bsz, N_SG, 1, 2 * SG_STATE), F32)),
        scratch_shapes=[pltpu.VMEM((nc, kw), BF16),
                        pltpu.VMEM((nc, 2 * SG_STATE), F32),
                        pltpu.VMEM((nc, 2 * SG_STATE), F32),
                        pltpu.VMEM((kw, kw), BF16),
                        pltpu.VMEM((kw, 2 * SG_STATE), BF16),
                        pltpu.VMEM((2 * SG_STATE, kw), BF16)],
        compiler_params=_cparams(("arbitrary", "arbitrary")),
        name="ssm_prompt",
    )(u, kc, mc, nc_tab, a_pow, d_row, h0)


def _ssm_step_kernel(u_ref, kc_ref, mc_ref, nc_ref, a1_ref, d_ref, h0_ref, y_ref, hf_ref):
    ns = SG_STATE
    kw = 2 * ns
    lg = lambda n: n.bit_length() - 1
    sh_c, sh_p = lg(SSM_GROUP), lg(SSM_STATE)
    iota = lambda shape, d: lax.broadcasted_iota(jnp.int32, shape, d)
    r1, q1 = iota((LANES, LANES), 0), iota((LANES, LANES), 1)
    e_c = jnp.where((r1 < SSM_GROUP) & ((r1 & (SSM_GROUP - 1)) == (q1 & (SSM_GROUP - 1))), 1.0, 0.0).astype(BF16)
    r2, q2 = iota((LANES, kw), 0), iota((LANES, kw), 1)
    e_p = jnp.where(((r2 >> sh_p) == (q2 >> lg(ns))) & ((r2 & (SSM_STATE - 1)) == (q2 & (SSM_STATE - 1))),
                    1.0, 0.0).astype(BF16)
    same_t = (r1 >> sh_c) == (q1 >> sh_c)
    same_s = (r2 >> sh_c) == ((q2 & (ns - 1)) >> sh_p)
    r3, q3 = iota((kw, LANES), 0), iota((kw, LANES), 1)
    same_o = ((r3 & (ns - 1)) >> sh_p) == (q3 >> sh_c)
    last = (SSM_CHUNK - 1) * LANES
    for s in range(N_SG):
        wt = jnp.where(same_t, _dot(kc_ref[s].astype(BF16), e_c), 0.0).astype(BF16)
        ws = jnp.where(same_s, _dot(mc_ref[s, last:last + LANES, :].astype(BF16), e_p), 0.0).astype(BF16)
        wo = jnp.where(same_o, _dot(nc_ref[s].astype(BF16), e_c), 0.0).astype(BF16)
        us = u_ref[:, s * LANES:(s + 1) * LANES]
        ub = us.astype(BF16)
        h0 = h0_ref[s]
        x = _dot(ub, ws)
        ar, ai = a1_ref[s, :, :ns], a1_ref[s, :, ns:]
        hr, hi = h0[:, :ns], h0[:, ns:]
        hf_ref[s] = jnp.concatenate([x[:, :ns] + ar * hr - ai * hi,
                                     x[:, ns:] + ar * hi + ai * hr], axis=1)
        y = _dot(ub, wt) + _dot(h0.astype(BF16), wo)
        y_ref[:, s * LANES:(s + 1) * LANES] = y + d_ref[:, s * LANES:(s + 1) * LANES] * us


def _ssm_step(u, kc, mc, nc_tab, a_one, d_row, h0):
    m = u.shape[0]
    full = lambda a: pl.BlockSpec(a.shape, lambda i: (0,) * a.ndim)
    args = (u, kc, mc, nc_tab, a_one, d_row, h0)
    shapes = [(m, SSM_WIDTH), (N_SG, m, 2 * SG_STATE)]
    return pl.pallas_call(
        _ssm_step_kernel,
        grid=(1,),
        in_specs=[full(a) for a in args],
        out_specs=tuple(pl.BlockSpec(s, lambda i, n=len(s): (0,) * n) for s in shapes),
        out_shape=tuple(jax.ShapeDtypeStruct(s, F32) for s in shapes),
        compiler_params=_cparams(("arbitrary",)),
        name="ssm_step",
    )(*args)


FFN_SPLIT = 2
FFN_CHUNK = FFN_HIDDEN // FFN_SPLIT


def _post_kernel(x_ref, o_ref, y_ref, sga_ref, sgs_ref, wap_ref, wglu_ref, bglu_ref, wsp_ref,
                 wout_ref, nffn_ref, wfi_ref, wfo_ref, nfin_ref, out_ref):
    z = jax.nn.gelu(y_ref[...])
    z = z * jax.nn.sigmoid(_dot(z.astype(BF16), wglu_ref[...]) + bglu_ref[...])
    ssm_out = _dot(z.astype(BF16), wsp_ref[...])
    attn_out = _dot(o_ref[...].astype(BF16), wap_ref[...])
    merged = sga_ref[...].astype(F32) * attn_out + sgs_ref[...].astype(F32) * ssm_out
    x1 = x_ref[...] + _dot(merged.astype(BF16), wout_ref[...])
    hf = _rms(x1, nffn_ref[...]).astype(BF16)
    acc = x1
    for c in range(FFN_SPLIT):
        lo = c * FFN_CHUNK
        a = _dot(hf, wfi_ref[:, lo:lo + FFN_CHUNK])
        g = _dot(hf, wfi_ref[:, FFN_HIDDEN + lo:FFN_HIDDEN + lo + FFN_CHUNK])
        act = (jax.nn.silu(a) * g).astype(BF16)
        acc = acc + _dot(act, wfo_ref[lo:lo + FFN_CHUNK, :])
    out_ref[...] = _rms(acc, nfin_ref[...])


def _post(x2d, o, y, sga, sgs, wap, wglu, bglu, wsp, wout, nffn, wfi, wfo, nfin, tm):
    m = x2d.shape[0]
    tok = lambda w: pl.BlockSpec((tm, w), lambda i: (i, 0))
    const = lambda a: pl.BlockSpec(a.shape, lambda i: (0,) * a.ndim, pipeline_mode=pl.Buffered(1))
    weights = (wap, wglu, bglu, wsp, wout, nffn, wfi, wfo, nfin)
    return pl.pallas_call(
        _post_kernel,
        grid=(m // tm,),
        in_specs=[tok(D_MODEL), tok(ATTN_WIDTH), tok(SSM_WIDTH), tok(D_MODEL), tok(D_MODEL)]
                 + [const(w) for w in weights],
        out_specs=tok(D_MODEL),
        out_shape=jax.ShapeDtypeStruct((m, D_MODEL), F32),
        compiler_params=_cparams(("parallel",)),
        name="post",
    )(x2d, o, y, sga, sgs, *weights)


def _rope_tables(pos):
    half = HEAD_DIM // 2
    inv = jnp.power(jnp.float32(ROPE_THETA), -2.0 * jnp.arange(half, dtype=F32) / HEAD_DIM)
    ang = pos.astype(F32)[:, None] * inv[None, :]
    cos, sin = jnp.cos(ang), jnp.sin(ang)
    reps = LANES // HEAD_DIM
    return (jnp.tile(jnp.concatenate([cos, cos], axis=1), (1, reps)),
            jnp.tile(jnp.concatenate([-sin, sin], axis=1), (1, reps)))


def _leaf_from_T(xT, bsz, seq):
    return xT.reshape(bsz, N_HEADS, HEAD_DIM, seq).transpose(0, 3, 1, 2)[None]


def _state_in(re, im):
    n = re.shape[0]
    h = jnp.concatenate([re.reshape(n, N_SG, SG_STATE), im.reshape(n, N_SG, SG_STATE)], axis=-1)
    return h.transpose(1, 0, 2)


def _state_out(h):
    n = h.shape[0]
    return (h[..., :SG_STATE].reshape(1, n, SSM_GROUPS, SSM_STATE),
            h[..., SG_STATE:].reshape(1, n, SSM_GROUPS, SSM_STATE))


def kernel(x_prompt, x_sample, cache_k, cache_v, state_ssm_re, state_ssm_im, page_table, norm_mix, w_in,
           w_attn_proj, ssm_a_re, ssm_a_im, ssm_log_dt, ssm_b_re, ssm_b_im, ssm_c_re, ssm_c_im, ssm_d, w_glu,
           b_glu, w_ssm_proj, w_out, norm_ffn, w_ffn_in, w_ffn_out, norm_final):
    assert w_in.shape[0] == 1, "single layer"
    bsz, seq = x_prompt.shape[:2]
    nseq = x_sample.shape[0]
    past_len = page_table.shape[1] * PAGE_SIZE
    assert seq % MOBA_BLOCK == 0 and past_len % MOBA_BLOCK == 0 and x_sample.shape[1] == 1

    ssm_p = (ssm_a_re[0], ssm_a_im[0], ssm_log_dt[0], ssm_b_re[0], ssm_b_im[0], ssm_c_re[0], ssm_c_im[0])

    cos_s, sin_s = _rope_tables(jnp.full((1,), past_len, dtype=jnp.int32))
    xs2 = x_sample.reshape(nseq, D_MODEL)
    q_s, k_s, v_s, u_s, sga_s, sgs_s, w_in_bf = _inproj_sample(xs2, norm_mix, w_in[0], cos_s, sin_s)

    cos_p, sin_p = _rope_tables(jnp.arange(seq, dtype=jnp.int32))
    xp2 = x_prompt.reshape(bsz * seq, D_MODEL)
    cast_ws = (w_attn_proj[0], w_glu[0], w_ssm_proj[0], w_out[0], w_ffn_in[0], w_ffn_out[0])
    (qT, kT, kbf, kmean, vT, vTb, u_p, sga_p, sgs_p,
     wap_bf, wglu_bf, wsp_bf, wout_bf, wfi_bf, wfo_bf) = _inproj_prompt(
        xp2, norm_mix, w_in_bf, cos_p, sin_p, cast_ws, bsz, seq, tm=512)
    post_w = (wap_bf, wglu_bf, b_glu, wsp_bf, wout_bf, norm_ffn, wfi_bf, wfo_bf, norm_final[None])
    o_p = _moba_prompt(qT, kbf, kmean.reshape(bsz, seq // MOBA_BLOCK, ATTN_WIDTH), vTb, bsz, seq)
    kc, mc, nc_tab, ap, a_one = _ssm_tables(*ssm_p, chunk=SSM_CHUNK, n_pow=SUBLANES)
    h0_p = jnp.zeros((bsz, N_SG, 1, 2 * SG_STATE), F32)
    y_p, hf_p = _ssm_prompt(u_p, kc, mc, nc_tab, ap, ssm_d, h0_p, bsz, seq)
    y_prompt = _post(xp2, o_p, y_p, sga_p, sgs_p, *post_w, tm=512).reshape(bsz, seq, D_MODEL)
    new_ssm_re_p, new_ssm_im_p = _state_out(hf_p.reshape(bsz, N_SG, 2 * SG_STATE))

    cache_kT = cache_k[0].transpose(0, 2, 3, 1)
    cache_vT = cache_v[0].transpose(0, 2, 3, 1)
    dh = lambda t: t.reshape(nseq, N_HEADS, HEAD_DIM).transpose(0, 2, 1)
    o_s = _moba_paged(page_table, dh(q_s), dh(k_s), dh(v_s), cache_kT, cache_vT).transpose(0, 2, 1)
    h0_s = _state_in(state_ssm_re[0], state_ssm_im[0])
    y_s, hf_s = _ssm_step(u_s, kc, mc, nc_tab, a_one, ssm_d, h0_s)
    y_sample = _post(xs2, o_s.reshape(nseq, ATTN_WIDTH), y_s, sga_s, sgs_s, *post_w, tm=nseq)
    new_ssm_re_s, new_ssm_im_s = _state_out(hf_s.transpose(1, 0, 2))

    return (y_prompt, y_sample.reshape(nseq, 1, D_MODEL),
            _leaf_from_T(kT, bsz, seq), _leaf_from_T(vT, bsz, seq), new_ssm_re_p, new_ssm_im_p,
            k_s.reshape(1, nseq, 1, N_HEADS, HEAD_DIM), v_s.reshape(1, nseq, 1, N_HEADS, HEAD_DIM),
            new_ssm_re_s, new_ssm_im_s)
```

```python
import functools
import math

import jax
import jax.numpy as jnp
from jax import lax
from jax.experimental import pallas as pl
from jax.experimental.pallas import tpu as pltpu

F32 = jnp.float32
BF16 = jnp.bfloat16

D_MODEL = 1024
N_HEADS = 8
HEAD_DIM = 64
ATTN_WIDTH = N_HEADS * HEAD_DIM
MOBA_BLOCK = 256
MOBA_TOPK = 3
ROPE_THETA = 10000.0
SSM_WIDTH = 512
SSM_GROUP = 16
SSM_GROUPS = 32
SSM_STATE = 64
FFN_HIDDEN = 2816
RMS_EPS = 1e-6
PAGE_SIZE = 128

LANES = 128
SUBLANES = 8
BF16_SUBLANES = 16
GROUPS_PER_SG = LANES // SSM_GROUP
N_SG = SSM_GROUPS // GROUPS_PER_SG
SG_STATE = GROUPS_PER_SG * SSM_STATE
SSM_CHUNK = 8
NEG_BIG = -1e30
MOBA_HEADS_PER_STEP = 4
MOBA_LOOKAHEAD = 2
LOG2E = math.log2(math.e)
MOBA_SUM_ROWS = 16
PAGED_SCORE_UNROLL = 4
VMEM_LIMIT = 56 * 1024 * 1024


def _cparams(sem):
    return pltpu.CompilerParams(dimension_semantics=sem, vmem_limit_bytes=VMEM_LIMIT)


def _dot(a, b):
    return jnp.dot(a, b, preferred_element_type=F32)


def _rms(x, g):
    return x * lax.rsqrt(jnp.mean(x * x, axis=-1, keepdims=True) + RMS_EPS) * g


def _inproj_core(x, g, w_ref, cos, sin, q_scale):
    h = _rms(x, g).astype(BF16)
    lane = lax.broadcasted_iota(jnp.int32, (1, ATTN_WIDTH), 1)
    first_half = (lane % HEAD_DIM) < (HEAD_DIM // 2)

    def rot(t):
        partner = jnp.where(first_half,
                            pltpu.roll(t, ATTN_WIDTH - HEAD_DIM // 2, 1),
                            pltpu.roll(t, HEAD_DIM // 2, 1))
        return t * cos + partner * sin

    a = ATTN_WIDTH
    q = rot(_dot(h, w_ref[:, 0:a])) * q_scale
    k = rot(_dot(h, w_ref[:, a:2 * a]))
    v = _dot(h, w_ref[:, 2 * a:3 * a])
    u = _dot(h, w_ref[:, 3 * a:3 * a + SSM_WIDTH])
    o = 3 * a + SSM_WIDTH
    sga = jax.nn.sigmoid(_dot(h, w_ref[:, o:o + D_MODEL]))
    sgs = jax.nn.sigmoid(_dot(h, w_ref[:, o + D_MODEL:o + 2 * D_MODEL]))
    return q, k, v, u, sga, sgs


def _inproj_prompt_kernel(x_ref, g_ref, w_ref, cos_ref, sin_ref, *rest):
    n_cast = (len(rest) - 9) // 2
    cast_in = rest[:n_cast]
    qT_ref, kT_ref, kbf_ref, kmean_ref, vT_ref, vTb_ref, u_ref, sga_ref, sgs_ref = rest[n_cast:n_cast + 9]
    cast_out = rest[n_cast + 9:]
    for wi_ref, wo_ref in zip(cast_in, cast_out):
        wo_ref[...] = wi_ref[...].astype(BF16)
    reps = ATTN_WIDTH // LANES
    cos = jnp.tile(cos_ref[...], (1, reps))
    sin = jnp.tile(sin_ref[...], (1, reps))
    q, k, v, u, sga, sgs = _inproj_core(x_ref[...], g_ref[...], w_ref, cos, sin, LOG2E * HEAD_DIM ** -0.5)
    tm = q.shape[0]
    kT_ref[0] = k.T
    vT_ref[0] = v.T
    kbf_ref[...] = k.astype(BF16)
    for s in range(tm // MOBA_BLOCK):
        rows = slice(s * MOBA_BLOCK, (s + 1) * MOBA_BLOCK)
        qT_ref[0, s] = q[rows].T.astype(BF16)
        vTb_ref[0, s] = v[rows].T.astype(BF16)
        kmean_ref[0, s] = jnp.mean(k[rows], axis=0, keepdims=True)
    u_ref[...] = u
    sga_ref[...] = sga.astype(BF16)
    sgs_ref[...] = sgs.astype(BF16)


def _inproj_prompt(x2d, g, w_bf, cos, sin, cast_ws, bsz, seq, tm):
    m = bsz * seq
    nb = seq // MOBA_BLOCK
    tpb = seq // tm
    sub = tm // MOBA_BLOCK
    a = ATTN_WIDTH
    full = lambda shape: pl.BlockSpec(shape, lambda b, t: (0,) * len(shape))
    tok = lambda w: pl.BlockSpec((tm, w), lambda b, t: (b * tpb + t, 0))
    out_shape = (
        jax.ShapeDtypeStruct((bsz, nb, a, MOBA_BLOCK), BF16),
        jax.ShapeDtypeStruct((bsz, a, seq), F32),
        jax.ShapeDtypeStruct((m, a), BF16),
        jax.ShapeDtypeStruct((bsz, nb, 1, a), F32),
        jax.ShapeDtypeStruct((bsz, a, seq), F32),
        jax.ShapeDtypeStruct((bsz, nb, a, MOBA_BLOCK), BF16),
        jax.ShapeDtypeStruct((m, SSM_WIDTH), F32),
        jax.ShapeDtypeStruct((m, D_MODEL), BF16),
        jax.ShapeDtypeStruct((m, D_MODEL), BF16),
    )
    blk_t = pl.BlockSpec((1, sub, a, MOBA_BLOCK), lambda b, t: (b, t, 0, 0))
    lane_t = pl.BlockSpec((1, a, tm), lambda b, t: (b, 0, t))
    out_specs = (blk_t, lane_t, tok(a),
                 pl.BlockSpec((1, sub, 1, a), lambda b, t: (b, t, 0, 0)),
                 lane_t, blk_t, tok(SSM_WIDTH), tok(D_MODEL), tok(D_MODEL))
    n_steps = bsz * tpb
    cast_specs = []
    for w in cast_ws:
        rows = w.shape[0]
        nblk = n_steps
        while rows % nblk or (rows // nblk) % BF16_SUBLANES:
            nblk //= 2
        cast_specs.append(pl.BlockSpec((rows // nblk, w.shape[1]),
                                       lambda b, t, n=nblk: (jnp.minimum(b * tpb + t, n - 1), 0)))
    return pl.pallas_call(
        _inproj_prompt_kernel,
        grid=(bsz, tpb),
        in_specs=[tok(D_MODEL), full((1, D_MODEL)), full(w_bf.shape),
                  pl.BlockSpec((tm, LANES), lambda b, t: (t, 0)),
                  pl.BlockSpec((tm, LANES), lambda b, t: (t, 0))] + cast_specs,
        out_specs=out_specs + tuple(cast_specs),
        out_shape=out_shape + tuple(jax.ShapeDtypeStruct(w.shape, BF16) for w in cast_ws),
        compiler_params=_cparams(("arbitrary", "arbitrary")),
        name="inproj_prompt",
    )(x2d, g, w_bf, cos, sin, *cast_ws)


def _inproj_sample_kernel(x_ref, g_ref, w_ref, cos_ref, sin_ref,
                          q_ref, k_ref, v_ref, u_ref, sga_ref, sgs_ref, wb_ref):
    step = ATTN_WIDTH
    for c in range(w_ref.shape[1] // step):
        wb_ref[:, c * step:(c + 1) * step] = w_ref[:, c * step:(c + 1) * step].astype(BF16)
    reps = ATTN_WIDTH // LANES
    cos = jnp.tile(cos_ref[...], (1, reps))
    sin = jnp.tile(sin_ref[...], (1, reps))
    q, k, v, u, sga, sgs = _inproj_core(x_ref[...], g_ref[...], wb_ref, cos, sin, HEAD_DIM ** -0.5)
    q_ref[...] = q
    k_ref[...] = k
    v_ref[...] = v
    u_ref[...] = u
    sga_ref[...] = sga.astype(BF16)
    sgs_ref[...] = sgs.astype(BF16)


def _inproj_sample(x2d, g, w_f32, cos, sin):
    m = x2d.shape[0]
    a = ATTN_WIDTH
    full = lambda shape, **kw: pl.BlockSpec(shape, lambda i: (0,) * len(shape), **kw)
    shapes = [(m, a), (m, a), (m, a), (m, SSM_WIDTH), (m, D_MODEL), (m, D_MODEL), w_f32.shape]
    dts = [F32, F32, F32, F32, BF16, BF16, BF16]
    return pl.pallas_call(
        _inproj_sample_kernel,
        grid=(1,),
        in_specs=[full(x2d.shape), full(g.shape), full(w_f32.shape, pipeline_mode=pl.Buffered(1)),
                  full(cos.shape), full(sin.shape)],
        out_specs=tuple(full(s) for s in shapes),
        out_shape=tuple(jax.ShapeDtypeStruct(s, d) for s, d in zip(shapes, dts)),
        compiler_params=_cparams(("arbitrary",)),
        name="inproj_sample",
    )(x2d, g, w_f32, cos, sin)


def _moba_prompt_kernel(qT_ref, k_ref, kmean_ref, vT_ref, o_ref, bias_ref, qz_s, m_s, acc_s):
    nb = qT_ref.shape[1]
    blk = MOBA_BLOCK
    nh = MOBA_HEADS_PER_STEP
    row2 = lax.broadcasted_iota(jnp.int32, (nh * HEAD_DIM, 1), 0)
    lane_km = lax.broadcasted_iota(jnp.int32, (1, nh * HEAD_DIM), 1)
    blk_row = lax.broadcasted_iota(jnp.int32, (nb, blk), 0)
    key_i = lax.broadcasted_iota(jnp.int32, (blk, blk), 0)
    qry_i = lax.broadcasted_iota(jnp.int32, (blk, blk), 1)
    causal = key_i <= qry_i
    in_head = [(row2 >= hh * HEAD_DIM) & (row2 < (hh + 1) * HEAD_DIM) for hh in range(nh)]
    hrows = [slice(hh * HEAD_DIM, (hh + 1) * HEAD_DIM) for hh in range(nh)]

    km_all = jnp.concatenate(
        [jnp.where((lane_km >= hh * HEAD_DIM) & (lane_km < (hh + 1) * HEAD_DIM), kmean_ref[0], 0.0)
         for hh in range(nh)], axis=0)
    km_terms = []
    rest = km_all
    for _ in range(3):
        term = rest.astype(BF16)
        km_terms.append(term)
        rest = rest - term.astype(F32)
    km_split = jnp.concatenate(km_terms, axis=0)
    for hh in range(nh):
        bias_ref[hh, 0] = jnp.full((nb, blk), NEG_BIG, F32)
    for i in range(1, nb):
        sb3 = _dot(km_split, qT_ref[0, i])
        sb_all = (sb3[0:nh * nb] + sb3[nh * nb:2 * nh * nb]) + sb3[2 * nh * nb:]
        for hh in range(nh):
            sb = jnp.where(blk_row < i, sb_all[hh * nb:(hh + 1) * nb], -jnp.inf)
            bias = jnp.full((nb, blk), NEG_BIG, F32)
            for _r in range(min(MOBA_TOPK, i)):
                mx = jnp.max(sb, axis=0, keepdims=True)
                first = jnp.min(jnp.where(sb == mx, blk_row, nb), axis=0, keepdims=True)
                pick = blk_row == first
                bias = jnp.where(pick, 0.0, bias)
                sb = jnp.where(pick, -jnp.inf, sb)
            bias_ref[hh, i] = bias

    n_items = nb // 2
    ones_rows = jnp.ones((MOBA_SUM_ROWS, 2 * blk), BF16)

    def couple(i, _):
        iq = (i, nb - 1 - i)
        n_first = (i + 1) // 2

        def diag_scores(x):
            q_pair = qT_ref[0, iq[x]]
            k_own = k_ref[pl.ds(pl.multiple_of(iq[x] * blk, blk), blk), :]
            out = []
            for hh in range(nh):
                qz = jnp.where(in_head[hh], q_pair, jnp.zeros_like(q_pair))
                qz_s[x, hh] = qz
                out.append(_dot(k_own, qz))
            return out

        def diag_absorb(x, s_own):
            for hh in range(nh):
                s = jnp.where(causal, s_own[hh], NEG_BIG)
                m0 = jnp.max(s, axis=0, keepdims=True)
                p = jnp.exp2(s - m0)
                m_s[x, hh] = m0
                vv = jnp.concatenate([vT_ref[0, iq[x], hrows[hh], :], ones_rows[:, :blk]], axis=0)
                acc_s[x, hh] = _dot(vv, p.astype(BF16))

        def item_params(k):
            first = k < n_first
            x = jnp.where(first, 0, 1)
            return x, jnp.where(first, iq[0], iq[1]), jnp.where(first, k, k - n_first)

        def item_scores(k):
            x, _, jp = item_params(k)
            kk = k_ref[pl.ds(pl.multiple_of(2 * jp * blk, blk), 2 * blk), :]
            out = []
            for hh in range(nh):
                s = _dot(kk, qz_s[x, hh])
                out.append((s, jnp.max(s[:blk], axis=0, keepdims=True), jnp.max(s[blk:], axis=0, keepdims=True)))
            return out

        def item_absorb(k, sc):
            x, qb, jp = item_params(k)
            j0 = 2 * jp
            for hh in range(nh):
                s, cma, cmb = sc[hh]
                ba = bias_ref[hh, qb, pl.ds(j0, 1), :]
                bb = bias_ref[hh, qb, pl.ds(j0 + 1, 1), :]
                m = m_s[x, hh]
                m_new = jnp.maximum(m, jnp.maximum(cma + ba, cmb + bb))
                alpha = jnp.exp2(m - m_new)
                pa = jnp.exp2(s[:blk] - (m_new - ba))
                pb = jnp.exp2(s[blk:] - (m_new - bb))
                m_s[x, hh] = m_new
                pp = jnp.concatenate([pa, pb], axis=0).astype(BF16)
                vv = jnp.concatenate([vT_ref[0, j0, hrows[hh], :], vT_ref[0, j0 + 1, hrows[hh], :]], axis=1)
                vv = jnp.concatenate([vv, ones_rows], axis=0)
                acc_s[x, hh] = alpha * acc_s[x, hh] + _dot(vv, pp)

        s_diag = [diag_scores(0), diag_scores(1)]
        pending = [item_scores(k) for k in range(min(MOBA_LOOKAHEAD, n_items))]
        diag_absorb(0, s_diag[0])
        diag_absorb(1, s_diag[1])
        for k in range(n_items):
            if k + MOBA_LOOKAHEAD < n_items:
                pending.append(item_scores(k + MOBA_LOOKAHEAD))
            item_absorb(k, pending[k])
        for x in range(2):
            oT = jnp.concatenate([acc_s[x, hh, 0:HEAD_DIM, :] / acc_s[x, hh, HEAD_DIM:HEAD_DIM + 1, :]
                                  for hh in range(nh)], axis=0)
            o_ref[pl.ds(pl.multiple_of(iq[x] * blk, blk), blk), :] = oT.T.astype(o_ref.dtype)
        return 0

    lax.fori_loop(0, nb // 2, couple, 0)


def _moba_prompt(qT, kbf, kmean, vTb, bsz, seq):
    nb = seq // MOBA_BLOCK
    hp = N_HEADS // MOBA_HEADS_PER_STEP
    pair = MOBA_HEADS_PER_STEP * HEAD_DIM
    return pl.pallas_call(
        _moba_prompt_kernel,
        grid=(bsz, hp),
        in_specs=[pl.BlockSpec((1, nb, pair, MOBA_BLOCK), lambda b, h: (b, 0, h, 0)),
                  pl.BlockSpec((seq, pair), lambda b, h: (b, h)),
                  pl.BlockSpec((1, nb, pair), lambda b, h: (b, 0, h)),
                  pl.BlockSpec((1, nb, pair, MOBA_BLOCK), lambda b, h: (b, 0, h, 0))],
        out_specs=pl.BlockSpec((seq, pair), lambda b, h: (b, h)),
        out_shape=jax.ShapeDtypeStruct((bsz * seq, ATTN_WIDTH), BF16),
        scratch_shapes=[pltpu.VMEM((MOBA_HEADS_PER_STEP, nb, nb, MOBA_BLOCK), F32),
                        pltpu.VMEM((2, MOBA_HEADS_PER_STEP, pair, MOBA_BLOCK), BF16),
                        pltpu.VMEM((2, MOBA_HEADS_PER_STEP, 1, MOBA_BLOCK), F32),
                        pltpu.VMEM((2, MOBA_HEADS_PER_STEP, HEAD_DIM + MOBA_SUM_ROWS, MOBA_BLOCK), F32)],
        compiler_params=_cparams(("parallel", "parallel")),
        name="moba_prompt",
    )(qT, kbf, kmean, vTb)


def _moba_paged_kernel(pt_ref, qcol_ref, kn_ref, vn_ref, ck_ref, cv_ref, o_ref,
                       kbuf, vbuf, s_ref, psel_ref, stat_ref, qb_ref, ksem, vsem):
    b = pl.program_id(0)
    nseq = pl.num_programs(0) - 1
    n_pages = kbuf.shape[1]
    nblk = n_pages // 2
    slot = b % 2

    def k_copy(seq_i, sl, p):
        return pltpu.make_async_copy(ck_ref.at[pt_ref[seq_i, p]], kbuf.at[sl, p], ksem.at[sl])

    def start_k(seq_i, sl):
        def body(p, _):
            k_copy(seq_i, sl, p).start()
            return 0
        lax.fori_loop(0, n_pages, body, 0)

    def v_copy(page, h, r, par):
        return pltpu.make_async_copy(cv_ref.at[page, h], vbuf.at[h, r, par], vsem.at[0])

    @pl.when(b == 0)
    def _():
        start_k(0, 0)

    @pl.when(b < nseq)
    def _():
        def wait_body(p, _):
            k_copy(b, slot, p).wait()
            return 0
        lax.fori_loop(0, n_pages, wait_body, 0)

    @pl.when(b + 1 < nseq)
    def _():
        start_k(b + 1, 1 - slot)

    @pl.when(b >= 1)
    def _():
        for h in range(N_HEADS):
            for r in range(MOBA_TOPK):
                for par in range(2):
                    v_copy(0, h, r, par).wait()
        vn = vn_ref[0]
        for h in range(N_HEADS):
            acc = jnp.zeros((HEAD_DIM, PAGE_SIZE), F32)
            for r in range(MOBA_TOPK):
                for par in range(2):
                    acc = acc + vbuf[h, r, par] * psel_ref[h, 2 * r + par]
            p_own = stat_ref[0, h][:, 0:1]
            l = stat_ref[1, h][:, 0:1]
            o_h = jnp.sum(acc, axis=1, keepdims=True) + p_own * vn[:, h:h + 1]
            o_ref[0, :, h:h + 1] = o_h / l

    @pl.when(b < nseq)
    def _():
        qcol = qcol_ref[0]
        own = jnp.sum(qcol * kn_ref[0], axis=0, keepdims=True)
        blk_i = lax.broadcasted_iota(jnp.int32, (nblk, 1), 0)
        for h in range(N_HEADS):
            qb_ref[h] = jnp.broadcast_to(qcol[:, h:h + 1], (HEAD_DIM, PAGE_SIZE))
        for h in range(N_HEADS):
            def score_body(g, _):
                qb = qb_ref[h]
                for dn in range(PAGED_SCORE_UNROLL):
                    n = g * PAGED_SCORE_UNROLL + dn
                    for par in range(2):
                        kt = kbuf[slot, 2 * n + par, h]
                        s_ref[par, h, pl.ds(n, 1), :] = jnp.sum(kt * qb, axis=0, keepdims=True)
                return 0
            lax.fori_loop(0, nblk // PAGED_SCORE_UNROLL, score_body, 0)

        for h in range(N_HEADS):
            s0 = s_ref[0, h]
            s1 = s_ref[1, h]
            bs = jnp.sum(s0 + s1, axis=1, keepdims=True)
            sel = jnp.zeros((nblk, 1), dtype=jnp.bool_)
            picks = []
            for r in range(MOBA_TOPK):
                mx = jnp.max(bs, axis=0, keepdims=True)
                first = jnp.min(jnp.where(bs == mx, blk_i, nblk), axis=0, keepdims=True)
                pick = blk_i == first
                sel = sel | pick
                bs = jnp.where(pick, -jnp.inf, bs)
                blk_id = jnp.max(first)
                for par in range(2):
                    v_copy(pt_ref[b, 2 * blk_id + par], h, r, par).start()
                picks.append(blk_id)
            s_own = own[:, h:h + 1]
            sm0 = jnp.where(sel, s0, NEG_BIG)
            sm1 = jnp.where(sel, s1, NEG_BIG)
            mx = jnp.maximum(jnp.max(jnp.max(jnp.maximum(sm0, sm1), axis=1, keepdims=True),
                                     axis=0, keepdims=True), s_own)
            p0 = jnp.exp(sm0 - mx)
            p1 = jnp.exp(sm1 - mx)
            p_own = jnp.exp(s_own - mx)
            l = jnp.sum(jnp.sum(p0 + p1, axis=1, keepdims=True), axis=0, keepdims=True) + p_own
            s_ref[0, h] = p0
            s_ref[1, h] = p1
            for r in range(MOBA_TOPK):
                for par in range(2):
                    psel_ref[h, 2 * r + par] = s_ref[par, h, pl.ds(picks[r], 1), :]
            stat_ref[0, h] = jnp.broadcast_to(p_own, (1, PAGE_SIZE))
            stat_ref[1, h] = jnp.broadcast_to(l, (1, PAGE_SIZE))


def _moba_paged(page_table, qcol, kncol, vncol, cache_kT, cache_vT):
    nseq, n_pages = page_table.shape
    assert n_pages // 2 >= MOBA_TOPK, "every pick must find an unpicked cached block"
    cur = pl.BlockSpec((1, HEAD_DIM, N_HEADS), lambda b, pt: (jnp.minimum(b, nseq - 1), 0, 0))
    prev = pl.BlockSpec((1, HEAD_DIM, N_HEADS), lambda b, pt: (jnp.maximum(b - 1, 0), 0, 0))
    any_spec = pl.BlockSpec(memory_space=pl.ANY)
    grid_spec = pltpu.PrefetchScalarGridSpec(
        num_scalar_prefetch=1,
        grid=(nseq + 1,),
        in_specs=[cur, cur, prev, any_spec, any_spec],
        out_specs=prev,
        scratch_shapes=[
            pltpu.VMEM((2, n_pages, N_HEADS, HEAD_DIM, PAGE_SIZE), F32),
            pltpu.VMEM((N_HEADS, MOBA_TOPK, 2, HEAD_DIM, PAGE_SIZE), F32),
            pltpu.VMEM((2, N_HEADS, n_pages // 2, PAGE_SIZE), F32),
            pltpu.VMEM((N_HEADS, 2 * MOBA_TOPK, 1, PAGE_SIZE), F32),
            pltpu.VMEM((2, N_HEADS, 1, PAGE_SIZE), F32),
            pltpu.VMEM((N_HEADS, HEAD_DIM, PAGE_SIZE), F32),
            pltpu.SemaphoreType.DMA((2,)),
            pltpu.SemaphoreType.DMA((1,)),
        ])
    return pl.pallas_call(
        _moba_paged_kernel,
        grid_spec=grid_spec,
        out_shape=jax.ShapeDtypeStruct((nseq, HEAD_DIM, N_HEADS), F32),
        compiler_params=_cparams(("arbitrary",)),
        name="moba_paged",
    )(page_table, qcol, kncol, vncol, cache_kT, cache_vT)


def _ssm_tables(a_re, a_im, log_dt, b_re, b_im, c_re, c_im, chunk, n_pow):
    t_ = chunk
    lam = lax.complex(a_re.astype(F32), a_im.astype(F32))
    ldt = lam * jnp.exp(log_dt.astype(F32))[:, None]
    a_bar = jnp.exp(ldt)
    b_bar = ((a_bar - 1.0) / lam)[..., None] * lax.complex(b_re.astype(F32), b_im.astype(F32))
    c_c = lax.complex(c_re.astype(F32), c_im.astype(F32))
    taus = jnp.arange(t_ + 1, dtype=F32).astype(jnp.complex64)
    apow = jnp.exp(ldt[None] * taus[:, None, None])
    gq, ssg, c_, p_ = GROUPS_PER_SG, N_SG, SSM_GROUP, SSM_STATE
    c4 = c_c.reshape(ssg, gq, c_, p_)
    b4 = b_bar.reshape(ssg, gq, p_, c_)
    ap4 = apow.reshape(t_ + 1, ssg, gq, p_)

    kc = jnp.einsum('sjcp,tsjp,sjpd->sjdtc', c4, ap4[:t_], b4).real
    kc = kc.reshape(ssg, LANES, t_ * c_)
    mst = jnp.einsum('tsjp,sjpd->stjdp', ap4[:t_][::-1], b4)
    mc = jnp.concatenate([mst.real, mst.imag], axis=-1).reshape(ssg, t_ * LANES, 2 * p_)
    nout = jnp.einsum('sjcp,tsjp->sjptc', c4, ap4[1:])
    nc = jnp.concatenate([nout.real, -nout.imag], axis=1).reshape(ssg, 2 * SG_STATE, t_ * c_)
    rs = jnp.arange(1, n_pow + 1, dtype=F32).astype(jnp.complex64)
    ap = jnp.exp((ldt * t_)[None] * rs[:, None, None])
    ap = ap.reshape(n_pow, ssg, SG_STATE).transpose(1, 0, 2)
    a_pow = jnp.concatenate([ap.real, ap.imag], axis=-1)
    a1 = a_bar.reshape(ssg, 1, SG_STATE)
    a_one = jnp.concatenate([a1.real, a1.imag], axis=-1)
    return kc, mc, nc, a_pow, a_one


def _expand_ssm_weights(kc_ref, mc_ref, nc_ref, wt_s, ws_s, wo_s):
    t_ = SSM_CHUNK
    kw = t_ * LANES
    lg = lambda n: n.bit_length() - 1
    r = lax.broadcasted_iota(jnp.int32, (LANES, kw), 0)
    q = lax.broadcasted_iota(jnp.int32, (LANES, kw), 1)
    sh_c, sh_p = lg(SSM_GROUP), lg(SSM_STATE)
    ecol = jnp.where(((r >> sh_c) == (q >> lg(LANES))) & ((r & (SSM_GROUP - 1)) == (q & (SSM_GROUP - 1))),
                     1.0, 0.0).astype(BF16)
    e2 = jnp.where(((r >> sh_p) == (q >> lg(SG_STATE))) & ((r & (SSM_STATE - 1)) == (q & (SSM_STATE - 1))),
                   1.0, 0.0).astype(BF16)
    col_c = (q & (LANES - 1)) >> sh_c
    taps = jnp.where((r >> sh_c) == col_c, _dot(kc_ref[0].astype(BF16), ecol), 0.0).astype(BF16)
    blank = jnp.zeros((LANES, LANES), BF16)
    for s in range(t_):
        for t in range(t_):
            wt_s[s * LANES:(s + 1) * LANES, t * LANES:(t + 1) * LANES] = (
                taps[:, (t - s) * LANES:(t - s + 1) * LANES] if t >= s else blank)
    rr = lax.broadcasted_iota(jnp.int32, (kw, kw), 0)
    qq = lax.broadcasted_iota(jnp.int32, (kw, kw), 1)
    row_c = (rr & (LANES - 1)) >> sh_c
    row_p = (rr & (SG_STATE - 1)) >> sh_p
    colq_c = (qq & (LANES - 1)) >> sh_c
    colq_p = (qq & (SG_STATE - 1)) >> sh_p
    ws_s[...] = jnp.where(row_c == colq_p, _dot(mc_ref[0].astype(BF16), e2), 0.0).astype(BF16)
    wo_s[...] = jnp.where(row_p == colq_c, _dot(nc_ref[0].astype(BF16), ecol), 0.0).astype(BF16)


def _ssm_prompt_kernel(u_ref, kc_ref, mc_ref, nc_ref, ap_ref, d_ref, h0_ref,
                       y_ref, hf_ref, uc_ref, x_ref, hin_ref, wt_s, ws_s, wo_s):
    t_ = SSM_CHUNK
    nc = uc_ref.shape[0]
    ns = SG_STATE

    @pl.when(pl.program_id(1) == 0)
    def _():
        _expand_ssm_weights(kc_ref, mc_ref, nc_ref, wt_s, ws_s, wo_s)

    for t in range(t_):
        uc_ref[:, t * LANES:(t + 1) * LANES] = u_ref[pl.ds(t, nc, stride=t_), :].astype(BF16)
    uc = uc_ref[...]
    x_ref[...] = _dot(uc, ws_s[...])

    ap = ap_ref[0]
    apr, api = ap[:, :ns], ap[:, ns:]
    row = lax.broadcasted_iota(jnp.int32, (SUBLANES, 1), 0)

    def tile_body(k, carry):
        hr, hi = carry
        rows = pl.ds(k * SUBLANES, SUBLANES)
        xr = x_ref[rows, 0:ns]
        xi = x_ref[rows, ns:2 * ns]
        for d in (1, 2, 4):
            ar, ai = apr[d - 1:d], api[d - 1:d]
            sr = jnp.where(row >= d, pltpu.roll(xr, d, 0), 0.0)
            si = jnp.where(row >= d, pltpu.roll(xi, d, 0), 0.0)
            xr, xi = xr + ar * sr - ai * si, xi + ar * si + ai * sr
        outr = xr + apr * hr - api * hi
        outi = xi + apr * hi + api * hr
        hin_ref[rows, 0:ns] = jnp.where(row >= 1, pltpu.roll(outr, 1, 0), hr)
        hin_ref[rows, ns:2 * ns] = jnp.where(row >= 1, pltpu.roll(outi, 1, 0), hi)
        return outr[SUBLANES - 1:SUBLANES], outi[SUBLANES - 1:SUBLANES]

    y_intra = _dot(uc, wt_s[...])
    h0 = h0_ref[0, 0]
    carry = (h0[:, :ns], h0[:, ns:])
    for k in range(nc // SUBLANES):
        carry = tile_body(k, carry)
    hf_ref[0, 0] = jnp.concatenate(carry, axis=1)

    y = y_intra + _dot(hin_ref[...].astype(BF16), wo_s[...])
    dv = d_ref[...]
    for t in range(t_):
        rows = pl.ds(t, nc, stride=t_)
        y_ref[rows, :] = y[:, t * LANES:(t + 1) * LANES] + dv * u_ref[rows, :]


def _ssm_prompt(u, kc, mc, nc_tab, a_pow, d_row, h0, bsz, seq):
    t_ = SSM_CHUNK
    nc = seq // t_
    kw = t_ * LANES
    assert kc.shape[2] == LANES and 2 * SG_STATE == kw
    wspec = lambda shape: pl.BlockSpec((1,) + shape, lambda s, b: (s, 0, 0))
    return pl.pallas_call(
        _ssm_prompt_kernel,
        grid=(N_SG, bsz),
        in_specs=[pl.BlockSpec((seq, LANES), lambda s, b: (b, s)),
                  wspec(kc.shape[1:]), wspec(mc.shape[1:]), wspec(nc_tab.shape[1:]),
                  wspec((SUBLANES, 2 * SG_STATE)),
                  pl.BlockSpec((1, LANES), lambda s, b: (0, s)),
                  pl.BlockSpec((1, 1, 1, 2 * SG_STATE), lambda s, b: (b, s, 0, 0))],
        out_specs=(pl.BlockSpec((seq, LANES), lambda s, b: (b, s)),
                   pl.BlockSpec((1, 1, 1, 2 * SG_STATE), lambda s, b: (b, s, 0, 0))),
        out_shape=(jax.ShapeDtypeStruct((bsz * seq, SSM_WIDTH), F32),
                   jax.ShapeDtypeStruct((bsz, N_SG, 1, 2 * SG_STATE), F32)),
        scratch_shapes=[pltpu.VMEM((nc, kw), BF16),
                        pltpu.VMEM((nc, 2 * SG_STATE), F32),
                        pltpu.VMEM((nc, 2 * SG_STATE), F32),
                        pltpu.VMEM((kw, kw), BF16),
                        pltpu.VMEM((kw, 2 * SG_STATE), BF16),
                        pltpu.VMEM((2 * SG_STATE, kw), BF16)],
        compiler_params=_cparams(("arbitrary", "arbitrary")),
        name="ssm_prompt",
    )(u, kc, mc, nc_tab, a_pow, d_row, h0)


def _ssm_step_kernel(u_ref, kc_ref, mc_ref, nc_ref, a1_ref, d_ref, h0_ref, y_ref, hf_ref):
    ns = SG_STATE
    kw = 2 * ns
    lg = lambda n: n.bit_length() - 1
    sh_c, sh_p = lg(SSM_GROUP), lg(SSM_STATE)
    iota = lambda shape, d: lax.broadcasted_iota(jnp.int32, shape, d)
    r1, q1 = iota((LANES, LANES), 0), iota((LANES, LANES), 1)
    e_c = jnp.where((r1 < SSM_GROUP) & ((r1 & (SSM_GROUP - 1)) == (q1 & (SSM_GROUP - 1))), 1.0, 0.0).astype(BF16)
    r2, q2 = iota((LANES, kw), 0), iota((LANES, kw), 1)
    e_p = jnp.where(((r2 >> sh_p) == (q2 >> lg(ns))) & ((r2 & (SSM_STATE - 1)) == (q2 & (SSM_STATE - 1))),
                    1.0, 0.0).astype(BF16)
    same_t = (r1 >> sh_c) == (q1 >> sh_c)
    same_s = (r2 >> sh_c) == ((q2 & (ns - 1)) >> sh_p)
    r3, q3 = iota((kw, LANES), 0), iota((kw, LANES), 1)
    same_o = ((r3 & (ns - 1)) >> sh_p) == (q3 >> sh_c)
    last = (SSM_CHUNK - 1) * LANES
    for s in range(N_SG):
        wt = jnp.where(same_t, _dot(kc_ref[s].astype(BF16), e_c), 0.0).astype(BF16)
        ws = jnp.where(same_s, _dot(mc_ref[s, last:last + LANES, :].astype(BF16), e_p), 0.0).astype(BF16)
        wo = jnp.where(same_o, _dot(nc_ref[s].astype(BF16), e_c), 0.0).astype(BF16)
        us = u_ref[:, s * LANES:(s + 1) * LANES]
        ub = us.astype(BF16)
        h0 = h0_ref[s]
        x = _dot(ub, ws)
        ar, ai = a1_ref[s, :, :ns], a1_ref[s, :, ns:]
        hr, hi = h0[:, :ns], h0[:, ns:]
        hf_ref[s] = jnp.concatenate([x[:, :ns] + ar * hr - ai * hi,
                                     x[:, ns:] + ar * hi + ai * hr], axis=1)
        y = _dot(ub, wt) + _dot(h0.astype(BF16), wo)
        y_ref[:, s * LANES:(s + 1) * LANES] = y + d_ref[:, s * LANES:(s + 1) * LANES] * us


def _ssm_step(u, kc, mc, nc_tab, a_one, d_row, h0):
    m = u.shape[0]
    full = lambda a: pl.BlockSpec(a.shape, lambda i: (0,) * a.ndim)
    args = (u, kc, mc, nc_tab, a_one, d_row, h0)
    shapes = [(m, SSM_WIDTH), (N_SG, m, 2 * SG_STATE)]
    return pl.pallas_call(
        _ssm_step_kernel,
        grid=(1,),
        in_specs=[full(a) for a in args],
        out_specs=tuple(pl.BlockSpec(s, lambda i, n=len(s): (0,) * n) for s in shapes),
        out_shape=tuple(jax.ShapeDtypeStruct(s, F32) for s in shapes),
        compiler_params=_cparams(("arbitrary",)),
        name="ssm_step",
    )(*args)


MXU_COLS = 256
FFN_SPLIT = FFN_HIDDEN // MXU_COLS
FFN_CHUNK = FFN_HIDDEN // FFN_SPLIT


def _post_kernel(x_ref, o_ref, y_ref, sga_ref, sgs_ref, wap_ref, wglu_ref, bglu_ref, wsp_ref,
                 wout_ref, nffn_ref, wfi_ref, wfo_ref, nfin_ref, out_ref):
    z = jax.nn.gelu(y_ref[...])
    z = z * jax.nn.sigmoid(_dot(z.astype(BF16), wglu_ref[...]) + bglu_ref[...])
    ssm_out = _dot(z.astype(BF16), wsp_ref[...])
    attn_out = _dot(o_ref[...].astype(BF16), wap_ref[...])
    merged = sga_ref[...].astype(F32) * attn_out + sgs_ref[...].astype(F32) * ssm_out
    x1 = x_ref[...] + _dot(merged.astype(BF16), wout_ref[...])
    hf = _rms(x1, nffn_ref[...]).astype(BF16)
    acc = x1
    for c in range(FFN_SPLIT):
        lo = c * FFN_CHUNK
        a = _dot(hf, wfi_ref[:, lo:lo + FFN_CHUNK])
        g = _dot(hf, wfi_ref[:, FFN_HIDDEN + lo:FFN_HIDDEN + lo + FFN_CHUNK])
        act = (jax.nn.silu(a) * g).astype(BF16)
        acc = acc + _dot(act, wfo_ref[lo:lo + FFN_CHUNK, :])
    out_ref[...] = _rms(acc, nfin_ref[...])


def _post(x2d, o, y, sga, sgs, wap, wglu, bglu, wsp, wout, nffn, wfi, wfo, nfin, tm):
    m = x2d.shape[0]
    tok = lambda w: pl.BlockSpec((tm, w), lambda i: (i, 0))
    const = lambda a: pl.BlockSpec(a.shape, lambda i: (0,) * a.ndim, pipeline_mode=pl.Buffered(1))
    weights = (wap, wglu, bglu, wsp, wout, nffn, wfi, wfo, nfin)
    return pl.pallas_call(
        _post_kernel,
        grid=(m // tm,),
        in_specs=[tok(D_MODEL), tok(ATTN_WIDTH), tok(SSM_WIDTH), tok(D_MODEL), tok(D_MODEL)]
                 + [const(w) for w in weights],
        out_specs=tok(D_MODEL),
        out_shape=jax.ShapeDtypeStruct((m, D_MODEL), F32),
        compiler_params=_cparams(("parallel",)),
        name="post",
    )(x2d, o, y, sga, sgs, *weights)


def _rope_tables(pos):
    half = HEAD_DIM // 2
    inv = jnp.power(jnp.float32(ROPE_THETA), -2.0 * jnp.arange(half, dtype=F32) / HEAD_DIM)
    ang = pos.astype(F32)[:, None] * inv[None, :]
    cos, sin = jnp.cos(ang), jnp.sin(ang)
    reps = LANES // HEAD_DIM
    return (jnp.tile(jnp.concatenate([cos, cos], axis=1), (1, reps)),
            jnp.tile(jnp.concatenate([-sin, sin], axis=1), (1, reps)))


def _leaf_from_T(xT, bsz, seq):
    return xT.reshape(bsz, N_HEADS, HEAD_DIM, seq).transpose(0, 3, 1, 2)[None]


def _state_in(re, im):
    n = re.shape[0]
    h = jnp.concatenate([re.reshape(n, N_SG, SG_STATE), im.reshape(n, N_SG, SG_STATE)], axis=-1)
    return h.transpose(1, 0, 2)


def _state_out(h):
    n = h.shape[0]
    return (h[..., :SG_STATE].reshape(1, n, SSM_GROUPS, SSM_STATE),
            h[..., SG_STATE:].reshape(1, n, SSM_GROUPS, SSM_STATE))


def kernel(x_prompt, x_sample, cache_k, cache_v, state_ssm_re, state_ssm_im, page_table, norm_mix, w_in,
           w_attn_proj, ssm_a_re, ssm_a_im, ssm_log_dt, ssm_b_re, ssm_b_im, ssm_c_re, ssm_c_im, ssm_d, w_glu,
           b_glu, w_ssm_proj, w_out, norm_ffn, w_ffn_in, w_ffn_out, norm_final):
    assert w_in.shape[0] == 1, "single layer"
    bsz, seq = x_prompt.shape[:2]
    nseq = x_sample.shape[0]
    past_len = page_table.shape[1] * PAGE_SIZE
    assert seq % MOBA_BLOCK == 0 and past_len % MOBA_BLOCK == 0 and x_sample.shape[1] == 1

    ssm_p = (ssm_a_re[0], ssm_a_im[0], ssm_log_dt[0], ssm_b_re[0], ssm_b_im[0], ssm_c_re[0], ssm_c_im[0])

    cos_s, sin_s = _rope_tables(jnp.full((1,), past_len, dtype=jnp.int32))
    xs2 = x_sample.reshape(nseq, D_MODEL)
    q_s, k_s, v_s, u_s, sga_s, sgs_s, w_in_bf = _inproj_sample(xs2, norm_mix, w_in[0], cos_s, sin_s)

    cos_p, sin_p = _rope_tables(jnp.arange(seq, dtype=jnp.int32))
    xp2 = x_prompt.reshape(bsz * seq, D_MODEL)
    cast_ws = (w_attn_proj[0], w_glu[0], w_ssm_proj[0], w_out[0], w_ffn_in[0], w_ffn_out[0])
    (qT, kT, kbf, kmean, vT, vTb, u_p, sga_p, sgs_p,
     wap_bf, wglu_bf, wsp_bf, wout_bf, wfi_bf, wfo_bf) = _inproj_prompt(
        xp2, norm_mix, w_in_bf, cos_p, sin_p, cast_ws, bsz, seq, tm=512)
    post_w = (wap_bf, wglu_bf, b_glu, wsp_bf, wout_bf, norm_ffn, wfi_bf, wfo_bf, norm_final[None])
    o_p = _moba_prompt(qT, kbf, kmean.reshape(bsz, seq // MOBA_BLOCK, ATTN_WIDTH), vTb, bsz, seq)
    kc, mc, nc_tab, ap, a_one = _ssm_tables(*ssm_p, chunk=SSM_CHUNK, n_pow=SUBLANES)
    h0_p = jnp.zeros((bsz, N_SG, 1, 2 * SG_STATE), F32)
    y_p, hf_p = _ssm_prompt(u_p, kc, mc, nc_tab, ap, ssm_d, h0_p, bsz, seq)
    y_prompt = _post(xp2, o_p, y_p, sga_p, sgs_p, *post_w, tm=512).reshape(bsz, seq, D_MODEL)
    new_ssm_re_p, new_ssm_im_p = _state_out(hf_p.reshape(bsz, N_SG, 2 * SG_STATE))

    cache_kT = cache_k[0].transpose(0, 2, 3, 1)
    cache_vT = cache_v[0].transpose(0, 2, 3, 1)
    dh = lambda t: t.reshape(nseq, N_HEADS, HEAD_DIM).transpose(0, 2, 1)
    o_s = _moba_paged(page_table, dh(q_s), dh(k_s), dh(v_s), cache_kT, cache_vT).transpose(0, 2, 1)
    h0_s = _state_in(state_ssm_re[0], state_ssm_im[0])
    y_s, hf_s = _ssm_step(u_s, kc, mc, nc_tab, a_one, ssm_d, h0_s)
    y_sample = _post(xs2, o_s.reshape(nseq, ATTN_WIDTH), y_s, sga_s, sgs_s, *post_w, tm=nseq)
    new_ssm_re_s, new_ssm_im_s = _state_out(hf_s.transpose(1, 0, 2))

    return (y_prompt, y_sample.reshape(nseq, 1, D_MODEL),
            _leaf_from_T(kT, bsz, seq), _leaf_from_T(vT, bsz, seq), new_ssm_re_p, new_ssm_im_p,
            k_s.reshape(1, nseq, 1, N_HEADS, HEAD_DIM), v_s.reshape(1, nseq, 1, N_HEADS, HEAD_DIM),
            new_ssm_re_s, new_ssm_im_s)
```

```python
import functools
import math

import jax
import jax.numpy as jnp
from jax import lax
from jax.experimental import pallas as pl
from jax.experimental.pallas import tpu as pltpu

F32 = jnp.float32
BF16 = jnp.bfloat16

D_MODEL = 1024
N_HEADS = 8
HEAD_DIM = 64
ATTN_WIDTH = N_HEADS * HEAD_DIM
MOBA_BLOCK = 256
MOBA_TOPK = 3
ROPE_THETA = 10000.0
SSM_WIDTH = 512
SSM_GROUP = 16
SSM_GROUPS = 32
SSM_STATE = 64
FFN_HIDDEN = 2816
RMS_EPS = 1e-6
PAGE_SIZE = 128

LANES = 128
SUBLANES = 8
BF16_SUBLANES = 16
MXU_COLS = 256
GROUPS_PER_SG = LANES // SSM_GROUP
N_SG = SSM_GROUPS // GROUPS_PER_SG
SG_STATE = GROUPS_PER_SG * SSM_STATE
SSM_CHUNK = 8
NEG_BIG = -1e30
MOBA_GROUPS_PER_STEP = 1
MOBA_HEADS_PER_GROUP = 4
MOBA_LOOKAHEAD = 2
LOG2E = math.log2(math.e)
MOBA_SUM_ROWS = 16
PAGED_SCORE_UNROLL = 4
VMEM_LIMIT = 56 * 1024 * 1024


def _cparams(sem):
    return pltpu.CompilerParams(dimension_semantics=sem, vmem_limit_bytes=VMEM_LIMIT)


def _dot(a, b):
    return jnp.dot(a, b, preferred_element_type=F32)


def _rms(x, g):
    return x * lax.rsqrt(jnp.mean(x * x, axis=-1, keepdims=True) + RMS_EPS) * g


def _inproj_core(x, g, w_ref, cos, sin, q_scale):
    h = _rms(x, g).astype(BF16)
    lane = lax.broadcasted_iota(jnp.int32, (1, ATTN_WIDTH), 1)
    first_half = (lane % HEAD_DIM) < (HEAD_DIM // 2)

    def rot(t):
        partner = jnp.where(first_half,
                            pltpu.roll(t, ATTN_WIDTH - HEAD_DIM // 2, 1),
                            pltpu.roll(t, HEAD_DIM // 2, 1))
        return t * cos + partner * sin

    a = ATTN_WIDTH
    q = rot(_dot(h, w_ref[:, 0:a])) * q_scale
    k = rot(_dot(h, w_ref[:, a:2 * a]))
    v = _dot(h, w_ref[:, 2 * a:3 * a])
    u = _dot(h, w_ref[:, 3 * a:3 * a + SSM_WIDTH])
    o = 3 * a + SSM_WIDTH
    sga = jax.nn.sigmoid(_dot(h, w_ref[:, o:o + D_MODEL]))
    sgs = jax.nn.sigmoid(_dot(h, w_ref[:, o + D_MODEL:o + 2 * D_MODEL]))
    return q, k, v, u, sga, sgs


def _inproj_prompt_kernel(x_ref, g_ref, w_ref, cos_ref, sin_ref, *rest):
    n_cast = (len(rest) - 9) // 2
    cast_in = rest[:n_cast]
    qT_ref, kT_ref, kbf_ref, kmean_ref, vT_ref, vTb_ref, u_ref, sga_ref, sgs_ref = rest[n_cast:n_cast + 9]
    cast_out = rest[n_cast + 9:]
    for wi_ref, wo_ref in zip(cast_in, cast_out):
        wo_ref[...] = wi_ref[...].astype(BF16)
    reps = ATTN_WIDTH // LANES
    cos = jnp.tile(cos_ref[...], (1, reps))
    sin = jnp.tile(sin_ref[...], (1, reps))
    q, k, v, u, sga, sgs = _inproj_core(x_ref[...], g_ref[...], w_ref, cos, sin, LOG2E * HEAD_DIM ** -0.5)
    tm = q.shape[0]
    kT_ref[0] = k.T
    vT_ref[0] = v.T
    kbf_ref[...] = k.astype(BF16)
    for s in range(tm // MOBA_BLOCK):
        rows = slice(s * MOBA_BLOCK, (s + 1) * MOBA_BLOCK)
        qT_ref[0, s] = q[rows].T.astype(BF16)
        vTb_ref[0, s] = v[rows].T.astype(BF16)
        kmean_ref[0, s] = jnp.mean(k[rows], axis=0, keepdims=True)
    u_ref[...] = u
    sga_ref[...] = sga.astype(BF16)
    sgs_ref[...] = sgs.astype(BF16)


def _inproj_prompt(x2d, g, w_bf, cos, sin, cast_ws, bsz, seq, tm):
    m = bsz * seq
    nb = seq // MOBA_BLOCK
    tpb = seq // tm
    sub = tm // MOBA_BLOCK
    a = ATTN_WIDTH
    full = lambda shape: pl.BlockSpec(shape, lambda b, t: (0,) * len(shape))
    tok = lambda w: pl.BlockSpec((tm, w), lambda b, t: (b * tpb + t, 0))
    out_shape = (
        jax.ShapeDtypeStruct((bsz, nb, a, MOBA_BLOCK), BF16),
        jax.ShapeDtypeStruct((bsz, a, seq), F32),
        jax.ShapeDtypeStruct((m, a), BF16),
        jax.ShapeDtypeStruct((bsz, nb, 1, a), F32),
        jax.ShapeDtypeStruct((bsz, a, seq), F32),
        jax.ShapeDtypeStruct((bsz, nb, a, MOBA_BLOCK), BF16),
        jax.ShapeDtypeStruct((m, SSM_WIDTH), F32),
        jax.ShapeDtypeStruct((m, D_MODEL), BF16),
        jax.ShapeDtypeStruct((m, D_MODEL), BF16),
    )
    blk_t = pl.BlockSpec((1, sub, a, MOBA_BLOCK), lambda b, t: (b, t, 0, 0))
    lane_t = pl.BlockSpec((1, a, tm), lambda b, t: (b, 0, t))
    out_specs = (blk_t, lane_t, tok(a),
                 pl.BlockSpec((1, sub, 1, a), lambda b, t: (b, t, 0, 0)),
                 lane_t, blk_t, tok(SSM_WIDTH), tok(D_MODEL), tok(D_MODEL))
    n_steps = bsz * tpb
    cast_specs = []
    for w in cast_ws:
        rows = w.shape[0]
        nblk = n_steps
        while rows % nblk or (rows // nblk) % BF16_SUBLANES:
            nblk //= 2
        cast_specs.append(pl.BlockSpec((rows // nblk, w.shape[1]),
                                       lambda b, t, n=nblk: (jnp.minimum(b * tpb + t, n - 1), 0)))
    return pl.pallas_call(
        _inproj_prompt_kernel,
        grid=(bsz, tpb),
        in_specs=[tok(D_MODEL), full((1, D_MODEL)), full(w_bf.shape),
                  pl.BlockSpec((tm, LANES), lambda b, t: (t, 0)),
                  pl.BlockSpec((tm, LANES), lambda b, t: (t, 0))] + cast_specs,
        out_specs=out_specs + tuple(cast_specs),
        out_shape=out_shape + tuple(jax.ShapeDtypeStruct(w.shape, BF16) for w in cast_ws),
        compiler_params=_cparams(("arbitrary", "arbitrary")),
        name="inproj_prompt",
    )(x2d, g, w_bf, cos, sin, *cast_ws)


def _inproj_sample_kernel(x_ref, g_ref, w_ref, cos_ref, sin_ref,
                          q_ref, k_ref, v_ref, u_ref, sga_ref, sgs_ref, wb_ref):
    step = ATTN_WIDTH
    for c in range(w_ref.shape[1] // step):
        wb_ref[:, c * step:(c + 1) * step] = w_ref[:, c * step:(c + 1) * step].astype(BF16)
    reps = ATTN_WIDTH // LANES
    cos = jnp.tile(cos_ref[...], (1, reps))
    sin = jnp.tile(sin_ref[...], (1, reps))
    q, k, v, u, sga, sgs = _inproj_core(x_ref[...], g_ref[...], wb_ref, cos, sin, HEAD_DIM ** -0.5)
    q_ref[...] = q
    k_ref[...] = k
    v_ref[...] = v
    u_ref[...] = u
    sga_ref[...] = sga.astype(BF16)
    sgs_ref[...] = sgs.astype(BF16)


def _inproj_sample(x2d, g, w_f32, cos, sin):
    m = x2d.shape[0]
    a = ATTN_WIDTH
    full = lambda shape, **kw: pl.BlockSpec(shape, lambda i: (0,) * len(shape), **kw)
    shapes = [(m, a), (m, a), (m, a), (m, SSM_WIDTH), (m, D_MODEL), (m, D_MODEL), w_f32.shape]
    dts = [F32, F32, F32, F32, BF16, BF16, BF16]
    return pl.pallas_call(
        _inproj_sample_kernel,
        grid=(1,),
        in_specs=[full(x2d.shape), full(g.shape), full(w_f32.shape, pipeline_mode=pl.Buffered(1)),
                  full(cos.shape), full(sin.shape)],
        out_specs=tuple(full(s) for s in shapes),
        out_shape=tuple(jax.ShapeDtypeStruct(s, d) for s, d in zip(shapes, dts)),
        compiler_params=_cparams(("arbitrary",)),
        name="inproj_sample",
    )(x2d, g, w_f32, cos, sin)


def _moba_prompt_kernel(qT_ref, k_ref, kmean_ref, vT_ref, o_ref, bias_ref, qz_s, m_s, acc_s):
    nb = qT_ref.shape[1]
    blk = MOBA_BLOCK
    nh = MOBA_HEADS_PER_GROUP
    gw = nh * HEAD_DIM
    ng = qT_ref.shape[2] // gw
    heads = [(g, hh) for g in range(ng) for hh in range(nh)]
    gcols = [slice(g * gw, (g + 1) * gw) for g in range(ng)]
    row2 = lax.broadcasted_iota(jnp.int32, (gw, 1), 0)
    lane_km = lax.broadcasted_iota(jnp.int32, (1, gw), 1)
    blk_row = lax.broadcasted_iota(jnp.int32, (nb, blk), 0)
    key_i = lax.broadcasted_iota(jnp.int32, (blk, blk), 0)
    qry_i = lax.broadcasted_iota(jnp.int32, (blk, blk), 1)
    causal = key_i <= qry_i
    in_head = [(row2 >= hh * HEAD_DIM) & (row2 < (hh + 1) * HEAD_DIM) for hh in range(nh)]
    hrows = [slice(h * HEAD_DIM, (h + 1) * HEAD_DIM) for h in range(ng * nh)]

    for h in range(ng * nh):
        bias_ref[h, 0] = jnp.full((nb, blk), NEG_BIG, F32)
    for g in range(ng):
        km_g = kmean_ref[0][:, gcols[g]]
        km_all = jnp.concatenate(
            [jnp.where((lane_km >= hh * HEAD_DIM) & (lane_km < (hh + 1) * HEAD_DIM), km_g, 0.0)
             for hh in range(nh)], axis=0)
        km_terms = []
        rest = km_all
        for _ in range(3):
            term = rest.astype(BF16)
            km_terms.append(term)
            rest = rest - term.astype(F32)
        km_split = jnp.concatenate(km_terms, axis=0)
        for i in range(1, nb):
            sb3 = _dot(km_split, qT_ref[0, i, gcols[g], :])
            sb_all = (sb3[0:nh * nb] + sb3[nh * nb:2 * nh * nb]) + sb3[2 * nh * nb:]
            for hh in range(nh):
                sb = jnp.where(blk_row < i, sb_all[hh * nb:(hh + 1) * nb], -jnp.inf)
                bias = jnp.full((nb, blk), NEG_BIG, F32)
                for _r in range(min(MOBA_TOPK, i)):
                    mx = jnp.max(sb, axis=0, keepdims=True)
                    first = jnp.min(jnp.where(sb == mx, blk_row, nb), axis=0, keepdims=True)
                    pick = blk_row == first
                    bias = jnp.where(pick, 0.0, bias)
                    sb = jnp.where(pick, -jnp.inf, sb)
                bias_ref[g * nh + hh, i] = bias

    n_items = nb // 2
    ones_rows = jnp.ones((MOBA_SUM_ROWS, 2 * blk), BF16)

    def couple(i, _):
        iq = (i, nb - 1 - i)
        n_first = (i + 1) // 2

        def diag_scores(x):
            rows = pl.ds(pl.multiple_of(iq[x] * blk, blk), blk)
            out = []
            for h, (g, hh) in enumerate(heads):
                q_grp = qT_ref[0, iq[x], gcols[g], :]
                qz = jnp.where(in_head[hh], q_grp, jnp.zeros_like(q_grp))
                qz_s[x, h] = qz
                out.append(_dot(k_ref[rows, gcols[g]], qz))
            return out

        def diag_absorb(x, s_own):
            for h in range(len(heads)):
                s = jnp.where(causal, s_own[h], NEG_BIG)
                m0 = jnp.max(s, axis=0, keepdims=True)
                p = jnp.exp2(s - m0)
                m_s[x, h] = m0
                vv = jnp.concatenate([vT_ref[0, iq[x], hrows[h], :], ones_rows[:, :blk]], axis=0)
                acc_s[x, h] = _dot(vv, p.astype(BF16))

        def item_params(k):
            first = k < n_first
            x = jnp.where(first, 0, 1)
            return x, jnp.where(first, iq[0], iq[1]), jnp.where(first, k, k - n_first)

        def item_scores(k):
            x, _, jp = item_params(k)
            rows = pl.ds(pl.multiple_of(2 * jp * blk, blk), 2 * blk)
            out = []
            for h, (g, _hh) in enumerate(heads):
                s = _dot(k_ref[rows, gcols[g]], qz_s[x, h])
                out.append((s, jnp.max(s[:blk], axis=0, keepdims=True), jnp.max(s[blk:], axis=0, keepdims=True)))
            return out

        def item_absorb(k, sc):
            x, qb, jp = item_params(k)
            j0 = 2 * jp
            for h in range(len(heads)):
                s, cma, cmb = sc[h]
                ba = bias_ref[h, qb, pl.ds(j0, 1), :]
                bb = bias_ref[h, qb, pl.ds(j0 + 1, 1), :]
                m = m_s[x, h]
                m_new = jnp.maximum(m, jnp.maximum(cma + ba, cmb + bb))
                alpha = jnp.exp2(m - m_new)
                pa = jnp.exp2(s[:blk] - (m_new - ba))
                pb = jnp.exp2(s[blk:] - (m_new - bb))
                m_s[x, h] = m_new
                pp = jnp.concatenate([pa, pb], axis=0).astype(BF16)
                vv = jnp.concatenate([vT_ref[0, j0, hrows[h], :], vT_ref[0, j0 + 1, hrows[h], :]], axis=1)
                vv = jnp.concatenate([vv, ones_rows], axis=0)
                acc_s[x, h] = alpha * acc_s[x, h] + _dot(vv, pp)

        s_diag = [diag_scores(0), diag_scores(1)]
        pending = [item_scores(k) for k in range(min(MOBA_LOOKAHEAD, n_items))]
        diag_absorb(0, s_diag[0])
        diag_absorb(1, s_diag[1])
        for k in range(n_items):
            if k + MOBA_LOOKAHEAD < n_items:
                pending.append(item_scores(k + MOBA_LOOKAHEAD))
            item_absorb(k, pending[k])
        for x in range(2):
            oT = jnp.concatenate([acc_s[x, h, 0:HEAD_DIM, :] / acc_s[x, h, HEAD_DIM:HEAD_DIM + 1, :]
                                  for h in range(len(heads))], axis=0)
            o_ref[pl.ds(pl.multiple_of(iq[x] * blk, blk), blk), :] = oT.T.astype(o_ref.dtype)
        return 0

    lax.fori_loop(0, nb // 2, couple, 0)


def _moba_prompt(qT, kbf, kmean, vTb, bsz, seq):
    nb = seq // MOBA_BLOCK
    hps = MOBA_HEADS_PER_GROUP * MOBA_GROUPS_PER_STEP
    hp = N_HEADS // hps
    pair = hps * HEAD_DIM
    gw = MOBA_HEADS_PER_GROUP * HEAD_DIM
    return pl.pallas_call(
        _moba_prompt_kernel,
        grid=(bsz, hp),
        in_specs=[pl.BlockSpec((1, nb, pair, MOBA_BLOCK), lambda b, h: (b, 0, h, 0)),
                  pl.BlockSpec((seq, pair), lambda b, h: (b, h)),
                  pl.BlockSpec((1, nb, pair), lambda b, h: (b, 0, h)),
                  pl.BlockSpec((1, nb, pair, MOBA_BLOCK), lambda b, h: (b, 0, h, 0))],
        out_specs=pl.BlockSpec((seq, pair), lambda b, h: (b, h)),
        out_shape=jax.ShapeDtypeStruct((bsz * seq, ATTN_WIDTH), BF16),
        scratch_shapes=[pltpu.VMEM((hps, nb, nb, MOBA_BLOCK), F32),
                        pltpu.VMEM((2, hps, gw, MOBA_BLOCK), BF16),
                        pltpu.VMEM((2, hps, 1, MOBA_BLOCK), F32),
                        pltpu.VMEM((2, hps, HEAD_DIM + MOBA_SUM_ROWS, MOBA_BLOCK), F32)],
        compiler_params=_cparams(("parallel", "parallel")),
        name="moba_prompt",
    )(qT, kbf, kmean, vTb)


def _moba_paged_kernel(pt_ref, qcol_ref, kn_ref, vn_ref, ck_ref, cv_ref, o_ref,
                       kbuf, vbuf, s_ref, psel_ref, stat_ref, qb_ref, ksem, vsem):
    b = pl.program_id(0)
    nseq = pl.num_programs(0) - 1
    n_pages = kbuf.shape[1]
    nblk = n_pages // 2
    slot = b % 2

    def k_copy(seq_i, sl, p):
        return pltpu.make_async_copy(ck_ref.at[pt_ref[seq_i, p]], kbuf.at[sl, p], ksem.at[sl])

    def start_k(seq_i, sl):
        def body(p, _):
            k_copy(seq_i, sl, p).start()
            return 0
        lax.fori_loop(0, n_pages, body, 0)

    def v_copy(page, h, r, par):
        return pltpu.make_async_copy(cv_ref.at[page, h], vbuf.at[h, r, par], vsem.at[0])

    @pl.when(b == 0)
    def _():
        start_k(0, 0)
        start_k(1, 1)

    @pl.when(b < nseq)
    def _():
        def wait_body(p, _):
            k_copy(b, slot, p).wait()
            return 0
        lax.fori_loop(0, n_pages, wait_body, 0)

    @pl.when(b >= 1)
    def _():
        for h in range(N_HEADS):
            for r in range(MOBA_TOPK):
                for par in range(2):
                    v_copy(0, h, r, par).wait()
        vn = vn_ref[0]
        for h in range(N_HEADS):
            acc = jnp.zeros((HEAD_DIM, PAGE_SIZE), F32)
            for r in range(MOBA_TOPK):
                for par in range(2):
                    acc = acc + vbuf[h, r, par] * psel_ref[h, 2 * r + par]
            p_own = stat_ref[0, h][:, 0:1]
            l = stat_ref[1, h][:, 0:1]
            o_h = jnp.sum(acc, axis=1, keepdims=True) + p_own * vn[:, h:h + 1]
            o_ref[0, :, h:h + 1] = o_h / l

    @pl.when(b < nseq)
    def _():
        qcol = qcol_ref[0]
        own = jnp.sum(qcol * kn_ref[0], axis=0, keepdims=True)
        blk_i = lax.broadcasted_iota(jnp.int32, (nblk, 1), 0)
        for h in range(N_HEADS):
            qb_ref[h] = jnp.broadcast_to(qcol[:, h:h + 1], (HEAD_DIM, PAGE_SIZE))
        for h in range(N_HEADS):
            def score_body(g, _):
                qb = qb_ref[h]
                for dn in range(PAGED_SCORE_UNROLL):
                    n = g * PAGED_SCORE_UNROLL + dn
                    for par in range(2):
                        kt = kbuf[slot, 2 * n + par, h]
                        s_ref[par, h, pl.ds(n, 1), :] = jnp.sum(kt * qb, axis=0, keepdims=True)
                return 0
            lax.fori_loop(0, nblk // PAGED_SCORE_UNROLL, score_body, 0)

        @pl.when(b + 2 < nseq)
        def _():
            start_k(b + 2, slot)

        for h in range(N_HEADS):
            s0 = s_ref[0, h]
            s1 = s_ref[1, h]
            bs = jnp.sum(s0 + s1, axis=1, keepdims=True)
            sel = jnp.zeros((nblk, 1), dtype=jnp.bool_)
            picks = []
            for r in range(MOBA_TOPK):
                mx = jnp.max(bs, axis=0, keepdims=True)
                first = jnp.min(jnp.where(bs == mx, blk_i, nblk), axis=0, keepdims=True)
                pick = blk_i == first
                sel = sel | pick
                bs = jnp.where(pick, -jnp.inf, bs)
                blk_id = jnp.max(first)
                for par in range(2):
                    v_copy(pt_ref[b, 2 * blk_id + par], h, r, par).start()
                picks.append(blk_id)
            s_own = own[:, h:h + 1]
            sm0 = jnp.where(sel, s0, NEG_BIG)
            sm1 = jnp.where(sel, s1, NEG_BIG)
            mx = jnp.maximum(jnp.max(jnp.max(jnp.maximum(sm0, sm1), axis=1, keepdims=True),
                                     axis=0, keepdims=True), s_own)
            p0 = jnp.exp(sm0 - mx)
            p1 = jnp.exp(sm1 - mx)
            p_own = jnp.exp(s_own - mx)
            l = jnp.sum(jnp.sum(p0 + p1, axis=1, keepdims=True), axis=0, keepdims=True) + p_own
            s_ref[0, h] = p0
            s_ref[1, h] = p1
            for r in range(MOBA_TOPK):
                for par in range(2):
                    psel_ref[h, 2 * r + par] = s_ref[par, h, pl.ds(picks[r], 1), :]
            stat_ref[0, h] = jnp.broadcast_to(p_own, (1, PAGE_SIZE))
            stat_ref[1, h] = jnp.broadcast_to(l, (1, PAGE_SIZE))


def _moba_paged(page_table, qcol, kncol, vncol, cache_kT, cache_vT):
    nseq, n_pages = page_table.shape
    assert n_pages // 2 >= MOBA_TOPK, "every pick must find an unpicked cached block"
    assert nseq >= 2, "the first grid step prefetches two sequences"
    cur = pl.BlockSpec((1, HEAD_DIM, N_HEADS), lambda b, pt: (jnp.minimum(b, nseq - 1), 0, 0))
    prev = pl.BlockSpec((1, HEAD_DIM, N_HEADS), lambda b, pt: (jnp.maximum(b - 1, 0), 0, 0))
    any_spec = pl.BlockSpec(memory_space=pl.ANY)
    grid_spec = pltpu.PrefetchScalarGridSpec(
        num_scalar_prefetch=1,
        grid=(nseq + 1,),
        in_specs=[cur, cur, prev, any_spec, any_spec],
        out_specs=prev,
        scratch_shapes=[
            pltpu.VMEM((2, n_pages, N_HEADS, HEAD_DIM, PAGE_SIZE), F32),
            pltpu.VMEM((N_HEADS, MOBA_TOPK, 2, HEAD_DIM, PAGE_SIZE), F32),
            pltpu.VMEM((2, N_HEADS, n_pages // 2, PAGE_SIZE), F32),
            pltpu.VMEM((N_HEADS, 2 * MOBA_TOPK, 1, PAGE_SIZE), F32),
            pltpu.VMEM((2, N_HEADS, 1, PAGE_SIZE), F32),
            pltpu.VMEM((N_HEADS, HEAD_DIM, PAGE_SIZE), F32),
            pltpu.SemaphoreType.DMA((2,)),
            pltpu.SemaphoreType.DMA((1,)),
        ])
    return pl.pallas_call(
        _moba_paged_kernel,
        grid_spec=grid_spec,
        out_shape=jax.ShapeDtypeStruct((nseq, HEAD_DIM, N_HEADS), F32),
        compiler_params=_cparams(("arbitrary",)),
        name="moba_paged",
    )(page_table, qcol, kncol, vncol, cache_kT, cache_vT)


def _ssm_tables(a_re, a_im, log_dt, b_re, b_im, c_re, c_im, chunk, n_pow):
    t_ = chunk
    lam = lax.complex(a_re.astype(F32), a_im.astype(F32))
    ldt = lam * jnp.exp(log_dt.astype(F32))[:, None]
    a_bar = jnp.exp(ldt)
    b_bar = ((a_bar - 1.0) / lam)[..., None] * lax.complex(b_re.astype(F32), b_im.astype(F32))
    c_c = lax.complex(c_re.astype(F32), c_im.astype(F32))
    taus = jnp.arange(t_ + 1, dtype=F32).astype(jnp.complex64)
    apow = jnp.exp(ldt[None] * taus[:, None, None])
    gq, ssg, c_, p_ = GROUPS_PER_SG, N_SG, SSM_GROUP, SSM_STATE
    c4 = c_c.reshape(ssg, gq, c_, p_)
    b4 = b_bar.reshape(ssg, gq, p_, c_)
    ap4 = apow.reshape(t_ + 1, ssg, gq, p_)

    kc = jnp.einsum('sjcp,tsjp,sjpd->sjdtc', c4, ap4[:t_], b4).real
    kc = kc.reshape(ssg, LANES, t_ * c_)
    mst = jnp.einsum('tsjp,sjpd->stjdp', ap4[:t_][::-1], b4)
    mc = jnp.concatenate([mst.real, mst.imag], axis=-1).reshape(ssg, t_ * LANES, 2 * p_)
    nout = jnp.einsum('sjcp,tsjp->sjptc', c4, ap4[1:])
    nc = jnp.concatenate([nout.real, -nout.imag], axis=1).reshape(ssg, 2 * SG_STATE, t_ * c_)
    rs = jnp.arange(1, n_pow + 1, dtype=F32).astype(jnp.complex64)
    ap = jnp.exp((ldt * t_)[None] * rs[:, None, None])
    ap = ap.reshape(n_pow, ssg, SG_STATE).transpose(1, 0, 2)
    a_pow = jnp.concatenate([ap.real, ap.imag], axis=-1)
    a1 = a_bar.reshape(ssg, 1, SG_STATE)
    a_one = jnp.concatenate([a1.real, a1.imag], axis=-1)
    return kc, mc, nc, a_pow, a_one


def _expand_ssm_weights(kc_ref, mc_ref, nc_ref, wt_s, ws_s, wo_s):
    t_ = SSM_CHUNK
    kw = t_ * LANES
    lg = lambda n: n.bit_length() - 1
    r = lax.broadcasted_iota(jnp.int32, (LANES, kw), 0)
    q = lax.broadcasted_iota(jnp.int32, (LANES, kw), 1)
    sh_c, sh_p = lg(SSM_GROUP), lg(SSM_STATE)
    ecol = jnp.where(((r >> sh_c) == (q >> lg(LANES))) & ((r & (SSM_GROUP - 1)) == (q & (SSM_GROUP - 1))),
                     1.0, 0.0).astype(BF16)
    e2 = jnp.where(((r >> sh_p) == (q >> lg(SG_STATE))) & ((r & (SSM_STATE - 1)) == (q & (SSM_STATE - 1))),
                   1.0, 0.0).astype(BF16)
    col_c = (q & (LANES - 1)) >> sh_c
    taps = jnp.where((r >> sh_c) == col_c, _dot(kc_ref[0].astype(BF16), ecol), 0.0).astype(BF16)
    blank = jnp.zeros((LANES, LANES), BF16)
    for s in range(t_):
        for t in range(t_):
            wt_s[s * LANES:(s + 1) * LANES, t * LANES:(t + 1) * LANES] = (
                taps[:, (t - s) * LANES:(t - s + 1) * LANES] if t >= s else blank)
    rr = lax.broadcasted_iota(jnp.int32, (kw, kw), 0)
    qq = lax.broadcasted_iota(jnp.int32, (kw, kw), 1)
    row_c = (rr & (LANES - 1)) >> sh_c
    row_p = (rr & (SG_STATE - 1)) >> sh_p
    colq_c = (qq & (LANES - 1)) >> sh_c
    colq_p = (qq & (SG_STATE - 1)) >> sh_p
    ws_s[...] = jnp.where(row_c == colq_p, _dot(mc_ref[0].astype(BF16), e2), 0.0).astype(BF16)
    wo_s[...] = jnp.where(row_p == colq_c, _dot(nc_ref[0].astype(BF16), ecol), 0.0).astype(BF16)


def _ssm_prompt_kernel(u_ref, kc_ref, mc_ref, nc_ref, ap_ref, d_ref, h0_ref,
                       y_ref, hf_ref, uc_ref, x_ref, hin_ref, wt_s, ws_s, wo_s):
    t_ = SSM_CHUNK
    nc = uc_ref.shape[0]
    ns = SG_STATE

    @pl.when(pl.program_id(1) == 0)
    def _():
        _expand_ssm_weights(kc_ref, mc_ref, nc_ref, wt_s, ws_s, wo_s)

    for t in range(t_):
        uc_ref[:, t * LANES:(t + 1) * LANES] = u_ref[pl.ds(t, nc, stride=t_), :].astype(BF16)
    uc = uc_ref[...]
    x_ref[...] = _dot(uc, ws_s[...])

    ap = ap_ref[0]
    apr, api = ap[:, :ns], ap[:, ns:]
    row = lax.broadcasted_iota(jnp.int32, (SUBLANES, 1), 0)

    def tile_body(k, carry):
        hr, hi = carry
        rows = pl.ds(k * SUBLANES, SUBLANES)
        xr = x_ref[rows, 0:ns]
        xi = x_ref[rows, ns:2 * ns]
        for d in (1, 2, 4):
            ar, ai = apr[d - 1:d], api[d - 1:d]
            sr = jnp.where(row >= d, pltpu.roll(xr, d, 0), 0.0)
            si = jnp.where(row >= d, pltpu.roll(xi, d, 0), 0.0)
            xr, xi = xr + ar * sr - ai * si, xi + ar * si + ai * sr
        outr = xr + apr * hr - api * hi
        outi = xi + apr * hi + api * hr
        hin_ref[rows, 0:ns] = jnp.where(row >= 1, pltpu.roll(outr, 1, 0), hr)
        hin_ref[rows, ns:2 * ns] = jnp.where(row >= 1, pltpu.roll(outi, 1, 0), hi)
        return outr[SUBLANES - 1:SUBLANES], outi[SUBLANES - 1:SUBLANES]

    cols = []
    for nt in range(t_ * LANES // MXU_COLS):
        acc = None
        for kt in range(nt + 1):
            part = _dot(uc[:, kt * MXU_COLS:(kt + 1) * MXU_COLS],
                        wt_s[kt * MXU_COLS:(kt + 1) * MXU_COLS, nt * MXU_COLS:(nt + 1) * MXU_COLS])
            acc = part if acc is None else acc + part
        cols.append(acc)
    y_intra = jnp.concatenate(cols, axis=1)
    h0 = h0_ref[0, 0]
    carry = (h0[:, :ns], h0[:, ns:])
    for k in range(nc // SUBLANES):
        carry = tile_body(k, carry)
    hf_ref[0, 0] = jnp.concatenate(carry, axis=1)

    y = y_intra + _dot(hin_ref[...].astype(BF16), wo_s[...])
    dv = d_ref[...]
    for t in range(t_):
        rows = pl.ds(t, nc, stride=t_)
        y_ref[rows, :] = y[:, t * LANES:(t + 1) * LANES] + dv * u_ref[rows, :]


def _ssm_prompt(u, kc, mc, nc_tab, a_pow, d_row, h0, bsz, seq):
    t_ = SSM_CHUNK
    nc = seq // t_
    kw = t_ * LANES
    assert kc.shape[2] == LANES and 2 * SG_STATE == kw
    wspec = lambda shape: pl.BlockSpec((1,) + shape, lambda s, b: (s, 0, 0))
    return pl.pallas_call(
        _ssm_prompt_kernel,
        grid=(N_SG, bsz),
        in_specs=[pl.BlockSpec((seq, LANES), lambda s, b: (b, s)),
                  wspec(kc.shape[1:]), wspec(mc.shape[1:]), wspec(nc_tab.shape[1:]),
                  wspec((SUBLANES, 2 * SG_STATE)),
                  pl.BlockSpec((1, LANES), lambda s, b: (0, s)),
                  pl.BlockSpec((1, 1, 1, 2 * SG_STATE), lambda s, b: (b, s, 0, 0))],
        out_specs=(pl.BlockSpec((seq, LANES), lambda s, b: (b, s)),
                   pl.BlockSpec((1, 1, 1, 2 * SG_STATE), lambda s, b: (b, s, 0, 0))),
        out_shape=(jax.ShapeDtypeStruct((bsz * seq, SSM_WIDTH), F32),
                   jax.ShapeDtypeStruct((bsz, N_SG, 1, 2 * SG_STATE), F32)),
        scratch_shapes=[pltpu.VMEM((nc, kw), BF16),
                        pltpu.VMEM((nc, 2 * SG_STATE), F32),
                        pltpu.VMEM((nc, 2 * SG_STATE), F32),
                        pltpu.VMEM((kw, kw), BF16),
                        pltpu.VMEM((kw, 2 * SG_STATE), BF16),
                        pltpu.VMEM((2 * SG_STATE, kw), BF16)],
        compiler_params=_cparams(("arbitrary", "arbitrary")),
        name="ssm_prompt",
    )(u, kc, mc, nc_tab, a_pow, d_row, h0)


def _ssm_step_kernel(u_ref, kc_ref, mc_ref, nc_ref, a1_ref, d_ref, h0_ref, y_ref, hf_ref):
    ns = SG_STATE
    kw = 2 * ns
    lg = lambda n: n.bit_length() - 1
    sh_c, sh_p = lg(SSM_GROUP), lg(SSM_STATE)
    iota = lambda shape, d: lax.broadcasted_iota(jnp.int32, shape, d)
    r1, q1 = iota((LANES, LANES), 0), iota((LANES, LANES), 1)
    e_c = jnp.where((r1 < SSM_GROUP) & ((r1 & (SSM_GROUP - 1)) == (q1 & (SSM_GROUP - 1))), 1.0, 0.0).astype(BF16)
    r2, q2 = iota((LANES, kw), 0), iota((LANES, kw), 1)
    e_p = jnp.where(((r2 >> sh_p) == (q2 >> lg(ns))) & ((r2 & (SSM_STATE - 1)) == (q2 & (SSM_STATE - 1))),
                    1.0, 0.0).astype(BF16)
    same_t = (r1 >> sh_c) == (q1 >> sh_c)
    same_s = (r2 >> sh_c) == ((q2 & (ns - 1)) >> sh_p)
    r3, q3 = iota((kw, LANES), 0), iota((kw, LANES), 1)
    same_o = ((r3 & (ns - 1)) >> sh_p) == (q3 >> sh_c)
    last = (SSM_CHUNK - 1) * LANES
    for s in range(N_SG):
        wt = jnp.where(same_t, _dot(kc_ref[s].astype(BF16), e_c), 0.0).astype(BF16)
        ws = jnp.where(same_s, _dot(mc_ref[s, last:last + LANES, :].astype(BF16), e_p), 0.0).astype(BF16)
        wo = jnp.where(same_o, _dot(nc_ref[s].astype(BF16), e_c), 0.0).astype(BF16)
        us = u_ref[:, s * LANES:(s + 1) * LANES]
        ub = us.astype(BF16)
        h0 = h0_ref[s]
        x = _dot(ub, ws)
        ar, ai = a1_ref[s, :, :ns], a1_ref[s, :, ns:]
        hr, hi = h0[:, :ns], h0[:, ns:]
        hf_ref[s] = jnp.concatenate([x[:, :ns] + ar * hr - ai * hi,
                                     x[:, ns:] + ar * hi + ai * hr], axis=1)
        y = _dot(ub, wt) + _dot(h0.astype(BF16), wo)
        y_ref[:, s * LANES:(s + 1) * LANES] = y + d_ref[:, s * LANES:(s + 1) * LANES] * us


def _ssm_step(u, kc, mc, nc_tab, a_one, d_row, h0):
    m = u.shape[0]
    full = lambda a: pl.BlockSpec(a.shape, lambda i: (0,) * a.ndim)
    args = (u, kc, mc, nc_tab, a_one, d_row, h0)
    shapes = [(m, SSM_WIDTH), (N_SG, m, 2 * SG_STATE)]
    return pl.pallas_call(
        _ssm_step_kernel,
        grid=(1,),
        in_specs=[full(a) for a in args],
        out_specs=tuple(pl.BlockSpec(s, lambda i, n=len(s): (0,) * n) for s in shapes),
        out_shape=tuple(jax.ShapeDtypeStruct(s, F32) for s in shapes),
        compiler_params=_cparams(("arbitrary",)),
        name="ssm_step",
    )(*args)


FFN_SPLIT = FFN_HIDDEN // MXU_COLS
FFN_CHUNK = FFN_HIDDEN // FFN_SPLIT


def _post_kernel(x_ref, o_ref, y_ref, sga_ref, sgs_ref, wap_ref, wglu_ref, bglu_ref, wsp_ref,
                 wout_ref, nffn_ref, wfi_ref, wfo_ref, nfin_ref, out_ref):
    z = jax.nn.gelu(y_ref[...])
    z = z * jax.nn.sigmoid(_dot(z.astype(BF16), wglu_ref[...]) + bglu_ref[...])
    ssm_out = _dot(z.astype(BF16), wsp_ref[...])
    attn_out = _dot(o_ref[...].astype(BF16), wap_ref[...])
    merged = sga_ref[...].astype(F32) * attn_out + sgs_ref[...].astype(F32) * ssm_out
    x1 = x_ref[...] + _dot(merged.astype(BF16), wout_ref[...])
    hf = _rms(x1, nffn_ref[...]).astype(BF16)
    acc = x1
    for c in range(FFN_SPLIT):
        lo = c * FFN_CHUNK
        a = _dot(hf, wfi_ref[:, lo:lo + FFN_CHUNK])
        g = _dot(hf, wfi_ref[:, FFN_HIDDEN + lo:FFN_HIDDEN + lo + FFN_CHUNK])
        act = (jax.nn.silu(a) * g).astype(BF16)
        acc = acc + _dot(act, wfo_ref[lo:lo + FFN_CHUNK, :])
    out_ref[...] = _rms(acc, nfin_ref[...])


def _post(x2d, o, y, sga, sgs, wap, wglu, bglu, wsp, wout, nffn, wfi, wfo, nfin, tm):
    m = x2d.shape[0]
    tok = lambda w: pl.BlockSpec((tm, w), lambda i: (i, 0))
    const = lambda a: pl.BlockSpec(a.shape, lambda i: (0,) * a.ndim, pipeline_mode=pl.Buffered(1))
    weights = (wap, wglu, bglu, wsp, wout, nffn, wfi, wfo, nfin)
    return pl.pallas_call(
        _post_kernel,
        grid=(m // tm,),
        in_specs=[tok(D_MODEL), tok(ATTN_WIDTH), tok(SSM_WIDTH), tok(D_MODEL), tok(D_MODEL)]
                 + [const(w) for w in weights],
        out_specs=tok(D_MODEL),
        out_shape=jax.ShapeDtypeStruct((m, D_MODEL), F32),
        compiler_params=_cparams(("parallel",)),
        name="post",
    )(x2d, o, y, sga, sgs, *weights)


def _rope_tables(pos):
    half = HEAD_DIM // 2
    inv = jnp.power(jnp.float32(ROPE_THETA), -2.0 * jnp.arange(half, dtype=F32) / HEAD_DIM)
    ang = pos.astype(F32)[:, None] * inv[None, :]
    cos, sin = jnp.cos(ang), jnp.sin(ang)
    reps = LANES // HEAD_DIM
    return (jnp.tile(jnp.concatenate([cos, cos], axis=1), (1, reps)),
            jnp.tile(jnp.concatenate([-sin, sin], axis=1), (1, reps)))


def _leaf_from_T(xT, bsz, seq):
    return xT.reshape(bsz, N_HEADS, HEAD_DIM, seq).transpose(0, 3, 1, 2)[None]


def _state_in(re, im):
    n = re.shape[0]
    h = jnp.concatenate([re.reshape(n, N_SG, SG_STATE), im.reshape(n, N_SG, SG_STATE)], axis=-1)
    return h.transpose(1, 0, 2)


def _state_out(h):
    n = h.shape[0]
    return (h[..., :SG_STATE].reshape(1, n, SSM_GROUPS, SSM_STATE),
            h[..., SG_STATE:].reshape(1, n, SSM_GROUPS, SSM_STATE))


def kernel(x_prompt, x_sample, cache_k, cache_v, state_ssm_re, state_ssm_im, page_table, norm_mix, w_in,
           w_attn_proj, ssm_a_re, ssm_a_im, ssm_log_dt, ssm_b_re, ssm_b_im, ssm_c_re, ssm_c_im, ssm_d, w_glu,
           b_glu, w_ssm_proj, w_out, norm_ffn, w_ffn_in, w_ffn_out, norm_final):
    assert w_in.shape[0] == 1, "single layer"
    bsz, seq = x_prompt.shape[:2]
    nseq = x_sample.shape[0]
    past_len = page_table.shape[1] * PAGE_SIZE
    assert seq % MOBA_BLOCK == 0 and past_len % MOBA_BLOCK == 0 and x_sample.shape[1] == 1

    ssm_p = (ssm_a_re[0], ssm_a_im[0], ssm_log_dt[0], ssm_b_re[0], ssm_b_im[0], ssm_c_re[0], ssm_c_im[0])

    cos_s, sin_s = _rope_tables(jnp.full((1,), past_len, dtype=jnp.int32))
    xs2 = x_sample.reshape(nseq, D_MODEL)
    q_s, k_s, v_s, u_s, sga_s, sgs_s, w_in_bf = _inproj_sample(xs2, norm_mix, w_in[0], cos_s, sin_s)

    cos_p, sin_p = _rope_tables(jnp.arange(seq, dtype=jnp.int32))
    xp2 = x_prompt.reshape(bsz * seq, D_MODEL)
    cast_ws = (w_attn_proj[0], w_glu[0], w_ssm_proj[0], w_out[0], w_ffn_in[0], w_ffn_out[0])
    (qT, kT, kbf, kmean, vT, vTb, u_p, sga_p, sgs_p,
     wap_bf, wglu_bf, wsp_bf, wout_bf, wfi_bf, wfo_bf) = _inproj_prompt(
        xp2, norm_mix, w_in_bf, cos_p, sin_p, cast_ws, bsz, seq, tm=512)
    post_w = (wap_bf, wglu_bf, b_glu, wsp_bf, wout_bf, norm_ffn, wfi_bf, wfo_bf, norm_final[None])
    o_p = _moba_prompt(qT, kbf, kmean.reshape(bsz, seq // MOBA_BLOCK, ATTN_WIDTH), vTb, bsz, seq)
    kc, mc, nc_tab, ap, a_one = _ssm_tables(*ssm_p, chunk=SSM_CHUNK, n_pow=SUBLANES)
    h0_p = jnp.zeros((bsz, N_SG, 1, 2 * SG_STATE), F32)
    y_p, hf_p = _ssm_prompt(u_p, kc, mc, nc_tab, ap, ssm_d, h0_p, bsz, seq)
    y_prompt = _post(xp2, o_p, y_p, sga_p, sgs_p, *post_w, tm=512).reshape(bsz, seq, D_MODEL)
    new_ssm_re_p, new_ssm_im_p = _state_out(hf_p.reshape(bsz, N_SG, 2 * SG_STATE))

    cache_kT = cache_k[0].transpose(0, 2, 3, 1)
    cache_vT = cache_v[0].transpose(0, 2, 3, 1)
    dh = lambda t: t.reshape(nseq, N_HEADS, HEAD_DIM).transpose(0, 2, 1)
    o_s = _moba_paged(page_table, dh(q_s), dh(k_s), dh(v_s), cache_kT, cache_vT).transpose(0, 2, 1)
    h0_s = _state_in(state_ssm_re[0], state_ssm_im[0])
    y_s, hf_s = _ssm_step(u_s, kc, mc, nc_tab, a_one, ssm_d, h0_s)
    y_sample = _post(xs2, o_s.reshape(nseq, ATTN_WIDTH), y_s, sga_s, sgs_s, *post_w, tm=nseq)
    new_ssm_re_s, new_ssm_im_s = _state_out(hf_s.transpose(1, 0, 2))

    return (y_prompt, y_sample.reshape(nseq, 1, D_MODEL),
            _leaf_from_T(kT, bsz, seq), _leaf_from_T(vT, bsz, seq), new_ssm_re_p, new_ssm_im_p,
            k_s.reshape(1, nseq, 1, N_HEADS, HEAD_DIM), v_s.reshape(1, nseq, 1, N_HEADS, HEAD_DIM),
            new_ssm_re_s, new_ssm_im_s)
```

```python
import functools
import math

import jax
import jax.numpy as jnp
from jax import lax
from jax.experimental import pallas as pl
from jax.experimental.pallas import tpu as pltpu

F32 = jnp.float32
BF16 = jnp.bfloat16

D_MODEL = 1024
N_HEADS = 8
HEAD_DIM = 64
ATTN_WIDTH = N_HEADS * HEAD_DIM
MOBA_BLOCK = 256
MOBA_TOPK = 3
ROPE_THETA = 10000.0
SSM_WIDTH = 512
SSM_GROUP = 16
SSM_GROUPS = 32
SSM_STATE = 64
FFN_HIDDEN = 2816
RMS_EPS = 1e-6
PAGE_SIZE = 128

LANES = 128
SUBLANES = 8
BF16_SUBLANES = 16
MXU_COLS = 256
GROUPS_PER_SG = LANES // SSM_GROUP
N_SG = SSM_GROUPS // GROUPS_PER_SG
SG_STATE = GROUPS_PER_SG * SSM_STATE
SSM_CHUNK = 8
NEG_BIG = -1e30
MOBA_GROUPS_PER_STEP = 1
MOBA_HEADS_PER_GROUP = 4
MOBA_LOOKAHEAD = 2
LOG2E = math.log2(math.e)
MOBA_SUM_ROWS = 16
PAGED_SCORE_UNROLL = 4
VMEM_LIMIT = 56 * 1024 * 1024


def _cparams(sem):
    return pltpu.CompilerParams(dimension_semantics=sem, vmem_limit_bytes=VMEM_LIMIT)


def _dot(a, b):
    return jnp.dot(a, b, preferred_element_type=F32)


def _rms(x, g):
    return x * lax.rsqrt(jnp.mean(x * x, axis=-1, keepdims=True) + RMS_EPS) * g


def _inproj_core(x, g, w_ref, cos, sin, q_scale):
    h = _rms(x, g).astype(BF16)
    lane = lax.broadcasted_iota(jnp.int32, (1, ATTN_WIDTH), 1)
    first_half = (lane % HEAD_DIM) < (HEAD_DIM // 2)

    def rot(t):
        partner = jnp.where(first_half,
                            pltpu.roll(t, ATTN_WIDTH - HEAD_DIM // 2, 1),
                            pltpu.roll(t, HEAD_DIM // 2, 1))
        return t * cos + partner * sin

    a = ATTN_WIDTH
    q = rot(_dot(h, w_ref[:, 0:a])) * q_scale
    k = rot(_dot(h, w_ref[:, a:2 * a]))
    v = _dot(h, w_ref[:, 2 * a:3 * a])
    u = _dot(h, w_ref[:, 3 * a:3 * a + SSM_WIDTH])
    o = 3 * a + SSM_WIDTH
    sga = jax.nn.sigmoid(_dot(h, w_ref[:, o:o + D_MODEL]))
    sgs = jax.nn.sigmoid(_dot(h, w_ref[:, o + D_MODEL:o + 2 * D_MODEL]))
    return q, k, v, u, sga, sgs


def _inproj_prompt_kernel(x_ref, g_ref, w_ref, cos_ref, sin_ref, *rest):
    n_cast = (len(rest) - 9) // 2
    cast_in = rest[:n_cast]
    qT_ref, kT_ref, kbf_ref, kmean_ref, vT_ref, vTb_ref, u_ref, sga_ref, sgs_ref = rest[n_cast:n_cast + 9]
    cast_out = rest[n_cast + 9:]
    for wi_ref, wo_ref in zip(cast_in, cast_out):
        wo_ref[...] = wi_ref[...].astype(BF16)
    reps = ATTN_WIDTH // LANES
    cos = jnp.tile(cos_ref[...], (1, reps))
    sin = jnp.tile(sin_ref[...], (1, reps))
    q, k, v, u, sga, sgs = _inproj_core(x_ref[...], g_ref[...], w_ref, cos, sin, LOG2E * HEAD_DIM ** -0.5)
    tm = q.shape[0]
    kT_ref[0] = k.T
    vT_ref[0] = v.T
    kbf_ref[...] = k.astype(BF16)
    for s in range(tm // MOBA_BLOCK):
        rows = slice(s * MOBA_BLOCK, (s + 1) * MOBA_BLOCK)
        qT_ref[0, s] = q[rows].T.astype(BF16)
        vTb_ref[0, s] = v[rows].T.astype(BF16)
        kmean_ref[0, s] = jnp.mean(k[rows], axis=0, keepdims=True)
    u_ref[...] = u
    sga_ref[...] = sga.astype(BF16)
    sgs_ref[...] = sgs.astype(BF16)


def _inproj_prompt(x2d, g, w_bf, cos, sin, cast_ws, bsz, seq, tm):
    m = bsz * seq
    nb = seq // MOBA_BLOCK
    tpb = seq // tm
    sub = tm // MOBA_BLOCK
    a = ATTN_WIDTH
    full = lambda shape: pl.BlockSpec(shape, lambda b, t: (0,) * len(shape))
    tok = lambda w: pl.BlockSpec((tm, w), lambda b, t: (b * tpb + t, 0))
    out_shape = (
        jax.ShapeDtypeStruct((bsz, nb, a, MOBA_BLOCK), BF16),
        jax.ShapeDtypeStruct((bsz, a, seq), F32),
        jax.ShapeDtypeStruct((m, a), BF16),
        jax.ShapeDtypeStruct((bsz, nb, 1, a), F32),
        jax.ShapeDtypeStruct((bsz, a, seq), F32),
        jax.ShapeDtypeStruct((bsz, nb, a, MOBA_BLOCK), BF16),
        jax.ShapeDtypeStruct((m, SSM_WIDTH), F32),
        jax.ShapeDtypeStruct((m, D_MODEL), BF16),
        jax.ShapeDtypeStruct((m, D_MODEL), BF16),
    )
    blk_t = pl.BlockSpec((1, sub, a, MOBA_BLOCK), lambda b, t: (b, t, 0, 0))
    lane_t = pl.BlockSpec((1, a, tm), lambda b, t: (b, 0, t))
    out_specs = (blk_t, lane_t, tok(a),
                 pl.BlockSpec((1, sub, 1, a), lambda b, t: (b, t, 0, 0)),
                 lane_t, blk_t, tok(SSM_WIDTH), tok(D_MODEL), tok(D_MODEL))
    n_steps = bsz * tpb
    cast_specs = []
    for w in cast_ws:
        rows = w.shape[0]
        nblk = n_steps
        while rows % nblk or (rows // nblk) % BF16_SUBLANES:
            nblk //= 2
        cast_specs.append(pl.BlockSpec((rows // nblk, w.shape[1]),
                                       lambda b, t, n=nblk: (jnp.minimum(b * tpb + t, n - 1), 0)))
    return pl.pallas_call(
        _inproj_prompt_kernel,
        grid=(bsz, tpb),
        in_specs=[tok(D_MODEL), full((1, D_MODEL)), full(w_bf.shape),
                  pl.BlockSpec((tm, LANES), lambda b, t: (t, 0)),
                  pl.BlockSpec((tm, LANES), lambda b, t: (t, 0))] + cast_specs,
        out_specs=out_specs + tuple(cast_specs),
        out_shape=out_shape + tuple(jax.ShapeDtypeStruct(w.shape, BF16) for w in cast_ws),
        compiler_params=_cparams(("arbitrary", "arbitrary")),
        name="inproj_prompt",
    )(x2d, g, w_bf, cos, sin, *cast_ws)


def _inproj_sample_kernel(x_ref, g_ref, w_ref, cos_ref, sin_ref,
                          q_ref, k_ref, v_ref, u_ref, sga_ref, sgs_ref, wb_ref):
    step = ATTN_WIDTH
    for c in range(w_ref.shape[1] // step):
        wb_ref[:, c * step:(c + 1) * step] = w_ref[:, c * step:(c + 1) * step].astype(BF16)
    reps = ATTN_WIDTH // LANES
    cos = jnp.tile(cos_ref[...], (1, reps))
    sin = jnp.tile(sin_ref[...], (1, reps))
    q, k, v, u, sga, sgs = _inproj_core(x_ref[...], g_ref[...], wb_ref, cos, sin, HEAD_DIM ** -0.5)
    q_ref[...] = q
    k_ref[...] = k
    v_ref[...] = v
    u_ref[...] = u
    sga_ref[...] = sga.astype(BF16)
    sgs_ref[...] = sgs.astype(BF16)


def _inproj_sample(x2d, g, w_f32, cos, sin):
    m = x2d.shape[0]
    a = ATTN_WIDTH
    full = lambda shape, **kw: pl.BlockSpec(shape, lambda i: (0,) * len(shape), **kw)
    shapes = [(m, a), (m, a), (m, a), (m, SSM_WIDTH), (m, D_MODEL), (m, D_MODEL), w_f32.shape]
    dts = [F32, F32, F32, F32, BF16, BF16, BF16]
    return pl.pallas_call(
        _inproj_sample_kernel,
        grid=(1,),
        in_specs=[full(x2d.shape), full(g.shape), full(w_f32.shape, pipeline_mode=pl.Buffered(1)),
                  full(cos.shape), full(sin.shape)],
        out_specs=tuple(full(s) for s in shapes),
        out_shape=tuple(jax.ShapeDtypeStruct(s, d) for s, d in zip(shapes, dts)),
        compiler_params=_cparams(("arbitrary",)),
        name="inproj_sample",
    )(x2d, g, w_f32, cos, sin)


def _moba_prompt_kernel(qT_ref, k_ref, kmean_ref, vT_ref, o_ref, bias_ref, qz_s, m_s, acc_s):
    nb = qT_ref.shape[1]
    blk = MOBA_BLOCK
    nh = MOBA_HEADS_PER_GROUP
    gw = nh * HEAD_DIM
    ng = qT_ref.shape[2] // gw
    heads = [(g, hh) for g in range(ng) for hh in range(nh)]
    gcols = [slice(g * gw, (g + 1) * gw) for g in range(ng)]
    row2 = lax.broadcasted_iota(jnp.int32, (gw, 1), 0)
    lane_km = lax.broadcasted_iota(jnp.int32, (1, gw), 1)
    blk_row = lax.broadcasted_iota(jnp.int32, (nb, blk), 0)
    key_i = lax.broadcasted_iota(jnp.int32, (blk, blk), 0)
    qry_i = lax.broadcasted_iota(jnp.int32, (blk, blk), 1)
    causal = key_i <= qry_i
    in_head = [(row2 >= hh * HEAD_DIM) & (row2 < (hh + 1) * HEAD_DIM) for hh in range(nh)]
    hrows = [slice(h * HEAD_DIM, (h + 1) * HEAD_DIM) for h in range(ng * nh)]

    for h in range(ng * nh):
        bias_ref[h, 0] = jnp.full((nb, blk), NEG_BIG, F32)
    for g in range(ng):
        km_g = kmean_ref[0][:, gcols[g]]
        km_all = jnp.concatenate(
            [jnp.where((lane_km >= hh * HEAD_DIM) & (lane_km < (hh + 1) * HEAD_DIM), km_g, 0.0)
             for hh in range(nh)], axis=0)
        km_terms = []
        rest = km_all
        for _ in range(3):
            term = rest.astype(BF16)
            km_terms.append(term)
            rest = rest - term.astype(F32)
        km_split = jnp.concatenate(km_terms, axis=0)
        for i in range(1, nb):
            sb3 = _dot(km_split, qT_ref[0, i, gcols[g], :])
            sb_all = (sb3[0:nh * nb] + sb3[nh * nb:2 * nh * nb]) + sb3[2 * nh * nb:]
            for hh in range(nh):
                sb = jnp.where(blk_row < i, sb_all[hh * nb:(hh + 1) * nb], -jnp.inf)
                bias = jnp.full((nb, blk), NEG_BIG, F32)
                for _r in range(min(MOBA_TOPK, i)):
                    mx = jnp.max(sb, axis=0, keepdims=True)
                    first = jnp.min(jnp.where(sb == mx, blk_row, nb), axis=0, keepdims=True)
                    pick = blk_row == first
                    bias = jnp.where(pick, 0.0, bias)
                    sb = jnp.where(pick, -jnp.inf, sb)
                bias_ref[g * nh + hh, i] = bias

    n_items = nb // 2
    ones_rows = jnp.ones((MOBA_SUM_ROWS, 2 * blk), BF16)

    def couple(i, _):
        iq = (i, nb - 1 - i)
        n_first = (i + 1) // 2

        def diag_scores(x):
            rows = pl.ds(pl.multiple_of(iq[x] * blk, blk), blk)
            out = []
            for h, (g, hh) in enumerate(heads):
                q_grp = qT_ref[0, iq[x], gcols[g], :]
                qz = jnp.where(in_head[hh], q_grp, jnp.zeros_like(q_grp))
                qz_s[x, h] = qz
                out.append(_dot(k_ref[rows, gcols[g]], qz))
            return out

        def diag_absorb(x, s_own):
            for h in range(len(heads)):
                s = jnp.where(causal, s_own[h], NEG_BIG)
                m0 = jnp.max(s, axis=0, keepdims=True)
                p = jnp.exp2(s - m0)
                m_s[x, h] = m0
                vv = jnp.concatenate([vT_ref[0, iq[x], hrows[h], :], ones_rows[:, :blk]], axis=0)
                acc_s[x, h] = _dot(vv, p.astype(BF16))

        def item_params(k):
            first = k < n_first
            x = jnp.where(first, 0, 1)
            return x, jnp.where(first, iq[0], iq[1]), jnp.where(first, k, k - n_first)

        def item_scores(k):
            x, _, jp = item_params(k)
            rows = pl.ds(pl.multiple_of(2 * jp * blk, blk), 2 * blk)
            out = []
            for h, (g, _hh) in enumerate(heads):
                s = _dot(k_ref[rows, gcols[g]], qz_s[x, h])
                out.append((s, jnp.max(s[:blk], axis=0, keepdims=True), jnp.max(s[blk:], axis=0, keepdims=True)))
            return out

        def item_absorb(k, sc):
            x, qb, jp = item_params(k)
            j0 = 2 * jp
            for h in range(len(heads)):
                s, cma, cmb = sc[h]
                ba = bias_ref[h, qb, pl.ds(j0, 1), :]
                bb = bias_ref[h, qb, pl.ds(j0 + 1, 1), :]
                m = m_s[x, h]
                m_new = jnp.maximum(m, jnp.maximum(cma + ba, cmb + bb))
                alpha = jnp.exp2(m - m_new)
                pa = jnp.exp2(s[:blk] - (m_new - ba))
                pb = jnp.exp2(s[blk:] - (m_new - bb))
                m_s[x, h] = m_new
                pp = jnp.concatenate([pa, pb], axis=0).astype(BF16)
                vv = jnp.concatenate([vT_ref[0, j0, hrows[h], :], vT_ref[0, j0 + 1, hrows[h], :]], axis=1)
                vv = jnp.concatenate([vv, ones_rows], axis=0)
                acc_s[x, h] = alpha * acc_s[x, h] + _dot(vv, pp)

        s_diag = [diag_scores(0), diag_scores(1)]
        pending = [item_scores(k) for k in range(min(MOBA_LOOKAHEAD, n_items))]
        diag_absorb(0, s_diag[0])
        diag_absorb(1, s_diag[1])
        for k in range(n_items):
            if k + MOBA_LOOKAHEAD < n_items:
                pending.append(item_scores(k + MOBA_LOOKAHEAD))
            item_absorb(k, pending[k])
        for x in range(2):
            oT = jnp.concatenate([acc_s[x, h, 0:HEAD_DIM, :] / acc_s[x, h, HEAD_DIM:HEAD_DIM + 1, :]
                                  for h in range(len(heads))], axis=0)
            o_ref[pl.ds(pl.multiple_of(iq[x] * blk, blk), blk), :] = oT.T.astype(o_ref.dtype)
        return 0

    lax.fori_loop(0, nb // 2, couple, 0)


def _moba_prompt(qT, kbf, kmean, vTb, bsz, seq):
    nb = seq // MOBA_BLOCK
    hps = MOBA_HEADS_PER_GROUP * MOBA_GROUPS_PER_STEP
    hp = N_HEADS // hps
    pair = hps * HEAD_DIM
    gw = MOBA_HEADS_PER_GROUP * HEAD_DIM
    return pl.pallas_call(
        _moba_prompt_kernel,
        grid=(bsz, hp),
        in_specs=[pl.BlockSpec((1, nb, pair, MOBA_BLOCK), lambda b, h: (b, 0, h, 0)),
                  pl.BlockSpec((seq, pair), lambda b, h: (b, h)),
                  pl.BlockSpec((1, nb, pair), lambda b, h: (b, 0, h)),
                  pl.BlockSpec((1, nb, pair, MOBA_BLOCK), lambda b, h: (b, 0, h, 0))],
        out_specs=pl.BlockSpec((seq, pair), lambda b, h: (b, h)),
        out_shape=jax.ShapeDtypeStruct((bsz * seq, ATTN_WIDTH), BF16),
        scratch_shapes=[pltpu.VMEM((hps, nb, nb, MOBA_BLOCK), F32),
                        pltpu.VMEM((2, hps, gw, MOBA_BLOCK), BF16),
                        pltpu.VMEM((2, hps, 1, MOBA_BLOCK), F32),
                        pltpu.VMEM((2, hps, HEAD_DIM + MOBA_SUM_ROWS, MOBA_BLOCK), F32)],
        compiler_params=_cparams(("parallel", "parallel")),
        name="moba_prompt",
    )(qT, kbf, kmean, vTb)


def _moba_paged_kernel(pt_ref, qcol_ref, kn_ref, vn_ref, ck_ref, cv_ref, o_ref,
                       kbuf, vbuf, s_ref, psel_ref, stat_ref, qb_ref, ksem, vsem):
    b = pl.program_id(0)
    nseq = pl.num_programs(0) - 1
    n_pages = kbuf.shape[1]
    nblk = n_pages // 2
    slot = b % 2

    def k_copy(seq_i, sl, p):
        return pltpu.make_async_copy(ck_ref.at[pt_ref[seq_i, p]], kbuf.at[sl, p], ksem.at[sl])

    def start_k(seq_i, sl):
        def body(p, _):
            k_copy(seq_i, sl, p).start()
            return 0
        lax.fori_loop(0, n_pages, body, 0)

    def v_copy(page, h, r, par):
        return pltpu.make_async_copy(cv_ref.at[page, h], vbuf.at[h, r, par], vsem.at[0])

    @pl.when(b == 0)
    def _():
        start_k(0, 0)

    @pl.when(b < nseq)
    def _():
        def wait_body(p, _):
            k_copy(b, slot, p).wait()
            return 0
        lax.fori_loop(0, n_pages, wait_body, 0)

    @pl.when(b + 1 < nseq)
    def _():
        start_k(b + 1, 1 - slot)

    @pl.when(b >= 1)
    def _():
        for h in range(N_HEADS):
            for r in range(MOBA_TOPK):
                for par in range(2):
                    v_copy(0, h, r, par).wait()
        vn = vn_ref[0]
        for h in range(N_HEADS):
            acc = jnp.zeros((HEAD_DIM, PAGE_SIZE), F32)
            for r in range(MOBA_TOPK):
                for par in range(2):
                    acc = acc + vbuf[h, r, par] * psel_ref[h, 2 * r + par]
            p_own = stat_ref[0, h][:, 0:1]
            l = stat_ref[1, h][:, 0:1]
            o_h = jnp.sum(acc, axis=1, keepdims=True) + p_own * vn[:, h:h + 1]
            o_ref[0, :, h:h + 1] = o_h / l

    @pl.when(b < nseq)
    def _():
        qcol = qcol_ref[0]
        own = jnp.sum(qcol * kn_ref[0], axis=0, keepdims=True)
        blk_i = lax.broadcasted_iota(jnp.int32, (nblk, 1), 0)
        for h in range(N_HEADS):
            qb_ref[h] = jnp.broadcast_to(qcol[:, h:h + 1], (HEAD_DIM, PAGE_SIZE))
        for h in range(N_HEADS):
            def score_body(g, _):
                qb = qb_ref[h]
                for dn in range(PAGED_SCORE_UNROLL):
                    n = g * PAGED_SCORE_UNROLL + dn
                    for par in range(2):
                        kt = kbuf[slot, 2 * n + par, h]
                        s_ref[par, h, pl.ds(n, 1), :] = jnp.sum(kt * qb, axis=0, keepdims=True)
                return 0
            lax.fori_loop(0, nblk // PAGED_SCORE_UNROLL, score_body, 0)

        for h in range(N_HEADS):
            s0 = s_ref[0, h]
            s1 = s_ref[1, h]
            bs = jnp.sum(s0 + s1, axis=1, keepdims=True)
            sel = jnp.zeros((nblk, 1), dtype=jnp.bool_)
            picks = []
            for r in range(MOBA_TOPK):
                mx = jnp.max(bs, axis=0, keepdims=True)
                first = jnp.min(jnp.where(bs == mx, blk_i, nblk), axis=0, keepdims=True)
                pick = blk_i == first
                sel = sel | pick
                bs = jnp.where(pick, -jnp.inf, bs)
                blk_id = jnp.max(first)
                for par in range(2):
                    v_copy(pt_ref[b, 2 * blk_id + par], h, r, par).start()
                picks.append(blk_id)
            s_own = own[:, h:h + 1]
            sm0 = jnp.where(sel, s0, NEG_BIG)
            sm1 = jnp.where(sel, s1, NEG_BIG)
            mx = jnp.maximum(jnp.max(jnp.max(jnp.maximum(sm0, sm1), axis=1, keepdims=True),
                                     axis=0, keepdims=True), s_own)
            p0 = jnp.exp(sm0 - mx)
            p1 = jnp.exp(sm1 - mx)
            p_own = jnp.exp(s_own - mx)
            l = jnp.sum(jnp.sum(p0 + p1, axis=1, keepdims=True), axis=0, keepdims=True) + p_own
            s_ref[0, h] = p0
            s_ref[1, h] = p1
            for r in range(MOBA_TOPK):
                for par in range(2):
                    psel_ref[h, 2 * r + par] = s_ref[par, h, pl.ds(picks[r], 1), :]
            stat_ref[0, h] = jnp.broadcast_to(p_own, (1, PAGE_SIZE))
            stat_ref[1, h] = jnp.broadcast_to(l, (1, PAGE_SIZE))


def _moba_paged(page_table, qcol, kncol, vncol, cache_kT, cache_vT):
    nseq, n_pages = page_table.shape
    assert n_pages // 2 >= MOBA_TOPK, "every pick must find an unpicked cached block"
    cur = pl.BlockSpec((1, HEAD_DIM, N_HEADS), lambda b, pt: (jnp.minimum(b, nseq - 1), 0, 0))
    prev = pl.BlockSpec((1, HEAD_DIM, N_HEADS), lambda b, pt: (jnp.maximum(b - 1, 0), 0, 0))
    any_spec = pl.BlockSpec(memory_space=pl.ANY)
    grid_spec = pltpu.PrefetchScalarGridSpec(
        num_scalar_prefetch=1,
        grid=(nseq + 1,),
        in_specs=[cur, cur, prev, any_spec, any_spec],
        out_specs=prev,
        scratch_shapes=[
            pltpu.VMEM((2, n_pages, N_HEADS, HEAD_DIM, PAGE_SIZE), F32),
            pltpu.VMEM((N_HEADS, MOBA_TOPK, 2, HEAD_DIM, PAGE_SIZE), F32),
            pltpu.VMEM((2, N_HEADS, n_pages // 2, PAGE_SIZE), F32),
            pltpu.VMEM((N_HEADS, 2 * MOBA_TOPK, 1, PAGE_SIZE), F32),
            pltpu.VMEM((2, N_HEADS, 1, PAGE_SIZE), F32),
            pltpu.VMEM((N_HEADS, HEAD_DIM, PAGE_SIZE), F32),
            pltpu.SemaphoreType.DMA((2,)),
            pltpu.SemaphoreType.DMA((1,)),
        ])
    return pl.pallas_call(
        _moba_paged_kernel,
        grid_spec=grid_spec,
        out_shape=jax.ShapeDtypeStruct((nseq, HEAD_DIM, N_HEADS), F32),
        compiler_params=_cparams(("arbitrary",)),
        name="moba_paged",
    )(page_table, qcol, kncol, vncol, cache_kT, cache_vT)


def _ssm_tables(a_re, a_im, log_dt, b_re, b_im, c_re, c_im, chunk, n_pow):
    t_ = chunk
    lam = lax.complex(a_re.astype(F32), a_im.astype(F32))
    ldt = lam * jnp.exp(log_dt.astype(F32))[:, None]
    a_bar = jnp.exp(ldt)
    b_bar = ((a_bar - 1.0) / lam)[..., None] * lax.complex(b_re.astype(F32), b_im.astype(F32))
    c_c = lax.complex(c_re.astype(F32), c_im.astype(F32))
    taus = jnp.arange(t_ + 1, dtype=F32).astype(jnp.complex64)
    apow = jnp.exp(ldt[None] * taus[:, None, None])
    gq, ssg, c_, p_ = GROUPS_PER_SG, N_SG, SSM_GROUP, SSM_STATE
    c4 = c_c.reshape(ssg, gq, c_, p_)
    b4 = b_bar.reshape(ssg, gq, p_, c_)
    ap4 = apow.reshape(t_ + 1, ssg, gq, p_)

    kc = jnp.einsum('sjcp,tsjp,sjpd->sjdtc', c4, ap4[:t_], b4).real
    kc = kc.reshape(ssg, LANES, t_ * c_)
    mst = jnp.einsum('tsjp,sjpd->stjdp', ap4[:t_][::-1], b4)
    mc = jnp.concatenate([mst.real, mst.imag], axis=-1).reshape(ssg, t_ * LANES, 2 * p_)
    nout = jnp.einsum('sjcp,tsjp->sjptc', c4, ap4[1:])
    nc = jnp.concatenate([nout.real, -nout.imag], axis=1).reshape(ssg, 2 * SG_STATE, t_ * c_)
    rs = jnp.arange(1, n_pow + 1, dtype=F32).astype(jnp.complex64)
    ap = jnp.exp((ldt * t_)[None] * rs[:, None, None])
    ap = ap.reshape(n_pow, ssg, SG_STATE).transpose(1, 0, 2)
    a_pow = jnp.concatenate([ap.real, ap.imag], axis=-1)
    a1 = a_bar.reshape(ssg, 1, SG_STATE)
    a_one = jnp.concatenate([a1.real, a1.imag], axis=-1)
    return kc, mc, nc, a_pow, a_one


def _expand_ssm_weights(kc_ref, mc_ref, nc_ref, wt_s, ws_s, wo_s):
    t_ = SSM_CHUNK
    kw = t_ * LANES
    lg = lambda n: n.bit_length() - 1
    r = lax.broadcasted_iota(jnp.int32, (LANES, kw), 0)
    q = lax.broadcasted_iota(jnp.int32, (LANES, kw), 1)
    sh_c, sh_p = lg(SSM_GROUP), lg(SSM_STATE)
    ecol = jnp.where(((r >> sh_c) == (q >> lg(LANES))) & ((r & (SSM_GROUP - 1)) == (q & (SSM_GROUP - 1))),
                     1.0, 0.0).astype(BF16)
    e2 = jnp.where(((r >> sh_p) == (q >> lg(SG_STATE))) & ((r & (SSM_STATE - 1)) == (q & (SSM_STATE - 1))),
                   1.0, 0.0).astype(BF16)
    col_c = (q & (LANES - 1)) >> sh_c
    taps = jnp.where((r >> sh_c) == col_c, _dot(kc_ref[0].astype(BF16), ecol), 0.0).astype(BF16)
    blank = jnp.zeros((LANES, LANES), BF16)
    for s in range(t_):
        for t in range(t_):
            wt_s[s * LANES:(s + 1) * LANES, t * LANES:(t + 1) * LANES] = (
                taps[:, (t - s) * LANES:(t - s + 1) * LANES] if t >= s else blank)
    rr = lax.broadcasted_iota(jnp.int32, (kw, kw), 0)
    qq = lax.broadcasted_iota(jnp.int32, (kw, kw), 1)
    row_c = (rr & (LANES - 1)) >> sh_c
    row_p = (rr & (SG_STATE - 1)) >> sh_p
    colq_c = (qq & (LANES - 1)) >> sh_c
    colq_p = (qq & (SG_STATE - 1)) >> sh_p
    ws_s[...] = jnp.where(row_c == colq_p, _dot(mc_ref[0].astype(BF16), e2), 0.0).astype(BF16)
    wo_s[...] = jnp.where(row_p == colq_c, _dot(nc_ref[0].astype(BF16), ecol), 0.0).astype(BF16)


def _ssm_prompt_kernel(u_ref, kc_ref, mc_ref, nc_ref, ap_ref, d_ref, h0_ref,
                       y_ref, hf_ref, uc_ref, x_ref, hin_ref, wt_s, ws_s, wo_s):
    t_ = SSM_CHUNK
    nc = uc_ref.shape[0]
    ns = SG_STATE

    @pl.when(pl.program_id(1) == 0)
    def _():
        _expand_ssm_weights(kc_ref, mc_ref, nc_ref, wt_s, ws_s, wo_s)

    for t in range(t_):
        uc_ref[:, t * LANES:(t + 1) * LANES] = u_ref[pl.ds(t, nc, stride=t_), :].astype(BF16)
    uc = uc_ref[...]
    x_ref[...] = _dot(uc, ws_s[...])

    ap = ap_ref[0]
    apr, api = ap[:, :ns], ap[:, ns:]
    row = lax.broadcasted_iota(jnp.int32, (SUBLANES, 1), 0)

    def tile_body(k, carry):
        hr, hi = carry
        rows = pl.ds(k * SUBLANES, SUBLANES)
        xr = x_ref[rows, 0:ns]
        xi = x_ref[rows, ns:2 * ns]
        for d in (1, 2, 4):
            ar, ai = apr[d - 1:d], api[d - 1:d]
            sr = jnp.where(row >= d, pltpu.roll(xr, d, 0), 0.0)
            si = jnp.where(row >= d, pltpu.roll(xi, d, 0), 0.0)
            xr, xi = xr + ar * sr - ai * si, xi + ar * si + ai * sr
        outr = xr + apr * hr - api * hi
        outi = xi + apr * hi + api * hr
        hin_ref[rows, 0:ns] = jnp.where(row >= 1, pltpu.roll(outr, 1, 0), hr)
        hin_ref[rows, ns:2 * ns] = jnp.where(row >= 1, pltpu.roll(outi, 1, 0), hi)
        return outr[SUBLANES - 1:SUBLANES], outi[SUBLANES - 1:SUBLANES]

    cols = []
    for nt in range(t_ * LANES // MXU_COLS):
        acc = None
        for kt in range(nt + 1):
            part = _dot(uc[:, kt * MXU_COLS:(kt + 1) * MXU_COLS],
                        wt_s[kt * MXU_COLS:(kt + 1) * MXU_COLS, nt * MXU_COLS:(nt + 1) * MXU_COLS])
            acc = part if acc is None else acc + part
        cols.append(acc)
    y_intra = jnp.concatenate(cols, axis=1)
    h0 = h0_ref[0, 0]
    carry = (h0[:, :ns], h0[:, ns:])
    for k in range(nc // SUBLANES):
        carry = tile_body(k, carry)
    hf_ref[0, 0] = jnp.concatenate(carry, axis=1)

    y = y_intra + _dot(hin_ref[...].astype(BF16), wo_s[...])
    dv = d_ref[...]
    for t in range(t_):
        rows = pl.ds(t, nc, stride=t_)
        y_ref[rows, :] = y[:, t * LANES:(t + 1) * LANES] + dv * u_ref[rows, :]


def _ssm_prompt(u, kc, mc, nc_tab, a_pow, d_row, h0, bsz, seq):
    t_ = SSM_CHUNK
    nc = seq // t_
    kw = t_ * LANES
    assert kc.shape[2] == LANES and 2 * SG_STATE == kw
    wspec = lambda shape: pl.BlockSpec((1,) + shape, lambda s, b: (s, 0, 0))
    return pl.pallas_call(
        _ssm_prompt_kernel,
        grid=(N_SG, bsz),
        in_specs=[pl.BlockSpec((seq, LANES), lambda s, b: (b, s)),
                  wspec(kc.shape[1:]), wspec(mc.shape[1:]), wspec(nc_tab.shape[1:]),
                  wspec((SUBLANES, 2 * SG_STATE)),
                  pl.BlockSpec((1, LANES), lambda s, b: (0, s)),
                  pl.BlockSpec((1, 1, 1, 2 * SG_STATE), lambda s, b: (b, s, 0, 0))],
        out_specs=(pl.BlockSpec((seq, LANES), lambda s, b: (b, s)),
                   pl.BlockSpec((1, 1, 1, 2 * SG_STATE), lambda s, b: (b, s, 0, 0))),
        out_shape=(jax.ShapeDtypeStruct((bsz * seq, SSM_WIDTH), F32),
                   jax.ShapeDtypeStruct((bsz, N_SG, 1, 2 * SG_STATE), F32)),
        scratch_shapes=[pltpu.VMEM((nc, kw), BF16),
                        pltpu.VMEM((nc, 2 * SG_STATE), F32),
                        pltpu.VMEM((nc, 2 * SG_STATE), F32),
                        pltpu.VMEM((kw, kw), BF16),
                        pltpu.VMEM((kw, 2 * SG_STATE), BF16),
                        pltpu.VMEM((2 * SG_STATE, kw), BF16)],
        compiler_params=_cparams(("arbitrary", "arbitrary")),
        name="ssm_prompt",
    )(u, kc, mc, nc_tab, a_pow, d_row, h0)


def _ssm_step_kernel(u_ref, kc_ref, mc_ref, nc_ref, a1_ref, d_ref, h0_ref, y_ref, hf_ref):
    ns = SG_STATE
    kw = 2 * ns
    lg = lambda n: n.bit_length() - 1
    sh_c, sh_p = lg(SSM_GROUP), lg(SSM_STATE)
    iota = lambda shape, d: lax.broadcasted_iota(jnp.int32, shape, d)
    r1, q1 = iota((LANES, LANES), 0), iota((LANES, LANES), 1)
    e_c = jnp.where((r1 < SSM_GROUP) & ((r1 & (SSM_GROUP - 1)) == (q1 & (SSM_GROUP - 1))), 1.0, 0.0).astype(BF16)
    r2, q2 = iota((LANES, kw), 0), iota((LANES, kw), 1)
    e_p = jnp.where(((r2 >> sh_p) == (q2 >> lg(ns))) & ((r2 & (SSM_STATE - 1)) == (q2 & (SSM_STATE - 1))),
                    1.0, 0.0).astype(BF16)
    same_t = (r1 >> sh_c) == (q1 >> sh_c)
    same_s = (r2 >> sh_c) == ((q2 & (ns - 1)) >> sh_p)
    r3, q3 = iota((kw, LANES), 0), iota((kw, LANES), 1)
    same_o = ((r3 & (ns - 1)) >> sh_p) == (q3 >> sh_c)
    last = (SSM_CHUNK - 1) * LANES
    for s in range(N_SG):
        wt = jnp.where(same_t, _dot(kc_ref[s].astype(BF16), e_c), 0.0).astype(BF16)
        ws = jnp.where(same_s, _dot(mc_ref[s, last:last + LANES, :].astype(BF16), e_p), 0.0).astype(BF16)
        wo = jnp.where(same_o, _dot(nc_ref[s].astype(BF16), e_c), 0.0).astype(BF16)
        us = u_ref[:, s * LANES:(s + 1) * LANES]
        ub = us.astype(BF16)
        h0 = h0_ref[s]
        x = _dot(ub, ws)
        ar, ai = a1_ref[s, :, :ns], a1_ref[s, :, ns:]
        hr, hi = h0[:, :ns], h0[:, ns:]
        hf_ref[s] = jnp.concatenate([x[:, :ns] + ar * hr - ai * hi,
                                     x[:, ns:] + ar * hi + ai * hr], axis=1)
        y = _dot(ub, wt) + _dot(h0.astype(BF16), wo)
        y_ref[:, s * LANES:(s + 1) * LANES] = y + d_ref[:, s * LANES:(s + 1) * LANES] * us


def _ssm_step(u, kc, mc, nc_tab, a_one, d_row, h0):
    m = u.shape[0]
    full = lambda a: pl.BlockSpec(a.shape, lambda i: (0,) * a.ndim)
    args = (u, kc, mc, nc_tab, a_one, d_row, h0)
    shapes = [(m, SSM_WIDTH), (N_SG, m, 2 * SG_STATE)]
    return pl.pallas_call(
        _ssm_step_kernel,
        grid=(1,),
        in_specs=[full(a) for a in args],
        out_specs=tuple(pl.BlockSpec(s, lambda i, n=len(s): (0,) * n) for s in shapes),
        out_shape=tuple(jax.ShapeDtypeStruct(s, F32) for s in shapes),
        compiler_params=_cparams(("arbitrary",)),
        name="ssm_step",
    )(*args)


FFN_SPLIT = FFN_HIDDEN // MXU_COLS
FFN_CHUNK = FFN_HIDDEN // FFN_SPLIT


def _post_kernel(x_ref, o_ref, y_ref, sga_ref, sgs_ref, wap_ref, wglu_ref, bglu_ref, wsp_ref,
                 wout_ref, nffn_ref, wfi_ref, wfo_ref, nfin_ref, out_ref):
    z = jax.nn.gelu(y_ref[...])
    z = z * jax.nn.sigmoid(_dot(z.astype(BF16), wglu_ref[...]) + bglu_ref[...])
    ssm_out = _dot(z.astype(BF16), wsp_ref[...])
    attn_out = _dot(o_ref[...].astype(BF16), wap_ref[...])
    merged = sga_ref[...].astype(F32) * attn_out + sgs_ref[...].astype(F32) * ssm_out
    x1 = x_ref[...] + _dot(merged.astype(BF16), wout_ref[...])
    hf = _rms(x1, nffn_ref[...]).astype(BF16)
    acc = x1
    for c in range(FFN_SPLIT):
        lo = c * FFN_CHUNK
        a = _dot(hf, wfi_ref[:, lo:lo + FFN_CHUNK])
        g = _dot(hf, wfi_ref[:, FFN_HIDDEN + lo:FFN_HIDDEN + lo + FFN_CHUNK])
        act = (jax.nn.silu(a) * g).astype(BF16)
        acc = acc + _dot(act, wfo_ref[lo:lo + FFN_CHUNK, :])
    out_ref[...] = _rms(acc, nfin_ref[...])


def _post(x2d, o, y, sga, sgs, wap, wglu, bglu, wsp, wout, nffn, wfi, wfo, nfin, tm):
    m = x2d.shape[0]
    tok = lambda w: pl.BlockSpec((tm, w), lambda i: (i, 0))
    const = lambda a: pl.BlockSpec(a.shape, lambda i: (0,) * a.ndim, pipeline_mode=pl.Buffered(1))
    weights = (wap, wglu, bglu, wsp, wout, nffn, wfi, wfo, nfin)
    return pl.pallas_call(
        _post_kernel,
        grid=(m // tm,),
        in_specs=[tok(D_MODEL), tok(ATTN_WIDTH), tok(SSM_WIDTH), tok(D_MODEL), tok(D_MODEL)]
                 + [const(w) for w in weights],
        out_specs=tok(D_MODEL),
        out_shape=jax.ShapeDtypeStruct((m, D_MODEL), F32),
        compiler_params=_cparams(("parallel",)),
        name="post",
    )(x2d, o, y, sga, sgs, *weights)


def _rope_tables(pos):
    half = HEAD_DIM // 2
    inv = jnp.power(jnp.float32(ROPE_THETA), -2.0 * jnp.arange(half, dtype=F32) / HEAD_DIM)
    ang = pos.astype(F32)[:, None] * inv[None, :]
    cos, sin = jnp.cos(ang), jnp.sin(ang)
    reps = LANES // HEAD_DIM
    return (jnp.tile(jnp.concatenate([cos, cos], axis=1), (1, reps)),
            jnp.tile(jnp.concatenate([-sin, sin], axis=1), (1, reps)))


def _leaf_from_T(xT, bsz, seq):
    return xT.reshape(bsz, N_HEADS, HEAD_DIM, seq).transpose(0, 3, 1, 2)[None]


def _state_in(re, im):
    n = re.shape[0]
    h = jnp.concatenate([re.reshape(n, N_SG, SG_STATE), im.reshape(n, N_SG, SG_STATE)], axis=-1)
    return h.transpose(1, 0, 2)


def _state_out(h):
    n = h.shape[0]
    return (h[..., :SG_STATE].reshape(1, n, SSM_GROUPS, SSM_STATE),
            h[..., SG_STATE:].reshape(1, n, SSM_GROUPS, SSM_STATE))


def kernel(x_prompt, x_sample, cache_k, cache_v, state_ssm_re, state_ssm_im, page_table, norm_mix, w_in,
           w_attn_proj, ssm_a_re, ssm_a_im, ssm_log_dt, ssm_b_re, ssm_b_im, ssm_c_re, ssm_c_im, ssm_d, w_glu,
           b_glu, w_ssm_proj, w_out, norm_ffn, w_ffn_in, w_ffn_out, norm_final):
    assert w_in.shape[0] == 1, "single layer"
    bsz, seq = x_prompt.shape[:2]
    nseq = x_sample.shape[0]
    past_len = page_table.shape[1] * PAGE_SIZE
    assert seq % MOBA_BLOCK == 0 and past_len % MOBA_BLOCK == 0 and x_sample.shape[1] == 1

    ssm_p = (ssm_a_re[0], ssm_a_im[0], ssm_log_dt[0], ssm_b_re[0], ssm_b_im[0], ssm_c_re[0], ssm_c_im[0])

    cos_s, sin_s = _rope_tables(jnp.full((1,), past_len, dtype=jnp.int32))
    xs2 = x_sample.reshape(nseq, D_MODEL)
    q_s, k_s, v_s, u_s, sga_s, sgs_s, w_in_bf = _inproj_sample(xs2, norm_mix, w_in[0], cos_s, sin_s)

    cos_p, sin_p = _rope_tables(jnp.arange(seq, dtype=jnp.int32))
    xp2 = x_prompt.reshape(bsz * seq, D_MODEL)
    cast_ws = (w_attn_proj[0], w_glu[0], w_ssm_proj[0], w_out[0], w_ffn_in[0], w_ffn_out[0])
    (qT, kT, kbf, kmean, vT, vTb, u_p, sga_p, sgs_p,
     wap_bf, wglu_bf, wsp_bf, wout_bf, wfi_bf, wfo_bf) = _inproj_prompt(
        xp2, norm_mix, w_in_bf, cos_p, sin_p, cast_ws, bsz, seq, tm=512)
    post_w = (wap_bf, wglu_bf, b_glu, wsp_bf, wout_bf, norm_ffn, wfi_bf, wfo_bf, norm_final[None])
    o_p = _moba_prompt(qT, kbf, kmean.reshape(bsz, seq // MOBA_BLOCK, ATTN_WIDTH), vTb, bsz, seq)
    kc, mc, nc_tab, ap, a_one = _ssm_tables(*ssm_p, chunk=SSM_CHUNK, n_pow=SUBLANES)
    h0_p = jnp.zeros((bsz, N_SG, 1, 2 * SG_STATE), F32)
    y_p, hf_p = _ssm_prompt(u_p, kc, mc, nc_tab, ap, ssm_d, h0_p, bsz, seq)
    y_prompt = _post(xp2, o_p, y_p, sga_p, sgs_p, *post_w, tm=512).reshape(bsz, seq, D_MODEL)
    new_ssm_re_p, new_ssm_im_p = _state_out(hf_p.reshape(bsz, N_SG, 2 * SG_STATE))

    cache_kT = cache_k[0].transpose(0, 2, 3, 1)
    cache_vT = cache_v[0].transpose(0, 2, 3, 1)
    dh = lambda t: t.reshape(nseq, N_HEADS, HEAD_DIM).transpose(0, 2, 1)
    o_s = _moba_paged(page_table, dh(q_s), dh(k_s), dh(v_s), cache_kT, cache_vT).transpose(0, 2, 1)
    h0_s = _state_in(state_ssm_re[0], state_ssm_im[0])
    y_s, hf_s = _ssm_step(u_s, kc, mc, nc_tab, a_one, ssm_d, h0_s)
    y_sample = _post(xs2, o_s.reshape(nseq, ATTN_WIDTH), y_s, sga_s, sgs_s, *post_w, tm=nseq)
    new_ssm_re_s, new_ssm_im_s = _state_out(hf_s.transpose(1, 0, 2))

    return (y_prompt, y_sample.reshape(nseq, 1, D_MODEL),
            _leaf_from_T(kT, bsz, seq), _leaf_from_T(vT, bsz, seq), new_ssm_re_p, new_ssm_im_p,
            k_s.reshape(1, nseq, 1, N_HEADS, HEAD_DIM), v_s.reshape(1, nseq, 1, N_HEADS, HEAD_DIM),
            new_ssm_re_s, new_ssm_im_s)
```

```python
import functools
import math

import jax
import jax.numpy as jnp
from jax import lax
from jax.experimental import pallas as pl
from jax.experimental.pallas import tpu as pltpu

F32 = jnp.float32
BF16 = jnp.bfloat16

D_MODEL = 1024
N_HEADS = 8
HEAD_DIM = 64
ATTN_WIDTH = N_HEADS * HEAD_DIM
MOBA_BLOCK = 256
MOBA_TOPK = 3
ROPE_THETA = 10000.0
SSM_WIDTH = 512
SSM_GROUP = 16
SSM_GROUPS = 32
SSM_STATE = 64
FFN_HIDDEN = 2816
RMS_EPS = 1e-6
PAGE_SIZE = 128

LANES = 128
SUBLANES = 8
BF16_SUBLANES = 16
MXU_COLS = 256
GROUPS_PER_SG = LANES // SSM_GROUP
N_SG = SSM_GROUPS // GROUPS_PER_SG
SG_STATE = GROUPS_PER_SG * SSM_STATE
SSM_CHUNK = 8
NEG_BIG = -1e30
MOBA_GROUPS_PER_STEP = 1
MOBA_HEADS_PER_GROUP = 4
MOBA_LOOKAHEAD = 2
LOG2E = math.log2(math.e)
MOBA_EXP_ROWS = 64
MOBA_SUM_ROWS = 16
PAGED_SCORE_UNROLL = 4
VMEM_LIMIT = 56 * 1024 * 1024


def _cparams(sem):
    return pltpu.CompilerParams(dimension_semantics=sem, vmem_limit_bytes=VMEM_LIMIT)


def _dot(a, b):
    return jnp.dot(a, b, preferred_element_type=F32)


def _rms(x, g):
    return x * lax.rsqrt(jnp.mean(x * x, axis=-1, keepdims=True) + RMS_EPS) * g


def _inproj_core(x, g, w_ref, cos, sin, q_scale, emit):
    h = _rms(x, g).astype(BF16)
    lane = lax.broadcasted_iota(jnp.int32, (1, ATTN_WIDTH), 1)
    first_half = (lane % HEAD_DIM) < (HEAD_DIM // 2)

    def rot(t):
        partner = jnp.where(first_half,
                            pltpu.roll(t, ATTN_WIDTH - HEAD_DIM // 2, 1),
                            pltpu.roll(t, HEAD_DIM // 2, 1))
        return t * cos + partner * sin

    a = ATTN_WIDTH
    o = 3 * a + SSM_WIDTH
    emit["q"](rot(_dot(h, w_ref[:, 0:a])) * q_scale)
    emit["k"](rot(_dot(h, w_ref[:, a:2 * a])))
    emit["v"](_dot(h, w_ref[:, 2 * a:3 * a]))
    emit["sga"](jax.nn.sigmoid(_dot(h, w_ref[:, o:o + D_MODEL])))
    emit["sgs"](jax.nn.sigmoid(_dot(h, w_ref[:, o + D_MODEL:o + 2 * D_MODEL])))
    emit["u"](_dot(h, w_ref[:, 3 * a:3 * a + SSM_WIDTH]))


def _inproj_prompt_kernel(x_ref, g_ref, w_ref, cos_ref, sin_ref, *rest):
    n_cast = (len(rest) - 9) // 2
    cast_in = rest[:n_cast]
    qT_ref, kT_ref, kbf_ref, kmean_ref, vT_ref, vTb_ref, u_ref, sga_ref, sgs_ref = rest[n_cast:n_cast + 9]
    cast_out = rest[n_cast + 9:]
    for wi_ref, wo_ref in zip(cast_in, cast_out):
        wo_ref[...] = wi_ref[...].astype(BF16)
    reps = ATTN_WIDTH // LANES
    cos = jnp.tile(cos_ref[...], (1, reps))
    sin = jnp.tile(sin_ref[...], (1, reps))
    blocks = [slice(s * MOBA_BLOCK, (s + 1) * MOBA_BLOCK) for s in range(x_ref.shape[0] // MOBA_BLOCK)]

    def put_q(q):
        for s, rows in enumerate(blocks):
            qT_ref[0, s] = q[rows].T.astype(BF16)

    def put_k(k):
        kT_ref[0] = k.T
        kbf_ref[...] = k.astype(BF16)
        for s, rows in enumerate(blocks):
            kmean_ref[0, s] = jnp.mean(k[rows], axis=0, keepdims=True)

    def put_v(v):
        vT_ref[0] = v.T
        for s, rows in enumerate(blocks):
            vTb_ref[0, s] = v[rows].T.astype(BF16)

    def put(ref, dtype):
        def store(t):
            ref[...] = t.astype(dtype)
        return store

    _inproj_core(x_ref[...], g_ref[...], w_ref, cos, sin, LOG2E * HEAD_DIM ** -0.5,
                 dict(q=put_q, k=put_k, v=put_v, u=put(u_ref, F32), sga=put(sga_ref, BF16), sgs=put(sgs_ref, BF16)))


def _inproj_prompt(x2d, g, w_bf, cos, sin, cast_ws, bsz, seq, tm):
    m = bsz * seq
    nb = seq // MOBA_BLOCK
    tpb = seq // tm
    sub = tm // MOBA_BLOCK
    a = ATTN_WIDTH
    full = lambda shape: pl.BlockSpec(shape, lambda b, t: (0,) * len(shape))
    tok = lambda w: pl.BlockSpec((tm, w), lambda b, t: (b * tpb + t, 0))
    out_shape = (
        jax.ShapeDtypeStruct((bsz, nb, a, MOBA_BLOCK), BF16),
        jax.ShapeDtypeStruct((bsz, a, seq), F32),
        jax.ShapeDtypeStruct((m, a), BF16),
        jax.ShapeDtypeStruct((bsz, nb, 1, a), F32),
        jax.ShapeDtypeStruct((bsz, a, seq), F32),
        jax.ShapeDtypeStruct((bsz, nb, a, MOBA_BLOCK), BF16),
        jax.ShapeDtypeStruct((m, SSM_WIDTH), F32),
        jax.ShapeDtypeStruct((m, D_MODEL), BF16),
        jax.ShapeDtypeStruct((m, D_MODEL), BF16),
    )
    blk_t = pl.BlockSpec((1, sub, a, MOBA_BLOCK), lambda b, t: (b, t, 0, 0))
    lane_t = pl.BlockSpec((1, a, tm), lambda b, t: (b, 0, t))
    out_specs = (blk_t, lane_t, tok(a),
                 pl.BlockSpec((1, sub, 1, a), lambda b, t: (b, t, 0, 0)),
                 lane_t, blk_t, tok(SSM_WIDTH), tok(D_MODEL), tok(D_MODEL))
    n_steps = bsz * tpb
    cast_specs = []
    for w in cast_ws:
        rows = w.shape[0]
        nblk = n_steps
        while rows % nblk or (rows // nblk) % BF16_SUBLANES:
            nblk //= 2
        cast_specs.append(pl.BlockSpec((rows // nblk, w.shape[1]),
                                       lambda b, t, n=nblk: (jnp.minimum(b * tpb + t, n - 1), 0)))
    return pl.pallas_call(
        _inproj_prompt_kernel,
        grid=(bsz, tpb),
        in_specs=[tok(D_MODEL), full((1, D_MODEL)), full(w_bf.shape),
                  pl.BlockSpec((tm, LANES), lambda b, t: (t, 0)),
                  pl.BlockSpec((tm, LANES), lambda b, t: (t, 0))] + cast_specs,
        out_specs=out_specs + tuple(cast_specs),
        out_shape=out_shape + tuple(jax.ShapeDtypeStruct(w.shape, BF16) for w in cast_ws),
        compiler_params=_cparams(("arbitrary", "arbitrary")),
        name="inproj_prompt",
    )(x2d, g, w_bf, cos, sin, *cast_ws)


def _inproj_sample_kernel(x_ref, g_ref, w_ref, cos_ref, sin_ref,
                          q_ref, k_ref, v_ref, u_ref, sga_ref, sgs_ref, wb_ref):
    step = ATTN_WIDTH
    for c in range(w_ref.shape[1] // step):
        wb_ref[:, c * step:(c + 1) * step] = w_ref[:, c * step:(c + 1) * step].astype(BF16)
    reps = ATTN_WIDTH // LANES
    cos = jnp.tile(cos_ref[...], (1, reps))
    sin = jnp.tile(sin_ref[...], (1, reps))
    def put(ref):
        def store(t):
            ref[...] = t.astype(ref.dtype)
        return store

    _inproj_core(x_ref[...], g_ref[...], wb_ref, cos, sin, HEAD_DIM ** -0.5,
                 dict(q=put(q_ref), k=put(k_ref), v=put(v_ref), u=put(u_ref), sga=put(sga_ref), sgs=put(sgs_ref)))


def _inproj_sample(x2d, g, w_f32, cos, sin):
    m = x2d.shape[0]
    a = ATTN_WIDTH
    full = lambda shape, **kw: pl.BlockSpec(shape, lambda i: (0,) * len(shape), **kw)
    shapes = [(m, a), (m, a), (m, a), (m, SSM_WIDTH), (m, D_MODEL), (m, D_MODEL), w_f32.shape]
    dts = [F32, F32, F32, F32, BF16, BF16, BF16]
    return pl.pallas_call(
        _inproj_sample_kernel,
        grid=(1,),
        in_specs=[full(x2d.shape), full(g.shape), full(w_f32.shape, pipeline_mode=pl.Buffered(1)),
                  full(cos.shape), full(sin.shape)],
        out_specs=tuple(full(s) for s in shapes),
        out_shape=tuple(jax.ShapeDtypeStruct(s, d) for s, d in zip(shapes, dts)),
        compiler_params=_cparams(("arbitrary",)),
        name="inproj_sample",
    )(x2d, g, w_f32, cos, sin)


def _moba_prompt_kernel(qT_ref, k_ref, kmean_ref, vT_ref, o_ref, bias_ref, qz_s, m_s, acc_s):
    nb = qT_ref.shape[1]
    blk = MOBA_BLOCK
    nh = MOBA_HEADS_PER_GROUP
    gw = nh * HEAD_DIM
    ng = qT_ref.shape[2] // gw
    heads = [(g, hh) for g in range(ng) for hh in range(nh)]
    gcols = [slice(g * gw, (g + 1) * gw) for g in range(ng)]
    row2 = lax.broadcasted_iota(jnp.int32, (gw, 1), 0)
    lane_km = lax.broadcasted_iota(jnp.int32, (1, gw), 1)
    blk_row = lax.broadcasted_iota(jnp.int32, (nb, blk), 0)
    key_i = lax.broadcasted_iota(jnp.int32, (blk, blk), 0)
    qry_i = lax.broadcasted_iota(jnp.int32, (blk, blk), 1)
    causal = key_i <= qry_i
    in_head = [(row2 >= hh * HEAD_DIM) & (row2 < (hh + 1) * HEAD_DIM) for hh in range(nh)]
    hrows = [slice(h * HEAD_DIM, (h + 1) * HEAD_DIM) for h in range(ng * nh)]

    for h in range(ng * nh):
        bias_ref[h, 0] = jnp.full((nb, blk), NEG_BIG, F32)
    for g in range(ng):
        km_g = kmean_ref[0][:, gcols[g]]
        km_all = jnp.concatenate(
            [jnp.where((lane_km >= hh * HEAD_DIM) & (lane_km < (hh + 1) * HEAD_DIM), km_g, 0.0)
             for hh in range(nh)], axis=0)
        km_terms = []
        rest = km_all
        for _ in range(3):
            term = rest.astype(BF16)
            km_terms.append(term)
            rest = rest - term.astype(F32)
        km_split = jnp.concatenate(km_terms, axis=0)
        for i in range(1, nb):
            sb3 = _dot(km_split, qT_ref[0, i, gcols[g], :])
            sb_all = (sb3[0:nh * nb] + sb3[nh * nb:2 * nh * nb]) + sb3[2 * nh * nb:]
            for hh in range(nh):
                sb = jnp.where(blk_row < i, sb_all[hh * nb:(hh + 1) * nb], -jnp.inf)
                bias = jnp.full((nb, blk), NEG_BIG, F32)
                for _r in range(min(MOBA_TOPK, i)):
                    mx = jnp.max(sb, axis=0, keepdims=True)
                    first = jnp.min(jnp.where(sb == mx, blk_row, nb), axis=0, keepdims=True)
                    pick = blk_row == first
                    bias = jnp.where(pick, 0.0, bias)
                    sb = jnp.where(pick, -jnp.inf, sb)
                bias_ref[g * nh + hh, i] = bias

    n_items = nb // 2
    ones_rows = jnp.ones((MOBA_SUM_ROWS, 2 * blk), BF16)

    def couple(i, _):
        iq = (i, nb - 1 - i)
        n_first = (i + 1) // 2

        def diag_scores(x):
            rows = pl.ds(pl.multiple_of(iq[x] * blk, blk), blk)
            out = []
            for h, (g, hh) in enumerate(heads):
                q_grp = qT_ref[0, iq[x], gcols[g], :]
                qz = jnp.where(in_head[hh], q_grp, jnp.zeros_like(q_grp))
                qz_s[x, h] = qz
                out.append(_dot(k_ref[rows, gcols[g]], qz))
            return out

        def diag_absorb(x, s_own):
            for h in range(len(heads)):
                s = jnp.where(causal, s_own[h], NEG_BIG)
                m0 = jnp.max(s, axis=0, keepdims=True)
                p = jnp.exp2(s - m0)
                m_s[x, h] = m0
                vv = jnp.concatenate([vT_ref[0, iq[x], hrows[h], :], ones_rows[:, :blk]], axis=0)
                acc_s[x, h] = _dot(vv, p.astype(BF16))

        def item_params(k):
            first = k < n_first
            x = jnp.where(first, 0, 1)
            return x, jnp.where(first, iq[0], iq[1]), jnp.where(first, k, k - n_first)

        def item_scores(k):
            x, _, jp = item_params(k)
            rows = pl.ds(pl.multiple_of(2 * jp * blk, blk), 2 * blk)
            out = []
            for h, (g, _hh) in enumerate(heads):
                s = _dot(k_ref[rows, gcols[g]], qz_s[x, h])
                out.append((s, jnp.max(s[:blk], axis=0, keepdims=True), jnp.max(s[blk:], axis=0, keepdims=True)))
            return out

        def item_absorb(k, sc):
            x, qb, jp = item_params(k)
            j0 = 2 * jp
            for h in range(len(heads)):
                s, cma, cmb = sc[h]
                ba = bias_ref[h, qb, pl.ds(j0, 1), :]
                bb = bias_ref[h, qb, pl.ds(j0 + 1, 1), :]
                m = m_s[x, h]
                m_new = jnp.maximum(m, jnp.maximum(cma + ba, cmb + bb))
                alpha = jnp.exp2(m - m_new)
                m_s[x, h] = m_new
                sub = (m_new - ba, m_new - bb)
                pp = jnp.concatenate(
                    [jnp.exp2(s[r:r + MOBA_EXP_ROWS] - sub[r // blk]).astype(BF16)
                     for r in range(0, 2 * blk, MOBA_EXP_ROWS)], axis=0)
                vv = jnp.concatenate([vT_ref[0, j0, hrows[h], :], vT_ref[0, j0 + 1, hrows[h], :]], axis=1)
                vv = jnp.concatenate([vv, ones_rows], axis=0)
                acc_s[x, h] = alpha * acc_s[x, h] + _dot(vv, pp)

        s_diag = [diag_scores(0), diag_scores(1)]
        pending = [item_scores(k) for k in range(min(MOBA_LOOKAHEAD, n_items))]
        diag_absorb(0, s_diag[0])
        diag_absorb(1, s_diag[1])
        for k in range(n_items):
            if k + MOBA_LOOKAHEAD < n_items:
                pending.append(item_scores(k + MOBA_LOOKAHEAD))
            item_absorb(k, pending[k])
        for x in range(2):
            oT = jnp.concatenate([acc_s[x, h, 0:HEAD_DIM, :] / acc_s[x, h, HEAD_DIM:HEAD_DIM + 1, :]
                                  for h in range(len(heads))], axis=0)
            o_ref[pl.ds(pl.multiple_of(iq[x] * blk, blk), blk), :] = oT.T.astype(o_ref.dtype)
        return 0

    lax.fori_loop(0, nb // 2, couple, 0)


def _moba_prompt(qT, kbf, kmean, vTb, bsz, seq):
    nb = seq // MOBA_BLOCK
    hps = MOBA_HEADS_PER_GROUP * MOBA_GROUPS_PER_STEP
    hp = N_HEADS // hps
    pair = hps * HEAD_DIM
    gw = MOBA_HEADS_PER_GROUP * HEAD_DIM
    return pl.pallas_call(
        _moba_prompt_kernel,
        grid=(bsz, hp),
        in_specs=[pl.BlockSpec((1, nb, pair, MOBA_BLOCK), lambda b, h: (b, 0, h, 0)),
                  pl.BlockSpec((seq, pair), lambda b, h: (b, h)),
                  pl.BlockSpec((1, nb, pair), lambda b, h: (b, 0, h)),
                  pl.BlockSpec((1, nb, pair, MOBA_BLOCK), lambda b, h: (b, 0, h, 0))],
        out_specs=pl.BlockSpec((seq, pair), lambda b, h: (b, h)),
        out_shape=jax.ShapeDtypeStruct((bsz * seq, ATTN_WIDTH), BF16),
        scratch_shapes=[pltpu.VMEM((hps, nb, nb, MOBA_BLOCK), F32),
                        pltpu.VMEM((2, hps, gw, MOBA_BLOCK), BF16),
                        pltpu.VMEM((2, hps, 1, MOBA_BLOCK), F32),
                        pltpu.VMEM((2, hps, HEAD_DIM + MOBA_SUM_ROWS, MOBA_BLOCK), F32)],
        compiler_params=_cparams(("parallel", "parallel")),
        name="moba_prompt",
    )(qT, kbf, kmean, vTb)


def _moba_paged_kernel(pt_ref, qcol_ref, kn_ref, vn_ref, ck_ref, cv_ref, o_ref,
                       kbuf, vbuf, s_ref, psel_ref, stat_ref, qb_ref, ksem, vsem):
    b = pl.program_id(0)
    nseq = pl.num_programs(0) - 1
    n_pages = kbuf.shape[1]
    nblk = n_pages // 2
    slot = b % 2

    def k_copy(seq_i, sl, p):
        return pltpu.make_async_copy(ck_ref.at[pt_ref[seq_i, p]], kbuf.at[sl, p], ksem.at[sl])

    def start_k(seq_i, sl):
        def body(p, _):
            k_copy(seq_i, sl, p).start()
            return 0
        lax.fori_loop(0, n_pages, body, 0)

    def v_copy(page, h, r, par):
        return pltpu.make_async_copy(cv_ref.at[page, h], vbuf.at[h, r, par], vsem.at[0])

    @pl.when(b == 0)
    def _():
        start_k(0, 0)

    @pl.when(b < nseq)
    def _():
        def wait_body(p, _):
            k_copy(b, slot, p).wait()
            return 0
        lax.fori_loop(0, n_pages, wait_body, 0)

    @pl.when(b + 1 < nseq)
    def _():
        start_k(b + 1, 1 - slot)

    @pl.when(b >= 1)
    def _():
        for h in range(N_HEADS):
            for r in range(MOBA_TOPK):
                for par in range(2):
                    v_copy(0, h, r, par).wait()
        vn = vn_ref[0]
        for h in range(N_HEADS):
            acc = jnp.zeros((HEAD_DIM, PAGE_SIZE), F32)
            for r in range(MOBA_TOPK):
                for par in range(2):
                    acc = acc + vbuf[h, r, par] * psel_ref[h, 2 * r + par]
            p_own = stat_ref[0, h][:, 0:1]
            l = stat_ref[1, h][:, 0:1]
            o_h = jnp.sum(acc, axis=1, keepdims=True) + p_own * vn[:, h:h + 1]
            o_ref[0, :, h:h + 1] = o_h / l

    @pl.when(b < nseq)
    def _():
        qcol = qcol_ref[0]
        own = jnp.sum(qcol * kn_ref[0], axis=0, keepdims=True)
        blk_i = lax.broadcasted_iota(jnp.int32, (nblk, 1), 0)
        for h in range(N_HEADS):
            qb_ref[h] = jnp.broadcast_to(qcol[:, h:h + 1], (HEAD_DIM, PAGE_SIZE))
        for h in range(N_HEADS):
            def score_body(g, _):
                qb = qb_ref[h]
                for dn in range(PAGED_SCORE_UNROLL):
                    n = g * PAGED_SCORE_UNROLL + dn
                    for par in range(2):
                        kt = kbuf[slot, 2 * n + par, h]
                        s_ref[par, h, pl.ds(n, 1), :] = jnp.sum(kt * qb, axis=0, keepdims=True)
                return 0
            lax.fori_loop(0, nblk // PAGED_SCORE_UNROLL, score_body, 0)

        for h in range(N_HEADS):
            s0 = s_ref[0, h]
            s1 = s_ref[1, h]
            bs = jnp.sum(s0 + s1, axis=1, keepdims=True)
            sel = jnp.zeros((nblk, 1), dtype=jnp.bool_)
            picks = []
            for r in range(MOBA_TOPK):
                mx = jnp.max(bs, axis=0, keepdims=True)
                first = jnp.min(jnp.where(bs == mx, blk_i, nblk), axis=0, keepdims=True)
                pick = blk_i == first
                sel = sel | pick
                bs = jnp.where(pick, -jnp.inf, bs)
                blk_id = jnp.max(first)
                for par in range(2):
                    v_copy(pt_ref[b, 2 * blk_id + par], h, r, par).start()
                picks.append(blk_id)
            s_own = own[:, h:h + 1]
            sm0 = jnp.where(sel, s0, NEG_BIG)
            sm1 = jnp.where(sel, s1, NEG_BIG)
            mx = jnp.maximum(jnp.max(jnp.max(jnp.maximum(sm0, sm1), axis=1, keepdims=True),
                                     axis=0, keepdims=True), s_own)
            p0 = jnp.exp(sm0 - mx)
            p1 = jnp.exp(sm1 - mx)
            p_own = jnp.exp(s_own - mx)
            l = jnp.sum(jnp.sum(p0 + p1, axis=1, keepdims=True), axis=0, keepdims=True) + p_own
            s_ref[0, h] = p0
            s_ref[1, h] = p1
            for r in range(MOBA_TOPK):
                for par in range(2):
                    psel_ref[h, 2 * r + par] = s_ref[par, h, pl.ds(picks[r], 1), :]
            stat_ref[0, h] = jnp.broadcast_to(p_own, (1, PAGE_SIZE))
            stat_ref[1, h] = jnp.broadcast_to(l, (1, PAGE_SIZE))


def _moba_paged(page_table, qcol, kncol, vncol, cache_kT, cache_vT):
    nseq, n_pages = page_table.shape
    assert n_pages // 2 >= MOBA_TOPK, "every pick must find an unpicked cached block"
    cur = pl.BlockSpec((1, HEAD_DIM, N_HEADS), lambda b, pt: (jnp.minimum(b, nseq - 1), 0, 0))
    prev = pl.BlockSpec((1, HEAD_DIM, N_HEADS), lambda b, pt: (jnp.maximum(b - 1, 0), 0, 0))
    any_spec = pl.BlockSpec(memory_space=pl.ANY)
    grid_spec = pltpu.PrefetchScalarGridSpec(
        num_scalar_prefetch=1,
        grid=(nseq + 1,),
        in_specs=[cur, cur, prev, any_spec, any_spec],
        out_specs=prev,
        scratch_shapes=[
            pltpu.VMEM((2, n_pages, N_HEADS, HEAD_DIM, PAGE_SIZE), F32),
            pltpu.VMEM((N_HEADS, MOBA_TOPK, 2, HEAD_DIM, PAGE_SIZE), F32),
            pltpu.VMEM((2, N_HEADS, n_pages // 2, PAGE_SIZE), F32),
            pltpu.VMEM((N_HEADS, 2 * MOBA_TOPK, 1, PAGE_SIZE), F32),
            pltpu.VMEM((2, N_HEADS, 1, PAGE_SIZE), F32),
            pltpu.VMEM((N_HEADS, HEAD_DIM, PAGE_SIZE), F32),
            pltpu.SemaphoreType.DMA((2,)),
            pltpu.SemaphoreType.DMA((1,)),
        ])
    return pl.pallas_call(
        _moba_paged_kernel,
        grid_spec=grid_spec,
        out_shape=jax.ShapeDtypeStruct((nseq, HEAD_DIM, N_HEADS), F32),
        compiler_params=_cparams(("arbitrary",)),
        name="moba_paged",
    )(page_table, qcol, kncol, vncol, cache_kT, cache_vT)


def _ssm_tables(a_re, a_im, log_dt, b_re, b_im, c_re, c_im, chunk, n_pow):
    t_ = chunk
    lam = lax.complex(a_re.astype(F32), a_im.astype(F32))
    ldt = lam * jnp.exp(log_dt.astype(F32))[:, None]
    a_bar = jnp.exp(ldt)
    b_bar = ((a_bar - 1.0) / lam)[..., None] * lax.complex(b_re.astype(F32), b_im.astype(F32))
    c_c = lax.complex(c_re.astype(F32), c_im.astype(F32))
    taus = jnp.arange(t_ + 1, dtype=F32).astype(jnp.complex64)
    apow = jnp.exp(ldt[None] * taus[:, None, None])
    gq, ssg, c_, p_ = GROUPS_PER_SG, N_SG, SSM_GROUP, SSM_STATE
    c4 = c_c.reshape(ssg, gq, c_, p_)
    b4 = b_bar.reshape(ssg, gq, p_, c_)
    ap4 = apow.reshape(t_ + 1, ssg, gq, p_)

    kc = jnp.einsum('sjcp,tsjp,sjpd->sjdtc', c4, ap4[:t_], b4).real
    kc = kc.reshape(ssg, LANES, t_ * c_)
    mst = jnp.einsum('tsjp,sjpd->stjdp', ap4[:t_][::-1], b4)
    mc = jnp.concatenate([mst.real, mst.imag], axis=-1).reshape(ssg, t_ * LANES, 2 * p_)
    nout = jnp.einsum('sjcp,tsjp->sjptc', c4, ap4[1:])
    nc = jnp.concatenate([nout.real, -nout.imag], axis=1).reshape(ssg, 2 * SG_STATE, t_ * c_)
    rs = jnp.arange(1, n_pow + 1, dtype=F32).astype(jnp.complex64)
    ap = jnp.exp((ldt * t_)[None] * rs[:, None, None])
    ap = ap.reshape(n_pow, ssg, SG_STATE).transpose(1, 0, 2)
    a_pow = jnp.concatenate([ap.real, ap.imag], axis=-1)
    a1 = a_bar.reshape(ssg, 1, SG_STATE)
    a_one = jnp.concatenate([a1.real, a1.imag], axis=-1)
    return kc, mc, nc, a_pow, a_one


def _expand_ssm_weights(kc_ref, mc_ref, nc_ref, wt_s, ws_s, wo_s):
    t_ = SSM_CHUNK
    kw = t_ * LANES
    lg = lambda n: n.bit_length() - 1
    r = lax.broadcasted_iota(jnp.int32, (LANES, kw), 0)
    q = lax.broadcasted_iota(jnp.int32, (LANES, kw), 1)
    sh_c, sh_p = lg(SSM_GROUP), lg(SSM_STATE)
    ecol = jnp.where(((r >> sh_c) == (q >> lg(LANES))) & ((r & (SSM_GROUP - 1)) == (q & (SSM_GROUP - 1))),
                     1.0, 0.0).astype(BF16)
    e2 = jnp.where(((r >> sh_p) == (q >> lg(SG_STATE))) & ((r & (SSM_STATE - 1)) == (q & (SSM_STATE - 1))),
                   1.0, 0.0).astype(BF16)
    col_c = (q & (LANES - 1)) >> sh_c
    taps = jnp.where((r >> sh_c) == col_c, _dot(kc_ref[0].astype(BF16), ecol), 0.0).astype(BF16)
    blank = jnp.zeros((LANES, LANES), BF16)
    for s in range(t_):
        for t in range(t_):
            wt_s[s * LANES:(s + 1) * LANES, t * LANES:(t + 1) * LANES] = (
                taps[:, (t - s) * LANES:(t - s + 1) * LANES] if t >= s else blank)
    rr = lax.broadcasted_iota(jnp.int32, (kw, kw), 0)
    qq = lax.broadcasted_iota(jnp.int32, (kw, kw), 1)
    row_c = (rr & (LANES - 1)) >> sh_c
    row_p = (rr & (SG_STATE - 1)) >> sh_p
    colq_c = (qq & (LANES - 1)) >> sh_c
    colq_p = (qq & (SG_STATE - 1)) >> sh_p
    ws_s[...] = jnp.where(row_c == colq_p, _dot(mc_ref[0].astype(BF16), e2), 0.0).astype(BF16)
    wo_s[...] = jnp.where(row_p == colq_c, _dot(nc_ref[0].astype(BF16), ecol), 0.0).astype(BF16)


def _ssm_prompt_kernel(u_ref, kc_ref, mc_ref, nc_ref, ap_ref, d_ref, h0_ref,
                       y_ref, hf_ref, uc_ref, x_ref, hin_ref, wt_s, ws_s, wo_s):
    t_ = SSM_CHUNK
    nc = uc_ref.shape[0]
    ns = SG_STATE

    @pl.when(pl.program_id(1) == 0)
    def _():
        _expand_ssm_weights(kc_ref, mc_ref, nc_ref, wt_s, ws_s, wo_s)

    for t in range(t_):
        uc_ref[:, t * LANES:(t + 1) * LANES] = u_ref[pl.ds(t, nc, stride=t_), :].astype(BF16)
    uc = uc_ref[...]
    x_ref[...] = _dot(uc, ws_s[...])

    ap = ap_ref[0]
    apr, api = ap[:, :ns], ap[:, ns:]
    row = lax.broadcasted_iota(jnp.int32, (SUBLANES, 1), 0)

    def tile_body(k, carry):
        hr, hi = carry
        rows = pl.ds(k * SUBLANES, SUBLANES)
        xr = x_ref[rows, 0:ns]
        xi = x_ref[rows, ns:2 * ns]
        for d in (1, 2, 4):
            ar, ai = apr[d - 1:d], api[d - 1:d]
            sr = jnp.where(row >= d, pltpu.roll(xr, d, 0), 0.0)
            si = jnp.where(row >= d, pltpu.roll(xi, d, 0), 0.0)
            xr, xi = xr + ar * sr - ai * si, xi + ar * si + ai * sr
        outr = xr + apr * hr - api * hi
        outi = xi + apr * hi + api * hr
        hin_ref[rows, 0:ns] = jnp.where(row >= 1, pltpu.roll(outr, 1, 0), hr)
        hin_ref[rows, ns:2 * ns] = jnp.where(row >= 1, pltpu.roll(outi, 1, 0), hi)
        return outr[SUBLANES - 1:SUBLANES], outi[SUBLANES - 1:SUBLANES]

    cols = []
    for nt in range(t_ * LANES // MXU_COLS):
        acc = None
        for kt in range(nt + 1):
            part = _dot(uc[:, kt * MXU_COLS:(kt + 1) * MXU_COLS],
                        wt_s[kt * MXU_COLS:(kt + 1) * MXU_COLS, nt * MXU_COLS:(nt + 1) * MXU_COLS])
            acc = part if acc is None else acc + part
        cols.append(acc)
    y_intra = jnp.concatenate(cols, axis=1)
    h0 = h0_ref[0, 0]
    carry = (h0[:, :ns], h0[:, ns:])
    for k in range(nc // SUBLANES):
        carry = tile_body(k, carry)
    hf_ref[0, 0] = jnp.concatenate(carry, axis=1)

    y = y_intra + _dot(hin_ref[...].astype(BF16), wo_s[...])
    dv = d_ref[...]
    for t in range(t_):
        rows = pl.ds(t, nc, stride=t_)
        y_ref[rows, :] = y[:, t * LANES:(t + 1) * LANES] + dv * u_ref[rows, :]


def _ssm_prompt(u, kc, mc, nc_tab, a_pow, d_row, h0, bsz, seq):
    t_ = SSM_CHUNK
    nc = seq // t_
    kw = t_ * LANES
    assert kc.shape[2] == LANES and 2 * SG_STATE == kw
    wspec = lambda shape: pl.BlockSpec((1,) + shape, lambda s, b: (s, 0, 0))
    return pl.pallas_call(
        _ssm_prompt_kernel,
        grid=(N_SG, bsz),
        in_specs=[pl.BlockSpec((seq, LANES), lambda s, b: (b, s)),
                  wspec(kc.shape[1:]), wspec(mc.shape[1:]), wspec(nc_tab.shape[1:]),
                  wspec((SUBLANES, 2 * SG_STATE)),
                  pl.BlockSpec((1, LANES), lambda s, b: (0, s)),
                  pl.BlockSpec((1, 1, 1, 2 * SG_STATE), lambda s, b: (b, s, 0, 0))],
        out_specs=(pl.BlockSpec((seq, LANES), lambda s, b: (b, s)),
                   pl.BlockSpec((1, 1, 1, 2 * SG_STATE), lambda s, b: (b, s, 0, 0))),
        out_shape=(jax.ShapeDtypeStruct((bsz * seq, SSM_WIDTH), F32),
                   jax.ShapeDtypeStruct((bsz, N_SG, 1, 2 * SG_STATE), F32)),
        scratch_shapes=[pltpu.VMEM((nc, kw), BF16),
                        pltpu.VMEM((nc, 2 * SG_STATE), F32),
                        pltpu.VMEM((nc, 2 * SG_STATE), F32),
                        pltpu.VMEM((kw, kw), BF16),
                        pltpu.VMEM((kw, 2 * SG_STATE), BF16),
                        pltpu.VMEM((2 * SG_STATE, kw), BF16)],
        compiler_params=_cparams(("arbitrary", "arbitrary")),
        name="ssm_prompt",
    )(u, kc, mc, nc_tab, a_pow, d_row, h0)


def _ssm_step_kernel(u_ref, kc_ref, mc_ref, nc_ref, a1_ref, d_ref, h0_ref, y_ref, hf_ref):
    ns = SG_STATE
    kw = 2 * ns
    lg = lambda n: n.bit_length() - 1
    sh_c, sh_p = lg(SSM_GROUP), lg(SSM_STATE)
    iota = lambda shape, d: lax.broadcasted_iota(jnp.int32, shape, d)
    r1, q1 = iota((LANES, LANES), 0), iota((LANES, LANES), 1)
    e_c = jnp.where((r1 < SSM_GROUP) & ((r1 & (SSM_GROUP - 1)) == (q1 & (SSM_GROUP - 1))), 1.0, 0.0).astype(BF16)
    r2, q2 = iota((LANES, kw), 0), iota((LANES, kw), 1)
    e_p = jnp.where(((r2 >> sh_p) == (q2 >> lg(ns))) & ((r2 & (SSM_STATE - 1)) == (q2 & (SSM_STATE - 1))),
                    1.0, 0.0).astype(BF16)
    same_t = (r1 >> sh_c) == (q1 >> sh_c)
    same_s = (r2 >> sh_c) == ((q2 & (ns - 1)) >> sh_p)
    r3, q3 = iota((kw, LANES), 0), iota((kw, LANES), 1)
    same_o = ((r3 & (ns - 1)) >> sh_p) == (q3 >> sh_c)
    last = (SSM_CHUNK - 1) * LANES
    for s in range(N_SG):
        wt = jnp.where(same_t, _dot(kc_ref[s].astype(BF16), e_c), 0.0).astype(BF16)
        ws = jnp.where(same_s, _dot(mc_ref[s, last:last + LANES, :].astype(BF16), e_p), 0.0).astype(BF16)
        wo = jnp.where(same_o, _dot(nc_ref[s].astype(BF16), e_c), 0.0).astype(BF16)
        us = u_ref[:, s * LANES:(s + 1) * LANES]
        ub = us.astype(BF16)
        h0 = h0_ref[s]
        x = _dot(ub, ws)
        ar, ai = a1_ref[s, :, :ns], a1_ref[s, :, ns:]
        hr, hi = h0[:, :ns], h0[:, ns:]
        hf_ref[s] = jnp.concatenate([x[:, :ns] + ar * hr - ai * hi,
                                     x[:, ns:] + ar * hi + ai * hr], axis=1)
        y = _dot(ub, wt) + _dot(h0.astype(BF16), wo)
        y_ref[:, s * LANES:(s + 1) * LANES] = y + d_ref[:, s * LANES:(s + 1) * LANES] * us


def _ssm_step(u, kc, mc, nc_tab, a_one, d_row, h0):
    m = u.shape[0]
    full = lambda a: pl.BlockSpec(a.shape, lambda i: (0,) * a.ndim)
    args = (u, kc, mc, nc_tab, a_one, d_row, h0)
    shapes = [(m, SSM_WIDTH), (N_SG, m, 2 * SG_STATE)]
    return pl.pallas_call(
        _ssm_step_kernel,
        grid=(1,),
        in_specs=[full(a) for a in args],
        out_specs=tuple(pl.BlockSpec(s, lambda i, n=len(s): (0,) * n) for s in shapes),
        out_shape=tuple(jax.ShapeDtypeStruct(s, F32) for s in shapes),
        compiler_params=_cparams(("arbitrary",)),
        name="ssm_step",
    )(*args)


FFN_SPLIT = FFN_HIDDEN // MXU_COLS
FFN_CHUNK = FFN_HIDDEN // FFN_SPLIT


def _post_kernel(x_ref, o_ref, y_ref, sga_ref, sgs_ref, wap_ref, wglu_ref, bglu_ref, wsp_ref,
                 wout_ref, nffn_ref, wfi_ref, wfo_ref, nfin_ref, out_ref):
    z = jax.nn.gelu(y_ref[...])
    z = z * jax.nn.sigmoid(_dot(z.astype(BF16), wglu_ref[...]) + bglu_ref[...])
    ssm_out = _dot(z.astype(BF16), wsp_ref[...])
    attn_out = _dot(o_ref[...].astype(BF16), wap_ref[...])
    merged = sga_ref[...].astype(F32) * attn_out + sgs_ref[...].astype(F32) * ssm_out
    x1 = x_ref[...] + _dot(merged.astype(BF16), wout_ref[...])
    hf = _rms(x1, nffn_ref[...]).astype(BF16)
    acc = x1
    for c in range(FFN_SPLIT):
        lo = c * FFN_CHUNK
        a = _dot(hf, wfi_ref[:, lo:lo + FFN_CHUNK])
        g = _dot(hf, wfi_ref[:, FFN_HIDDEN + lo:FFN_HIDDEN + lo + FFN_CHUNK])
        act = (jax.nn.silu(a) * g).astype(BF16)
        acc = acc + _dot(act, wfo_ref[lo:lo + FFN_CHUNK, :])
    out_ref[...] = _rms(acc, nfin_ref[...])


def _post(x2d, o, y, sga, sgs, wap, wglu, bglu, wsp, wout, nffn, wfi, wfo, nfin, tm):
    m = x2d.shape[0]
    tok = lambda w: pl.BlockSpec((tm, w), lambda i: (i, 0))
    const = lambda a: pl.BlockSpec(a.shape, lambda i: (0,) * a.ndim, pipeline_mode=pl.Buffered(1))
    weights = (wap, wglu, bglu, wsp, wout, nffn, wfi, wfo, nfin)
    return pl.pallas_call(
        _post_kernel,
        grid=(m // tm,),
        in_specs=[tok(D_MODEL), tok(ATTN_WIDTH), tok(SSM_WIDTH), tok(D_MODEL), tok(D_MODEL)]
                 + [const(w) for w in weights],
        out_specs=tok(D_MODEL),
        out_shape=jax.ShapeDtypeStruct((m, D_MODEL), F32),
        compiler_params=_cparams(("parallel",)),
        name="post",
    )(x2d, o, y, sga, sgs, *weights)


def _rope_tables(pos):
    half = HEAD_DIM // 2
    lane = jnp.arange(LANES)
    inv = jnp.power(jnp.float32(ROPE_THETA), -2.0 * (lane % half).astype(F32) / HEAD_DIM)
    sign = jnp.where((lane % HEAD_DIM) < half, -1.0, 1.0).astype(F32)
    ang = pos.astype(F32)[:, None] * inv[None, :]
    return jnp.cos(ang), jnp.sin(ang) * sign[None, :]


def _leaf_from_T(xT, bsz, seq):
    return xT.reshape(bsz, N_HEADS, HEAD_DIM, seq).transpose(0, 3, 1, 2)[None]


def _state_in(re, im):
    n = re.shape[0]
    h = jnp.concatenate([re.reshape(n, N_SG, SG_STATE), im.reshape(n, N_SG, SG_STATE)], axis=-1)
    return h.transpose(1, 0, 2)


def _state_out(h):
    n = h.shape[0]
    return (h[..., :SG_STATE].reshape(1, n, SSM_GROUPS, SSM_STATE),
            h[..., SG_STATE:].reshape(1, n, SSM_GROUPS, SSM_STATE))


def kernel(x_prompt, x_sample, cache_k, cache_v, state_ssm_re, state_ssm_im, page_table, norm_mix, w_in,
           w_attn_proj, ssm_a_re, ssm_a_im, ssm_log_dt, ssm_b_re, ssm_b_im, ssm_c_re, ssm_c_im, ssm_d, w_glu,
           b_glu, w_ssm_proj, w_out, norm_ffn, w_ffn_in, w_ffn_out, norm_final):
    assert w_in.shape[0] == 1, "single layer"
    bsz, seq = x_prompt.shape[:2]
    nseq = x_sample.shape[0]
    past_len = page_table.shape[1] * PAGE_SIZE
    assert seq % MOBA_BLOCK == 0 and past_len % MOBA_BLOCK == 0 and x_sample.shape[1] == 1

    ssm_p = (ssm_a_re[0], ssm_a_im[0], ssm_log_dt[0], ssm_b_re[0], ssm_b_im[0], ssm_c_re[0], ssm_c_im[0])

    cos_s, sin_s = _rope_tables(jnp.full((1,), past_len, dtype=jnp.int32))
    xs2 = x_sample.reshape(nseq, D_MODEL)
    q_s, k_s, v_s, u_s, sga_s, sgs_s, w_in_bf = _inproj_sample(xs2, norm_mix, w_in[0], cos_s, sin_s)

    cos_p, sin_p = _rope_tables(jnp.arange(seq, dtype=jnp.int32))
    xp2 = x_prompt.reshape(bsz * seq, D_MODEL)
    cast_ws = (w_attn_proj[0], w_glu[0], w_ssm_proj[0], w_out[0], w_ffn_in[0], w_ffn_out[0])
    (qT, kT, kbf, kmean, vT, vTb, u_p, sga_p, sgs_p,
     wap_bf, wglu_bf, wsp_bf, wout_bf, wfi_bf, wfo_bf) = _inproj_prompt(
        xp2, norm_mix, w_in_bf, cos_p, sin_p, cast_ws, bsz, seq, tm=512)
    post_w = (wap_bf, wglu_bf, b_glu, wsp_bf, wout_bf, norm_ffn, wfi_bf, wfo_bf, norm_final[None])
    o_p = _moba_prompt(qT, kbf, kmean.reshape(bsz, seq // MOBA_BLOCK, ATTN_WIDTH), vTb, bsz, seq)
    kc, mc, nc_tab, ap, a_one = _ssm_tables(*ssm_p, chunk=SSM_CHUNK, n_pow=SUBLANES)
    h0_p = jnp.zeros((bsz, N_SG, 1, 2 * SG_STATE), F32)
    y_p, hf_p = _ssm_prompt(u_p, kc, mc, nc_tab, ap, ssm_d, h0_p, bsz, seq)
    y_prompt = _post(xp2, o_p, y_p, sga_p, sgs_p, *post_w, tm=512).reshape(bsz, seq, D_MODEL)
    new_ssm_re_p, new_ssm_im_p = _state_out(hf_p.reshape(bsz, N_SG, 2 * SG_STATE))

    cache_kT = cache_k[0].transpose(0, 2, 3, 1)
    cache_vT = cache_v[0].transpose(0, 2, 3, 1)
    dh = lambda t: t.reshape(nseq, N_HEADS, HEAD_DIM).transpose(0, 2, 1)
    o_s = _moba_paged(page_table, dh(q_s), dh(k_s), dh(v_s), cache_kT, cache_vT).transpose(0, 2, 1)
    h0_s = _state_in(state_ssm_re[0], state_ssm_im[0])
    y_s, hf_s = _ssm_step(u_s, kc, mc, nc_tab, a_one, ssm_d, h0_s)
    y_sample = _post(xs2, o_s.reshape(nseq, ATTN_WIDTH), y_s, sga_s, sgs_s, *post_w, tm=nseq)
    new_ssm_re_s, new_ssm_im_s = _state_out(hf_s.transpose(1, 0, 2))

    return (y_prompt, y_sample.reshape(nseq, 1, D_MODEL),
            _leaf_from_T(kT, bsz, seq), _leaf_from_T(vT, bsz, seq), new_ssm_re_p, new_ssm_im_p,
            k_s.reshape(1, nseq, 1, N_HEADS, HEAD_DIM), v_s.reshape(1, nseq, 1, N_HEADS, HEAD_DIM),
            new_ssm_re_s, new_ssm_im_s)
```

```python
import functools
import math

import jax
import jax.numpy as jnp
from jax import lax
from jax.experimental import pallas as pl
from jax.experimental.pallas import tpu as pltpu

F32 = jnp.float32
BF16 = jnp.bfloat16

D_MODEL = 1024
N_HEADS = 8
HEAD_DIM = 64
ATTN_WIDTH = N_HEADS * HEAD_DIM
MOBA_BLOCK = 256
MOBA_TOPK = 3
ROPE_THETA = 10000.0
SSM_WIDTH = 512
SSM_GROUP = 16
SSM_GROUPS = 32
SSM_STATE = 64
FFN_HIDDEN = 2816
RMS_EPS = 1e-6
PAGE_SIZE = 128

LANES = 128
SUBLANES = 8
BF16_SUBLANES = 16
MXU_COLS = 256
GROUPS_PER_SG = LANES // SSM_GROUP
N_SG = SSM_GROUPS // GROUPS_PER_SG
SG_STATE = GROUPS_PER_SG * SSM_STATE
SSM_CHUNK = 8
NEG_BIG = -1e30
MOBA_GROUPS_PER_STEP = 1
MOBA_HEADS_PER_GROUP = 4
MOBA_LOOKAHEAD = 2
LOG2E = math.log2(math.e)
MOBA_SUM_ROWS = 16
PAGED_PREFETCH_PARTS = 4
PAGED_SCORE_UNROLL = 4
VMEM_LIMIT = 56 * 1024 * 1024


def _cparams(sem):
    return pltpu.CompilerParams(dimension_semantics=sem, vmem_limit_bytes=VMEM_LIMIT)


def _dot(a, b):
    return jnp.dot(a, b, preferred_element_type=F32)


def _rms(x, g):
    return x * lax.rsqrt(jnp.mean(x * x, axis=-1, keepdims=True) + RMS_EPS) * g


def _inproj_core(x, g, w_ref, cos, sin, q_scale):
    h = _rms(x, g).astype(BF16)
    lane = lax.broadcasted_iota(jnp.int32, (1, ATTN_WIDTH), 1)
    first_half = (lane % HEAD_DIM) < (HEAD_DIM // 2)

    def rot(t):
        partner = jnp.where(first_half,
                            pltpu.roll(t, ATTN_WIDTH - HEAD_DIM // 2, 1),
                            pltpu.roll(t, HEAD_DIM // 2, 1))
        return t * cos + partner * sin

    a = ATTN_WIDTH
    q = rot(_dot(h, w_ref[:, 0:a])) * q_scale
    k = rot(_dot(h, w_ref[:, a:2 * a]))
    v = _dot(h, w_ref[:, 2 * a:3 * a])
    u = _dot(h, w_ref[:, 3 * a:3 * a + SSM_WIDTH])
    o = 3 * a + SSM_WIDTH
    sga = jax.nn.sigmoid(_dot(h, w_ref[:, o:o + D_MODEL]))
    sgs = jax.nn.sigmoid(_dot(h, w_ref[:, o + D_MODEL:o + 2 * D_MODEL]))
    return q, k, v, u, sga, sgs


def _inproj_prompt_kernel(x_ref, g_ref, w_ref, cos_ref, sin_ref, *rest):
    n_cast = (len(rest) - 9) // 2
    cast_in = rest[:n_cast]
    qT_ref, kT_ref, kbf_ref, kmean_ref, vT_ref, vTb_ref, u_ref, sga_ref, sgs_ref = rest[n_cast:n_cast + 9]
    cast_out = rest[n_cast + 9:]
    for wi_ref, wo_ref in zip(cast_in, cast_out):
        wo_ref[...] = wi_ref[...].astype(BF16)
    reps = ATTN_WIDTH // LANES
    cos = jnp.tile(cos_ref[...], (1, reps))
    sin = jnp.tile(sin_ref[...], (1, reps))
    q, k, v, u, sga, sgs = _inproj_core(x_ref[...], g_ref[...], w_ref, cos, sin, LOG2E * HEAD_DIM ** -0.5)
    tm = q.shape[0]
    kT_ref[0] = k.T
    vT_ref[0] = v.T
    kbf_ref[...] = k.astype(BF16)
    for s in range(tm // MOBA_BLOCK):
        rows = slice(s * MOBA_BLOCK, (s + 1) * MOBA_BLOCK)
        qT_ref[0, s] = q[rows].T.astype(BF16)
        vTb_ref[0, s] = v[rows].T.astype(BF16)
        kmean_ref[0, s] = jnp.mean(k[rows], axis=0, keepdims=True)
    u_ref[...] = u
    sga_ref[...] = sga.astype(BF16)
    sgs_ref[...] = sgs.astype(BF16)


def _inproj_prompt(x2d, g, w_bf, cos, sin, cast_ws, bsz, seq, tm):
    m = bsz * seq
    nb = seq // MOBA_BLOCK
    tpb = seq // tm
    sub = tm // MOBA_BLOCK
    a = ATTN_WIDTH
    full = lambda shape: pl.BlockSpec(shape, lambda b, t: (0,) * len(shape))
    tok = lambda w: pl.BlockSpec((tm, w), lambda b, t: (b * tpb + t, 0))
    out_shape = (
        jax.ShapeDtypeStruct((bsz, nb, a, MOBA_BLOCK), BF16),
        jax.ShapeDtypeStruct((bsz, a, seq), F32),
        jax.ShapeDtypeStruct((m, a), BF16),
        jax.ShapeDtypeStruct((bsz, nb, 1, a), F32),
        jax.ShapeDtypeStruct((bsz, a, seq), F32),
        jax.ShapeDtypeStruct((bsz, nb, a, MOBA_BLOCK), BF16),
        jax.ShapeDtypeStruct((m, SSM_WIDTH), F32),
        jax.ShapeDtypeStruct((m, D_MODEL), BF16),
        jax.ShapeDtypeStruct((m, D_MODEL), BF16),
    )
    blk_t = pl.BlockSpec((1, sub, a, MOBA_BLOCK), lambda b, t: (b, t, 0, 0))
    lane_t = pl.BlockSpec((1, a, tm), lambda b, t: (b, 0, t))
    out_specs = (blk_t, lane_t, tok(a),
                 pl.BlockSpec((1, sub, 1, a), lambda b, t: (b, t, 0, 0)),
                 lane_t, blk_t, tok(SSM_WIDTH), tok(D_MODEL), tok(D_MODEL))
    n_steps = bsz * tpb
    cast_specs = []
    for w in cast_ws:
        rows = w.shape[0]
        nblk = n_steps
        while rows % nblk or (rows // nblk) % BF16_SUBLANES:
            nblk //= 2
        cast_specs.append(pl.BlockSpec((rows // nblk, w.shape[1]),
                                       lambda b, t, n=nblk: (jnp.minimum(b * tpb + t, n - 1), 0)))
    return pl.pallas_call(
        _inproj_prompt_kernel,
        grid=(bsz, tpb),
        in_specs=[tok(D_MODEL), full((1, D_MODEL)), full(w_bf.shape),
                  pl.BlockSpec((tm, LANES), lambda b, t: (t, 0)),
                  pl.BlockSpec((tm, LANES), lambda b, t: (t, 0))] + cast_specs,
        out_specs=out_specs + tuple(cast_specs),
        out_shape=out_shape + tuple(jax.ShapeDtypeStruct(w.shape, BF16) for w in cast_ws),
        compiler_params=_cparams(("arbitrary", "arbitrary")),
        name="inproj_prompt",
    )(x2d, g, w_bf, cos, sin, *cast_ws)


def _inproj_sample_kernel(x_ref, g_ref, w_ref, cos_ref, sin_ref,
                          q_ref, k_ref, v_ref, u_ref, sga_ref, sgs_ref, wb_ref):
    step = ATTN_WIDTH
    for c in range(w_ref.shape[1] // step):
        wb_ref[:, c * step:(c + 1) * step] = w_ref[:, c * step:(c + 1) * step].astype(BF16)
    reps = ATTN_WIDTH // LANES
    cos = jnp.tile(cos_ref[...], (1, reps))
    sin = jnp.tile(sin_ref[...], (1, reps))
    q, k, v, u, sga, sgs = _inproj_core(x_ref[...], g_ref[...], wb_ref, cos, sin, HEAD_DIM ** -0.5)
    q_ref[...] = q
    k_ref[...] = k
    v_ref[...] = v
    u_ref[...] = u
    sga_ref[...] = sga.astype(BF16)
    sgs_ref[...] = sgs.astype(BF16)


def _inproj_sample(x2d, g, w_f32, cos, sin):
    m = x2d.shape[0]
    a = ATTN_WIDTH
    full = lambda shape, **kw: pl.BlockSpec(shape, lambda i: (0,) * len(shape), **kw)
    shapes = [(m, a), (m, a), (m, a), (m, SSM_WIDTH), (m, D_MODEL), (m, D_MODEL), w_f32.shape]
    dts = [F32, F32, F32, F32, BF16, BF16, BF16]
    return pl.pallas_call(
        _inproj_sample_kernel,
        grid=(1,),
        in_specs=[full(x2d.shape), full(g.shape), full(w_f32.shape, pipeline_mode=pl.Buffered(1)),
                  full(cos.shape), full(sin.shape)],
        out_specs=tuple(full(s) for s in shapes),
        out_shape=tuple(jax.ShapeDtypeStruct(s, d) for s, d in zip(shapes, dts)),
        compiler_params=_cparams(("arbitrary",)),
        name="inproj_sample",
    )(x2d, g, w_f32, cos, sin)


def _moba_prompt_kernel(qT_ref, k_ref, kmean_ref, vT_ref, o_ref, bias_ref, qz_s, m_s, acc_s):
    nb = qT_ref.shape[1]
    blk = MOBA_BLOCK
    nh = MOBA_HEADS_PER_GROUP
    gw = nh * HEAD_DIM
    ng = qT_ref.shape[2] // gw
    heads = [(g, hh) for g in range(ng) for hh in range(nh)]
    gcols = [slice(g * gw, (g + 1) * gw) for g in range(ng)]
    row2 = lax.broadcasted_iota(jnp.int32, (gw, 1), 0)
    lane_km = lax.broadcasted_iota(jnp.int32, (1, gw), 1)
    blk_row = lax.broadcasted_iota(jnp.int32, (nb, blk), 0)
    key_i = lax.broadcasted_iota(jnp.int32, (blk, blk), 0)
    qry_i = lax.broadcasted_iota(jnp.int32, (blk, blk), 1)
    causal = key_i <= qry_i
    in_head = [(row2 >= hh * HEAD_DIM) & (row2 < (hh + 1) * HEAD_DIM) for hh in range(nh)]
    hrows = [slice(h * HEAD_DIM, (h + 1) * HEAD_DIM) for h in range(ng * nh)]

    for h in range(ng * nh):
        bias_ref[h, 0] = jnp.full((nb, blk), NEG_BIG, F32)
    for g in range(ng):
        km_g = kmean_ref[0][:, gcols[g]]
        km_all = jnp.concatenate(
            [jnp.where((lane_km >= hh * HEAD_DIM) & (lane_km < (hh + 1) * HEAD_DIM), km_g, 0.0)
             for hh in range(nh)], axis=0)
        km_terms = []
        rest = km_all
        for _ in range(3):
            term = rest.astype(BF16)
            km_terms.append(term)
            rest = rest - term.astype(F32)
        km_split = jnp.concatenate(km_terms, axis=0)
        for i in range(1, nb):
            sb3 = _dot(km_split, qT_ref[0, i, gcols[g], :])
            sb_all = (sb3[0:nh * nb] + sb3[nh * nb:2 * nh * nb]) + sb3[2 * nh * nb:]
            for hh in range(nh):
                sb = jnp.where(blk_row < i, sb_all[hh * nb:(hh + 1) * nb], -jnp.inf)
                bias = jnp.full((nb, blk), NEG_BIG, F32)
                for _r in range(min(MOBA_TOPK, i)):
                    mx = jnp.max(sb, axis=0, keepdims=True)
                    first = jnp.min(jnp.where(sb == mx, blk_row, nb), axis=0, keepdims=True)
                    pick = blk_row == first
                    bias = jnp.where(pick, 0.0, bias)
                    sb = jnp.where(pick, -jnp.inf, sb)
                bias_ref[g * nh + hh, i] = bias

    n_items = nb // 2
    ones_rows = jnp.ones((MOBA_SUM_ROWS, 2 * blk), BF16)

    def couple(i, _):
        iq = (i, nb - 1 - i)
        n_first = (i + 1) // 2

        def diag_scores(x):
            rows = pl.ds(pl.multiple_of(iq[x] * blk, blk), blk)
            out = []
            for h, (g, hh) in enumerate(heads):
                q_grp = qT_ref[0, iq[x], gcols[g], :]
                qz = jnp.where(in_head[hh], q_grp, jnp.zeros_like(q_grp))
                qz_s[x, h] = qz
                out.append(_dot(k_ref[rows, gcols[g]], qz))
            return out

        def diag_absorb(x, s_own):
            for h in range(len(heads)):
                s = jnp.where(causal, s_own[h], NEG_BIG)
                m0 = jnp.max(s, axis=0, keepdims=True)
                p = jnp.exp2(s - m0)
                m_s[x, h] = m0
                vv = jnp.concatenate([vT_ref[0, iq[x], hrows[h], :], ones_rows[:, :blk]], axis=0)
                acc_s[x, h] = _dot(vv, p.astype(BF16))

        def item_params(k):
            first = k < n_first
            x = jnp.where(first, 0, 1)
            return x, jnp.where(first, iq[0], iq[1]), jnp.where(first, k, k - n_first)

        def item_scores(k):
            x, _, jp = item_params(k)
            rows = pl.ds(pl.multiple_of(2 * jp * blk, blk), 2 * blk)
            out = []
            for h, (g, _hh) in enumerate(heads):
                s = _dot(k_ref[rows, gcols[g]], qz_s[x, h])
                out.append((s, jnp.max(s[:blk], axis=0, keepdims=True), jnp.max(s[blk:], axis=0, keepdims=True)))
            return out

        def item_absorb(k, sc):
            x, qb, jp = item_params(k)
            j0 = 2 * jp
            for h in range(len(heads)):
                s, cma, cmb = sc[h]
                ba = bias_ref[h, qb, pl.ds(j0, 1), :]
                bb = bias_ref[h, qb, pl.ds(j0 + 1, 1), :]
                m = m_s[x, h]
                m_new = jnp.maximum(m, jnp.maximum(cma + ba, cmb + bb))
                alpha = jnp.exp2(m - m_new)
                pa = jnp.exp2(s[:blk] - (m_new - ba))
                pb = jnp.exp2(s[blk:] - (m_new - bb))
                m_s[x, h] = m_new
                pp = jnp.concatenate([pa, pb], axis=0).astype(BF16)
                vv = jnp.concatenate([vT_ref[0, j0, hrows[h], :], vT_ref[0, j0 + 1, hrows[h], :]], axis=1)
                vv = jnp.concatenate([vv, ones_rows], axis=0)
                acc_s[x, h] = alpha * acc_s[x, h] + _dot(vv, pp)

        s_diag = [diag_scores(0), diag_scores(1)]
        pending = [item_scores(k) for k in range(min(MOBA_LOOKAHEAD, n_items))]
        diag_absorb(0, s_diag[0])
        diag_absorb(1, s_diag[1])
        for k in range(n_items):
            if k + MOBA_LOOKAHEAD < n_items:
                pending.append(item_scores(k + MOBA_LOOKAHEAD))
            item_absorb(k, pending[k])
        for x in range(2):
            oT = jnp.concatenate([acc_s[x, h, 0:HEAD_DIM, :] / acc_s[x, h, HEAD_DIM:HEAD_DIM + 1, :]
                                  for h in range(len(heads))], axis=0)
            o_ref[pl.ds(pl.multiple_of(iq[x] * blk, blk), blk), :] = oT.T.astype(o_ref.dtype)
        return 0

    lax.fori_loop(0, nb // 2, couple, 0)


def _moba_prompt(qT, kbf, kmean, vTb, bsz, seq):
    nb = seq // MOBA_BLOCK
    hps = MOBA_HEADS_PER_GROUP * MOBA_GROUPS_PER_STEP
    hp = N_HEADS // hps
    pair = hps * HEAD_DIM
    gw = MOBA_HEADS_PER_GROUP * HEAD_DIM
    return pl.pallas_call(
        _moba_prompt_kernel,
        grid=(bsz, hp),
        in_specs=[pl.BlockSpec((1, nb, pair, MOBA_BLOCK), lambda b, h: (b, 0, h, 0)),
                  pl.BlockSpec((seq, pair), lambda b, h: (b, h)),
                  pl.BlockSpec((1, nb, pair), lambda b, h: (b, 0, h)),
                  pl.BlockSpec((1, nb, pair, MOBA_BLOCK), lambda b, h: (b, 0, h, 0))],
        out_specs=pl.BlockSpec((seq, pair), lambda b, h: (b, h)),
        out_shape=jax.ShapeDtypeStruct((bsz * seq, ATTN_WIDTH), BF16),
        scratch_shapes=[pltpu.VMEM((hps, nb, nb, MOBA_BLOCK), F32),
                        pltpu.VMEM((2, hps, gw, MOBA_BLOCK), BF16),
                        pltpu.VMEM((2, hps, 1, MOBA_BLOCK), F32),
                        pltpu.VMEM((2, hps, HEAD_DIM + MOBA_SUM_ROWS, MOBA_BLOCK), F32)],
        compiler_params=_cparams(("parallel", "parallel")),
        name="moba_prompt",
    )(qT, kbf, kmean, vTb)


def _moba_paged_kernel(pt_ref, qcol_ref, kn_ref, vn_ref, ck_ref, cv_ref, o_ref,
                       kbuf, vbuf, s_ref, psel_ref, stat_ref, qb_ref, ksem, vsem):
    b = pl.program_id(0)
    nseq = pl.num_programs(0) - 1
    n_pages = kbuf.shape[1]
    nblk = n_pages // 2
    slot = b % 2

    def k_copy(seq_i, sl, p):
        return pltpu.make_async_copy(ck_ref.at[pt_ref[seq_i, p]], kbuf.at[sl, p], ksem.at[sl])

    part_pages = n_pages // PAGED_PREFETCH_PARTS

    def start_k(seq_i, sl, part):
        def body(p, _):
            k_copy(seq_i, sl, part * part_pages + p).start()
            return 0
        lax.fori_loop(0, part_pages, body, 0)

    def v_copy(page, h, r, par):
        return pltpu.make_async_copy(cv_ref.at[page, h], vbuf.at[h, r, par], vsem.at[0])

    @pl.when(b == 0)
    def _():
        for part in range(PAGED_PREFETCH_PARTS):
            start_k(0, 0, part)

    @pl.when(b < nseq)
    def _():
        def wait_body(p, _):
            k_copy(b, slot, p).wait()
            return 0
        lax.fori_loop(0, n_pages, wait_body, 0)

    @pl.when(b + 1 < nseq)
    def _():
        start_k(b + 1, 1 - slot, 0)

    @pl.when(b >= 1)
    def _():
        for h in range(N_HEADS):
            for r in range(MOBA_TOPK):
                for par in range(2):
                    v_copy(0, h, r, par).wait()
        vn = vn_ref[0]
        for h in range(N_HEADS):
            acc = jnp.zeros((HEAD_DIM, PAGE_SIZE), F32)
            for r in range(MOBA_TOPK):
                for par in range(2):
                    acc = acc + vbuf[h, r, par] * psel_ref[h, 2 * r + par]
            p_own = stat_ref[0, h][:, 0:1]
            l = stat_ref[1, h][:, 0:1]
            o_h = jnp.sum(acc, axis=1, keepdims=True) + p_own * vn[:, h:h + 1]
            o_ref[0, :, h:h + 1] = o_h / l

    @pl.when(b < nseq)
    def _():
        qcol = qcol_ref[0]
        own = jnp.sum(qcol * kn_ref[0], axis=0, keepdims=True)
        blk_i = lax.broadcasted_iota(jnp.int32, (nblk, 1), 0)
        for h in range(N_HEADS):
            qb_ref[h] = jnp.broadcast_to(qcol[:, h:h + 1], (HEAD_DIM, PAGE_SIZE))
        for h in range(N_HEADS):
            def score_body(g, _):
                qb = qb_ref[h]
                for dn in range(PAGED_SCORE_UNROLL):
                    n = g * PAGED_SCORE_UNROLL + dn
                    for par in range(2):
                        kt = kbuf[slot, 2 * n + par, h]
                        s_ref[par, h, pl.ds(n, 1), :] = jnp.sum(kt * qb, axis=0, keepdims=True)
                return 0
            lax.fori_loop(0, nblk // PAGED_SCORE_UNROLL, score_body, 0)
            heads_per_part = N_HEADS // PAGED_PREFETCH_PARTS
            if (h + 1) % heads_per_part == 0 and (h + 1) // heads_per_part < PAGED_PREFETCH_PARTS:
                @pl.when(b + 1 < nseq)
                def _():
                    start_k(b + 1, 1 - slot, (h + 1) // heads_per_part)

        for h in range(N_HEADS):
            s0 = s_ref[0, h]
            s1 = s_ref[1, h]
            bs = jnp.sum(s0 + s1, axis=1, keepdims=True)
            sel = jnp.zeros((nblk, 1), dtype=jnp.bool_)
            picks = []
            for r in range(MOBA_TOPK):
                mx = jnp.max(bs, axis=0, keepdims=True)
                first = jnp.min(jnp.where(bs == mx, blk_i, nblk), axis=0, keepdims=True)
                pick = blk_i == first
                sel = sel | pick
                bs = jnp.where(pick, -jnp.inf, bs)
                blk_id = jnp.max(first)
                for par in range(2):
                    v_copy(pt_ref[b, 2 * blk_id + par], h, r, par).start()
                picks.append(blk_id)
            s_own = own[:, h:h + 1]
            sm0 = jnp.where(sel, s0, NEG_BIG)
            sm1 = jnp.where(sel, s1, NEG_BIG)
            mx = jnp.maximum(jnp.max(jnp.max(jnp.maximum(sm0, sm1), axis=1, keepdims=True),
                                     axis=0, keepdims=True), s_own)
            p0 = jnp.exp(sm0 - mx)
            p1 = jnp.exp(sm1 - mx)
            p_own = jnp.exp(s_own - mx)
            l = jnp.sum(jnp.sum(p0 + p1, axis=1, keepdims=True), axis=0, keepdims=True) + p_own
            s_ref[0, h] = p0
            s_ref[1, h] = p1
            for r in range(MOBA_TOPK):
                for par in range(2):
                    psel_ref[h, 2 * r + par] = s_ref[par, h, pl.ds(picks[r], 1), :]
            stat_ref[0, h] = jnp.broadcast_to(p_own, (1, PAGE_SIZE))
            stat_ref[1, h] = jnp.broadcast_to(l, (1, PAGE_SIZE))


def _moba_paged(page_table, qcol, kncol, vncol, cache_kT, cache_vT):
    nseq, n_pages = page_table.shape
    assert n_pages // 2 >= MOBA_TOPK, "every pick must find an unpicked cached block"
    cur = pl.BlockSpec((1, HEAD_DIM, N_HEADS), lambda b, pt: (jnp.minimum(b, nseq - 1), 0, 0))
    prev = pl.BlockSpec((1, HEAD_DIM, N_HEADS), lambda b, pt: (jnp.maximum(b - 1, 0), 0, 0))
    any_spec = pl.BlockSpec(memory_space=pl.ANY)
    grid_spec = pltpu.PrefetchScalarGridSpec(
        num_scalar_prefetch=1,
        grid=(nseq + 1,),
        in_specs=[cur, cur, prev, any_spec, any_spec],
        out_specs=prev,
        scratch_shapes=[
            pltpu.VMEM((2, n_pages, N_HEADS, HEAD_DIM, PAGE_SIZE), F32),
            pltpu.VMEM((N_HEADS, MOBA_TOPK, 2, HEAD_DIM, PAGE_SIZE), F32),
            pltpu.VMEM((2, N_HEADS, n_pages // 2, PAGE_SIZE), F32),
            pltpu.VMEM((N_HEADS, 2 * MOBA_TOPK, 1, PAGE_SIZE), F32),
            pltpu.VMEM((2, N_HEADS, 1, PAGE_SIZE), F32),
            pltpu.VMEM((N_HEADS, HEAD_DIM, PAGE_SIZE), F32),
            pltpu.SemaphoreType.DMA((2,)),
            pltpu.SemaphoreType.DMA((1,)),
        ])
    return pl.pallas_call(
        _moba_paged_kernel,
        grid_spec=grid_spec,
        out_shape=jax.ShapeDtypeStruct((nseq, HEAD_DIM, N_HEADS), F32),
        compiler_params=_cparams(("arbitrary",)),
        name="moba_paged",
    )(page_table, qcol, kncol, vncol, cache_kT, cache_vT)


def _ssm_tables(a_re, a_im, log_dt, b_re, b_im, c_re, c_im, chunk, n_pow):
    t_ = chunk
    lam = lax.complex(a_re.astype(F32), a_im.astype(F32))
    ldt = lam * jnp.exp(log_dt.astype(F32))[:, None]
    a_bar = jnp.exp(ldt)
    b_bar = ((a_bar - 1.0) / lam)[..., None] * lax.complex(b_re.astype(F32), b_im.astype(F32))
    c_c = lax.complex(c_re.astype(F32), c_im.astype(F32))
    taus = jnp.arange(t_ + 1, dtype=F32).astype(jnp.complex64)
    apow = jnp.exp(ldt[None] * taus[:, None, None])
    gq, ssg, c_, p_ = GROUPS_PER_SG, N_SG, SSM_GROUP, SSM_STATE
    c4 = c_c.reshape(ssg, gq, c_, p_)
    b4 = b_bar.reshape(ssg, gq, p_, c_)
    ap4 = apow.reshape(t_ + 1, ssg, gq, p_)

    kc = jnp.einsum('sjcp,tsjp,sjpd->sjdtc', c4, ap4[:t_], b4).real
    kc = kc.reshape(ssg, LANES, t_ * c_)
    mst = jnp.einsum('tsjp,sjpd->stjdp', ap4[:t_][::-1], b4)
    mc = jnp.concatenate([mst.real, mst.imag], axis=-1).reshape(ssg, t_ * LANES, 2 * p_)
    nout = jnp.einsum('sjcp,tsjp->sjptc', c4, ap4[1:])
    nc = jnp.concatenate([nout.real, -nout.imag], axis=1).reshape(ssg, 2 * SG_STATE, t_ * c_)
    rs = jnp.arange(1, n_pow + 1, dtype=F32).astype(jnp.complex64)
    ap = jnp.exp((ldt * t_)[None] * rs[:, None, None])
    ap = ap.reshape(n_pow, ssg, SG_STATE).transpose(1, 0, 2)
    a_pow = jnp.concatenate([ap.real, ap.imag], axis=-1)
    a1 = a_bar.reshape(ssg, 1, SG_STATE)
    a_one = jnp.concatenate([a1.real, a1.imag], axis=-1)
    return kc, mc, nc, a_pow, a_one


def _expand_ssm_weights(kc_ref, mc_ref, nc_ref, wt_s, ws_s, wo_s):
    t_ = SSM_CHUNK
    kw = t_ * LANES
    lg = lambda n: n.bit_length() - 1
    r = lax.broadcasted_iota(jnp.int32, (LANES, kw), 0)
    q = lax.broadcasted_iota(jnp.int32, (LANES, kw), 1)
    sh_c, sh_p = lg(SSM_GROUP), lg(SSM_STATE)
    ecol = jnp.where(((r >> sh_c) == (q >> lg(LANES))) & ((r & (SSM_GROUP - 1)) == (q & (SSM_GROUP - 1))),
                     1.0, 0.0).astype(BF16)
    e2 = jnp.where(((r >> sh_p) == (q >> lg(SG_STATE))) & ((r & (SSM_STATE - 1)) == (q & (SSM_STATE - 1))),
                   1.0, 0.0).astype(BF16)
    col_c = (q & (LANES - 1)) >> sh_c
    taps = jnp.where((r >> sh_c) == col_c, _dot(kc_ref[0].astype(BF16), ecol), 0.0).astype(BF16)
    blank = jnp.zeros((LANES, LANES), BF16)
    for s in range(t_):
        for t in range(t_):
            wt_s[s * LANES:(s + 1) * LANES, t * LANES:(t + 1) * LANES] = (
                taps[:, (t - s) * LANES:(t - s + 1) * LANES] if t >= s else blank)
    rr = lax.broadcasted_iota(jnp.int32, (kw, kw), 0)
    qq = lax.broadcasted_iota(jnp.int32, (kw, kw), 1)
    row_c = (rr & (LANES - 1)) >> sh_c
    row_p = (rr & (SG_STATE - 1)) >> sh_p
    colq_c = (qq & (LANES - 1)) >> sh_c
    colq_p = (qq & (SG_STATE - 1)) >> sh_p
    ws_s[...] = jnp.where(row_c == colq_p, _dot(mc_ref[0].astype(BF16), e2), 0.0).astype(BF16)
    wo_s[...] = jnp.where(row_p == colq_c, _dot(nc_ref[0].astype(BF16), ecol), 0.0).astype(BF16)


def _ssm_prompt_kernel(u_ref, kc_ref, mc_ref, nc_ref, ap_ref, d_ref, h0_ref,
                       y_ref, hf_ref, uc_ref, x_ref, hin_ref, wt_s, ws_s, wo_s):
    t_ = SSM_CHUNK
    nc = uc_ref.shape[0]
    ns = SG_STATE

    @pl.when(pl.program_id(1) == 0)
    def _():
        _expand_ssm_weights(kc_ref, mc_ref, nc_ref, wt_s, ws_s, wo_s)

    for t in range(t_):
        uc_ref[:, t * LANES:(t + 1) * LANES] = u_ref[pl.ds(t, nc, stride=t_), :].astype(BF16)
    uc = uc_ref[...]
    x_ref[...] = _dot(uc, ws_s[...])

    ap = ap_ref[0]
    apr, api = ap[:, :ns], ap[:, ns:]
    row = lax.broadcasted_iota(jnp.int32, (SUBLANES, 1), 0)

    def tile_body(k, carry):
        hr, hi = carry
        rows = pl.ds(k * SUBLANES, SUBLANES)
        xr = x_ref[rows, 0:ns]
        xi = x_ref[rows, ns:2 * ns]
        for d in (1, 2, 4):
            ar, ai = apr[d - 1:d], api[d - 1:d]
            sr = jnp.where(row >= d, pltpu.roll(xr, d, 0), 0.0)
            si = jnp.where(row >= d, pltpu.roll(xi, d, 0), 0.0)
            xr, xi = xr + ar * sr - ai * si, xi + ar * si + ai * sr
        outr = xr + apr * hr - api * hi
        outi = xi + apr * hi + api * hr
        hin_ref[rows, 0:ns] = jnp.where(row >= 1, pltpu.roll(outr, 1, 0), hr)
        hin_ref[rows, ns:2 * ns] = jnp.where(row >= 1, pltpu.roll(outi, 1, 0), hi)
        return outr[SUBLANES - 1:SUBLANES], outi[SUBLANES - 1:SUBLANES]

    cols = []
    for nt in range(t_ * LANES // MXU_COLS):
        acc = None
        for kt in range(nt + 1):
            part = _dot(uc[:, kt * MXU_COLS:(kt + 1) * MXU_COLS],
                        wt_s[kt * MXU_COLS:(kt + 1) * MXU_COLS, nt * MXU_COLS:(nt + 1) * MXU_COLS])
            acc = part if acc is None else acc + part
        cols.append(acc)
    y_intra = jnp.concatenate(cols, axis=1)
    h0 = h0_ref[0, 0]
    carry = (h0[:, :ns], h0[:, ns:])
    for k in range(nc // SUBLANES):
        carry = tile_body(k, carry)
    hf_ref[0, 0] = jnp.concatenate(carry, axis=1)

    y = y_intra + _dot(hin_ref[...].astype(BF16), wo_s[...])
    dv = d_ref[...]
    for t in range(t_):
        rows = pl.ds(t, nc, stride=t_)
        y_ref[rows, :] = y[:, t * LANES:(t + 1) * LANES] + dv * u_ref[rows, :]


def _ssm_prompt(u, kc, mc, nc_tab, a_pow, d_row, h0, bsz, seq):
    t_ = SSM_CHUNK
    nc = seq // t_
    kw = t_ * LANES
    assert kc.shape[2] == LANES and 2 * SG_STATE == kw
    wspec = lambda shape: pl.BlockSpec((1,) + shape, lambda s, b: (s, 0, 0))
    return pl.pallas_call(
        _ssm_prompt_kernel,
        grid=(N_SG, bsz),
        in_specs=[pl.BlockSpec((seq, LANES), lambda s, b: (b, s)),
                  wspec(kc.shape[1:]), wspec(mc.shape[1:]), wspec(nc_tab.shape[1:]),
                  wspec((SUBLANES, 2 * SG_STATE)),
                  pl.BlockSpec((1, LANES), lambda s, b: (0, s)),
                  pl.BlockSpec((1, 1, 1, 2 * SG_STATE), lambda s, b: (b, s, 0, 0))],
        out_specs=(pl.BlockSpec((seq, LANES), lambda s, b: (b, s)),
                   pl.BlockSpec((1, 1, 1, 2 * SG_STATE), lambda s, b: (b, s, 0, 0))),
        out_shape=(jax.ShapeDtypeStruct((bsz * seq, SSM_WIDTH), F32),
                   jax.ShapeDtypeStruct((bsz, N_SG, 1, 2 * SG_STATE), F32)),
        scratch_shapes=[pltpu.VMEM((nc, kw), BF16),
                        pltpu.VMEM((nc, 2 * SG_STATE), F32),
                        pltpu.VMEM((nc, 2 * SG_STATE), F32),
                        pltpu.VMEM((kw, kw), BF16),
                        pltpu.VMEM((kw, 2 * SG_STATE), BF16),
                        pltpu.VMEM((2 * SG_STATE, kw), BF16)],
        compiler_params=_cparams(("arbitrary", "arbitrary")),
        name="ssm_prompt",
    )(u, kc, mc, nc_tab, a_pow, d_row, h0)


def _ssm_step_kernel(u_ref, kc_ref, mc_ref, nc_ref, a1_ref, d_ref, h0_ref, y_ref, hf_ref):
    ns = SG_STATE
    kw = 2 * ns
    lg = lambda n: n.bit_length() - 1
    sh_c, sh_p = lg(SSM_GROUP), lg(SSM_STATE)
    iota = lambda shape, d: lax.broadcasted_iota(jnp.int32, shape, d)
    r1, q1 = iota((LANES, LANES), 0), iota((LANES, LANES), 1)
    e_c = jnp.where((r1 < SSM_GROUP) & ((r1 & (SSM_GROUP - 1)) == (q1 & (SSM_GROUP - 1))), 1.0, 0.0).astype(BF16)
    r2, q2 = iota((LANES, kw), 0), iota((LANES, kw), 1)
    e_p = jnp.where(((r2 >> sh_p) == (q2 >> lg(ns))) & ((r2 & (SSM_STATE - 1)) == (q2 & (SSM_STATE - 1))),
                    1.0, 0.0).astype(BF16)
    same_t = (r1 >> sh_c) == (q1 >> sh_c)
    same_s = (r2 >> sh_c) == ((q2 & (ns - 1)) >> sh_p)
    r3, q3 = iota((kw, LANES), 0), iota((kw, LANES), 1)
    same_o = ((r3 & (ns - 1)) >> sh_p) == (q3 >> sh_c)
    last = (SSM_CHUNK - 1) * LANES
    for s in range(N_SG):
        wt = jnp.where(same_t, _dot(kc_ref[s].astype(BF16), e_c), 0.0).astype(BF16)
        ws = jnp.where(same_s, _dot(mc_ref[s, last:last + LANES, :].astype(BF16), e_p), 0.0).astype(BF16)
        wo = jnp.where(same_o, _dot(nc_ref[s].astype(BF16), e_c), 0.0).astype(BF16)
        us = u_ref[:, s * LANES:(s + 1) * LANES]
        ub = us.astype(BF16)
        h0 = h0_ref[s]
        x = _dot(ub, ws)
        ar, ai = a1_ref[s, :, :ns], a1_ref[s, :, ns:]
        hr, hi = h0[:, :ns], h0[:, ns:]
        hf_ref[s] = jnp.concatenate([x[:, :ns] + ar * hr - ai * hi,
                                     x[:, ns:] + ar * hi + ai * hr], axis=1)
        y = _dot(ub, wt) + _dot(h0.astype(BF16), wo)
        y_ref[:, s * LANES:(s + 1) * LANES] = y + d_ref[:, s * LANES:(s + 1) * LANES] * us


def _ssm_step(u, kc, mc, nc_tab, a_one, d_row, h0):
    m = u.shape[0]
    full = lambda a: pl.BlockSpec(a.shape, lambda i: (0,) * a.ndim)
    args = (u, kc, mc, nc_tab, a_one, d_row, h0)
    shapes = [(m, SSM_WIDTH), (N_SG, m, 2 * SG_STATE)]
    return pl.pallas_call(
        _ssm_step_kernel,
        grid=(1,),
        in_specs=[full(a) for a in args],
        out_specs=tuple(pl.BlockSpec(s, lambda i, n=len(s): (0,) * n) for s in shapes),
        out_shape=tuple(jax.ShapeDtypeStruct(s, F32) for s in shapes),
        compiler_params=_cparams(("arbitrary",)),
        name="ssm_step",
    )(*args)


FFN_SPLIT = FFN_HIDDEN // MXU_COLS
FFN_CHUNK = FFN_HIDDEN // FFN_SPLIT


def _post_kernel(x_ref, o_ref, y_ref, sga_ref, sgs_ref, wap_ref, wglu_ref, bglu_ref, wsp_ref,
                 wout_ref, nffn_ref, wfi_ref, wfo_ref, nfin_ref, out_ref):
    z = jax.nn.gelu(y_ref[...])
    z = z * jax.nn.sigmoid(_dot(z.astype(BF16), wglu_ref[...]) + bglu_ref[...])
    ssm_out = _dot(z.astype(BF16), wsp_ref[...])
    attn_out = _dot(o_ref[...].astype(BF16), wap_ref[...])
    merged = sga_ref[...].astype(F32) * attn_out + sgs_ref[...].astype(F32) * ssm_out
    x1 = x_ref[...] + _dot(merged.astype(BF16), wout_ref[...])
    hf = _rms(x1, nffn_ref[...]).astype(BF16)
    acc = x1
    for c in range(FFN_SPLIT):
        lo = c * FFN_CHUNK
        a = _dot(hf, wfi_ref[:, lo:lo + FFN_CHUNK])
        g = _dot(hf, wfi_ref[:, FFN_HIDDEN + lo:FFN_HIDDEN + lo + FFN_CHUNK])
        act = (jax.nn.silu(a) * g).astype(BF16)
        acc = acc + _dot(act, wfo_ref[lo:lo + FFN_CHUNK, :])
    out_ref[...] = _rms(acc, nfin_ref[...])


def _post(x2d, o, y, sga, sgs, wap, wglu, bglu, wsp, wout, nffn, wfi, wfo, nfin, tm):
    m = x2d.shape[0]
    tok = lambda w: pl.BlockSpec((tm, w), lambda i: (i, 0))
    const = lambda a: pl.BlockSpec(a.shape, lambda i: (0,) * a.ndim, pipeline_mode=pl.Buffered(1))
    weights = (wap, wglu, bglu, wsp, wout, nffn, wfi, wfo, nfin)
    return pl.pallas_call(
        _post_kernel,
        grid=(m // tm,),
        in_specs=[tok(D_MODEL), tok(ATTN_WIDTH), tok(SSM_WIDTH), tok(D_MODEL), tok(D_MODEL)]
                 + [const(w) for w in weights],
        out_specs=tok(D_MODEL),
        out_shape=jax.ShapeDtypeStruct((m, D_MODEL), F32),
        compiler_params=_cparams(("parallel",)),
        name="post",
    )(x2d, o, y, sga, sgs, *weights)


def _rope_tables(pos):
    half = HEAD_DIM // 2
    inv = jnp.power(jnp.float32(ROPE_THETA), -2.0 * jnp.arange(half, dtype=F32) / HEAD_DIM)
    ang = pos.astype(F32)[:, None] * inv[None, :]
    cos, sin = jnp.cos(ang), jnp.sin(ang)
    reps = LANES // HEAD_DIM
    return (jnp.tile(jnp.concatenate([cos, cos], axis=1), (1, reps)),
            jnp.tile(jnp.concatenate([-sin, sin], axis=1), (1, reps)))


def _leaf_from_T(xT, bsz, seq):
    return xT.reshape(bsz, N_HEADS, HEAD_DIM, seq).transpose(0, 3, 1, 2)[None]


def _state_in(re, im):
    n = re.shape[0]
    h = jnp.concatenate([re.reshape(n, N_SG, SG_STATE), im.reshape(n, N_SG, SG_STATE)], axis=-1)
    return h.transpose(1, 0, 2)


def _state_out(h):
    n = h.shape[0]
    return (h[..., :SG_STATE].reshape(1, n, SSM_GROUPS, SSM_STATE),
            h[..., SG_STATE:].reshape(1, n, SSM_GROUPS, SSM_STATE))


def kernel(x_prompt, x_sample, cache_k, cache_v, state_ssm_re, state_ssm_im, page_table, norm_mix, w_in,
           w_attn_proj, ssm_a_re, ssm_a_im, ssm_log_dt, ssm_b_re, ssm_b_im, ssm_c_re, ssm_c_im, ssm_d, w_glu,
           b_glu, w_ssm_proj, w_out, norm_ffn, w_ffn_in, w_ffn_out, norm_final):
    assert w_in.shape[0] == 1, "single layer"
    bsz, seq = x_prompt.shape[:2]
    nseq = x_sample.shape[0]
    past_len = page_table.shape[1] * PAGE_SIZE
    assert seq % MOBA_BLOCK == 0 and past_len % MOBA_BLOCK == 0 and x_sample.shape[1] == 1

    ssm_p = (ssm_a_re[0], ssm_a_im[0], ssm_log_dt[0], ssm_b_re[0], ssm_b_im[0], ssm_c_re[0], ssm_c_im[0])

    cos_s, sin_s = _rope_tables(jnp.full((1,), past_len, dtype=jnp.int32))
    xs2 = x_sample.reshape(nseq, D_MODEL)
    q_s, k_s, v_s, u_s, sga_s, sgs_s, w_in_bf = _inproj_sample(xs2, norm_mix, w_in[0], cos_s, sin_s)

    cos_p, sin_p = _rope_tables(jnp.arange(seq, dtype=jnp.int32))
    xp2 = x_prompt.reshape(bsz * seq, D_MODEL)
    cast_ws = (w_attn_proj[0], w_glu[0], w_ssm_proj[0], w_out[0], w_ffn_in[0], w_ffn_out[0])
    (qT, kT, kbf, kmean, vT, vTb, u_p, sga_p, sgs_p,
     wap_bf, wglu_bf, wsp_bf, wout_bf, wfi_bf, wfo_bf) = _inproj_prompt(
        xp2, norm_mix, w_in_bf, cos_p, sin_p, cast_ws, bsz, seq, tm=512)
    post_w = (wap_bf, wglu_bf, b_glu, wsp_bf, wout_bf, norm_ffn, wfi_bf, wfo_bf, norm_final[None])
    o_p = _moba_prompt(qT, kbf, kmean.reshape(bsz, seq // MOBA_BLOCK, ATTN_WIDTH), vTb, bsz, seq)
    kc, mc, nc_tab, ap, a_one = _ssm_tables(*ssm_p, chunk=SSM_CHUNK, n_pow=SUBLANES)
    h0_p = jnp.zeros((bsz, N_SG, 1, 2 * SG_STATE), F32)
    y_p, hf_p = _ssm_prompt(u_p, kc, mc, nc_tab, ap, ssm_d, h0_p, bsz, seq)
    y_prompt = _post(xp2, o_p, y_p, sga_p, sgs_p, *post_w, tm=512).reshape(bsz, seq, D_MODEL)
    new_ssm_re_p, new_ssm_im_p = _state_out(hf_p.reshape(bsz, N_SG, 2 * SG_STATE))

    cache_kT = cache_k[0].transpose(0, 2, 3, 1)
    cache_vT = cache_v[0].transpose(0, 2, 3, 1)
    dh = lambda t: t.reshape(nseq, N_HEADS, HEAD_DIM).transpose(0, 2, 1)
    o_s = _moba_paged(page_table, dh(q_s), dh(k_s), dh(v_s), cache_kT, cache_vT).transpose(0, 2, 1)
    h0_s = _state_in(state_ssm_re[0], state_ssm_im[0])
    y_s, hf_s = _ssm_step(u_s, kc, mc, nc_tab, a_one, ssm_d, h0_s)
    y_sample = _post(xs2, o_s.reshape(nseq, ATTN_WIDTH), y_s, sga_s, sgs_s, *post_w, tm=nseq)
    new_ssm_re_s, new_ssm_im_s = _state_out(hf_s.transpose(1, 0, 2))

    return (y_prompt, y_sample.reshape(nseq, 1, D_MODEL),
            _leaf_from_T(kT, bsz, seq), _leaf_from_T(vT, bsz, seq), new_ssm_re_p, new_ssm_im_p,
            k_s.reshape(1, nseq, 1, N_HEADS, HEAD_DIM), v_s.reshape(1, nseq, 1, N_HEADS, HEAD_DIM),
            new_ssm_re_s, new_ssm_im_s)
```

```python
import functools
import math

import jax
import jax.numpy as jnp
from jax import lax
from jax.experimental import pallas as pl
from jax.experimental.pallas import tpu as pltpu

F32 = jnp.float32
BF16 = jnp.bfloat16

D_MODEL = 1024
N_HEADS = 8
HEAD_DIM = 64
ATTN_WIDTH = N_HEADS * HEAD_DIM
MOBA_BLOCK = 256
MOBA_TOPK = 3
ROPE_THETA = 10000.0
SSM_WIDTH = 512
SSM_GROUP = 16
SSM_GROUPS = 32
SSM_STATE = 64
FFN_HIDDEN = 2816
RMS_EPS = 1e-6
PAGE_SIZE = 128

LANES = 128
SUBLANES = 8
BF16_SUBLANES = 16
MXU_COLS = 256
GROUPS_PER_SG = LANES // SSM_GROUP
N_SG = SSM_GROUPS // GROUPS_PER_SG
SG_STATE = GROUPS_PER_SG * SSM_STATE
SSM_CHUNK = 8
NEG_BIG = -1e30
MOBA_GROUPS_PER_STEP = 1
MOBA_HEADS_PER_GROUP = 4
MOBA_LOOKAHEAD = 2
LOG2E = math.log2(math.e)
MOBA_SUM_ROWS = 16
PAGED_SCORE_UNROLL = 4
VMEM_LIMIT = 60 * 1024 * 1024


def _cparams(sem):
    return pltpu.CompilerParams(dimension_semantics=sem, vmem_limit_bytes=VMEM_LIMIT)


def _dot(a, b):
    return jnp.dot(a, b, preferred_element_type=F32)


def _rms(x, g):
    return x * lax.rsqrt(jnp.mean(x * x, axis=-1, keepdims=True) + RMS_EPS) * g


def _inproj_core(x, g, w_ref, cos, sin, q_scale):
    h = _rms(x, g).astype(BF16)
    lane = lax.broadcasted_iota(jnp.int32, (1, ATTN_WIDTH), 1)
    first_half = (lane % HEAD_DIM) < (HEAD_DIM // 2)

    def rot(t):
        partner = jnp.where(first_half,
                            pltpu.roll(t, ATTN_WIDTH - HEAD_DIM // 2, 1),
                            pltpu.roll(t, HEAD_DIM // 2, 1))
        return t * cos + partner * sin

    a = ATTN_WIDTH
    q = rot(_dot(h, w_ref[:, 0:a])) * q_scale
    k = rot(_dot(h, w_ref[:, a:2 * a]))
    v = _dot(h, w_ref[:, 2 * a:3 * a])
    u = _dot(h, w_ref[:, 3 * a:3 * a + SSM_WIDTH])
    o = 3 * a + SSM_WIDTH
    sga = jax.nn.sigmoid(_dot(h, w_ref[:, o:o + D_MODEL]))
    sgs = jax.nn.sigmoid(_dot(h, w_ref[:, o + D_MODEL:o + 2 * D_MODEL]))
    return q, k, v, u, sga, sgs


def _inproj_prompt_kernel(x_ref, g_ref, w_ref, cos_ref, sin_ref, *rest):
    n_cast = (len(rest) - 9) // 2
    cast_in = rest[:n_cast]
    qT_ref, kT_ref, kbf_ref, kmean_ref, vT_ref, vTb_ref, u_ref, sga_ref, sgs_ref = rest[n_cast:n_cast + 9]
    cast_out = rest[n_cast + 9:]
    for wi_ref, wo_ref in zip(cast_in, cast_out):
        wo_ref[...] = wi_ref[...].astype(BF16)
    reps = ATTN_WIDTH // LANES
    cos = jnp.tile(cos_ref[...], (1, reps))
    sin = jnp.tile(sin_ref[...], (1, reps))
    q, k, v, u, sga, sgs = _inproj_core(x_ref[...], g_ref[...], w_ref, cos, sin, LOG2E * HEAD_DIM ** -0.5)
    tm = q.shape[0]
    kT_ref[0] = k.T
    vT_ref[0] = v.T
    kbf_ref[...] = k.astype(BF16)
    for s in range(tm // MOBA_BLOCK):
        rows = slice(s * MOBA_BLOCK, (s + 1) * MOBA_BLOCK)
        qT_ref[0, s] = q[rows].T.astype(BF16)
        vTb_ref[0, s] = v[rows].T.astype(BF16)
        kmean_ref[0, s] = jnp.mean(k[rows], axis=0, keepdims=True)
    u_ref[...] = u
    sga_ref[...] = sga.astype(BF16)
    sgs_ref[...] = sgs.astype(BF16)


def _inproj_prompt(x2d, g, w_bf, cos, sin, cast_ws, bsz, seq, tm):
    m = bsz * seq
    nb = seq // MOBA_BLOCK
    tpb = seq // tm
    sub = tm // MOBA_BLOCK
    a = ATTN_WIDTH
    full = lambda shape: pl.BlockSpec(shape, lambda b, t: (0,) * len(shape))
    tok = lambda w: pl.BlockSpec((tm, w), lambda b, t: (b * tpb + t, 0))
    out_shape = (
        jax.ShapeDtypeStruct((bsz, nb, a, MOBA_BLOCK), BF16),
        jax.ShapeDtypeStruct((bsz, a, seq), F32),
        jax.ShapeDtypeStruct((m, a), BF16),
        jax.ShapeDtypeStruct((bsz, nb, 1, a), F32),
        jax.ShapeDtypeStruct((bsz, a, seq), F32),
        jax.ShapeDtypeStruct((bsz, nb, a, MOBA_BLOCK), BF16),
        jax.ShapeDtypeStruct((m, SSM_WIDTH), F32),
        jax.ShapeDtypeStruct((m, D_MODEL), BF16),
        jax.ShapeDtypeStruct((m, D_MODEL), BF16),
    )
    blk_t = pl.BlockSpec((1, sub, a, MOBA_BLOCK), lambda b, t: (b, t, 0, 0))
    lane_t = pl.BlockSpec((1, a, tm), lambda b, t: (b, 0, t))
    out_specs = (blk_t, lane_t, tok(a),
                 pl.BlockSpec((1, sub, 1, a), lambda b, t: (b, t, 0, 0)),
                 lane_t, blk_t, tok(SSM_WIDTH), tok(D_MODEL), tok(D_MODEL))
    n_steps = bsz * tpb
    cast_specs = []
    for w in cast_ws:
        rows = w.shape[0]
        nblk = n_steps
        while rows % nblk or (rows // nblk) % BF16_SUBLANES:
            nblk //= 2
        cast_specs.append(pl.BlockSpec((rows // nblk, w.shape[1]),
                                       lambda b, t, n=nblk: (jnp.minimum(b * tpb + t, n - 1), 0)))
    return pl.pallas_call(
        _inproj_prompt_kernel,
        grid=(bsz, tpb),
        in_specs=[tok(D_MODEL), full((1, D_MODEL)),
                  pl.BlockSpec(w_bf.shape, lambda b, t: (0, 0), pipeline_mode=pl.Buffered(1)),
                  pl.BlockSpec((tm, LANES), lambda b, t: (t, 0)),
                  pl.BlockSpec((tm, LANES), lambda b, t: (t, 0))] + cast_specs,
        out_specs=out_specs + tuple(cast_specs),
        out_shape=out_shape + tuple(jax.ShapeDtypeStruct(w.shape, BF16) for w in cast_ws),
        compiler_params=_cparams(("arbitrary", "arbitrary")),
        name="inproj_prompt",
    )(x2d, g, w_bf, cos, sin, *cast_ws)


def _inproj_sample_kernel(x_ref, g_ref, w_ref, cos_ref, sin_ref,
                          q_ref, k_ref, v_ref, u_ref, sga_ref, sgs_ref, wb_ref):
    step = ATTN_WIDTH
    for c in range(w_ref.shape[1] // step):
        wb_ref[:, c * step:(c + 1) * step] = w_ref[:, c * step:(c + 1) * step].astype(BF16)
    reps = ATTN_WIDTH // LANES
    cos = jnp.tile(cos_ref[...], (1, reps))
    sin = jnp.tile(sin_ref[...], (1, reps))
    q, k, v, u, sga, sgs = _inproj_core(x_ref[...], g_ref[...], wb_ref, cos, sin, HEAD_DIM ** -0.5)
    q_ref[...] = q
    k_ref[...] = k
    v_ref[...] = v
    u_ref[...] = u
    sga_ref[...] = sga.astype(BF16)
    sgs_ref[...] = sgs.astype(BF16)


def _inproj_sample(x2d, g, w_f32, cos, sin):
    m = x2d.shape[0]
    a = ATTN_WIDTH
    full = lambda shape, **kw: pl.BlockSpec(shape, lambda i: (0,) * len(shape), **kw)
    shapes = [(m, a), (m, a), (m, a), (m, SSM_WIDTH), (m, D_MODEL), (m, D_MODEL), w_f32.shape]
    dts = [F32, F32, F32, F32, BF16, BF16, BF16]
    return pl.pallas_call(
        _inproj_sample_kernel,
        grid=(1,),
        in_specs=[full(x2d.shape), full(g.shape), full(w_f32.shape, pipeline_mode=pl.Buffered(1)),
                  full(cos.shape), full(sin.shape)],
        out_specs=tuple(full(s) for s in shapes),
        out_shape=tuple(jax.ShapeDtypeStruct(s, d) for s, d in zip(shapes, dts)),
        compiler_params=_cparams(("arbitrary",)),
        name="inproj_sample",
    )(x2d, g, w_f32, cos, sin)


def _moba_prompt_kernel(qT_ref, k_ref, kmean_ref, vT_ref, o_ref, bias_ref, qz_s, m_s, acc_s):
    nb = qT_ref.shape[1]
    blk = MOBA_BLOCK
    nh = MOBA_HEADS_PER_GROUP
    gw = nh * HEAD_DIM
    ng = qT_ref.shape[2] // gw
    heads = [(g, hh) for g in range(ng) for hh in range(nh)]
    gcols = [slice(g * gw, (g + 1) * gw) for g in range(ng)]
    row2 = lax.broadcasted_iota(jnp.int32, (gw, 1), 0)
    lane_km = lax.broadcasted_iota(jnp.int32, (1, gw), 1)
    blk_row = lax.broadcasted_iota(jnp.int32, (nb, blk), 0)
    key_i = lax.broadcasted_iota(jnp.int32, (blk, blk), 0)
    qry_i = lax.broadcasted_iota(jnp.int32, (blk, blk), 1)
    causal = key_i <= qry_i
    in_head = [(row2 >= hh * HEAD_DIM) & (row2 < (hh + 1) * HEAD_DIM) for hh in range(nh)]
    hrows = [slice(h * HEAD_DIM, (h + 1) * HEAD_DIM) for h in range(ng * nh)]

    for h in range(ng * nh):
        bias_ref[h, 0] = jnp.full((nb, blk), NEG_BIG, F32)
    for g in range(ng):
        km_g = kmean_ref[0][:, gcols[g]]
        km_all = jnp.concatenate(
            [jnp.where((lane_km >= hh * HEAD_DIM) & (lane_km < (hh + 1) * HEAD_DIM), km_g, 0.0)
             for hh in range(nh)], axis=0)
        km_terms = []
        rest = km_all
        for _ in range(3):
            term = rest.astype(BF16)
            km_terms.append(term)
            rest = rest - term.astype(F32)
        km_split = jnp.concatenate(km_terms, axis=0)
        for i in range(1, nb):
            sb3 = _dot(km_split, qT_ref[0, i, gcols[g], :])
            sb_all = (sb3[0:nh * nb] + sb3[nh * nb:2 * nh * nb]) + sb3[2 * nh * nb:]
            for hh in range(nh):
                sb = jnp.where(blk_row < i, sb_all[hh * nb:(hh + 1) * nb], -jnp.inf)
                bias = jnp.full((nb, blk), NEG_BIG, F32)
                for _r in range(min(MOBA_TOPK, i)):
                    mx = jnp.max(sb, axis=0, keepdims=True)
                    first = jnp.min(jnp.where(sb == mx, blk_row, nb), axis=0, keepdims=True)
                    pick = blk_row == first
                    bias = jnp.where(pick, 0.0, bias)
                    sb = jnp.where(pick, -jnp.inf, sb)
                bias_ref[g * nh + hh, i] = bias

    n_items = nb // 2
    ones_rows = jnp.ones((MOBA_SUM_ROWS, 2 * blk), BF16)

    def couple(i, _):
        iq = (i, nb - 1 - i)
        n_first = (i + 1) // 2

        def diag_scores(x):
            rows = pl.ds(pl.multiple_of(iq[x] * blk, blk), blk)
            out = []
            for h, (g, hh) in enumerate(heads):
                q_grp = qT_ref[0, iq[x], gcols[g], :]
                qz = jnp.where(in_head[hh], q_grp, jnp.zeros_like(q_grp))
                qz_s[x, h] = qz
                out.append(_dot(k_ref[rows, gcols[g]], qz))
            return out

        def diag_absorb(x, s_own):
            for h in range(len(heads)):
                s = jnp.where(causal, s_own[h], NEG_BIG)
                m0 = jnp.max(s, axis=0, keepdims=True)
                p = jnp.exp2(s - m0)
                m_s[x, h] = m0
                vv = jnp.concatenate([vT_ref[0, iq[x], hrows[h], :], ones_rows[:, :blk]], axis=0)
                acc_s[x, h] = _dot(vv, p.astype(BF16))

        def item_params(k):
            first = k < n_first
            x = jnp.where(first, 0, 1)
            return x, jnp.where(first, iq[0], iq[1]), jnp.where(first, k, k - n_first)

        def item_scores(k):
            x, _, jp = item_params(k)
            rows = pl.ds(pl.multiple_of(2 * jp * blk, blk), 2 * blk)
            out = []
            for h, (g, _hh) in enumerate(heads):
                s = _dot(k_ref[rows, gcols[g]], qz_s[x, h])
                out.append((s, jnp.max(s[:blk], axis=0, keepdims=True), jnp.max(s[blk:], axis=0, keepdims=True)))
            return out

        def item_absorb(k, sc):
            x, qb, jp = item_params(k)
            j0 = 2 * jp
            for h in range(len(heads)):
                s, cma, cmb = sc[h]
                ba = bias_ref[h, qb, pl.ds(j0, 1), :]
                bb = bias_ref[h, qb, pl.ds(j0 + 1, 1), :]
                m = m_s[x, h]
                m_new = jnp.maximum(m, jnp.maximum(cma + ba, cmb + bb))
                alpha = jnp.exp2(m - m_new)
                pa = jnp.exp2(s[:blk] - (m_new - ba))
                pb = jnp.exp2(s[blk:] - (m_new - bb))
                m_s[x, h] = m_new
                pp = jnp.concatenate([pa, pb], axis=0).astype(BF16)
                vv = jnp.concatenate([vT_ref[0, j0, hrows[h], :], vT_ref[0, j0 + 1, hrows[h], :]], axis=1)
                vv = jnp.concatenate([vv, ones_rows], axis=0)
                acc_s[x, h] = alpha * acc_s[x, h] + _dot(vv, pp)

        s_diag = [diag_scores(0), diag_scores(1)]
        pending = [item_scores(k) for k in range(min(MOBA_LOOKAHEAD, n_items))]
        diag_absorb(0, s_diag[0])
        diag_absorb(1, s_diag[1])
        for k in range(n_items):
            if k + MOBA_LOOKAHEAD < n_items:
                pending.append(item_scores(k + MOBA_LOOKAHEAD))
            item_absorb(k, pending[k])
        for x in range(2):
            oT = jnp.concatenate([acc_s[x, h, 0:HEAD_DIM, :] / acc_s[x, h, HEAD_DIM:HEAD_DIM + 1, :]
                                  for h in range(len(heads))], axis=0)
            o_ref[pl.ds(pl.multiple_of(iq[x] * blk, blk), blk), :] = oT.T.astype(o_ref.dtype)
        return 0

    lax.fori_loop(0, nb // 2, couple, 0)


def _moba_prompt(qT, kbf, kmean, vTb, bsz, seq):
    nb = seq // MOBA_BLOCK
    hps = MOBA_HEADS_PER_GROUP * MOBA_GROUPS_PER_STEP
    hp = N_HEADS // hps
    pair = hps * HEAD_DIM
    gw = MOBA_HEADS_PER_GROUP * HEAD_DIM
    return pl.pallas_call(
        _moba_prompt_kernel,
        grid=(bsz, hp),
        in_specs=[pl.BlockSpec((1, nb, pair, MOBA_BLOCK), lambda b, h: (b, 0, h, 0)),
                  pl.BlockSpec((seq, pair), lambda b, h: (b, h)),
                  pl.BlockSpec((1, nb, pair), lambda b, h: (b, 0, h)),
                  pl.BlockSpec((1, nb, pair, MOBA_BLOCK), lambda b, h: (b, 0, h, 0))],
        out_specs=pl.BlockSpec((seq, pair), lambda b, h: (b, h)),
        out_shape=jax.ShapeDtypeStruct((bsz * seq, ATTN_WIDTH), BF16),
        scratch_shapes=[pltpu.VMEM((hps, nb, nb, MOBA_BLOCK), F32),
                        pltpu.VMEM((2, hps, gw, MOBA_BLOCK), BF16),
                        pltpu.VMEM((2, hps, 1, MOBA_BLOCK), F32),
                        pltpu.VMEM((2, hps, HEAD_DIM + MOBA_SUM_ROWS, MOBA_BLOCK), F32)],
        compiler_params=_cparams(("parallel", "parallel")),
        name="moba_prompt",
    )(qT, kbf, kmean, vTb)


def _moba_paged_kernel(pt_ref, qcol_ref, kn_ref, vn_ref, ck_ref, cv_ref, o_ref,
                       kbuf, vbuf, s_ref, psel_ref, stat_ref, qb_ref, ksem, vsem):
    b = pl.program_id(0)
    nseq = pl.num_programs(0) - 1
    n_pages = kbuf.shape[1]
    nblk = n_pages // 2
    slot = b % 2

    def k_copy(seq_i, sl, p):
        return pltpu.make_async_copy(ck_ref.at[pt_ref[seq_i, p]], kbuf.at[sl, p], ksem.at[sl])

    def start_k(seq_i, sl):
        def body(p, _):
            k_copy(seq_i, sl, p).start()
            return 0
        lax.fori_loop(0, n_pages, body, 0)

    def v_copy(page, h, r, par):
        return pltpu.make_async_copy(cv_ref.at[page, h], vbuf.at[h, r, par], vsem.at[0])

    @pl.when(b == 0)
    def _():
        start_k(0, 0)

    @pl.when(b < nseq)
    def _():
        def wait_body(p, _):
            k_copy(b, slot, p).wait()
            return 0
        lax.fori_loop(0, n_pages, wait_body, 0)

    @pl.when(b + 1 < nseq)
    def _():
        start_k(b + 1, 1 - slot)

    @pl.when(b >= 1)
    def _():
        for h in range(N_HEADS):
            for r in range(MOBA_TOPK):
                for par in range(2):
                    v_copy(0, h, r, par).wait()
        vn = vn_ref[0]
        for h in range(N_HEADS):
            acc = jnp.zeros((HEAD_DIM, PAGE_SIZE), F32)
            for r in range(MOBA_TOPK):
                for par in range(2):
                    acc = acc + vbuf[h, r, par] * psel_ref[h, 2 * r + par]
            p_own = stat_ref[0, h][:, 0:1]
            l = stat_ref[1, h][:, 0:1]
            o_h = jnp.sum(acc, axis=1, keepdims=True) + p_own * vn[:, h:h + 1]
            o_ref[0, :, h:h + 1] = o_h / l

    @pl.when(b < nseq)
    def _():
        qcol = qcol_ref[0]
        own = jnp.sum(qcol * kn_ref[0], axis=0, keepdims=True)
        blk_i = lax.broadcasted_iota(jnp.int32, (nblk, 1), 0)
        for h in range(N_HEADS):
            qb_ref[h] = jnp.broadcast_to(qcol[:, h:h + 1], (HEAD_DIM, PAGE_SIZE))
        for h in range(N_HEADS):
            def score_body(g, _):
                qb = qb_ref[h]
                for dn in range(PAGED_SCORE_UNROLL):
                    n = g * PAGED_SCORE_UNROLL + dn
                    for par in range(2):
                        kt = kbuf[slot, 2 * n + par, h]
                        s_ref[par, h, pl.ds(n, 1), :] = jnp.sum(kt * qb, axis=0, keepdims=True)
                return 0
            lax.fori_loop(0, nblk // PAGED_SCORE_UNROLL, score_body, 0)

        for h in range(N_HEADS):
            s0 = s_ref[0, h]
            s1 = s_ref[1, h]
            bs = jnp.sum(s0 + s1, axis=1, keepdims=True)
            sel = jnp.zeros((nblk, 1), dtype=jnp.bool_)
            picks = []
            for r in range(MOBA_TOPK):
                mx = jnp.max(bs, axis=0, keepdims=True)
                first = jnp.min(jnp.where(bs == mx, blk_i, nblk), axis=0, keepdims=True)
                pick = blk_i == first
                sel = sel | pick
                bs = jnp.where(pick, -jnp.inf, bs)
                blk_id = jnp.max(first)
                for par in range(2):
                    v_copy(pt_ref[b, 2 * blk_id + par], h, r, par).start()
                picks.append(blk_id)
            s_own = own[:, h:h + 1]
            sm0 = jnp.where(sel, s0, NEG_BIG)
            sm1 = jnp.where(sel, s1, NEG_BIG)
            mx = jnp.maximum(jnp.max(jnp.max(jnp.maximum(sm0, sm1), axis=1, keepdims=True),
                                     axis=0, keepdims=True), s_own)
            p0 = jnp.exp(sm0 - mx)
            p1 = jnp.exp(sm1 - mx)
            p_own = jnp.exp(s_own - mx)
            l = jnp.sum(jnp.sum(p0 + p1, axis=1, keepdims=True), axis=0, keepdims=True) + p_own
            s_ref[0, h] = p0
            s_ref[1, h] = p1
            for r in range(MOBA_TOPK):
                for par in range(2):
                    psel_ref[h, 2 * r + par] = s_ref[par, h, pl.ds(picks[r], 1), :]
            stat_ref[0, h] = jnp.broadcast_to(p_own, (1, PAGE_SIZE))
            stat_ref[1, h] = jnp.broadcast_to(l, (1, PAGE_SIZE))


def _moba_paged(page_table, qcol, kncol, vncol, cache_kT, cache_vT):
    nseq, n_pages = page_table.shape
    assert n_pages // 2 >= MOBA_TOPK, "every pick must find an unpicked cached block"
    cur = pl.BlockSpec((1, HEAD_DIM, N_HEADS), lambda b, pt: (jnp.minimum(b, nseq - 1), 0, 0))
    prev = pl.BlockSpec((1, HEAD_DIM, N_HEADS), lambda b, pt: (jnp.maximum(b - 1, 0), 0, 0))
    any_spec = pl.BlockSpec(memory_space=pl.ANY)
    grid_spec = pltpu.PrefetchScalarGridSpec(
        num_scalar_prefetch=1,
        grid=(nseq + 1,),
        in_specs=[cur, cur, prev, any_spec, any_spec],
        out_specs=prev,
        scratch_shapes=[
            pltpu.VMEM((2, n_pages, N_HEADS, HEAD_DIM, PAGE_SIZE), F32),
            pltpu.VMEM((N_HEADS, MOBA_TOPK, 2, HEAD_DIM, PAGE_SIZE), F32),
            pltpu.VMEM((2, N_HEADS, n_pages // 2, PAGE_SIZE), F32),
            pltpu.VMEM((N_HEADS, 2 * MOBA_TOPK, 1, PAGE_SIZE), F32),
            pltpu.VMEM((2, N_HEADS, 1, PAGE_SIZE), F32),
            pltpu.VMEM((N_HEADS, HEAD_DIM, PAGE_SIZE), F32),
            pltpu.SemaphoreType.DMA((2,)),
            pltpu.SemaphoreType.DMA((1,)),
        ])
    return pl.pallas_call(
        _moba_paged_kernel,
        grid_spec=grid_spec,
        out_shape=jax.ShapeDtypeStruct((nseq, HEAD_DIM, N_HEADS), F32),
        compiler_params=_cparams(("arbitrary",)),
        name="moba_paged",
    )(page_table, qcol, kncol, vncol, cache_kT, cache_vT)


def _ssm_tables(a_re, a_im, log_dt, b_re, b_im, c_re, c_im, chunk, n_pow):
    t_ = chunk
    lam = lax.complex(a_re.astype(F32), a_im.astype(F32))
    ldt = lam * jnp.exp(log_dt.astype(F32))[:, None]
    a_bar = jnp.exp(ldt)
    b_bar = ((a_bar - 1.0) / lam)[..., None] * lax.complex(b_re.astype(F32), b_im.astype(F32))
    c_c = lax.complex(c_re.astype(F32), c_im.astype(F32))
    taus = jnp.arange(t_ + 1, dtype=F32).astype(jnp.complex64)
    apow = jnp.exp(ldt[None] * taus[:, None, None])
    gq, ssg, c_, p_ = GROUPS_PER_SG, N_SG, SSM_GROUP, SSM_STATE
    c4 = c_c.reshape(ssg, gq, c_, p_)
    b4 = b_bar.reshape(ssg, gq, p_, c_)
    ap4 = apow.reshape(t_ + 1, ssg, gq, p_)

    kc = jnp.einsum('sjcp,tsjp,sjpd->sjdtc', c4, ap4[:t_], b4).real
    kc = kc.reshape(ssg, LANES, t_ * c_)
    mst = jnp.einsum('tsjp,sjpd->stjdp', ap4[:t_][::-1], b4)
    mc = jnp.concatenate([mst.real, mst.imag], axis=-1).reshape(ssg, t_ * LANES, 2 * p_)
    nout = jnp.einsum('sjcp,tsjp->sjptc', c4, ap4[1:])
    nc = jnp.concatenate([nout.real, -nout.imag], axis=1).reshape(ssg, 2 * SG_STATE, t_ * c_)
    rs = jnp.arange(1, n_pow + 1, dtype=F32).astype(jnp.complex64)
    ap = jnp.exp((ldt * t_)[None] * rs[:, None, None])
    ap = ap.reshape(n_pow, ssg, SG_STATE).transpose(1, 0, 2)
    a_pow = jnp.concatenate([ap.real, ap.imag], axis=-1)
    a1 = a_bar.reshape(ssg, 1, SG_STATE)
    a_one = jnp.concatenate([a1.real, a1.imag], axis=-1)
    return kc, mc, nc, a_pow, a_one


def _expand_ssm_weights(kc_ref, mc_ref, nc_ref, wt_s, ws_s, wo_s):
    t_ = SSM_CHUNK
    kw = t_ * LANES
    lg = lambda n: n.bit_length() - 1
    r = lax.broadcasted_iota(jnp.int32, (LANES, kw), 0)
    q = lax.broadcasted_iota(jnp.int32, (LANES, kw), 1)
    sh_c, sh_p = lg(SSM_GROUP), lg(SSM_STATE)
    ecol = jnp.where(((r >> sh_c) == (q >> lg(LANES))) & ((r & (SSM_GROUP - 1)) == (q & (SSM_GROUP - 1))),
                     1.0, 0.0).astype(BF16)
    e2 = jnp.where(((r >> sh_p) == (q >> lg(SG_STATE))) & ((r & (SSM_STATE - 1)) == (q & (SSM_STATE - 1))),
                   1.0, 0.0).astype(BF16)
    col_c = (q & (LANES - 1)) >> sh_c
    taps = jnp.where((r >> sh_c) == col_c, _dot(kc_ref[0].astype(BF16), ecol), 0.0).astype(BF16)
    blank = jnp.zeros((LANES, LANES), BF16)
    for s in range(t_):
        for t in range(t_):
            wt_s[s * LANES:(s + 1) * LANES, t * LANES:(t + 1) * LANES] = (
                taps[:, (t - s) * LANES:(t - s + 1) * LANES] if t >= s else blank)
    rr = lax.broadcasted_iota(jnp.int32, (kw, kw), 0)
    qq = lax.broadcasted_iota(jnp.int32, (kw, kw), 1)
    row_c = (rr & (LANES - 1)) >> sh_c
    row_p = (rr & (SG_STATE - 1)) >> sh_p
    colq_c = (qq & (LANES - 1)) >> sh_c
    colq_p = (qq & (SG_STATE - 1)) >> sh_p
    ws_s[...] = jnp.where(row_c == colq_p, _dot(mc_ref[0].astype(BF16), e2), 0.0).astype(BF16)
    wo_s[...] = jnp.where(row_p == colq_c, _dot(nc_ref[0].astype(BF16), ecol), 0.0).astype(BF16)


def _ssm_prompt_kernel(u_ref, kc_ref, mc_ref, nc_ref, ap_ref, d_ref, h0_ref,
                       y_ref, hf_ref, uc_ref, x_ref, hin_ref, wt_s, ws_s, wo_s):
    t_ = SSM_CHUNK
    nc = uc_ref.shape[0]
    ns = SG_STATE

    @pl.when(pl.program_id(1) == 0)
    def _():
        _expand_ssm_weights(kc_ref, mc_ref, nc_ref, wt_s, ws_s, wo_s)

    for t in range(t_):
        uc_ref[:, t * LANES:(t + 1) * LANES] = u_ref[pl.ds(t, nc, stride=t_), :].astype(BF16)
    uc = uc_ref[...]
    x_ref[...] = _dot(uc, ws_s[...])

    ap = ap_ref[0]
    apr, api = ap[:, :ns], ap[:, ns:]
    row = lax.broadcasted_iota(jnp.int32, (SUBLANES, 1), 0)

    def tile_body(k, carry):
        hr, hi = carry
        rows = pl.ds(k * SUBLANES, SUBLANES)
        xr = x_ref[rows, 0:ns]
        xi = x_ref[rows, ns:2 * ns]
        for d in (1, 2, 4):
            ar, ai = apr[d - 1:d], api[d - 1:d]
            sr = jnp.where(row >= d, pltpu.roll(xr, d, 0), 0.0)
            si = jnp.where(row >= d, pltpu.roll(xi, d, 0), 0.0)
            xr, xi = xr + ar * sr - ai * si, xi + ar * si + ai * sr
        outr = xr + apr * hr - api * hi
        outi = xi + apr * hi + api * hr
        hin_ref[rows, 0:ns] = jnp.where(row >= 1, pltpu.roll(outr, 1, 0), hr)
        hin_ref[rows, ns:2 * ns] = jnp.where(row >= 1, pltpu.roll(outi, 1, 0), hi)
        return outr[SUBLANES - 1:SUBLANES], outi[SUBLANES - 1:SUBLANES]

    cols = []
    for nt in range(t_ * LANES // MXU_COLS):
        acc = None
        for kt in range(nt + 1):
            part = _dot(uc[:, kt * MXU_COLS:(kt + 1) * MXU_COLS],
                        wt_s[kt * MXU_COLS:(kt + 1) * MXU_COLS, nt * MXU_COLS:(nt + 1) * MXU_COLS])
            acc = part if acc is None else acc + part
        cols.append(acc)
    y_intra = jnp.concatenate(cols, axis=1)
    h0 = h0_ref[0, 0]
    carry = (h0[:, :ns], h0[:, ns:])
    for k in range(nc // SUBLANES):
        carry = tile_body(k, carry)
    hf_ref[0, 0] = jnp.concatenate(carry, axis=1)

    y = y_intra + _dot(hin_ref[...].astype(BF16), wo_s[...])
    dv = d_ref[...]
    for t in range(t_):
        rows = pl.ds(t, nc, stride=t_)
        y_ref[rows, :] = y[:, t * LANES:(t + 1) * LANES] + dv * u_ref[rows, :]


def _ssm_prompt(u, kc, mc, nc_tab, a_pow, d_row, h0, bsz, seq):
    t_ = SSM_CHUNK
    nc = seq // t_
    kw = t_ * LANES
    assert kc.shape[2] == LANES and 2 * SG_STATE == kw
    wspec = lambda shape: pl.BlockSpec((1,) + shape, lambda s, b: (s, 0, 0))
    return pl.pallas_call(
        _ssm_prompt_kernel,
        grid=(N_SG, bsz),
        in_specs=[pl.BlockSpec((seq, LANES), lambda s, b: (b, s)),
                  wspec(kc.shape[1:]), wspec(mc.shape[1:]), wspec(nc_tab.shape[1:]),
                  wspec((SUBLANES, 2 * SG_STATE)),
                  pl.BlockSpec((1, LANES), lambda s, b: (0, s)),
                  pl.BlockSpec((1, 1, 1, 2 * SG_STATE), lambda s, b: (b, s, 0, 0))],
        out_specs=(pl.BlockSpec((seq, LANES), lambda s, b: (b, s)),
                   pl.BlockSpec((1, 1, 1, 2 * SG_STATE), lambda s, b: (b, s, 0, 0))),
        out_shape=(jax.ShapeDtypeStruct((bsz * seq, SSM_WIDTH), F32),
                   jax.ShapeDtypeStruct((bsz, N_SG, 1, 2 * SG_STATE), F32)),
        scratch_shapes=[pltpu.VMEM((nc, kw), BF16),
                        pltpu.VMEM((nc, 2 * SG_STATE), F32),
                        pltpu.VMEM((nc, 2 * SG_STATE), F32),
                        pltpu.VMEM((kw, kw), BF16),
                        pltpu.VMEM((kw, 2 * SG_STATE), BF16),
                        pltpu.VMEM((2 * SG_STATE, kw), BF16)],
        compiler_params=_cparams(("arbitrary", "arbitrary")),
        name="ssm_prompt",
    )(u, kc, mc, nc_tab, a_pow, d_row, h0)


def _ssm_step_kernel(u_ref, kc_ref, mc_ref, nc_ref, a1_ref, d_ref, h0_ref, y_ref, hf_ref):
    ns = SG_STATE
    kw = 2 * ns
    lg = lambda n: n.bit_length() - 1
    sh_c, sh_p = lg(SSM_GROUP), lg(SSM_STATE)
    iota = lambda shape, d: lax.broadcasted_iota(jnp.int32, shape, d)
    r1, q1 = iota((LANES, LANES), 0), iota((LANES, LANES), 1)
    e_c = jnp.where((r1 < SSM_GROUP) & ((r1 & (SSM_GROUP - 1)) == (q1 & (SSM_GROUP - 1))), 1.0, 0.0).astype(BF16)
    r2, q2 = iota((LANES, kw), 0), iota((LANES, kw), 1)
    e_p = jnp.where(((r2 >> sh_p) == (q2 >> lg(ns))) & ((r2 & (SSM_STATE - 1)) == (q2 & (SSM_STATE - 1))),
                    1.0, 0.0).astype(BF16)
    same_t = (r1 >> sh_c) == (q1 >> sh_c)
    same_s = (r2 >> sh_c) == ((q2 & (ns - 1)) >> sh_p)
    r3, q3 = iota((kw, LANES), 0), iota((kw, LANES), 1)
    same_o = ((r3 & (ns - 1)) >> sh_p) == (q3 >> sh_c)
    last = (SSM_CHUNK - 1) * LANES
    for s in range(N_SG):
        wt = jnp.where(same_t, _dot(kc_ref[s].astype(BF16), e_c), 0.0).astype(BF16)
        ws = jnp.where(same_s, _dot(mc_ref[s, last:last + LANES, :].astype(BF16), e_p), 0.0).astype(BF16)
        wo = jnp.where(same_o, _dot(nc_ref[s].astype(BF16), e_c), 0.0).astype(BF16)
        us = u_ref[:, s * LANES:(s + 1) * LANES]
        ub = us.astype(BF16)
        h0 = h0_ref[s]
        x = _dot(ub, ws)
        ar, ai = a1_ref[s, :, :ns], a1_ref[s, :, ns:]
        hr, hi = h0[:, :ns], h0[:, ns:]
        hf_ref[s] = jnp.concatenate([x[:, :ns] + ar * hr - ai * hi,
                                     x[:, ns:] + ar * hi + ai * hr], axis=1)
        y = _dot(ub, wt) + _dot(h0.astype(BF16), wo)
        y_ref[:, s * LANES:(s + 1) * LANES] = y + d_ref[:, s * LANES:(s + 1) * LANES] * us


def _ssm_step(u, kc, mc, nc_tab, a_one, d_row, h0):
    m = u.shape[0]
    full = lambda a: pl.BlockSpec(a.shape, lambda i: (0,) * a.ndim)
    args = (u, kc, mc, nc_tab, a_one, d_row, h0)
    shapes = [(m, SSM_WIDTH), (N_SG, m, 2 * SG_STATE)]
    return pl.pallas_call(
        _ssm_step_kernel,
        grid=(1,),
        in_specs=[full(a) for a in args],
        out_specs=tuple(pl.BlockSpec(s, lambda i, n=len(s): (0,) * n) for s in shapes),
        out_shape=tuple(jax.ShapeDtypeStruct(s, F32) for s in shapes),
        compiler_params=_cparams(("arbitrary",)),
        name="ssm_step",
    )(*args)


FFN_SPLIT = FFN_HIDDEN // MXU_COLS
FFN_CHUNK = FFN_HIDDEN // FFN_SPLIT


def _post_kernel(x_ref, o_ref, y_ref, sga_ref, sgs_ref, wap_ref, wglu_ref, bglu_ref, wsp_ref,
                 wout_ref, nffn_ref, wfi_ref, wfo_ref, nfin_ref, out_ref):
    z = jax.nn.gelu(y_ref[...])
    z = z * jax.nn.sigmoid(_dot(z.astype(BF16), wglu_ref[...]) + bglu_ref[...])
    ssm_out = _dot(z.astype(BF16), wsp_ref[...])
    attn_out = _dot(o_ref[...].astype(BF16), wap_ref[...])
    merged = sga_ref[...].astype(F32) * attn_out + sgs_ref[...].astype(F32) * ssm_out
    x1 = x_ref[...] + _dot(merged.astype(BF16), wout_ref[...])
    hf = _rms(x1, nffn_ref[...]).astype(BF16)
    acc = x1
    for c in range(FFN_SPLIT):
        lo = c * FFN_CHUNK
        a = _dot(hf, wfi_ref[:, lo:lo + FFN_CHUNK])
        g = _dot(hf, wfi_ref[:, FFN_HIDDEN + lo:FFN_HIDDEN + lo + FFN_CHUNK])
        act = (jax.nn.silu(a) * g).astype(BF16)
        acc = acc + _dot(act, wfo_ref[lo:lo + FFN_CHUNK, :])
    out_ref[...] = _rms(acc, nfin_ref[...])


def _post(x2d, o, y, sga, sgs, wap, wglu, bglu, wsp, wout, nffn, wfi, wfo, nfin, tm):
    m = x2d.shape[0]
    tok = lambda w: pl.BlockSpec((tm, w), lambda i: (i, 0))
    const = lambda a: pl.BlockSpec(a.shape, lambda i: (0,) * a.ndim, pipeline_mode=pl.Buffered(1))
    weights = (wap, wglu, bglu, wsp, wout, nffn, wfi, wfo, nfin)
    return pl.pallas_call(
        _post_kernel,
        grid=(m // tm,),
        in_specs=[tok(D_MODEL), tok(ATTN_WIDTH), tok(SSM_WIDTH), tok(D_MODEL), tok(D_MODEL)]
                 + [const(w) for w in weights],
        out_specs=tok(D_MODEL),
        out_shape=jax.ShapeDtypeStruct((m, D_MODEL), F32),
        compiler_params=_cparams(("parallel",)),
        name="post",
    )(x2d, o, y, sga, sgs, *weights)


def _rope_tables(pos):
    half = HEAD_DIM // 2
    inv = jnp.power(jnp.float32(ROPE_THETA), -2.0 * jnp.arange(half, dtype=F32) / HEAD_DIM)
    ang = pos.astype(F32)[:, None] * inv[None, :]
    cos, sin = jnp.cos(ang), jnp.sin(ang)
    reps = LANES // HEAD_DIM
    return (jnp.tile(jnp.concatenate([cos, cos], axis=1), (1, reps)),
            jnp.tile(jnp.concatenate([-sin, sin], axis=1), (1, reps)))


def _leaf_from_T(xT, bsz, seq):
    return xT.reshape(bsz, N_HEADS, HEAD_DIM, seq).transpose(0, 3, 1, 2)[None]


def _state_in(re, im):
    n = re.shape[0]
    h = jnp.concatenate([re.reshape(n, N_SG, SG_STATE), im.reshape(n, N_SG, SG_STATE)], axis=-1)
    return h.transpose(1, 0, 2)


def _state_out(h):
    n = h.shape[0]
    return (h[..., :SG_STATE].reshape(1, n, SSM_GROUPS, SSM_STATE),
            h[..., SG_STATE:].reshape(1, n, SSM_GROUPS, SSM_STATE))


def kernel(x_prompt, x_sample, cache_k, cache_v, state_ssm_re, state_ssm_im, page_table, norm_mix, w_in,
           w_attn_proj, ssm_a_re, ssm_a_im, ssm_log_dt, ssm_b_re, ssm_b_im, ssm_c_re, ssm_c_im, ssm_d, w_glu,
           b_glu, w_ssm_proj, w_out, norm_ffn, w_ffn_in, w_ffn_out, norm_final):
    assert w_in.shape[0] == 1, "single layer"
    bsz, seq = x_prompt.shape[:2]
    nseq = x_sample.shape[0]
    past_len = page_table.shape[1] * PAGE_SIZE
    assert seq % MOBA_BLOCK == 0 and past_len % MOBA_BLOCK == 0 and x_sample.shape[1] == 1

    ssm_p = (ssm_a_re[0], ssm_a_im[0], ssm_log_dt[0], ssm_b_re[0], ssm_b_im[0], ssm_c_re[0], ssm_c_im[0])

    cos_s, sin_s = _rope_tables(jnp.full((1,), past_len, dtype=jnp.int32))
    xs2 = x_sample.reshape(nseq, D_MODEL)
    q_s, k_s, v_s, u_s, sga_s, sgs_s, w_in_bf = _inproj_sample(xs2, norm_mix, w_in[0], cos_s, sin_s)

    cos_p, sin_p = _rope_tables(jnp.arange(seq, dtype=jnp.int32))
    xp2 = x_prompt.reshape(bsz * seq, D_MODEL)
    cast_ws = (w_attn_proj[0], w_glu[0], w_ssm_proj[0], w_out[0], w_ffn_in[0], w_ffn_out[0])
    (qT, kT, kbf, kmean, vT, vTb, u_p, sga_p, sgs_p,
     wap_bf, wglu_bf, wsp_bf, wout_bf, wfi_bf, wfo_bf) = _inproj_prompt(
        xp2, norm_mix, w_in_bf, cos_p, sin_p, cast_ws, bsz, seq, tm=1024)
    post_w = (wap_bf, wglu_bf, b_glu, wsp_bf, wout_bf, norm_ffn, wfi_bf, wfo_bf, norm_final[None])
    o_p = _moba_prompt(qT, kbf, kmean.reshape(bsz, seq // MOBA_BLOCK, ATTN_WIDTH), vTb, bsz, seq)
    kc, mc, nc_tab, ap, a_one = _ssm_tables(*ssm_p, chunk=SSM_CHUNK, n_pow=SUBLANES)
    h0_p = jnp.zeros((bsz, N_SG, 1, 2 * SG_STATE), F32)
    y_p, hf_p = _ssm_prompt(u_p, kc, mc, nc_tab, ap, ssm_d, h0_p, bsz, seq)
    y_prompt = _post(xp2, o_p, y_p, sga_p, sgs_p, *post_w, tm=512).reshape(bsz, seq, D_MODEL)
    new_ssm_re_p, new_ssm_im_p = _state_out(hf_p.reshape(bsz, N_SG, 2 * SG_STATE))

    cache_kT = cache_k[0].transpose(0, 2, 3, 1)
    cache_vT = cache_v[0].transpose(0, 2, 3, 1)
    dh = lambda t: t.reshape(nseq, N_HEADS, HEAD_DIM).transpose(0, 2, 1)
    o_s = _moba_paged(page_table, dh(q_s), dh(k_s), dh(v_s), cache_kT, cache_vT).transpose(0, 2, 1)
    h0_s = _state_in(state_ssm_re[0], state_ssm_im[0])
    y_s, hf_s = _ssm_step(u_s, kc, mc, nc_tab, a_one, ssm_d, h0_s)
    y_sample = _post(xs2, o_s.reshape(nseq, ATTN_WIDTH), y_s, sga_s, sgs_s, *post_w, tm=nseq)
    new_ssm_re_s, new_ssm_im_s = _state_out(hf_s.transpose(1, 0, 2))

    return (y_prompt, y_sample.reshape(nseq, 1, D_MODEL),
            _leaf_from_T(kT, bsz, seq), _leaf_from_T(vT, bsz, seq), new_ssm_re_p, new_ssm_im_p,
            k_s.reshape(1, nseq, 1, N_HEADS, HEAD_DIM), v_s.reshape(1, nseq, 1, N_HEADS, HEAD_DIM),
            new_ssm_re_s, new_ssm_im_s)
```

```python
import functools
import math

import jax
import jax.numpy as jnp
from jax import lax
from jax.experimental import pallas as pl
from jax.experimental.pallas import tpu as pltpu

F32 = jnp.float32
BF16 = jnp.bfloat16

D_MODEL = 1024
N_HEADS = 8
HEAD_DIM = 64
ATTN_WIDTH = N_HEADS * HEAD_DIM
MOBA_BLOCK = 256
MOBA_TOPK = 3
ROPE_THETA = 10000.0
SSM_WIDTH = 512
SSM_GROUP = 16
SSM_GROUPS = 32
SSM_STATE = 64
FFN_HIDDEN = 2816
RMS_EPS = 1e-6
PAGE_SIZE = 128

LANES = 128
SUBLANES = 8
BF16_SUBLANES = 16
MXU_COLS = 256
GROUPS_PER_SG = LANES // SSM_GROUP
N_SG = SSM_GROUPS // GROUPS_PER_SG
SG_STATE = GROUPS_PER_SG * SSM_STATE
SSM_CHUNK = 8
NEG_BIG = -1e30
MOBA_GROUPS_PER_STEP = 1
MOBA_HEADS_PER_GROUP = 4
MOBA_LOOKAHEAD = 2
LOG2E = math.log2(math.e)
MOBA_SUM_ROWS = 16
PAGED_SCORE_UNROLL = 4
VMEM_LIMIT = 60 * 1024 * 1024


def _cparams(sem):
    return pltpu.CompilerParams(dimension_semantics=sem, vmem_limit_bytes=VMEM_LIMIT)


def _dot(a, b):
    return jnp.dot(a, b, preferred_element_type=F32)


def _rms(x, g):
    return x * lax.rsqrt(jnp.mean(x * x, axis=-1, keepdims=True) + RMS_EPS) * g


def _inproj_core(x, g, w_ref, cos, sin, q_scale):
    h = _rms(x, g).astype(BF16)
    lane = lax.broadcasted_iota(jnp.int32, (1, ATTN_WIDTH), 1)
    first_half = (lane % HEAD_DIM) < (HEAD_DIM // 2)

    def rot(t):
        partner = jnp.where(first_half,
                            pltpu.roll(t, ATTN_WIDTH - HEAD_DIM // 2, 1),
                            pltpu.roll(t, HEAD_DIM // 2, 1))
        return t * cos + partner * sin

    a = ATTN_WIDTH
    q = rot(_dot(h, w_ref[:, 0:a])) * q_scale
    k = rot(_dot(h, w_ref[:, a:2 * a]))
    v = _dot(h, w_ref[:, 2 * a:3 * a])
    u = _dot(h, w_ref[:, 3 * a:3 * a + SSM_WIDTH])
    o = 3 * a + SSM_WIDTH
    sga = jax.nn.sigmoid(_dot(h, w_ref[:, o:o + D_MODEL]))
    sgs = jax.nn.sigmoid(_dot(h, w_ref[:, o + D_MODEL:o + 2 * D_MODEL]))
    return q, k, v, u, sga, sgs


def _inproj_prompt_kernel(x_ref, g_ref, w_ref, cos_ref, sin_ref, *rest):
    n_cast = (len(rest) - 9) // 2
    cast_in = rest[:n_cast]
    qT_ref, kT_ref, kbf_ref, kmean_ref, vT_ref, vTb_ref, u_ref, sga_ref, sgs_ref = rest[n_cast:n_cast + 9]
    cast_out = rest[n_cast + 9:]
    for wi_ref, wo_ref in zip(cast_in, cast_out):
        wo_ref[...] = wi_ref[...].astype(BF16)
    reps = ATTN_WIDTH // LANES
    cos = jnp.tile(cos_ref[...], (1, reps))
    sin = jnp.tile(sin_ref[...], (1, reps))
    q, k, v, u, sga, sgs = _inproj_core(x_ref[...], g_ref[...], w_ref, cos, sin, LOG2E * HEAD_DIM ** -0.5)
    tm = q.shape[0]
    kT_ref[0] = k.T
    vT_ref[0] = v.T
    kbf_ref[...] = k.astype(BF16)
    for s in range(tm // MOBA_BLOCK):
        rows = slice(s * MOBA_BLOCK, (s + 1) * MOBA_BLOCK)
        qT_ref[0, s] = q[rows].T.astype(BF16)
        vTb_ref[0, s] = v[rows].T.astype(BF16)
        kmean_ref[0, s] = jnp.mean(k[rows], axis=0, keepdims=True)
    u_ref[...] = u
    sga_ref[...] = sga.astype(BF16)
    sgs_ref[...] = sgs.astype(BF16)


def _inproj_prompt(x2d, g, w_bf, cos, sin, cast_ws, bsz, seq, tm):
    m = bsz * seq
    nb = seq // MOBA_BLOCK
    tpb = seq // tm
    sub = tm // MOBA_BLOCK
    a = ATTN_WIDTH
    full = lambda shape: pl.BlockSpec(shape, lambda b, t: (0,) * len(shape))
    tok = lambda w: pl.BlockSpec((tm, w), lambda b, t: (b * tpb + t, 0))
    out_shape = (
        jax.ShapeDtypeStruct((bsz, nb, a, MOBA_BLOCK), BF16),
        jax.ShapeDtypeStruct((bsz, a, seq), F32),
        jax.ShapeDtypeStruct((m, a), BF16),
        jax.ShapeDtypeStruct((bsz, nb, 1, a), F32),
        jax.ShapeDtypeStruct((bsz, a, seq), F32),
        jax.ShapeDtypeStruct((bsz, nb, a, MOBA_BLOCK), BF16),
        jax.ShapeDtypeStruct((m, SSM_WIDTH), F32),
        jax.ShapeDtypeStruct((m, D_MODEL), BF16),
        jax.ShapeDtypeStruct((m, D_MODEL), BF16),
    )
    blk_t = pl.BlockSpec((1, sub, a, MOBA_BLOCK), lambda b, t: (b, t, 0, 0))
    lane_t = pl.BlockSpec((1, a, tm), lambda b, t: (b, 0, t))
    out_specs = (blk_t, lane_t, tok(a),
                 pl.BlockSpec((1, sub, 1, a), lambda b, t: (b, t, 0, 0)),
                 lane_t, blk_t, tok(SSM_WIDTH), tok(D_MODEL), tok(D_MODEL))
    n_steps = bsz * tpb
    cast_specs = []
    for w in cast_ws:
        rows = w.shape[0]
        nblk = n_steps
        while rows % nblk or (rows // nblk) % BF16_SUBLANES:
            nblk //= 2
        cast_specs.append(pl.BlockSpec((rows // nblk, w.shape[1]),
                                       lambda b, t, n=nblk: (jnp.minimum(b * tpb + t, n - 1), 0)))
    return pl.pallas_call(
        _inproj_prompt_kernel,
        grid=(bsz, tpb),
        in_specs=[tok(D_MODEL), full((1, D_MODEL)),
                  pl.BlockSpec(w_bf.shape, lambda b, t: (0, 0), pipeline_mode=pl.Buffered(1)),
                  pl.BlockSpec((tm, LANES), lambda b, t: (t, 0)),
                  pl.BlockSpec((tm, LANES), lambda b, t: (t, 0))] + cast_specs,
        out_specs=out_specs + tuple(cast_specs),
        out_shape=out_shape + tuple(jax.ShapeDtypeStruct(w.shape, BF16) for w in cast_ws),
        compiler_params=_cparams(("arbitrary", "arbitrary")),
        name="inproj_prompt",
    )(x2d, g, w_bf, cos, sin, *cast_ws)


def _inproj_sample_kernel(x_ref, g_ref, w_ref, cos_ref, sin_ref,
                          q_ref, k_ref, v_ref, u_ref, sga_ref, sgs_ref, wb_ref):
    step = ATTN_WIDTH
    for c in range(w_ref.shape[1] // step):
        wb_ref[:, c * step:(c + 1) * step] = w_ref[:, c * step:(c + 1) * step].astype(BF16)
    reps = ATTN_WIDTH // LANES
    cos = jnp.tile(cos_ref[...], (1, reps))
    sin = jnp.tile(sin_ref[...], (1, reps))
    q, k, v, u, sga, sgs = _inproj_core(x_ref[...], g_ref[...], wb_ref, cos, sin, HEAD_DIM ** -0.5)
    q_ref[...] = q
    k_ref[...] = k
    v_ref[...] = v
    u_ref[...] = u
    sga_ref[...] = sga.astype(BF16)
    sgs_ref[...] = sgs.astype(BF16)


def _inproj_sample(x2d, g, w_f32, cos, sin):
    m = x2d.shape[0]
    a = ATTN_WIDTH
    full = lambda shape, **kw: pl.BlockSpec(shape, lambda i: (0,) * len(shape), **kw)
    shapes = [(m, a), (m, a), (m, a), (m, SSM_WIDTH), (m, D_MODEL), (m, D_MODEL), w_f32.shape]
    dts = [F32, F32, F32, F32, BF16, BF16, BF16]
    return pl.pallas_call(
        _inproj_sample_kernel,
        grid=(1,),
        in_specs=[full(x2d.shape), full(g.shape), full(w_f32.shape, pipeline_mode=pl.Buffered(1)),
                  full(cos.shape), full(sin.shape)],
        out_specs=tuple(full(s) for s in shapes),
        out_shape=tuple(jax.ShapeDtypeStruct(s, d) for s, d in zip(shapes, dts)),
        compiler_params=_cparams(("arbitrary",)),
        name="inproj_sample",
    )(x2d, g, w_f32, cos, sin)


def _moba_prompt_kernel(qT_ref, k_ref, kmean_ref, vT_ref, o_ref, bias_ref, qz_s, m_s, acc_s):
    nb = qT_ref.shape[1]
    blk = MOBA_BLOCK
    nh = MOBA_HEADS_PER_GROUP
    gw = nh * HEAD_DIM
    ng = qT_ref.shape[2] // gw
    heads = [(g, hh) for g in range(ng) for hh in range(nh)]
    gcols = [slice(g * gw, (g + 1) * gw) for g in range(ng)]
    row2 = lax.broadcasted_iota(jnp.int32, (gw, 1), 0)
    lane_km = lax.broadcasted_iota(jnp.int32, (1, gw), 1)
    blk_row = lax.broadcasted_iota(jnp.int32, (nb, blk), 0)
    key_i = lax.broadcasted_iota(jnp.int32, (blk, blk), 0)
    qry_i = lax.broadcasted_iota(jnp.int32, (blk, blk), 1)
    causal = key_i <= qry_i
    in_head = [(row2 >= hh * HEAD_DIM) & (row2 < (hh + 1) * HEAD_DIM) for hh in range(nh)]
    hrows = [slice(h * HEAD_DIM, (h + 1) * HEAD_DIM) for h in range(ng * nh)]

    for h in range(ng * nh):
        bias_ref[h, 0] = jnp.full((nb, blk), NEG_BIG, F32)
    for g in range(ng):
        km_g = kmean_ref[0][:, gcols[g]]
        km_all = jnp.concatenate(
            [jnp.where((lane_km >= hh * HEAD_DIM) & (lane_km < (hh + 1) * HEAD_DIM), km_g, 0.0)
             for hh in range(nh)], axis=0)
        km_terms = []
        rest = km_all
        for _ in range(3):
            term = rest.astype(BF16)
            km_terms.append(term)
            rest = rest - term.astype(F32)
        km_split = jnp.concatenate(km_terms, axis=0)
        for i in range(1, nb):
            sb3 = _dot(km_split, qT_ref[0, i, gcols[g], :])
            sb_all = (sb3[0:nh * nb] + sb3[nh * nb:2 * nh * nb]) + sb3[2 * nh * nb:]
            for hh in range(nh):
                sb = jnp.where(blk_row < i, sb_all[hh * nb:(hh + 1) * nb], -jnp.inf)
                bias = jnp.full((nb, blk), NEG_BIG, F32)
                for _r in range(min(MOBA_TOPK, i)):
                    mx = jnp.max(sb, axis=0, keepdims=True)
                    first = jnp.min(jnp.where(sb == mx, blk_row, nb), axis=0, keepdims=True)
                    pick = blk_row == first
                    bias = jnp.where(pick, 0.0, bias)
                    sb = jnp.where(pick, -jnp.inf, sb)
                bias_ref[g * nh + hh, i] = bias

    n_items = nb // 2
    ones_rows = jnp.ones((MOBA_SUM_ROWS, 2 * blk), BF16)

    def couple(i, _):
        iq = (i, nb - 1 - i)
        n_first = (i + 1) // 2

        def diag_scores(x):
            rows = pl.ds(pl.multiple_of(iq[x] * blk, blk), blk)
            out = []
            for h, (g, hh) in enumerate(heads):
                q_grp = qT_ref[0, iq[x], gcols[g], :]
                qz = jnp.where(in_head[hh], q_grp, jnp.zeros_like(q_grp))
                qz_s[x, h] = qz
                out.append(_dot(k_ref[rows, gcols[g]], qz))
            return out

        def diag_absorb(x, s_own):
            for h in range(len(heads)):
                s = jnp.where(causal, s_own[h], NEG_BIG)
                m0 = jnp.max(s, axis=0, keepdims=True)
                p = jnp.exp2(s - m0)
                m_s[x, h] = m0
                vv = jnp.concatenate([vT_ref[0, iq[x], hrows[h], :], ones_rows[:, :blk]], axis=0)
                acc_s[x, h] = _dot(vv, p.astype(BF16))

        def item_params(k):
            first = k < n_first
            x = jnp.where(first, 0, 1)
            return x, jnp.where(first, iq[0], iq[1]), jnp.where(first, k, k - n_first)

        def item_scores(k):
            x, _, jp = item_params(k)
            rows = pl.ds(pl.multiple_of(2 * jp * blk, blk), 2 * blk)
            out = []
            for h, (g, _hh) in enumerate(heads):
                s = _dot(k_ref[rows, gcols[g]], qz_s[x, h])
                out.append((s, jnp.max(s[:blk], axis=0, keepdims=True), jnp.max(s[blk:], axis=0, keepdims=True)))
            return out

        def item_absorb(k, sc):
            x, qb, jp = item_params(k)
            j0 = 2 * jp
            for h in range(len(heads)):
                s, cma, cmb = sc[h]
                ba = bias_ref[h, qb, pl.ds(j0, 1), :]
                bb = bias_ref[h, qb, pl.ds(j0 + 1, 1), :]
                m = m_s[x, h]
                m_new = jnp.maximum(m, jnp.maximum(cma + ba, cmb + bb))
                alpha = jnp.exp2(m - m_new)
                pa = jnp.exp2(s[:blk] - (m_new - ba))
                pb = jnp.exp2(s[blk:] - (m_new - bb))
                m_s[x, h] = m_new
                pp = jnp.concatenate([pa, pb], axis=0).astype(BF16)
                vv = jnp.concatenate([vT_ref[0, j0, hrows[h], :], vT_ref[0, j0 + 1, hrows[h], :]], axis=1)
                vv = jnp.concatenate([vv, ones_rows], axis=0)
                acc_s[x, h] = alpha * acc_s[x, h] + _dot(vv, pp)

        s_diag = [diag_scores(0), diag_scores(1)]
        pending = [item_scores(k) for k in range(min(MOBA_LOOKAHEAD, n_items))]
        diag_absorb(0, s_diag[0])
        diag_absorb(1, s_diag[1])
        for k in range(n_items):
            if k + MOBA_LOOKAHEAD < n_items:
                pending.append(item_scores(k + MOBA_LOOKAHEAD))
            item_absorb(k, pending[k])
        for x in range(2):
            oT = jnp.concatenate([acc_s[x, h, 0:HEAD_DIM, :] / acc_s[x, h, HEAD_DIM:HEAD_DIM + 1, :]
                                  for h in range(len(heads))], axis=0)
            o_ref[pl.ds(pl.multiple_of(iq[x] * blk, blk), blk), :] = oT.T.astype(o_ref.dtype)
        return 0

    lax.fori_loop(0, nb // 2, couple, 0)


def _moba_prompt(qT, kbf, kmean, vTb, bsz, seq):
    nb = seq // MOBA_BLOCK
    hps = MOBA_HEADS_PER_GROUP * MOBA_GROUPS_PER_STEP
    hp = N_HEADS // hps
    pair = hps * HEAD_DIM
    gw = MOBA_HEADS_PER_GROUP * HEAD_DIM
    return pl.pallas_call(
        _moba_prompt_kernel,
        grid=(bsz, hp),
        in_specs=[pl.BlockSpec((1, nb, pair, MOBA_BLOCK), lambda b, h: (b, 0, h, 0)),
                  pl.BlockSpec((seq, pair), lambda b, h: (b, h)),
                  pl.BlockSpec((1, nb, pair), lambda b, h: (b, 0, h)),
                  pl.BlockSpec((1, nb, pair, MOBA_BLOCK), lambda b, h: (b, 0, h, 0))],
        out_specs=pl.BlockSpec((seq, pair), lambda b, h: (b, h)),
        out_shape=jax.ShapeDtypeStruct((bsz * seq, ATTN_WIDTH), BF16),
        scratch_shapes=[pltpu.VMEM((hps, nb, nb, MOBA_BLOCK), F32),
                        pltpu.VMEM((2, hps, gw, MOBA_BLOCK), BF16),
                        pltpu.VMEM((2, hps, 1, MOBA_BLOCK), F32),
                        pltpu.VMEM((2, hps, HEAD_DIM + MOBA_SUM_ROWS, MOBA_BLOCK), F32)],
        compiler_params=_cparams(("parallel", "parallel")),
        name="moba_prompt",
    )(qT, kbf, kmean, vTb)


def _moba_paged_kernel(pt_ref, qcol_ref, kn_ref, vn_ref, ck_ref, cv_ref, o_ref,
                       kbuf, vbuf, s_ref, psel_ref, stat_ref, qb_ref, ksem, vsem):
    b = pl.program_id(0)
    nseq = pl.num_programs(0) - 1
    n_pages = kbuf.shape[1]
    nblk = n_pages // 2
    slot = b % 2

    def k_copy(seq_i, sl, p):
        return pltpu.make_async_copy(ck_ref.at[pt_ref[seq_i, p]], kbuf.at[sl, p], ksem.at[sl])

    def start_k(seq_i, sl):
        def body(p, _):
            for pr in range(2):
                k_copy(seq_i, sl, 2 * p + pr).start(priority=pr)
            return 0
        lax.fori_loop(0, n_pages // 2, body, 0)

    def v_copy(page, h, r, par):
        return pltpu.make_async_copy(cv_ref.at[page, h], vbuf.at[h, r, par], vsem.at[0])

    @pl.when(b == 0)
    def _():
        start_k(0, 0)

    @pl.when(b < nseq)
    def _():
        def wait_body(p, _):
            k_copy(b, slot, p).wait()
            return 0
        lax.fori_loop(0, n_pages, wait_body, 0)

    @pl.when(b + 1 < nseq)
    def _():
        start_k(b + 1, 1 - slot)

    @pl.when(b >= 1)
    def _():
        for h in range(N_HEADS):
            for r in range(MOBA_TOPK):
                for par in range(2):
                    v_copy(0, h, r, par).wait()
        vn = vn_ref[0]
        for h in range(N_HEADS):
            acc = jnp.zeros((HEAD_DIM, PAGE_SIZE), F32)
            for r in range(MOBA_TOPK):
                for par in range(2):
                    acc = acc + vbuf[h, r, par] * psel_ref[h, 2 * r + par]
            p_own = stat_ref[0, h][:, 0:1]
            l = stat_ref[1, h][:, 0:1]
            o_h = jnp.sum(acc, axis=1, keepdims=True) + p_own * vn[:, h:h + 1]
            o_ref[0, :, h:h + 1] = o_h / l

    @pl.when(b < nseq)
    def _():
        qcol = qcol_ref[0]
        own = jnp.sum(qcol * kn_ref[0], axis=0, keepdims=True)
        blk_i = lax.broadcasted_iota(jnp.int32, (nblk, 1), 0)
        for h in range(N_HEADS):
            qb_ref[h] = jnp.broadcast_to(qcol[:, h:h + 1], (HEAD_DIM, PAGE_SIZE))
        for h in range(N_HEADS):
            def score_body(g, _):
                qb = qb_ref[h]
                for dn in range(PAGED_SCORE_UNROLL):
                    n = g * PAGED_SCORE_UNROLL + dn
                    for par in range(2):
                        kt = kbuf[slot, 2 * n + par, h]
                        s_ref[par, h, pl.ds(n, 1), :] = jnp.sum(kt * qb, axis=0, keepdims=True)
                return 0
            lax.fori_loop(0, nblk // PAGED_SCORE_UNROLL, score_body, 0)

        for h in range(N_HEADS):
            s0 = s_ref[0, h]
            s1 = s_ref[1, h]
            bs = jnp.sum(s0 + s1, axis=1, keepdims=True)
            sel = jnp.zeros((nblk, 1), dtype=jnp.bool_)
            picks = []
            for r in range(MOBA_TOPK):
                mx = jnp.max(bs, axis=0, keepdims=True)
                first = jnp.min(jnp.where(bs == mx, blk_i, nblk), axis=0, keepdims=True)
                pick = blk_i == first
                sel = sel | pick
                bs = jnp.where(pick, -jnp.inf, bs)
                blk_id = jnp.max(first)
                for par in range(2):
                    v_copy(pt_ref[b, 2 * blk_id + par], h, r, par).start(priority=par)
                picks.append(blk_id)
            s_own = own[:, h:h + 1]
            sm0 = jnp.where(sel, s0, NEG_BIG)
            sm1 = jnp.where(sel, s1, NEG_BIG)
            mx = jnp.maximum(jnp.max(jnp.max(jnp.maximum(sm0, sm1), axis=1, keepdims=True),
                                     axis=0, keepdims=True), s_own)
            p0 = jnp.exp(sm0 - mx)
            p1 = jnp.exp(sm1 - mx)
            p_own = jnp.exp(s_own - mx)
            l = jnp.sum(jnp.sum(p0 + p1, axis=1, keepdims=True), axis=0, keepdims=True) + p_own
            s_ref[0, h] = p0
            s_ref[1, h] = p1
            for r in range(MOBA_TOPK):
                for par in range(2):
                    psel_ref[h, 2 * r + par] = s_ref[par, h, pl.ds(picks[r], 1), :]
            stat_ref[0, h] = jnp.broadcast_to(p_own, (1, PAGE_SIZE))
            stat_ref[1, h] = jnp.broadcast_to(l, (1, PAGE_SIZE))


def _moba_paged(page_table, qcol, kncol, vncol, cache_kT, cache_vT):
    nseq, n_pages = page_table.shape
    assert n_pages // 2 >= MOBA_TOPK, "every pick must find an unpicked cached block"
    cur = pl.BlockSpec((1, HEAD_DIM, N_HEADS), lambda b, pt: (jnp.minimum(b, nseq - 1), 0, 0))
    prev = pl.BlockSpec((1, HEAD_DIM, N_HEADS), lambda b, pt: (jnp.maximum(b - 1, 0), 0, 0))
    any_spec = pl.BlockSpec(memory_space=pl.ANY)
    grid_spec = pltpu.PrefetchScalarGridSpec(
        num_scalar_prefetch=1,
        grid=(nseq + 1,),
        in_specs=[cur, cur, prev, any_spec, any_spec],
        out_specs=prev,
        scratch_shapes=[
            pltpu.VMEM((2, n_pages, N_HEADS, HEAD_DIM, PAGE_SIZE), F32),
            pltpu.VMEM((N_HEADS, MOBA_TOPK, 2, HEAD_DIM, PAGE_SIZE), F32),
            pltpu.VMEM((2, N_HEADS, n_pages // 2, PAGE_SIZE), F32),
            pltpu.VMEM((N_HEADS, 2 * MOBA_TOPK, 1, PAGE_SIZE), F32),
            pltpu.VMEM((2, N_HEADS, 1, PAGE_SIZE), F32),
            pltpu.VMEM((N_HEADS, HEAD_DIM, PAGE_SIZE), F32),
            pltpu.SemaphoreType.DMA((2,)),
            pltpu.SemaphoreType.DMA((1,)),
        ])
    return pl.pallas_call(
        _moba_paged_kernel,
        grid_spec=grid_spec,
        out_shape=jax.ShapeDtypeStruct((nseq, HEAD_DIM, N_HEADS), F32),
        compiler_params=_cparams(("arbitrary",)),
        name="moba_paged",
    )(page_table, qcol, kncol, vncol, cache_kT, cache_vT)


def _ssm_tables(a_re, a_im, log_dt, b_re, b_im, c_re, c_im, chunk, n_pow):
    t_ = chunk
    lam = lax.complex(a_re.astype(F32), a_im.astype(F32))
    ldt = lam * jnp.exp(log_dt.astype(F32))[:, None]
    a_bar = jnp.exp(ldt)
    b_bar = ((a_bar - 1.0) / lam)[..., None] * lax.complex(b_re.astype(F32), b_im.astype(F32))
    c_c = lax.complex(c_re.astype(F32), c_im.astype(F32))
    taus = jnp.arange(t_ + 1, dtype=F32).astype(jnp.complex64)
    apow = jnp.exp(ldt[None] * taus[:, None, None])
    gq, ssg, c_, p_ = GROUPS_PER_SG, N_SG, SSM_GROUP, SSM_STATE
    c4 = c_c.reshape(ssg, gq, c_, p_)
    b4 = b_bar.reshape(ssg, gq, p_, c_)
    ap4 = apow.reshape(t_ + 1, ssg, gq, p_)

    kc = jnp.einsum('sjcp,tsjp,sjpd->sjdtc', c4, ap4[:t_], b4).real
    kc = kc.reshape(ssg, LANES, t_ * c_)
    mst = jnp.einsum('tsjp,sjpd->stjdp', ap4[:t_][::-1], b4)
    mc = jnp.concatenate([mst.real, mst.imag], axis=-1).reshape(ssg, t_ * LANES, 2 * p_)
    nout = jnp.einsum('sjcp,tsjp->sjptc', c4, ap4[1:])
    nc = jnp.concatenate([nout.real, -nout.imag], axis=1).reshape(ssg, 2 * SG_STATE, t_ * c_)
    rs = jnp.arange(1, n_pow + 1, dtype=F32).astype(jnp.complex64)
    ap = jnp.exp((ldt * t_)[None] * rs[:, None, None])
    ap = ap.reshape(n_pow, ssg, SG_STATE).transpose(1, 0, 2)
    a_pow = jnp.concatenate([ap.real, ap.imag], axis=-1)
    a1 = a_bar.reshape(ssg, 1, SG_STATE)
    a_one = jnp.concatenate([a1.real, a1.imag], axis=-1)
    return kc, mc, nc, a_pow, a_one


def _expand_ssm_weights(kc_ref, mc_ref, nc_ref, wt_s, ws_s, wo_s):
    t_ = SSM_CHUNK
    kw = t_ * LANES
    lg = lambda n: n.bit_length() - 1
    r = lax.broadcasted_iota(jnp.int32, (LANES, kw), 0)
    q = lax.broadcasted_iota(jnp.int32, (LANES, kw), 1)
    sh_c, sh_p = lg(SSM_GROUP), lg(SSM_STATE)
    ecol = jnp.where(((r >> sh_c) == (q >> lg(LANES))) & ((r & (SSM_GROUP - 1)) == (q & (SSM_GROUP - 1))),
                     1.0, 0.0).astype(BF16)
    e2 = jnp.where(((r >> sh_p) == (q >> lg(SG_STATE))) & ((r & (SSM_STATE - 1)) == (q & (SSM_STATE - 1))),
                   1.0, 0.0).astype(BF16)
    col_c = (q & (LANES - 1)) >> sh_c
    taps = jnp.where((r >> sh_c) == col_c, _dot(kc_ref[0].astype(BF16), ecol), 0.0).astype(BF16)
    blank = jnp.zeros((LANES, LANES), BF16)
    for s in range(t_):
        for t in range(t_):
            wt_s[s * LANES:(s + 1) * LANES, t * LANES:(t + 1) * LANES] = (
                taps[:, (t - s) * LANES:(t - s + 1) * LANES] if t >= s else blank)
    rr = lax.broadcasted_iota(jnp.int32, (kw, kw), 0)
    qq = lax.broadcasted_iota(jnp.int32, (kw, kw), 1)
    row_c = (rr & (LANES - 1)) >> sh_c
    row_p = (rr & (SG_STATE - 1)) >> sh_p
    colq_c = (qq & (LANES - 1)) >> sh_c
    colq_p = (qq & (SG_STATE - 1)) >> sh_p
    ws_s[...] = jnp.where(row_c == colq_p, _dot(mc_ref[0].astype(BF16), e2), 0.0).astype(BF16)
    wo_s[...] = jnp.where(row_p == colq_c, _dot(nc_ref[0].astype(BF16), ecol), 0.0).astype(BF16)


def _ssm_prompt_kernel(u_ref, kc_ref, mc_ref, nc_ref, ap_ref, d_ref, h0_ref,
                       y_ref, hf_ref, uc_ref, x_ref, hin_ref, wt_s, ws_s, wo_s):
    t_ = SSM_CHUNK
    nc = uc_ref.shape[0]
    ns = SG_STATE

    @pl.when(pl.program_id(1) == 0)
    def _():
        _expand_ssm_weights(kc_ref, mc_ref, nc_ref, wt_s, ws_s, wo_s)

    for t in range(t_):
        uc_ref[:, t * LANES:(t + 1) * LANES] = u_ref[pl.ds(t, nc, stride=t_), :].astype(BF16)
    uc = uc_ref[...]
    x_ref[...] = _dot(uc, ws_s[...])

    ap = ap_ref[0]
    apr, api = ap[:, :ns], ap[:, ns:]
    row = lax.broadcasted_iota(jnp.int32, (SUBLANES, 1), 0)

    def tile_body(k, carry):
        hr, hi = carry
        rows = pl.ds(k * SUBLANES, SUBLANES)
        xr = x_ref[rows, 0:ns]
        xi = x_ref[rows, ns:2 * ns]
        for d in (1, 2, 4):
            ar, ai = apr[d - 1:d], api[d - 1:d]
            sr = jnp.where(row >= d, pltpu.roll(xr, d, 0), 0.0)
            si = jnp.where(row >= d, pltpu.roll(xi, d, 0), 0.0)
            xr, xi = xr + ar * sr - ai * si, xi + ar * si + ai * sr
        outr = xr + apr * hr - api * hi
        outi = xi + apr * hi + api * hr
        hin_ref[rows, 0:ns] = jnp.where(row >= 1, pltpu.roll(outr, 1, 0), hr)
        hin_ref[rows, ns:2 * ns] = jnp.where(row >= 1, pltpu.roll(outi, 1, 0), hi)
        return outr[SUBLANES - 1:SUBLANES], outi[SUBLANES - 1:SUBLANES]

    cols = []
    for nt in range(t_ * LANES // MXU_COLS):
        acc = None
        for kt in range(nt + 1):
            part = _dot(uc[:, kt * MXU_COLS:(kt + 1) * MXU_COLS],
                        wt_s[kt * MXU_COLS:(kt + 1) * MXU_COLS, nt * MXU_COLS:(nt + 1) * MXU_COLS])
            acc = part if acc is None else acc + part
        cols.append(acc)
    y_intra = jnp.concatenate(cols, axis=1)
    h0 = h0_ref[0, 0]
    carry = (h0[:, :ns], h0[:, ns:])
    for k in range(nc // SUBLANES):
        carry = tile_body(k, carry)
    hf_ref[0, 0] = jnp.concatenate(carry, axis=1)

    y = y_intra + _dot(hin_ref[...].astype(BF16), wo_s[...])
    dv = d_ref[...]
    for t in range(t_):
        rows = pl.ds(t, nc, stride=t_)
        y_ref[rows, :] = y[:, t * LANES:(t + 1) * LANES] + dv * u_ref[rows, :]


def _ssm_prompt(u, kc, mc, nc_tab, a_pow, d_row, h0, bsz, seq):
    t_ = SSM_CHUNK
    nc = seq // t_
    kw = t_ * LANES
    assert kc.shape[2] == LANES and 2 * SG_STATE == kw
    wspec = lambda shape: pl.BlockSpec((1,) + shape, lambda s, b: (s, 0, 0))
    return pl.pallas_call(
        _ssm_prompt_kernel,
        grid=(N_SG, bsz),
        in_specs=[pl.BlockSpec((seq, LANES), lambda s, b: (b, s)),
                  wspec(kc.shape[1:]), wspec(mc.shape[1:]), wspec(nc_tab.shape[1:]),
                  wspec((SUBLANES, 2 * SG_STATE)),
                  pl.BlockSpec((1, LANES), lambda s, b: (0, s)),
                  pl.BlockSpec((1, 1, 1, 2 * SG_STATE), lambda s, b: (b, s, 0, 0))],
        out_specs=(pl.BlockSpec((seq, LANES), lambda s, b: (b, s)),
                   pl.BlockSpec((1, 1, 1, 2 * SG_STATE), lambda s, b: (b, s, 0, 0))),
        out_shape=(jax.ShapeDtypeStruct((bsz * seq, SSM_WIDTH), F32),
                   jax.ShapeDtypeStruct((bsz, N_SG, 1, 2 * SG_STATE), F32)),
        scratch_shapes=[pltpu.VMEM((nc, kw), BF16),
                        pltpu.VMEM((nc, 2 * SG_STATE), F32),
                        pltpu.VMEM((nc, 2 * SG_STATE), F32),
                        pltpu.VMEM((kw, kw), BF16),
                        pltpu.VMEM((kw, 2 * SG_STATE), BF16),
                        pltpu.VMEM((2 * SG_STATE, kw), BF16)],
        compiler_params=_cparams(("arbitrary", "arbitrary")),
        name="ssm_prompt",
    )(u, kc, mc, nc_tab, a_pow, d_row, h0)


def _ssm_step_kernel(u_ref, kc_ref, mc_ref, nc_ref, a1_ref, d_ref, h0_ref, y_ref, hf_ref):
    ns = SG_STATE
    kw = 2 * ns
    lg = lambda n: n.bit_length() - 1
    sh_c, sh_p = lg(SSM_GROUP), lg(SSM_STATE)
    iota = lambda shape, d: lax.broadcasted_iota(jnp.int32, shape, d)
    r1, q1 = iota((LANES, LANES), 0), iota((LANES, LANES), 1)
    e_c = jnp.where((r1 < SSM_GROUP) & ((r1 & (SSM_GROUP - 1)) == (q1 & (SSM_GROUP - 1))), 1.0, 0.0).astype(BF16)
    r2, q2 = iota((LANES, kw), 0), iota((LANES, kw), 1)
    e_p = jnp.where(((r2 >> sh_p) == (q2 >> lg(ns))) & ((r2 & (SSM_STATE - 1)) == (q2 & (SSM_STATE - 1))),
                    1.0, 0.0).astype(BF16)
    same_t = (r1 >> sh_c) == (q1 >> sh_c)
    same_s = (r2 >> sh_c) == ((q2 & (ns - 1)) >> sh_p)
    r3, q3 = iota((kw, LANES), 0), iota((kw, LANES), 1)
    same_o = ((r3 & (ns - 1)) >> sh_p) == (q3 >> sh_c)
    last = (SSM_CHUNK - 1) * LANES
    for s in range(N_SG):
        wt = jnp.where(same_t, _dot(kc_ref[s].astype(BF16), e_c), 0.0).astype(BF16)
        ws = jnp.where(same_s, _dot(mc_ref[s, last:last + LANES, :].astype(BF16), e_p), 0.0).astype(BF16)
        wo = jnp.where(same_o, _dot(nc_ref[s].astype(BF16), e_c), 0.0).astype(BF16)
        us = u_ref[:, s * LANES:(s + 1) * LANES]
        ub = us.astype(BF16)
        h0 = h0_ref[s]
        x = _dot(ub, ws)
        ar, ai = a1_ref[s, :, :ns], a1_ref[s, :, ns:]
        hr, hi = h0[:, :ns], h0[:, ns:]
        hf_ref[s] = jnp.concatenate([x[:, :ns] + ar * hr - ai * hi,
                                     x[:, ns:] + ar * hi + ai * hr], axis=1)
        y = _dot(ub, wt) + _dot(h0.astype(BF16), wo)
        y_ref[:, s * LANES:(s + 1) * LANES] = y + d_ref[:, s * LANES:(s + 1) * LANES] * us


def _ssm_step(u, kc, mc, nc_tab, a_one, d_row, h0):
    m = u.shape[0]
    full = lambda a: pl.BlockSpec(a.shape, lambda i: (0,) * a.ndim)
    args = (u, kc, mc, nc_tab, a_one, d_row, h0)
    shapes = [(m, SSM_WIDTH), (N_SG, m, 2 * SG_STATE)]
    return pl.pallas_call(
        _ssm_step_kernel,
        grid=(1,),
        in_specs=[full(a) for a in args],
        out_specs=tuple(pl.BlockSpec(s, lambda i, n=len(s): (0,) * n) for s in shapes),
        out_shape=tuple(jax.ShapeDtypeStruct(s, F32) for s in shapes),
        compiler_params=_cparams(("arbitrary",)),
        name="ssm_step",
    )(*args)


FFN_SPLIT = FFN_HIDDEN // MXU_COLS
FFN_CHUNK = FFN_HIDDEN // FFN_SPLIT


def _post_kernel(x_ref, o_ref, y_ref, sga_ref, sgs_ref, wap_ref, wglu_ref, bglu_ref, wsp_ref,
                 wout_ref, nffn_ref, wfi_ref, wfo_ref, nfin_ref, out_ref):
    z = jax.nn.gelu(y_ref[...])
    z = z * jax.nn.sigmoid(_dot(z.astype(BF16), wglu_ref[...]) + bglu_ref[...])
    ssm_out = _dot(z.astype(BF16), wsp_ref[...])
    attn_out = _dot(o_ref[...].astype(BF16), wap_ref[...])
    merged = sga_ref[...].astype(F32) * attn_out + sgs_ref[...].astype(F32) * ssm_out
    x1 = x_ref[...] + _dot(merged.astype(BF16), wout_ref[...])
    hf = _rms(x1, nffn_ref[...]).astype(BF16)
    acc = x1
    for c in range(FFN_SPLIT):
        lo = c * FFN_CHUNK
        a = _dot(hf, wfi_ref[:, lo:lo + FFN_CHUNK])
        g = _dot(hf, wfi_ref[:, FFN_HIDDEN + lo:FFN_HIDDEN + lo + FFN_CHUNK])
        act = (jax.nn.silu(a) * g).astype(BF16)
        acc = acc + _dot(act, wfo_ref[lo:lo + FFN_CHUNK, :])
    out_ref[...] = _rms(acc, nfin_ref[...])


def _post(x2d, o, y, sga, sgs, wap, wglu, bglu, wsp, wout, nffn, wfi, wfo, nfin, tm):
    m = x2d.shape[0]
    tok = lambda w: pl.BlockSpec((tm, w), lambda i: (i, 0))
    const = lambda a: pl.BlockSpec(a.shape, lambda i: (0,) * a.ndim, pipeline_mode=pl.Buffered(1))
    weights = (wap, wglu, bglu, wsp, wout, nffn, wfi, wfo, nfin)
    return pl.pallas_call(
        _post_kernel,
        grid=(m // tm,),
        in_specs=[tok(D_MODEL), tok(ATTN_WIDTH), tok(SSM_WIDTH), tok(D_MODEL), tok(D_MODEL)]
                 + [const(w) for w in weights],
        out_specs=tok(D_MODEL),
        out_shape=jax.ShapeDtypeStruct((m, D_MODEL), F32),
        compiler_params=_cparams(("parallel",)),
        name="post",
    )(x2d, o, y, sga, sgs, *weights)


def _rope_tables(pos):
    half = HEAD_DIM // 2
    inv = jnp.power(jnp.float32(ROPE_THETA), -2.0 * jnp.arange(half, dtype=F32) / HEAD_DIM)
    ang = pos.astype(F32)[:, None] * inv[None, :]
    cos, sin = jnp.cos(ang), jnp.sin(ang)
    reps = LANES // HEAD_DIM
    return (jnp.tile(jnp.concatenate([cos, cos], axis=1), (1, reps)),
            jnp.tile(jnp.concatenate([-sin, sin], axis=1), (1, reps)))


def _leaf_from_T(xT, bsz, seq):
    return xT.reshape(bsz, N_HEADS, HEAD_DIM, seq).transpose(0, 3, 1, 2)[None]


def _state_in(re, im):
    n = re.shape[0]
    h = jnp.concatenate([re.reshape(n, N_SG, SG_STATE), im.reshape(n, N_SG, SG_STATE)], axis=-1)
    return h.transpose(1, 0, 2)


def _state_out(h):
    n = h.shape[0]
    return (h[..., :SG_STATE].reshape(1, n, SSM_GROUPS, SSM_STATE),
            h[..., SG_STATE:].reshape(1, n, SSM_GROUPS, SSM_STATE))


def kernel(x_prompt, x_sample, cache_k, cache_v, state_ssm_re, state_ssm_im, page_table, norm_mix, w_in,
           w_attn_proj, ssm_a_re, ssm_a_im, ssm_log_dt, ssm_b_re, ssm_b_im, ssm_c_re, ssm_c_im, ssm_d, w_glu,
           b_glu, w_ssm_proj, w_out, norm_ffn, w_ffn_in, w_ffn_out, norm_final):
    assert w_in.shape[0] == 1, "single layer"
    bsz, seq = x_prompt.shape[:2]
    nseq = x_sample.shape[0]
    past_len = page_table.shape[1] * PAGE_SIZE
    assert seq % MOBA_BLOCK == 0 and past_len % MOBA_BLOCK == 0 and x_sample.shape[1] == 1

    ssm_p = (ssm_a_re[0], ssm_a_im[0], ssm_log_dt[0], ssm_b_re[0], ssm_b_im[0], ssm_c_re[0], ssm_c_im[0])

    cos_s, sin_s = _rope_tables(jnp.full((1,), past_len, dtype=jnp.int32))
    xs2 = x_sample.reshape(nseq, D_MODEL)
    q_s, k_s, v_s, u_s, sga_s, sgs_s, w_in_bf = _inproj_sample(xs2, norm_mix, w_in[0], cos_s, sin_s)

    cos_p, sin_p = _rope_tables(jnp.arange(seq, dtype=jnp.int32))
    xp2 = x_prompt.reshape(bsz * seq, D_MODEL)
    cast_ws = (w_attn_proj[0], w_glu[0], w_ssm_proj[0], w_out[0], w_ffn_in[0], w_ffn_out[0])
    (qT, kT, kbf, kmean, vT, vTb, u_p, sga_p, sgs_p,
     wap_bf, wglu_bf, wsp_bf, wout_bf, wfi_bf, wfo_bf) = _inproj_prompt(
        xp2, norm_mix, w_in_bf, cos_p, sin_p, cast_ws, bsz, seq, tm=1024)
    post_w = (wap_bf, wglu_bf, b_glu, wsp_bf, wout_bf, norm_ffn, wfi_bf, wfo_bf, norm_final[None])
    o_p = _moba_prompt(qT, kbf, kmean.reshape(bsz, seq // MOBA_BLOCK, ATTN_WIDTH), vTb, bsz, seq)
    kc, mc, nc_tab, ap, a_one = _ssm_tables(*ssm_p, chunk=SSM_CHUNK, n_pow=SUBLANES)
    h0_p = jnp.zeros((bsz, N_SG, 1, 2 * SG_STATE), F32)
    y_p, hf_p = _ssm_prompt(u_p, kc, mc, nc_tab, ap, ssm_d, h0_p, bsz, seq)
    y_prompt = _post(xp2, o_p, y_p, sga_p, sgs_p, *post_w, tm=512).reshape(bsz, seq, D_MODEL)
    new_ssm_re_p, new_ssm_im_p = _state_out(hf_p.reshape(bsz, N_SG, 2 * SG_STATE))

    cache_kT = cache_k[0].transpose(0, 2, 3, 1)
    cache_vT = cache_v[0].transpose(0, 2, 3, 1)
    dh = lambda t: t.reshape(nseq, N_HEADS, HEAD_DIM).transpose(0, 2, 1)
    o_s = _moba_paged(page_table, dh(q_s), dh(k_s), dh(v_s), cache_kT, cache_vT).transpose(0, 2, 1)
    h0_s = _state_in(state_ssm_re[0], state_ssm_im[0])
    y_s, hf_s = _ssm_step(u_s, kc, mc, nc_tab, a_one, ssm_d, h0_s)
    y_sample = _post(xs2, o_s.reshape(nseq, ATTN_WIDTH), y_s, sga_s, sgs_s, *post_w, tm=nseq)
    new_ssm_re_s, new_ssm_im_s = _state_out(hf_s.transpose(1, 0, 2))

    return (y_prompt, y_sample.reshape(nseq, 1, D_MODEL),
            _leaf_from_T(kT, bsz, seq), _leaf_from_T(vT, bsz, seq), new_ssm_re_p, new_ssm_im_p,
            k_s.reshape(1, nseq, 1, N_HEADS, HEAD_DIM), v_s.reshape(1, nseq, 1, N_HEADS, HEAD_DIM),
            new_ssm_re_s, new_ssm_im_s)
```
